```python
import math
import jax
import jax.numpy as jnp
from jax import lax
import numpy as np

D_MODEL = 1024
BATCH = 8
SEQ = 2048
DEPTH = 2

N_MIXERS = 2
N_MLA_LAYERS = (DEPTH + 1) // 2
N_DIL_LAYERS = DEPTH // 2

MLA_HEADS = 16
MLA_NOPE = 64
MLA_ROPE = 32
MLA_QK = MLA_NOPE + MLA_ROPE
MLA_V = 64
Q_LORA = 3 * D_MODEL // 8
KV_LORA = D_MODEL // 4
MLA_IN = Q_LORA + KV_LORA + MLA_ROPE
ROPE_THETA = 10000.0
Q_BLOCK = 128

DIL_GROUPS = ((128, 1), (512, 4), (2048, 16))
N_DIL_GROUPS = len(DIL_GROUPS)
DIL_HEADS = 16
DIL_HEAD_DIM = 64
DIL_IN = N_DIL_GROUPS * 3 * DIL_HEADS * DIL_HEAD_DIM

N_BUCKETS = 32
MAX_DISTANCE = 1024

D_FF = 4 * D_MODEL
N_MOD = 6
EPS = 1e-6

kernel_name = "hybrid_mla_dilated_encoder"


def rms_norm(x, g):
    xf = x.astype(jnp.float32)
    y = xf * lax.rsqrt(jnp.mean(xf * xf, axis=-1, keepdims=True) + EPS)
    return (y * g.astype(jnp.float32)).astype(x.dtype)


def rope(x, pos):
    half = x.shape[-1] // 2
    inv = 1.0 / (ROPE_THETA ** (jnp.arange(half, dtype=jnp.float32) / half))
    ang = pos.astype(jnp.float32)[..., None] * inv
    ang = ang.reshape(ang.shape[:2] + (1,) * (x.ndim - 3) + (half,))
    cos, sin = jnp.cos(ang), jnp.sin(ang)
    x1 = x[..., :half].astype(jnp.float32)
    x2 = x[..., half:].astype(jnp.float32)
    out = jnp.concatenate([x1 * cos - x2 * sin, x2 * cos + x1 * sin], axis=-1)
    return out.astype(x.dtype)


def t5_bucket(rel):
    nb = N_BUCKETS // 2
    max_exact = nb // 2
    ret = jnp.where(rel > 0, nb, 0)
    n = jnp.abs(rel)
    nf = jnp.maximum(n, 1).astype(jnp.float32)
    large = max_exact + (jnp.log(nf / max_exact) / math.log(MAX_DISTANCE / max_exact)
                         * (nb - max_exact)).astype(jnp.int32)
    large = jnp.minimum(large, nb - 1)
    return ret + jnp.where(n < max_exact, n, large)


def dense_attention(q, k, v, scale):
    B, S, H, Dk = q.shape
    nq = S // Q_BLOCK
    qb = jnp.moveaxis(q.reshape(B, nq, Q_BLOCK, H, Dk), 1, 0)

    def block(qi):
        s = jnp.einsum('bqhd,bkhd->bhqk', qi, k, preferred_element_type=jnp.float32) * scale
        p = jax.nn.softmax(s, axis=-1).astype(v.dtype)
        return jnp.einsum('bhqk,bkhd->bqhd', p, v)

    o = lax.map(block, qb)
    return jnp.moveaxis(o, 0, 1).reshape(B, S, H, v.shape[-1])


def mla_mixer(h, pos, w_in, g_qa, w_qb, g_kva, w_kvb, g_q, g_k, w_o):
    B, S, _ = h.shape
    proj = h @ w_in
    q_lat = proj[..., :Q_LORA]
    kv_lat = proj[..., Q_LORA:Q_LORA + KV_LORA]
    k_rope = rope(proj[..., Q_LORA + KV_LORA:], pos)
    q = (rms_norm(q_lat, g_qa) @ w_qb).reshape(B, S, MLA_HEADS, MLA_QK)
    kv = (rms_norm(kv_lat, g_kva) @ w_kvb).reshape(B, S, MLA_HEADS, MLA_NOPE + MLA_V)
    q = jnp.concatenate([q[..., :MLA_NOPE], rope(q[..., MLA_NOPE:], pos)], axis=-1)
    k = jnp.concatenate([kv[..., :MLA_NOPE],
                         jnp.broadcast_to(k_rope[:, :, None, :], (B, S, MLA_HEADS, MLA_ROPE))],
                        axis=-1)
    v = kv[..., MLA_NOPE:]
    q = rms_norm(q, g_q)
    k = rms_norm(k, g_k)
    o = dense_attention(q, k, v, MLA_QK ** -0.5)
    return o.reshape(B, S, MLA_HEADS * MLA_V) @ w_o


def _to_residues(t, dilation):
    B, S = t.shape[:2]
    L = S // dilation
    t = t.reshape((B, L, dilation) + t.shape[2:])
    t = jnp.moveaxis(t, 2, 1)
    return t.reshape((B * dilation, L) + t.shape[3:])


def _from_residues(t, batch, dilation):
    L = t.shape[1]
    t = t.reshape((batch, dilation, L) + t.shape[2:])
    t = jnp.moveaxis(t, 1, 2)
    return t.reshape((batch, L * dilation) + t.shape[3:])


def dilated_group(q, k, v, pos, bias_table, window, dilation):
    B, S, H, Dh = q.shape
    half = window // (2 * dilation)
    blk = half
    L = S // dilation
    nb = -(-L // blk)
    Lp = nb * blk
    padq = Lp - L
    N = B * dilation
    qr, kr, vr, pr = (_to_residues(t, dilation) for t in (q, k, v, pos))
    qb = jnp.pad(qr, ((0, 0), (0, padq), (0, 0), (0, 0))).reshape(N, nb, blk, H, Dh)

    def key_blocks(t):
        pad = [(0, 0), (blk, padq + blk)] + [(0, 0)] * (t.ndim - 2)
        t = jnp.pad(t, pad).reshape((N, nb + 2, blk) + t.shape[2:])
        return jnp.concatenate([t[:, :-2], t[:, 1:-1], t[:, 2:]], axis=2)

    kb, vb, pk = key_blocks(kr), key_blocks(vr), key_blocks(pr)
    pq = jnp.pad(pr, ((0, 0), (0, padq))).reshape(N, nb, blk)
    qi = jnp.arange(nb)[:, None] * blk + jnp.arange(blk)[None, :]
    ki = jnp.arange(nb)[:, None] * blk - blk + jnp.arange(3 * blk)[None, :]
    valid = ((ki[:, None, :] >= 0) & (ki[:, None, :] < L)
             & (jnp.abs(qi[:, :, None] - ki[:, None, :]) <= half))
    bucket = t5_bucket(pk[:, :, None, :] - pq[:, :, :, None])
    bias = jnp.moveaxis(bias_table[bucket], -1, 2).astype(jnp.float32)
    s = jnp.einsum('nbqhd,nbkhd->nbhqk', qb, kb,
                   preferred_element_type=jnp.float32) * (DIL_HEAD_DIM ** -0.5) + bias
    s = jnp.where(valid[None, :, None], s, -jnp.inf)
    lse = jax.nn.logsumexp(s, axis=-1)
    p = jnp.exp(s - lse[..., None]).astype(v.dtype)
    o = jnp.einsum('nbhqk,nbkhd->nbqhd', p, vb).reshape(N, Lp, H, Dh)[:, :L]
    lse = jnp.moveaxis(lse, 2, 3).reshape(N, Lp, H)[:, :L]
    return _from_residues(o, B, dilation), _from_residues(lse, B, dilation)


def dilated_mixer(h, pos, w_in, g_q, g_k, rel_bias, w_o):
    B, S, _ = h.shape
    proj = (h @ w_in).reshape(B, S, N_DIL_GROUPS, 3, DIL_HEADS, DIL_HEAD_DIM)
    outs, lses = [], []
    for gi, (window, dilation) in enumerate(DIL_GROUPS):
        q = rms_norm(proj[:, :, gi, 0], g_q[gi])
        k = rms_norm(proj[:, :, gi, 1], g_k[gi])
        v = proj[:, :, gi, 2]
        table = rel_bias[:, gi * DIL_HEADS:(gi + 1) * DIL_HEADS]
        o, lse = dilated_group(q, k, v, pos, table, window, dilation)
        outs.append(o)
        lses.append(lse)
    o_all = jnp.stack(outs, axis=0)
    wts = jax.nn.softmax(jnp.stack(lses, axis=0), axis=0)
    o = jnp.einsum('gbsh,gbshd->bshd', wts.astype(o_all.dtype), o_all)
    return o.reshape(B, S, DIL_HEADS * DIL_HEAD_DIM) @ w_o


def setup_inputs(seed: int = 0) -> dict:
    key = jax.random.key(seed)
    ks = jax.random.split(key, 24)
    f32 = jnp.float32
    nrm = lambda k, shape, s: jax.random.normal(k, shape, f32) * s
    gain = lambda k, shape: 1.0 + 0.05 * jax.random.normal(k, shape, f32)
    D = D_MODEL
    return {
        "x": nrm(ks[0], (BATCH, SEQ, D), 1.0),
        "c": nrm(ks[1], (BATCH, D), 1.0),
        "positions": jnp.broadcast_to(jnp.arange(SEQ, dtype=jnp.int32), (BATCH, SEQ)),
        "ada_w": nrm(ks[2], (DEPTH, D, N_MOD * D), 0.5 * D ** -0.5),
        "ada_b": nrm(ks[3], (DEPTH, N_MOD * D), 0.02),
        "norm_mix": gain(ks[4], (DEPTH, D)),
        "norm_mlp": gain(ks[5], (DEPTH, D)),
        "mlp_w1": nrm(ks[6], (DEPTH, D, D_FF), D ** -0.5),
        "mlp_w2": nrm(ks[7], (DEPTH, D_FF, D), D_FF ** -0.5),
        "mla_w_in": nrm(ks[8], (N_MLA_LAYERS, D, MLA_IN), D ** -0.5),
        "mla_g_qa": gain(ks[9], (N_MLA_LAYERS, Q_LORA)),
        "mla_w_qb": nrm(ks[10], (N_MLA_LAYERS, Q_LORA, MLA_HEADS * MLA_QK), Q_LORA ** -0.5),
        "mla_g_kva": gain(ks[11], (N_MLA_LAYERS, KV_LORA)),
        "mla_w_kvb": nrm(ks[12], (N_MLA_LAYERS, KV_LORA, MLA_HEADS * (MLA_NOPE + MLA_V)), KV_LORA ** -0.5),
        "mla_g_q": gain(ks[13], (N_MLA_LAYERS, MLA_QK)),
        "mla_g_k": gain(ks[14], (N_MLA_LAYERS, MLA_QK)),
        "mla_w_o": nrm(ks[15], (N_MLA_LAYERS, MLA_HEADS * MLA_V, D), (MLA_HEADS * MLA_V) ** -0.5),
        "dil_w_in": nrm(ks[16], (N_DIL_LAYERS, D, DIL_IN), D ** -0.5),
        "dil_g_q": gain(ks[17], (N_DIL_LAYERS, N_DIL_GROUPS, DIL_HEAD_DIM)),
        "dil_g_k": gain(ks[18], (N_DIL_LAYERS, N_DIL_GROUPS, DIL_HEAD_DIM)),
        "dil_w_o": nrm(ks[19], (N_DIL_LAYERS, DIL_HEADS * DIL_HEAD_DIM, D), (DIL_HEADS * DIL_HEAD_DIM) ** -0.5),
        "rel_bias": nrm(ks[20], (N_BUCKETS, N_DIL_GROUPS * DIL_HEADS), 0.3),
    }


def reference(x, c, positions, ada_w, ada_b, norm_mix, norm_mlp, mlp_w1, mlp_w2,
              mla_w_in, mla_g_qa, mla_w_qb, mla_g_kva, mla_w_kvb, mla_g_q, mla_g_k, mla_w_o,
              dil_w_in, dil_g_q, dil_g_k, dil_w_o, rel_bias):
    B, S, D = x.shape
    cond = jax.nn.silu(c)
    for i in range(DEPTH):
        mod = (cond @ ada_w[i] + ada_b[i]).reshape(B, N_MOD, D)
        shift1, scale1, gate1 = mod[:, 0, None, :], mod[:, 1, None, :], mod[:, 2, None, :]
        shift2, scale2, gate2 = mod[:, 3, None, :], mod[:, 4, None, :], mod[:, 5, None, :]
        h = rms_norm(x, norm_mix[i]) * (1 + scale1) + shift1
        j = i // N_MIXERS
        if i % N_MIXERS == 0:
            y = mla_mixer(h, positions, mla_w_in[j], mla_g_qa[j], mla_w_qb[j], mla_g_kva[j],
                          mla_w_kvb[j], mla_g_q[j], mla_g_k[j], mla_w_o[j])
        else:
            y = dilated_mixer(h, positions, dil_w_in[j], dil_g_q[j], dil_g_k[j], rel_bias, dil_w_o[j])
        x = x + gate1 * y
        h = rms_norm(x, norm_mlp[i]) * (1 + scale2) + shift2
        x = x + gate2 * (jnp.square(jax.nn.relu(h @ mlp_w1[i])) @ mlp_w2[i])
    return x
```

```python
import functools
import math

import jax
import jax.numpy as jnp
from jax import lax
from jax.experimental import pallas as pl
from jax.experimental.pallas import tpu as pltpu

F32 = jnp.float32
BF16 = jnp.bfloat16

EPS = 1e-6
LOG2E = 1.4426950408889634
LN2 = 0.6931471805599453
MASKED = -1e30

N_MOD = 6
ROPE_THETA = 10000.0
MLA_ROPE = 32
DIL_GROUPS = ((128, 1), (512, 4), (2048, 16))
DIL_HEADS = 16
DIL_HEAD_DIM = 64
N_BUCKETS = 32
MAX_DISTANCE = 1024

LANES = 128
HEAD_PAD = 128

VMEM_LIMIT = 56 * 1024 * 1024

NT_DIMS = (((1,), (1,)), ((), ()))
TN_DIMS = (((0,), (0,)), ((), ()))


def _params(sem):
    return pltpu.CompilerParams(dimension_semantics=sem, vmem_limit_bytes=VMEM_LIMIT)


def _ada_norm(x, g, scale, shift):
    ms = jnp.mean(x * x, axis=-1, keepdims=True)
    return (x * lax.rsqrt(ms + EPS) * g) * (1.0 + scale) + shift


def _rms(x, g):
    ms = jnp.mean(x * x, axis=-1, keepdims=True)
    return x * lax.rsqrt(ms + EPS) * g


def _mod_kernel(c_ref, w_ref, b_ref, o_ref):
    c = c_ref[...]
    cond = c / (1.0 + jnp.exp(-c))
    o_ref[0] = (
        jnp.dot(cond.astype(BF16), w_ref[0].astype(BF16), preferred_element_type=F32) + b_ref[0]
    )


def _mod(c, ada_w, ada_b):
    depth, d, n = ada_w.shape
    b = c.shape[0]
    tn = 1536
    return pl.pallas_call(
        _mod_kernel,
        grid=(depth, n // tn),
        in_specs=[
            pl.BlockSpec((b, d), lambda l, j: (0, 0)),
            pl.BlockSpec((1, d, tn), lambda l, j: (l, 0, j)),
            pl.BlockSpec((1, 1, tn), lambda l, j: (l, 0, j)),
        ],
        out_specs=pl.BlockSpec((1, b, tn), lambda l, j: (l, 0, j)),
        out_shape=jax.ShapeDtypeStruct((depth, b, n), F32),
        compiler_params=_params(("arbitrary", "arbitrary")),
        name="mod",
    )(c, ada_w, ada_b.reshape(depth, 1, n))


def _mla_proj_kernel(x_ref, mod_ref, nmix_ref, pos_ref, inv_ref, wlat_ref, wkr_ref, gqa_ref, gkva_ref,
                     wqb_ref, wkvb_ref, wv_ref, gq_ref, gk_ref, qT_ref, k_ref, vT_ref, *, n_heads, q_lora, qk_dim):
    x = x_ref[...]
    tm = x.shape[0]
    m = mod_ref[0]
    hb = _ada_norm(x, nmix_ref[...], m[1:2], m[0:1]).astype(BF16)
    lat = jnp.dot(hb, wlat_ref[...], preferred_element_type=F32)
    qn = _rms(lat[:, :q_lora], gqa_ref[...]).astype(BF16)
    kvn = _rms(lat[:, q_lora:], gkva_ref[...]).astype(BF16)

    half = MLA_ROPE // 2
    nope = qk_dim - MLA_ROPE
    pos = pos_ref[...].astype(F32)
    ang = jnp.concatenate([inv_ref[...]] * (tm // LANES), axis=1) * pos
    cos_t = jnp.cos(ang)
    sin_t = jnp.sin(ang)

    def rope_rows(blk):
        x1 = blk[nope:nope + half]
        x2 = blk[nope + half:qk_dim]
        return jnp.concatenate(
            [blk[:nope], x1 * cos_t - x2 * sin_t, x2 * cos_t + x1 * sin_t, blk[qk_dim:]], axis=0)

    kr_t = lax.dot_general(wkr_ref[...], hb, NT_DIMS, preferred_element_type=F32)
    kr = rope_rows(kr_t).T

    q_t = lax.dot_general(wqb_ref[...], qn, NT_DIMS, preferred_element_type=F32)
    gq = jnp.concatenate([gq_ref[...]] * (tm // LANES), axis=1)
    for h in range(n_heads):
        blk = rope_rows(q_t[h * HEAD_PAD:(h + 1) * HEAD_PAD])
        ssq = jnp.sum(blk * blk, axis=0, keepdims=True)
        r = lax.rsqrt(ssq * (1.0 / qk_dim) + EPS)
        qT_ref[0, h * HEAD_PAD:(h + 1) * HEAD_PAD, :] = (blk * r * gq).astype(BF16)

    kv = jnp.dot(kvn, wkvb_ref[...], preferred_element_type=F32)
    lane = lax.broadcasted_iota(jnp.int32, (tm, HEAD_PAD), 1)
    gk = gk_ref[...]
    for h in range(n_heads):
        kb = jnp.where(lane < nope, kv[:, h * HEAD_PAD:(h + 1) * HEAD_PAD], 0.0) + kr
        ssq = jnp.sum(kb * kb, axis=1, keepdims=True)
        r = lax.rsqrt(ssq * (1.0 / qk_dim) + EPS)
        k_ref[:, h * HEAD_PAD:(h + 1) * HEAD_PAD] = (kb * r * gk).astype(BF16)

    vT_ref[0] = lax.dot_general(wv_ref[...], kvn, NT_DIMS, preferred_element_type=F32).astype(BF16)


def _mla_proj(x2d, mod_l, nmix, positions, w_in, g_qa, w_qb, g_kva, w_kvb, g_q, g_k, batch, seq):
    t, d = x2d.shape
    q_lora = g_qa.shape[0]
    kv_lora = g_kva.shape[0]
    qk_dim = g_q.shape[0]
    n_heads = w_qb.shape[1] // qk_dim
    nope = qk_dim - MLA_ROPE
    v_dim = w_kvb.shape[1] // n_heads - nope
    assert nope + v_dim == HEAD_PAD and qk_dim <= HEAD_PAD
    half = MLA_ROPE // 2
    tm = 512
    tpb = seq // tm

    w_lat = w_in[:, :q_lora + kv_lora].astype(BF16)
    w_kr = jnp.zeros((HEAD_PAD, d), F32).at[nope:qk_dim].set(w_in[:, q_lora + kv_lora:].T).astype(BF16)
    w_qb_t = jnp.pad(w_qb.T.reshape(n_heads, qk_dim, q_lora), ((0, 0), (0, HEAD_PAD - qk_dim), (0, 0)))
    w_qb_t = w_qb_t.reshape(n_heads * HEAD_PAD, q_lora).astype(BF16)
    w_kvb_b = w_kvb.astype(BF16)
    w_v_t = w_kvb.reshape(kv_lora, n_heads, nope + v_dim)[:, :, nope:].reshape(kv_lora, n_heads * v_dim).T.astype(BF16)
    q_gain = jnp.pad(g_q * (qk_dim ** -0.5 * LOG2E), (0, HEAD_PAD - qk_dim))
    gq_col = jnp.broadcast_to(q_gain[:, None], (HEAD_PAD, LANES))
    gk_row = jnp.pad(g_k, (0, HEAD_PAD - qk_dim))[None, :]
    inv = 1.0 / (ROPE_THETA ** (jnp.arange(half, dtype=F32) / half))
    inv_tab = jnp.broadcast_to(inv[:, None], (half, LANES))
    pos_row = positions.reshape(1, t)

    full = lambda shape: pl.BlockSpec(shape, lambda i: (0,) * len(shape))
    kern = functools.partial(_mla_proj_kernel, n_heads=n_heads, q_lora=q_lora, qk_dim=qk_dim)
    return pl.pallas_call(
        kern,
        grid=(t // tm,),
        in_specs=[
            pl.BlockSpec((tm, d), lambda i: (i, 0)),
            pl.BlockSpec((1, N_MOD, d), lambda i: (i // tpb, 0, 0)),
            full((1, d)),
            pl.BlockSpec((1, tm), lambda i: (0, i)),
            full((half, LANES)),
            full(w_lat.shape), full(w_kr.shape), full((1, q_lora)), full((1, kv_lora)),
            full(w_qb_t.shape), full(w_kvb_b.shape), full(w_v_t.shape),
            full((HEAD_PAD, LANES)), full((1, HEAD_PAD)),
        ],
        out_specs=[
            pl.BlockSpec((1, n_heads * HEAD_PAD, tm), lambda i: (i // tpb, 0, i % tpb)),
            pl.BlockSpec((tm, n_heads * HEAD_PAD), lambda i: (i, 0)),
            pl.BlockSpec((1, n_heads * v_dim, tm), lambda i: (i // tpb, 0, i % tpb)),
        ],
        out_shape=[
            jax.ShapeDtypeStruct((batch, n_heads * HEAD_PAD, seq), BF16),
            jax.ShapeDtypeStruct((t, n_heads * HEAD_PAD), BF16),
            jax.ShapeDtypeStruct((batch, n_heads * v_dim, seq), BF16),
        ],
        compiler_params=_params(("arbitrary",)),
        name="mla_proj",
    )(x2d, mod_l, nmix[None, :], pos_row, inv_tab, w_lat, w_kr, g_qa[None, :], g_kva[None, :],
      w_qb_t, w_kvb_b, w_v_t, gq_col, gk_row)


ONES_ROWS = 16


def _mla_attn_kernel(qT_ref, k_ref, vT_ref, oT_ref, *, tq):
    k = k_ref[0]
    seq = k.shape[0]
    v_dim = vT_ref.shape[1]
    v_ext = jnp.concatenate([vT_ref[0], jnp.ones((ONES_ROWS, seq), BF16)], axis=0)
    for i in range(seq // tq):
        q_t = qT_ref[0, :, i * tq:(i + 1) * tq]
        s_t = jnp.dot(k, q_t, preferred_element_type=F32)
        m = jnp.max(s_t, axis=0, keepdims=True)
        p = jnp.exp2(s_t - m).astype(BF16)
        r = jnp.dot(v_ext, p, preferred_element_type=F32)
        o = r[:v_dim] * (1.0 / r[v_dim:v_dim + 1])
        oT_ref[0, :, i * tq:(i + 1) * tq] = o.astype(BF16)


def _mla_attn(q_t, k_pad, v_t, n_heads):
    batch, _, seq = q_t.shape
    v_dim = v_t.shape[1] // n_heads
    k3 = k_pad.reshape(batch, seq, n_heads * HEAD_PAD)
    return pl.pallas_call(
        functools.partial(_mla_attn_kernel, tq=256),
        grid=(batch, n_heads),
        in_specs=[
            pl.BlockSpec((1, HEAD_PAD, seq), lambda b, h: (b, h, 0)),
            pl.BlockSpec((1, seq, HEAD_PAD), lambda b, h: (b, 0, h)),
            pl.BlockSpec((1, v_dim, seq), lambda b, h: (b, h, 0)),
        ],
        out_specs=pl.BlockSpec((1, v_dim, seq), lambda b, h: (b, h, 0)),
        out_shape=jax.ShapeDtypeStruct((batch, n_heads * v_dim, seq), BF16),
        compiler_params=_params(("arbitrary", "arbitrary")),
        name="mla_attn",
    )(q_t, k3, v_t)


def _attn_out_kernel(oT_ref, wo_ref, x_ref, mod_ref, nmlp_ref, x1_ref, h2_ref):
    y = lax.dot_general(oT_ref[0], wo_ref[...], TN_DIMS, preferred_element_type=F32)
    m = mod_ref[0]
    x1 = x_ref[...] + m[2:3] * y
    x1_ref[...] = x1
    h2_ref[...] = _ada_norm(x1, nmlp_ref[...], m[4:5], m[3:4]).astype(BF16)


def _attn_out(o_t, w_o, x2d, mod_l, nmlp, seq):
    t, d = x2d.shape
    tm = 512
    tpb = seq // tm
    c = o_t.shape[1]
    return pl.pallas_call(
        _attn_out_kernel,
        grid=(t // tm,),
        in_specs=[
            pl.BlockSpec((1, c, tm), lambda i: (i // tpb, 0, i % tpb)),
            pl.BlockSpec((c, d), lambda i: (0, 0)),
            pl.BlockSpec((tm, d), lambda i: (i, 0)),
            pl.BlockSpec((1, N_MOD, d), lambda i: (i // tpb, 0, 0)),
            pl.BlockSpec((1, d), lambda i: (0, 0)),
        ],
        out_specs=[pl.BlockSpec((tm, d), lambda i: (i, 0)), pl.BlockSpec((tm, d), lambda i: (i, 0))],
        out_shape=[jax.ShapeDtypeStruct((t, d), F32), jax.ShapeDtypeStruct((t, d), BF16)],
        compiler_params=_params(("arbitrary",)),
        name="attn_out",
    )(o_t, w_o.astype(BF16), x2d, mod_l, nmlp[None, :])


def _mlp_kernel(x_ref, h_ref, w1_ref, w2_ref, mod_ref, nnext_ref, modn_ref, x2_ref, hn_ref, acc_ref):
    j = pl.program_id(1)

    @pl.when(j == 0)
    def _():
        acc_ref[...] = jnp.zeros_like(acc_ref)

    u = jnp.dot(h_ref[...], w1_ref[...], preferred_element_type=F32)
    u = jnp.square(jnp.maximum(u, 0.0)).astype(BF16)
    acc_ref[...] += jnp.dot(u, w2_ref[...], preferred_element_type=F32)

    @pl.when(j == pl.num_programs(1) - 1)
    def _():
        x2 = x_ref[...] + mod_ref[0][5:6] * acc_ref[...]
        x2_ref[...] = x2
        mn = modn_ref[0]
        hn_ref[...] = _ada_norm(x2, nnext_ref[...], mn[1:2], mn[0:1]).astype(BF16)


def _mlp(x1, h2, w1, w2, mod_l, nnext, mod_next, seq):
    t, d = x1.shape
    ff = w1.shape[1]
    tm, tf = 1024, 1024
    tpb = seq // tm
    return pl.pallas_call(
        _mlp_kernel,
        grid=(t // tm, ff // tf),
        in_specs=[
            pl.BlockSpec((tm, d), lambda i, j: (i, 0)),
            pl.BlockSpec((tm, d), lambda i, j: (i, 0)),
            pl.BlockSpec((d, tf), lambda i, j: (0, j)),
            pl.BlockSpec((tf, d), lambda i, j: (j, 0)),
            pl.BlockSpec((1, N_MOD, d), lambda i, j: (i // tpb, 0, 0)),
            pl.BlockSpec((1, d), lambda i, j: (0, 0)),
            pl.BlockSpec((1, N_MOD, d), lambda i, j: (i // tpb, 0, 0)),
        ],
        out_specs=[pl.BlockSpec((tm, d), lambda i, j: (i, 0)), pl.BlockSpec((tm, d), lambda i, j: (i, 0))],
        out_shape=[jax.ShapeDtypeStruct((t, d), F32), jax.ShapeDtypeStruct((t, d), BF16)],
        scratch_shapes=[pltpu.VMEM((tm, d), F32)],
        compiler_params=_params(("arbitrary", "arbitrary")),
        name="mlp",
    )(x1, h2, w1.astype(BF16), w2.astype(BF16), mod_l, nnext[None, :], mod_next)


DIL_TQ = 128


def _dil_geometry(seq, dilation):
    length = seq // dilation
    kwin = min(2 * DIL_TQ, length)
    vwin = min(3 * DIL_TQ, length)
    return length, kwin, vwin, length // DIL_TQ


def _t5_bucket(rel):
    nb = N_BUCKETS // 2
    max_exact = nb // 2
    ret = jnp.where(rel > 0, nb, 0)
    n = jnp.abs(rel)
    nf = jnp.maximum(n, 1).astype(F32)
    large = max_exact + (jnp.log(nf / max_exact) / math.log(MAX_DISTANCE / max_exact)
                         * (nb - max_exact)).astype(jnp.int32)
    large = jnp.minimum(large, nb - 1)
    return ret + jnp.where(n < max_exact, n, large)


def _dil_bias_kernel(tab_ref, o_ref, *, group, dilation, kwin, n_var):
    var = pl.program_id(0)
    pair = pl.program_id(1)
    half = DIL_TQ // 2
    if n_var == 1:
        shift = 0
    else:
        shift = jnp.where(var == 0, 0, jnp.where(var == 1, -half, DIL_TQ - kwin))
    kk = lax.broadcasted_iota(jnp.int32, (kwin, DIL_TQ), 0)
    qi = lax.broadcasted_iota(jnp.int32, (kwin, DIL_TQ), 1)
    rel_a = kk + shift - qi
    valid = jnp.abs(rel_a) <= half
    bucket = _t5_bucket(rel_a * dilation)
    for side in range(2):
        col = group * DIL_HEADS + 2 * pair + side
        acc = jnp.zeros((kwin, DIL_TQ), F32)
        for b in range(N_BUCKETS):
            acc = jnp.where(bucket == b, tab_ref[b, col], acc)
        o_ref[0, 0, :, side * DIL_TQ:(side + 1) * DIL_TQ] = jnp.where(valid, acc * LOG2E, MASKED)


def _dil_bias(rel_bias, group, seq):
    _, dilation = DIL_GROUPS[group]
    _, kwin, _, n_tiles = _dil_geometry(seq, dilation)
    n_var = 1 if n_tiles == 1 else 3
    n_pairs = DIL_HEADS // 2
    kern = functools.partial(_dil_bias_kernel, group=group, dilation=dilation, kwin=kwin, n_var=n_var)
    return pl.pallas_call(
        kern,
        grid=(n_var, n_pairs),
        in_specs=[pl.BlockSpec(memory_space=pltpu.SMEM)],
        out_specs=pl.BlockSpec((1, 1, kwin, 2 * DIL_TQ), lambda v, p: (v, p, 0, 0)),
        out_shape=jax.ShapeDtypeStruct((n_var, n_pairs, kwin, 2 * DIL_TQ), F32),
        compiler_params=_params(("arbitrary", "arbitrary")),
        name=f"dil_bias{group}",
    )(rel_bias)


def _dil_proj_kernel(h_ref, wq_ref, wk_ref, wv_ref, gq_ref, gk_ref, qT_ref, k_ref, vT_ref, *, rb, d_model):
    ta = h_ref.shape[1]
    hd = DIL_HEAD_DIM
    hcat = jnp.concatenate([h_ref[0, :, j * d_model:(j + 1) * d_model] for j in range(rb)], axis=0)
    mt = rb * ta

    q_t = lax.dot_general(wq_ref[...], hcat, NT_DIMS, preferred_element_type=F32)
    gq = jnp.concatenate([gq_ref[...]] * (mt // LANES), axis=1)
    for h in range(DIL_HEADS):
        blk = q_t[h * hd:(h + 1) * hd]
        r = lax.rsqrt(jnp.sum(blk * blk, axis=0, keepdims=True) * (1.0 / hd) + EPS)
        qb = (blk * r * gq).astype(BF16)
        for j in range(rb):
            qT_ref[0, j, h * hd:(h + 1) * hd, :] = qb[:, j * ta:(j + 1) * ta]

    kf = jnp.dot(hcat, wk_ref[...], preferred_element_type=F32)
    lane = lax.broadcasted_iota(jnp.int32, (mt, LANES), 1)
    low = lane < hd
    for c in range(DIL_HEADS * hd // LANES):
        y = kf[:, c * LANES:(c + 1) * LANES]
        y2 = y * y
        s_lo = jnp.sum(jnp.where(low, y2, 0.0), axis=1, keepdims=True)
        s_hi = jnp.sum(jnp.where(low, 0.0, y2), axis=1, keepdims=True)
        r = lax.rsqrt(jnp.where(low, s_lo, s_hi) * (1.0 / hd) + EPS)
        kb = (y * r * gk_ref[...]).astype(BF16)
        for j in range(rb):
            k_ref[0, j, :, c * LANES:(c + 1) * LANES] = kb[j * ta:(j + 1) * ta]

    v_t = lax.dot_general(wv_ref[...], hcat, NT_DIMS, preferred_element_type=F32).astype(BF16)
    for j in range(rb):
        for s in range(ta // LANES):
            vT_ref[0, j, s] = v_t[:, j * ta + s * LANES:j * ta + (s + 1) * LANES]


def _dil_proj(hn, w_in_g, g_q, g_k, group, batch, seq):
    d_model = hn.shape[1]
    _, dilation = DIL_GROUPS[group]
    length = seq // dilation
    c = DIL_HEADS * DIL_HEAD_DIM
    ta = min(512, length)
    rb = min(dilation, 512 // ta)
    hview = hn.reshape(batch, length, dilation * d_model)
    wq_t = w_in_g[:, :c].T.astype(BF16)
    wk = w_in_g[:, c:2 * c].astype(BF16)
    wv_t = w_in_g[:, 2 * c:].T.astype(BF16)
    gq_col = jnp.broadcast_to((g_q * (DIL_HEAD_DIM ** -0.5 * LOG2E))[:, None], (DIL_HEAD_DIM, LANES))
    gk_row = jnp.tile(g_k, LANES // DIL_HEAD_DIM)[None, :]
    kern = functools.partial(_dil_proj_kernel, rb=rb, d_model=d_model)
    return pl.pallas_call(
        kern,
        grid=(batch, dilation // rb, length // ta),
        in_specs=[
            pl.BlockSpec((1, ta, rb * d_model), lambda b, r, t: (b, t, r)),
            pl.BlockSpec((c, d_model), lambda b, r, t: (0, 0)),
            pl.BlockSpec((d_model, c), lambda b, r, t: (0, 0)),
            pl.BlockSpec((c, d_model), lambda b, r, t: (0, 0)),
            pl.BlockSpec((DIL_HEAD_DIM, LANES), lambda b, r, t: (0, 0)),
            pl.BlockSpec((1, LANES), lambda b, r, t: (0, 0)),
        ],
        out_specs=[
            pl.BlockSpec((1, rb, c, ta), lambda b, r, t: (b, r, 0, t)),
            pl.BlockSpec((1, rb, ta, c), lambda b, r, t: (b, r, t, 0)),
            pl.BlockSpec((1, rb, ta // LANES, c, LANES), lambda b, r, t: (b, r, t, 0, 0)),
        ],
        out_shape=[
            jax.ShapeDtypeStruct((batch, dilation, c, length), BF16),
            jax.ShapeDtypeStruct((batch, dilation, length, c), BF16),
            jax.ShapeDtypeStruct((batch, dilation, length // LANES, c, LANES), BF16),
        ],
        compiler_params=_params(("arbitrary", "arbitrary", "arbitrary")),
        name=f"dil_proj{group}",
    )(hview, wq_t, wk, wv_t, gq_col, gk_row)


def _dil_attn_kernel(qT_ref, k_ref, vT_ref, bm_ref, o_ref, st_ref, *, length, kwin, vwin):
    t = pl.program_id(2)
    tq = DIL_TQ
    hd = DIL_HEAD_DIM
    n_pairs = DIL_HEADS // 2
    a0 = t * tq
    ks = pl.multiple_of(jnp.clip(a0 - tq // 2, 0, length - kwin), tq // 2)
    vblk = jnp.clip(t - 1, 0, (length - vwin) // LANES)
    off = ks - vblk * LANES

    row = lax.broadcasted_iota(jnp.int32, (LANES, tq), 0)
    stats = []
    for p in range(n_pairs):
        q2 = qT_ref[0, 0, p * LANES:(p + 1) * LANES, :]
        zero = jnp.zeros_like(q2)
        bd = jnp.concatenate([jnp.where(row < hd, q2, zero), jnp.where(row < hd, zero, q2)], axis=1)
        k2 = k_ref[0, 0, pl.ds(ks, kwin), p * LANES:(p + 1) * LANES]
        s_t = jnp.dot(k2, bd, preferred_element_type=F32) + bm_ref[0, p]
        m = jnp.max(s_t, axis=0, keepdims=True)
        pb = jnp.exp2(s_t - m).astype(BF16)
        if vwin == kwin:
            p_ext = pb
        else:
            pad = vwin - kwin
            z = jnp.zeros((pad, 2 * tq), BF16)
            zh = jnp.zeros((pad // 2, 2 * tq), BF16)
            p_ext = jnp.where(off == 0, jnp.concatenate([pb, z], axis=0),
                              jnp.where(off == pad, jnp.concatenate([z, pb], axis=0),
                                        jnp.concatenate([zh, pb, zh], axis=0)))
        v2 = jnp.concatenate(
            [vT_ref[0, 0, vblk + s, p * LANES:(p + 1) * LANES, :] for s in range(vwin // LANES)], axis=1)
        v_ext = jnp.concatenate([v2, jnp.ones((ONES_ROWS, vwin), BF16)], axis=0)
        res = jnp.dot(v_ext, p_ext, preferred_element_type=F32)
        den = res[LANES:LANES + 1]
        rinv = 1.0 / den
        o_pair = jnp.concatenate([res[:hd, :tq] * rinv[:, :tq], res[hd:LANES, tq:] * rinv[:, tq:]], axis=0)
        o_ref[0, :, p * LANES:(p + 1) * LANES] = o_pair.T.astype(BF16)
        lse = (m + jnp.log2(den)) * LN2
        stats += [lse[:, :tq], lse[:, tq:]]
    st = jnp.concatenate(stats + [jnp.zeros((LANES - DIL_HEADS, tq), F32)], axis=0)
    st_ref[0] = st.T


def _dil_attn(q_t, k, v_t, bias, group, batch, seq):
    _, dilation = DIL_GROUPS[group]
    length, kwin, vwin, n_tiles = _dil_geometry(seq, dilation)
    c = DIL_HEADS * DIL_HEAD_DIM
    n_var = bias.shape[0]
    if n_var == 1:
        var_of = lambda t: 0
    else:
        var_of = lambda t: jnp.where(t == 0, 0, jnp.where(t == n_tiles - 1, 2, 1))
    kern = functools.partial(_dil_attn_kernel, length=length, kwin=kwin, vwin=vwin)
    o, st = pl.pallas_call(
        kern,
        grid=(batch, dilation, n_tiles),
        in_specs=[
            pl.BlockSpec((1, 1, c, DIL_TQ), lambda b, r, t: (b, r, 0, t)),
            pl.BlockSpec((1, 1, length, c), lambda b, r, t: (b, r, 0, 0)),
            pl.BlockSpec((1, 1, length // LANES, c, LANES), lambda b, r, t: (b, r, 0, 0, 0)),
            pl.BlockSpec((1, DIL_HEADS // 2, kwin, 2 * DIL_TQ), lambda b, r, t: (var_of(t), 0, 0, 0)),
        ],
        out_specs=[
            pl.BlockSpec((1, DIL_TQ, c), lambda b, r, t: (b, t, r)),
            pl.BlockSpec((1, DIL_TQ, LANES), lambda b, r, t: (b, t, r)),
        ],
        out_shape=[
            jax.ShapeDtypeStruct((batch, length, dilation * c), BF16),
            jax.ShapeDtypeStruct((batch, length, dilation * LANES), F32),
        ],
        compiler_params=_params(("arbitrary", "arbitrary", "arbitrary")),
        name=f"dil_attn{group}",
    )(q_t, k, v_t, bias)
    return o.reshape(batch * seq, c), st.reshape(batch * seq, LANES)


def _dil_out_kernel(o0_ref, o1_ref, o2_ref, s0_ref, s1_ref, s2_ref, e_ref, wo_ref, x_ref, mod_ref, nmlp_ref,
                    x1_ref, h2_ref):
    l0, l1, l2 = s0_ref[...], s1_ref[...], s2_ref[...]
    mx = jnp.maximum(jnp.maximum(l0, l1), l2)
    e0, e1, e2 = jnp.exp(l0 - mx), jnp.exp(l1 - mx), jnp.exp(l2 - mx)
    rden = 1.0 / (e0 + e1 + e2)
    e = e_ref[...]

    def expand(w):
        hi = w.astype(BF16)
        lo = (w - hi.astype(F32)).astype(BF16)
        return jnp.dot(hi, e, preferred_element_type=F32) + jnp.dot(lo, e, preferred_element_type=F32)

    o = (expand(e0 * rden) * o0_ref[...].astype(F32) + expand(e1 * rden) * o1_ref[...].astype(F32)
         + expand(e2 * rden) * o2_ref[...].astype(F32))
    y = jnp.dot(o.astype(BF16), wo_ref[...], preferred_element_type=F32)
    m = mod_ref[0]
    x1 = x_ref[...] + m[2:3] * y
    x1_ref[...] = x1
    h2_ref[...] = _ada_norm(x1, nmlp_ref[...], m[4:5], m[3:4]).astype(BF16)


def _dil_out(outs, stats, w_o, x2d, mod_l, nmlp, seq):
    t, d = x2d.shape
    c = w_o.shape[0]
    tm = 512
    tpb = seq // tm
    expand = jnp.repeat(jnp.eye(DIL_HEADS, dtype=F32), DIL_HEAD_DIM, axis=1)
    expand = jnp.pad(expand, ((0, LANES - DIL_HEADS), (0, 0))).astype(BF16)
    row = lambda w: pl.BlockSpec((tm, w), lambda i: (i, 0))
    return pl.pallas_call(
        _dil_out_kernel,
        grid=(t // tm,),
        in_specs=[row(c), row(c), row(c), row(LANES), row(LANES), row(LANES),
                  pl.BlockSpec((LANES, c), lambda i: (0, 0)),
                  pl.BlockSpec((c, d), lambda i: (0, 0)),
                  row(d),
                  pl.BlockSpec((1, N_MOD, d), lambda i: (i // tpb, 0, 0)),
                  pl.BlockSpec((1, d), lambda i: (0, 0))],
        out_specs=[row(d), row(d)],
        out_shape=[jax.ShapeDtypeStruct((t, d), F32), jax.ShapeDtypeStruct((t, d), BF16)],
        compiler_params=_params(("arbitrary",)),
        name="dil_out",
    )(*outs, *stats, expand, w_o.astype(BF16), x2d, mod_l, nmlp[None, :])


def kernel(x, c, positions, ada_w, ada_b, norm_mix, norm_mlp, mlp_w1, mlp_w2, mla_w_in, mla_g_qa, mla_w_qb,
           mla_g_kva, mla_w_kvb, mla_g_q, mla_g_k, mla_w_o, dil_w_in, dil_g_q, dil_g_k, dil_w_o, rel_bias):
    batch, seq, d = x.shape
    depth = ada_w.shape[0]
    assert depth == 2 and len(DIL_GROUPS) == 3
    mod = _mod(c, ada_w, ada_b).reshape(depth, batch, N_MOD, d)
    x2d = x.reshape(batch * seq, d)

    n_heads = mla_w_qb.shape[2] // mla_g_q.shape[1]
    q_t, k_pad, v_t = _mla_proj(x2d, mod[0], norm_mix[0], positions, mla_w_in[0], mla_g_qa[0], mla_w_qb[0],
                                mla_g_kva[0], mla_w_kvb[0], mla_g_q[0], mla_g_k[0], batch, seq)
    o_t = _mla_attn(q_t, k_pad, v_t, n_heads)
    x1, h2 = _attn_out(o_t, mla_w_o[0], x2d, mod[0], norm_mlp[0], seq)
    x2, hn = _mlp(x1, h2, mlp_w1[0], mlp_w2[0], mod[0], norm_mix[1], mod[1], seq)

    c_grp = 3 * DIL_HEADS * DIL_HEAD_DIM
    outs, stats = [], []
    for g in range(len(DIL_GROUPS)):
        w_g = dil_w_in[0][:, g * c_grp:(g + 1) * c_grp]
        q_g, k_g, v_g = _dil_proj(hn, w_g, dil_g_q[0, g], dil_g_k[0, g], g, batch, seq)
        o_g, st_g = _dil_attn(q_g, k_g, v_g, _dil_bias(rel_bias, g, seq), g, batch, seq)
        outs.append(o_g)
        stats.append(st_g)
    x3, h4 = _dil_out(outs, stats, dil_w_o[0], x2, mod[1], norm_mlp[1], seq)
    x4, _ = _mlp(x3, h4, mlp_w1[1], mlp_w2[1], mod[1], norm_mix[1], mod[1], seq)
    return x4.reshape(batch, seq, d)
```

```python
import functools
import math

import jax
import jax.numpy as jnp
from jax import lax
from jax.experimental import pallas as pl
from jax.experimental.pallas import tpu as pltpu

F32 = jnp.float32
BF16 = jnp.bfloat16

EPS = 1e-6
LOG2E = 1.4426950408889634
LN2 = 0.6931471805599453
MASKED = -1e30

N_MOD = 6
ROPE_THETA = 10000.0
MLA_ROPE = 32
DIL_GROUPS = ((128, 1), (512, 4), (2048, 16))
DIL_HEADS = 16
DIL_HEAD_DIM = 64
N_BUCKETS = 32
MAX_DISTANCE = 1024

LANES = 128
HEAD_PAD = 128

VMEM_LIMIT = 56 * 1024 * 1024

NT_DIMS = (((1,), (1,)), ((), ()))
TN_DIMS = (((0,), (0,)), ((), ()))


def _params(sem):
    return pltpu.CompilerParams(dimension_semantics=sem, vmem_limit_bytes=VMEM_LIMIT)


def _ada_norm(x, g, scale, shift):
    ms = jnp.mean(x * x, axis=-1, keepdims=True)
    return (x * lax.rsqrt(ms + EPS) * g) * (1.0 + scale) + shift


def _rms(x, g):
    ms = jnp.mean(x * x, axis=-1, keepdims=True)
    return x * lax.rsqrt(ms + EPS) * g


def _mod_kernel(c_ref, w_ref, b_ref, o_ref):
    c = c_ref[...]
    cond = c / (1.0 + jnp.exp(-c))
    o_ref[0] = (
        jnp.dot(cond.astype(BF16), w_ref[0].astype(BF16), preferred_element_type=F32) + b_ref[0]
    )


def _mod(c, ada_w, ada_b):
    depth, d, n = ada_w.shape
    b = c.shape[0]
    tn = 1536
    return pl.pallas_call(
        _mod_kernel,
        grid=(depth, n // tn),
        in_specs=[
            pl.BlockSpec((b, d), lambda l, j: (0, 0)),
            pl.BlockSpec((1, d, tn), lambda l, j: (l, 0, j)),
            pl.BlockSpec((1, 1, tn), lambda l, j: (l, 0, j)),
        ],
        out_specs=pl.BlockSpec((1, b, tn), lambda l, j: (l, 0, j)),
        out_shape=jax.ShapeDtypeStruct((depth, b, n), F32),
        compiler_params=_params(("arbitrary", "arbitrary")),
        name="mod",
    )(c, ada_w, ada_b.reshape(depth, 1, n))


def _mla_proj_kernel(x_ref, mod_ref, nmix_ref, pos_ref, inv_ref, wlat_ref, wkr_ref, gqa_ref, gkva_ref,
                     wqb_ref, wkvb_ref, wv_ref, gq_ref, gk_ref, qT_ref, k_ref, vT_ref, *, n_heads, q_lora, qk_dim):
    x = x_ref[...]
    tm = x.shape[0]
    m = mod_ref[0]
    hb = _ada_norm(x, nmix_ref[...], m[1:2], m[0:1]).astype(BF16)
    lat = jnp.dot(hb, wlat_ref[...], preferred_element_type=F32)
    qn = _rms(lat[:, :q_lora], gqa_ref[...]).astype(BF16)
    kvn = _rms(lat[:, q_lora:], gkva_ref[...]).astype(BF16)

    half = MLA_ROPE // 2
    nope = qk_dim - MLA_ROPE
    pos = pos_ref[...].astype(F32)
    ang = jnp.concatenate([inv_ref[...]] * (tm // LANES), axis=1) * pos
    cos_t = jnp.cos(ang)
    sin_t = jnp.sin(ang)

    def rope_rows(blk):
        x1 = blk[nope:nope + half]
        x2 = blk[nope + half:qk_dim]
        return jnp.concatenate(
            [blk[:nope], x1 * cos_t - x2 * sin_t, x2 * cos_t + x1 * sin_t, blk[qk_dim:]], axis=0)

    kr_t = lax.dot_general(wkr_ref[...], hb, NT_DIMS, preferred_element_type=F32)
    kr = rope_rows(kr_t).T

    q_t = lax.dot_general(wqb_ref[...], qn, NT_DIMS, preferred_element_type=F32)
    gq = jnp.concatenate([gq_ref[...]] * (tm // LANES), axis=1)
    for h in range(n_heads):
        blk = rope_rows(q_t[h * HEAD_PAD:(h + 1) * HEAD_PAD])
        ssq = jnp.sum(blk * blk, axis=0, keepdims=True)
        r = lax.rsqrt(ssq * (1.0 / qk_dim) + EPS)
        qT_ref[0, h * HEAD_PAD:(h + 1) * HEAD_PAD, :] = (blk * r * gq).astype(BF16)

    kv = jnp.dot(kvn, wkvb_ref[...], preferred_element_type=F32)
    lane = lax.broadcasted_iota(jnp.int32, (tm, HEAD_PAD), 1)
    gk = gk_ref[...]
    for h in range(n_heads):
        kb = jnp.where(lane < nope, kv[:, h * HEAD_PAD:(h + 1) * HEAD_PAD], 0.0) + kr
        ssq = jnp.sum(kb * kb, axis=1, keepdims=True)
        r = lax.rsqrt(ssq * (1.0 / qk_dim) + EPS)
        k_ref[:, h * HEAD_PAD:(h + 1) * HEAD_PAD] = (kb * r * gk).astype(BF16)

    vT_ref[0] = lax.dot_general(wv_ref[...], kvn, NT_DIMS, preferred_element_type=F32).astype(BF16)


def _mla_proj(x2d, mod_l, nmix, positions, w_in, g_qa, w_qb, g_kva, w_kvb, g_q, g_k, batch, seq):
    t, d = x2d.shape
    q_lora = g_qa.shape[0]
    kv_lora = g_kva.shape[0]
    qk_dim = g_q.shape[0]
    n_heads = w_qb.shape[1] // qk_dim
    nope = qk_dim - MLA_ROPE
    v_dim = w_kvb.shape[1] // n_heads - nope
    assert nope + v_dim == HEAD_PAD and qk_dim <= HEAD_PAD
    half = MLA_ROPE // 2
    tm = 512
    tpb = seq // tm

    w_lat = w_in[:, :q_lora + kv_lora].astype(BF16)
    w_kr = jnp.zeros((HEAD_PAD, d), F32).at[nope:qk_dim].set(w_in[:, q_lora + kv_lora:].T).astype(BF16)
    w_qb_t = jnp.pad(w_qb.T.reshape(n_heads, qk_dim, q_lora), ((0, 0), (0, HEAD_PAD - qk_dim), (0, 0)))
    w_qb_t = w_qb_t.reshape(n_heads * HEAD_PAD, q_lora).astype(BF16)
    w_kvb_b = w_kvb.astype(BF16)
    w_v_t = w_kvb.reshape(kv_lora, n_heads, nope + v_dim)[:, :, nope:].reshape(kv_lora, n_heads * v_dim).T.astype(BF16)
    q_gain = jnp.pad(g_q * (qk_dim ** -0.5 * LOG2E), (0, HEAD_PAD - qk_dim))
    gq_col = jnp.broadcast_to(q_gain[:, None], (HEAD_PAD, LANES))
    gk_row = jnp.pad(g_k, (0, HEAD_PAD - qk_dim))[None, :]
    inv = 1.0 / (ROPE_THETA ** (jnp.arange(half, dtype=F32) / half))
    inv_tab = jnp.broadcast_to(inv[:, None], (half, LANES))
    pos_row = positions.reshape(1, t)

    full = lambda shape: pl.BlockSpec(shape, lambda i: (0,) * len(shape))
    kern = functools.partial(_mla_proj_kernel, n_heads=n_heads, q_lora=q_lora, qk_dim=qk_dim)
    return pl.pallas_call(
        kern,
        grid=(t // tm,),
        in_specs=[
            pl.BlockSpec((tm, d), lambda i: (i, 0)),
            pl.BlockSpec((1, N_MOD, d), lambda i: (i // tpb, 0, 0)),
            full((1, d)),
            pl.BlockSpec((1, tm), lambda i: (0, i)),
            full((half, LANES)),
            full(w_lat.shape), full(w_kr.shape), full((1, q_lora)), full((1, kv_lora)),
            full(w_qb_t.shape), full(w_kvb_b.shape), full(w_v_t.shape),
            full((HEAD_PAD, LANES)), full((1, HEAD_PAD)),
        ],
        out_specs=[
            pl.BlockSpec((1, n_heads * HEAD_PAD, tm), lambda i: (i // tpb, 0, i % tpb)),
            pl.BlockSpec((tm, n_heads * HEAD_PAD), lambda i: (i, 0)),
            pl.BlockSpec((1, n_heads * v_dim, tm), lambda i: (i // tpb, 0, i % tpb)),
        ],
        out_shape=[
            jax.ShapeDtypeStruct((batch, n_heads * HEAD_PAD, seq), BF16),
            jax.ShapeDtypeStruct((t, n_heads * HEAD_PAD), BF16),
            jax.ShapeDtypeStruct((batch, n_heads * v_dim, seq), BF16),
        ],
        compiler_params=_params(("arbitrary",)),
        name="mla_proj",
    )(x2d, mod_l, nmix[None, :], pos_row, inv_tab, w_lat, w_kr, g_qa[None, :], g_kva[None, :],
      w_qb_t, w_kvb_b, w_v_t, gq_col, gk_row)


ONES_ROWS = 16


def _mla_attn_kernel(qT_ref, k_ref, vT_ref, oT_ref, s_ref, *, tq, tn, kc):
    k = k_ref[0]
    seq = k.shape[0]
    v_dim = vT_ref.shape[1]
    v_ext = jnp.concatenate([vT_ref[0], jnp.ones((ONES_ROWS, seq), BF16)], axis=0)
    n_tiles = seq // tq

    def scores(i):
        s_ref[i % 2] = jnp.dot(k, qT_ref[0, :, i * tq:(i + 1) * tq], preferred_element_type=F32)

    scores(0)
    for i in range(n_tiles):
        if i + 1 < n_tiles:
            scores(i + 1)
        for j in range(tq // tn):
            m = jnp.full((1, tn), MASKED, F32)
            acc = jnp.zeros((v_dim + ONES_ROWS, tn), F32)
            for c in range(seq // kc):
                s_c = s_ref[i % 2, c * kc:(c + 1) * kc, j * tn:(j + 1) * tn]
                m_new = jnp.maximum(m, jnp.max(s_c, axis=0, keepdims=True))
                p = jnp.exp2(s_c - m_new).astype(BF16)
                acc = acc * jnp.exp2(m - m_new) + jnp.dot(v_ext[:, c * kc:(c + 1) * kc], p,
                                                          preferred_element_type=F32)
                m = m_new
            o = acc[:v_dim] * (1.0 / acc[v_dim:v_dim + 1])
            oT_ref[0, :, i * tq + j * tn:i * tq + (j + 1) * tn] = o.astype(BF16)


def _mla_attn(q_t, k_pad, v_t, n_heads):
    batch, _, seq = q_t.shape
    v_dim = v_t.shape[1] // n_heads
    k3 = k_pad.reshape(batch, seq, n_heads * HEAD_PAD)
    tq = 512
    return pl.pallas_call(
        functools.partial(_mla_attn_kernel, tq=tq, tn=256, kc=256),
        grid=(batch, n_heads),
        in_specs=[
            pl.BlockSpec((1, HEAD_PAD, seq), lambda b, h: (b, h, 0)),
            pl.BlockSpec((1, seq, HEAD_PAD), lambda b, h: (b, 0, h)),
            pl.BlockSpec((1, v_dim, seq), lambda b, h: (b, h, 0)),
        ],
        out_specs=pl.BlockSpec((1, v_dim, seq), lambda b, h: (b, h, 0)),
        out_shape=jax.ShapeDtypeStruct((batch, n_heads * v_dim, seq), BF16),
        scratch_shapes=[pltpu.VMEM((2, seq, tq), F32)],
        compiler_params=_params(("arbitrary", "arbitrary")),
        name="mla_attn",
    )(q_t, k3, v_t)


def _attn_out_kernel(oT_ref, wo_ref, x_ref, mod_ref, nmlp_ref, x1_ref, h2_ref):
    y = lax.dot_general(oT_ref[0], wo_ref[...], TN_DIMS, preferred_element_type=F32)
    m = mod_ref[0]
    x1 = x_ref[...] + m[2:3] * y
    x1_ref[...] = x1
    h2_ref[...] = _ada_norm(x1, nmlp_ref[...], m[4:5], m[3:4]).astype(BF16)


def _attn_out(o_t, w_o, x2d, mod_l, nmlp, seq):
    t, d = x2d.shape
    tm = 512
    tpb = seq // tm
    c = o_t.shape[1]
    return pl.pallas_call(
        _attn_out_kernel,
        grid=(t // tm,),
        in_specs=[
            pl.BlockSpec((1, c, tm), lambda i: (i // tpb, 0, i % tpb)),
            pl.BlockSpec((c, d), lambda i: (0, 0)),
            pl.BlockSpec((tm, d), lambda i: (i, 0)),
            pl.BlockSpec((1, N_MOD, d), lambda i: (i // tpb, 0, 0)),
            pl.BlockSpec((1, d), lambda i: (0, 0)),
        ],
        out_specs=[pl.BlockSpec((tm, d), lambda i: (i, 0)), pl.BlockSpec((tm, d), lambda i: (i, 0))],
        out_shape=[jax.ShapeDtypeStruct((t, d), F32), jax.ShapeDtypeStruct((t, d), BF16)],
        compiler_params=_params(("arbitrary",)),
        name="attn_out",
    )(o_t, w_o.astype(BF16), x2d, mod_l, nmlp[None, :])


def _mlp_kernel(x_ref, h_ref, w1_ref, w2_ref, mod_ref, nnext_ref, modn_ref, x2_ref, hn_ref, acc_ref):
    j = pl.program_id(1)

    @pl.when(j == 0)
    def _():
        acc_ref[...] = jnp.zeros_like(acc_ref)

    u = jnp.dot(h_ref[...], w1_ref[...], preferred_element_type=F32)
    u = jnp.square(jnp.maximum(u, 0.0)).astype(BF16)
    acc_ref[...] += jnp.dot(u, w2_ref[...], preferred_element_type=F32)

    @pl.when(j == pl.num_programs(1) - 1)
    def _():
        x2 = x_ref[...] + mod_ref[0][5:6] * acc_ref[...]
        x2_ref[...] = x2
        mn = modn_ref[0]
        hn_ref[...] = _ada_norm(x2, nnext_ref[...], mn[1:2], mn[0:1]).astype(BF16)


def _mlp(x1, h2, w1, w2, mod_l, nnext, mod_next, seq):
    t, d = x1.shape
    ff = w1.shape[1]
    tm, tf = 1024, 1024
    tpb = seq // tm
    return pl.pallas_call(
        _mlp_kernel,
        grid=(t // tm, ff // tf),
        in_specs=[
            pl.BlockSpec((tm, d), lambda i, j: (i, 0)),
            pl.BlockSpec((tm, d), lambda i, j: (i, 0)),
            pl.BlockSpec((d, tf), lambda i, j: (0, j)),
            pl.BlockSpec((tf, d), lambda i, j: (j, 0)),
            pl.BlockSpec((1, N_MOD, d), lambda i, j: (i // tpb, 0, 0)),
            pl.BlockSpec((1, d), lambda i, j: (0, 0)),
            pl.BlockSpec((1, N_MOD, d), lambda i, j: (i // tpb, 0, 0)),
        ],
        out_specs=[pl.BlockSpec((tm, d), lambda i, j: (i, 0)), pl.BlockSpec((tm, d), lambda i, j: (i, 0))],
        out_shape=[jax.ShapeDtypeStruct((t, d), F32), jax.ShapeDtypeStruct((t, d), BF16)],
        scratch_shapes=[pltpu.VMEM((tm, d), F32)],
        compiler_params=_params(("arbitrary", "arbitrary")),
        name="mlp",
    )(x1, h2, w1.astype(BF16), w2.astype(BF16), mod_l, nnext[None, :], mod_next)


DIL_TQ = 128


def _dil_geometry(seq, dilation):
    length = seq // dilation
    kwin = min(2 * DIL_TQ, length)
    vwin = min(3 * DIL_TQ, length)
    return length, kwin, vwin, length // DIL_TQ


def _t5_bucket(rel):
    nb = N_BUCKETS // 2
    max_exact = nb // 2
    ret = jnp.where(rel > 0, nb, 0)
    n = jnp.abs(rel)
    nf = jnp.maximum(n, 1).astype(F32)
    large = max_exact + (jnp.log(nf / max_exact) / math.log(MAX_DISTANCE / max_exact)
                         * (nb - max_exact)).astype(jnp.int32)
    large = jnp.minimum(large, nb - 1)
    return ret + jnp.where(n < max_exact, n, large)


def _dil_bias_kernel(tab_ref, o_ref, *, group, dilation, kwin, n_var):
    var = pl.program_id(0)
    pair = pl.program_id(1)
    half = DIL_TQ // 2
    if n_var == 1:
        shift = 0
    else:
        shift = jnp.where(var == 0, 0, jnp.where(var == 1, -half, DIL_TQ - kwin))
    kk = lax.broadcasted_iota(jnp.int32, (kwin, DIL_TQ), 0)
    qi = lax.broadcasted_iota(jnp.int32, (kwin, DIL_TQ), 1)
    rel_a = kk + shift - qi
    valid = jnp.abs(rel_a) <= half
    bucket = _t5_bucket(rel_a * dilation)
    for side in range(2):
        col = group * DIL_HEADS + 2 * pair + side
        acc = jnp.zeros((kwin, DIL_TQ), F32)
        for b in range(N_BUCKETS):
            acc = jnp.where(bucket == b, tab_ref[b, col], acc)
        o_ref[0, 0, :, side * DIL_TQ:(side + 1) * DIL_TQ] = jnp.where(valid, acc * LOG2E, MASKED)


def _dil_bias(rel_bias, group, seq):
    _, dilation = DIL_GROUPS[group]
    _, kwin, _, n_tiles = _dil_geometry(seq, dilation)
    n_var = 1 if n_tiles == 1 else 3
    n_pairs = DIL_HEADS // 2
    kern = functools.partial(_dil_bias_kernel, group=group, dilation=dilation, kwin=kwin, n_var=n_var)
    return pl.pallas_call(
        kern,
        grid=(n_var, n_pairs),
        in_specs=[pl.BlockSpec(memory_space=pltpu.SMEM)],
        out_specs=pl.BlockSpec((1, 1, kwin, 2 * DIL_TQ), lambda v, p: (v, p, 0, 0)),
        out_shape=jax.ShapeDtypeStruct((n_var, n_pairs, kwin, 2 * DIL_TQ), F32),
        compiler_params=_params(("arbitrary", "arbitrary")),
        name=f"dil_bias{group}",
    )(rel_bias)


def _dil_proj_kernel(h_ref, wq_ref, wk_ref, wv_ref, gq_ref, gk_ref, qT_ref, k_ref, vT_ref, *, rb, d_model):
    ta = h_ref.shape[1]
    hd = DIL_HEAD_DIM
    hcat = jnp.concatenate([h_ref[0, :, j * d_model:(j + 1) * d_model] for j in range(rb)], axis=0)
    mt = rb * ta

    q_t = lax.dot_general(wq_ref[...], hcat, NT_DIMS, preferred_element_type=F32)
    gq = jnp.concatenate([gq_ref[...]] * (mt // LANES), axis=1)
    for h in range(DIL_HEADS):
        blk = q_t[h * hd:(h + 1) * hd]
        r = lax.rsqrt(jnp.sum(blk * blk, axis=0, keepdims=True) * (1.0 / hd) + EPS)
        qb = (blk * r * gq).astype(BF16)
        for j in range(rb):
            qT_ref[0, j, h * hd:(h + 1) * hd, :] = qb[:, j * ta:(j + 1) * ta]

    kf = jnp.dot(hcat, wk_ref[...], preferred_element_type=F32)
    lane = lax.broadcasted_iota(jnp.int32, (mt, LANES), 1)
    low = lane < hd
    for c in range(DIL_HEADS * hd // LANES):
        y = kf[:, c * LANES:(c + 1) * LANES]
        y2 = y * y
        s_lo = jnp.sum(jnp.where(low, y2, 0.0), axis=1, keepdims=True)
        s_hi = jnp.sum(jnp.where(low, 0.0, y2), axis=1, keepdims=True)
        r = lax.rsqrt(jnp.where(low, s_lo, s_hi) * (1.0 / hd) + EPS)
        kb = (y * r * gk_ref[...]).astype(BF16)
        for j in range(rb):
            k_ref[0, j, :, c * LANES:(c + 1) * LANES] = kb[j * ta:(j + 1) * ta]

    v_t = lax.dot_general(wv_ref[...], hcat, NT_DIMS, preferred_element_type=F32).astype(BF16)
    for j in range(rb):
        for s in range(ta // LANES):
            vT_ref[0, j, s] = v_t[:, j * ta + s * LANES:j * ta + (s + 1) * LANES]


def _dil_proj(hn, w_in_g, g_q, g_k, group, batch, seq):
    d_model = hn.shape[1]
    _, dilation = DIL_GROUPS[group]
    length = seq // dilation
    c = DIL_HEADS * DIL_HEAD_DIM
    ta = min(512, length)
    rb = min(dilation, 512 // ta)
    hview = hn.reshape(batch, length, dilation * d_model)
    wq_t = w_in_g[:, :c].T.astype(BF16)
    wk = w_in_g[:, c:2 * c].astype(BF16)
    wv_t = w_in_g[:, 2 * c:].T.astype(BF16)
    gq_col = jnp.broadcast_to((g_q * (DIL_HEAD_DIM ** -0.5 * LOG2E))[:, None], (DIL_HEAD_DIM, LANES))
    gk_row = jnp.tile(g_k, LANES // DIL_HEAD_DIM)[None, :]
    kern = functools.partial(_dil_proj_kernel, rb=rb, d_model=d_model)
    return pl.pallas_call(
        kern,
        grid=(batch, dilation // rb, length // ta),
        in_specs=[
            pl.BlockSpec((1, ta, rb * d_model), lambda b, r, t: (b, t, r)),
            pl.BlockSpec((c, d_model), lambda b, r, t: (0, 0)),
            pl.BlockSpec((d_model, c), lambda b, r, t: (0, 0)),
            pl.BlockSpec((c, d_model), lambda b, r, t: (0, 0)),
            pl.BlockSpec((DIL_HEAD_DIM, LANES), lambda b, r, t: (0, 0)),
            pl.BlockSpec((1, LANES), lambda b, r, t: (0, 0)),
        ],
        out_specs=[
            pl.BlockSpec((1, rb, c, ta), lambda b, r, t: (b, r, 0, t)),
            pl.BlockSpec((1, rb, ta, c), lambda b, r, t: (b, r, t, 0)),
            pl.BlockSpec((1, rb, ta // LANES, c, LANES), lambda b, r, t: (b, r, t, 0, 0)),
        ],
        out_shape=[
            jax.ShapeDtypeStruct((batch, dilation, c, length), BF16),
            jax.ShapeDtypeStruct((batch, dilation, length, c), BF16),
            jax.ShapeDtypeStruct((batch, dilation, length // LANES, c, LANES), BF16),
        ],
        compiler_params=_params(("arbitrary", "arbitrary", "arbitrary")),
        name=f"dil_proj{group}",
    )(hview, wq_t, wk, wv_t, gq_col, gk_row)


def _dil_attn_kernel(qT_ref, k_ref, vT_ref, bm_ref, o_ref, st_ref, *, length, kwin, vwin):
    t = pl.program_id(2)
    tq = DIL_TQ
    hd = DIL_HEAD_DIM
    n_pairs = DIL_HEADS // 2
    a0 = t * tq
    ks = pl.multiple_of(jnp.clip(a0 - tq // 2, 0, length - kwin), tq // 2)
    vblk = jnp.clip(t - 1, 0, (length - vwin) // LANES)
    off = ks - vblk * LANES

    row = lax.broadcasted_iota(jnp.int32, (LANES, tq), 0)
    stats = []
    for p in range(n_pairs):
        q2 = qT_ref[0, 0, p * LANES:(p + 1) * LANES, :]
        zero = jnp.zeros_like(q2)
        bd = jnp.concatenate([jnp.where(row < hd, q2, zero), jnp.where(row < hd, zero, q2)], axis=1)
        k2 = k_ref[0, 0, pl.ds(ks, kwin), p * LANES:(p + 1) * LANES]
        s_t = jnp.dot(k2, bd, preferred_element_type=F32) + bm_ref[0, p]
        m = jnp.max(s_t, axis=0, keepdims=True)
        pb = jnp.exp2(s_t - m).astype(BF16)
        if vwin == kwin:
            p_ext = pb
        else:
            pad = vwin - kwin
            z = jnp.zeros((pad, 2 * tq), BF16)
            zh = jnp.zeros((pad // 2, 2 * tq), BF16)
            p_ext = jnp.where(off == 0, jnp.concatenate([pb, z], axis=0),
                              jnp.where(off == pad, jnp.concatenate([z, pb], axis=0),
                                        jnp.concatenate([zh, pb, zh], axis=0)))
        v2 = jnp.concatenate(
            [vT_ref[0, 0, vblk + s, p * LANES:(p + 1) * LANES, :] for s in range(vwin // LANES)], axis=1)
        v_ext = jnp.concatenate([v2, jnp.ones((ONES_ROWS, vwin), BF16)], axis=0)
        res = jnp.dot(v_ext, p_ext, preferred_element_type=F32)
        den = res[LANES:LANES + 1]
        rinv = 1.0 / den
        o_pair = jnp.concatenate([res[:hd, :tq] * rinv[:, :tq], res[hd:LANES, tq:] * rinv[:, tq:]], axis=0)
        o_ref[0, :, p * LANES:(p + 1) * LANES] = o_pair.T.astype(BF16)
        lse = (m + jnp.log2(den)) * LN2
        stats += [lse[:, :tq], lse[:, tq:]]
    st = jnp.concatenate(stats + [jnp.zeros((LANES - DIL_HEADS, tq), F32)], axis=0)
    st_ref[0] = st.T


def _dil_attn(q_t, k, v_t, bias, group, batch, seq):
    _, dilation = DIL_GROUPS[group]
    length, kwin, vwin, n_tiles = _dil_geometry(seq, dilation)
    c = DIL_HEADS * DIL_HEAD_DIM
    n_var = bias.shape[0]
    if n_var == 1:
        var_of = lambda t: 0
    else:
        var_of = lambda t: jnp.where(t == 0, 0, jnp.where(t == n_tiles - 1, 2, 1))
    kern = functools.partial(_dil_attn_kernel, length=length, kwin=kwin, vwin=vwin)
    o, st = pl.pallas_call(
        kern,
        grid=(batch, dilation, n_tiles),
        in_specs=[
            pl.BlockSpec((1, 1, c, DIL_TQ), lambda b, r, t: (b, r, 0, t)),
            pl.BlockSpec((1, 1, length, c), lambda b, r, t: (b, r, 0, 0)),
            pl.BlockSpec((1, 1, length // LANES, c, LANES), lambda b, r, t: (b, r, 0, 0, 0)),
            pl.BlockSpec((1, DIL_HEADS // 2, kwin, 2 * DIL_TQ), lambda b, r, t: (var_of(t), 0, 0, 0)),
        ],
        out_specs=[
            pl.BlockSpec((1, DIL_TQ, c), lambda b, r, t: (b, t, r)),
            pl.BlockSpec((1, DIL_TQ, LANES), lambda b, r, t: (b, t, r)),
        ],
        out_shape=[
            jax.ShapeDtypeStruct((batch, length, dilation * c), BF16),
            jax.ShapeDtypeStruct((batch, length, dilation * LANES), F32),
        ],
        compiler_params=_params(("arbitrary", "arbitrary", "arbitrary")),
        name=f"dil_attn{group}",
    )(q_t, k, v_t, bias)
    return o.reshape(batch * seq, c), st.reshape(batch * seq, LANES)


def _dil_out_kernel(o0_ref, o1_ref, o2_ref, s0_ref, s1_ref, s2_ref, e_ref, wo_ref, x_ref, mod_ref, nmlp_ref,
                    x1_ref, h2_ref):
    l0, l1, l2 = s0_ref[...], s1_ref[...], s2_ref[...]
    mx = jnp.maximum(jnp.maximum(l0, l1), l2)
    e0, e1, e2 = jnp.exp(l0 - mx), jnp.exp(l1 - mx), jnp.exp(l2 - mx)
    rden = 1.0 / (e0 + e1 + e2)
    e = e_ref[...]

    def expand(w):
        hi = w.astype(BF16)
        lo = (w - hi.astype(F32)).astype(BF16)
        return jnp.dot(hi, e, preferred_element_type=F32) + jnp.dot(lo, e, preferred_element_type=F32)

    o = (expand(e0 * rden) * o0_ref[...].astype(F32) + expand(e1 * rden) * o1_ref[...].astype(F32)
         + expand(e2 * rden) * o2_ref[...].astype(F32))
    y = jnp.dot(o.astype(BF16), wo_ref[...], preferred_element_type=F32)
    m = mod_ref[0]
    x1 = x_ref[...] + m[2:3] * y
    x1_ref[...] = x1
    h2_ref[...] = _ada_norm(x1, nmlp_ref[...], m[4:5], m[3:4]).astype(BF16)


def _dil_out(outs, stats, w_o, x2d, mod_l, nmlp, seq):
    t, d = x2d.shape
    c = w_o.shape[0]
    tm = 512
    tpb = seq // tm
    expand = jnp.repeat(jnp.eye(DIL_HEADS, dtype=F32), DIL_HEAD_DIM, axis=1)
    expand = jnp.pad(expand, ((0, LANES - DIL_HEADS), (0, 0))).astype(BF16)
    row = lambda w: pl.BlockSpec((tm, w), lambda i: (i, 0))
    return pl.pallas_call(
        _dil_out_kernel,
        grid=(t // tm,),
        in_specs=[row(c), row(c), row(c), row(LANES), row(LANES), row(LANES),
                  pl.BlockSpec((LANES, c), lambda i: (0, 0)),
                  pl.BlockSpec((c, d), lambda i: (0, 0)),
                  row(d),
                  pl.BlockSpec((1, N_MOD, d), lambda i: (i // tpb, 0, 0)),
                  pl.BlockSpec((1, d), lambda i: (0, 0))],
        out_specs=[row(d), row(d)],
        out_shape=[jax.ShapeDtypeStruct((t, d), F32), jax.ShapeDtypeStruct((t, d), BF16)],
        compiler_params=_params(("arbitrary",)),
        name="dil_out",
    )(*outs, *stats, expand, w_o.astype(BF16), x2d, mod_l, nmlp[None, :])


def kernel(x, c, positions, ada_w, ada_b, norm_mix, norm_mlp, mlp_w1, mlp_w2, mla_w_in, mla_g_qa, mla_w_qb,
           mla_g_kva, mla_w_kvb, mla_g_q, mla_g_k, mla_w_o, dil_w_in, dil_g_q, dil_g_k, dil_w_o, rel_bias):
    batch, seq, d = x.shape
    depth = ada_w.shape[0]
    assert depth == 2 and len(DIL_GROUPS) == 3
    mod = _mod(c, ada_w, ada_b).reshape(depth, batch, N_MOD, d)
    x2d = x.reshape(batch * seq, d)

    n_heads = mla_w_qb.shape[2] // mla_g_q.shape[1]
    q_t, k_pad, v_t = _mla_proj(x2d, mod[0], norm_mix[0], positions, mla_w_in[0], mla_g_qa[0], mla_w_qb[0],
                                mla_g_kva[0], mla_w_kvb[0], mla_g_q[0], mla_g_k[0], batch, seq)
    o_t = _mla_attn(q_t, k_pad, v_t, n_heads)
    x1, h2 = _attn_out(o_t, mla_w_o[0], x2d, mod[0], norm_mlp[0], seq)
    x2, hn = _mlp(x1, h2, mlp_w1[0], mlp_w2[0], mod[0], norm_mix[1], mod[1], seq)

    c_grp = 3 * DIL_HEADS * DIL_HEAD_DIM
    outs, stats = [], []
    for g in range(len(DIL_GROUPS)):
        w_g = dil_w_in[0][:, g * c_grp:(g + 1) * c_grp]
        q_g, k_g, v_g = _dil_proj(hn, w_g, dil_g_q[0, g], dil_g_k[0, g], g, batch, seq)
        o_g, st_g = _dil_attn(q_g, k_g, v_g, _dil_bias(rel_bias, g, seq), g, batch, seq)
        outs.append(o_g)
        stats.append(st_g)
    x3, h4 = _dil_out(outs, stats, dil_w_o[0], x2, mod[1], norm_mlp[1], seq)
    x4, _ = _mlp(x3, h4, mlp_w1[1], mlp_w2[1], mod[1], norm_mix[1], mod[1], seq)
    return x4.reshape(batch, seq, d)
```

```python
import functools
import math

import jax
import jax.numpy as jnp
from jax import lax
from jax.experimental import pallas as pl
from jax.experimental.pallas import tpu as pltpu

F32 = jnp.float32
BF16 = jnp.bfloat16

EPS = 1e-6
LOG2E = 1.4426950408889634
LN2 = 0.6931471805599453
MASKED = -1e30

N_MOD = 6
ROPE_THETA = 10000.0
MLA_ROPE = 32
DIL_GROUPS = ((128, 1), (512, 4), (2048, 16))
DIL_HEADS = 16
DIL_HEAD_DIM = 64
N_BUCKETS = 32
MAX_DISTANCE = 1024

LANES = 128
HEAD_PAD = 128

VMEM_LIMIT = 56 * 1024 * 1024

NT_DIMS = (((1,), (1,)), ((), ()))
TN_DIMS = (((0,), (0,)), ((), ()))


def _params(sem):
    return pltpu.CompilerParams(dimension_semantics=sem, vmem_limit_bytes=VMEM_LIMIT)


def _ada_norm(x, g, scale, shift):
    ms = jnp.mean(x * x, axis=-1, keepdims=True)
    return (x * lax.rsqrt(ms + EPS) * g) * (1.0 + scale) + shift


def _rms(x, g):
    ms = jnp.mean(x * x, axis=-1, keepdims=True)
    return x * lax.rsqrt(ms + EPS) * g


def _mod_kernel(c_ref, w_ref, b_ref, o_ref):
    c = c_ref[...]
    cond = c / (1.0 + jnp.exp(-c))
    o_ref[0] = (
        jnp.dot(cond.astype(BF16), w_ref[0].astype(BF16), preferred_element_type=F32) + b_ref[0]
    )


def _mod(c, ada_w, ada_b):
    depth, d, n = ada_w.shape
    b = c.shape[0]
    tn = 1536
    return pl.pallas_call(
        _mod_kernel,
        grid=(depth, n // tn),
        in_specs=[
            pl.BlockSpec((b, d), lambda l, j: (0, 0)),
            pl.BlockSpec((1, d, tn), lambda l, j: (l, 0, j)),
            pl.BlockSpec((1, 1, tn), lambda l, j: (l, 0, j)),
        ],
        out_specs=pl.BlockSpec((1, b, tn), lambda l, j: (l, 0, j)),
        out_shape=jax.ShapeDtypeStruct((depth, b, n), F32),
        compiler_params=_params(("arbitrary", "arbitrary")),
        name="mod",
    )(c, ada_w, ada_b.reshape(depth, 1, n))


def _mla_proj_kernel(x_ref, mod_ref, nmix_ref, pos_ref, inv_ref, wlat_ref, wkr_ref, gqa_ref, gkva_ref,
                     wqb_ref, wkvb_ref, wv_ref, gq_ref, gk_ref, qT_ref, k_ref, vT_ref, *, n_heads, q_lora, qk_dim):
    x = x_ref[...]
    tm = x.shape[0]
    m = mod_ref[0]
    hb = _ada_norm(x, nmix_ref[...], m[1:2], m[0:1]).astype(BF16)
    lat = jnp.dot(hb, wlat_ref[...], preferred_element_type=F32)
    qn = _rms(lat[:, :q_lora], gqa_ref[...]).astype(BF16)
    kvn = _rms(lat[:, q_lora:], gkva_ref[...]).astype(BF16)

    half = MLA_ROPE // 2
    nope = qk_dim - MLA_ROPE
    pos = pos_ref[...].astype(F32)
    ang = jnp.concatenate([inv_ref[...]] * (tm // LANES), axis=1) * pos
    cos_t = jnp.cos(ang)
    sin_t = jnp.sin(ang)

    def rope_rows(blk):
        x1 = blk[nope:nope + half]
        x2 = blk[nope + half:qk_dim]
        return jnp.concatenate(
            [blk[:nope], x1 * cos_t - x2 * sin_t, x2 * cos_t + x1 * sin_t, blk[qk_dim:]], axis=0)

    kr_t = lax.dot_general(wkr_ref[...], hb, NT_DIMS, preferred_element_type=F32)
    kr = rope_rows(kr_t).T

    q_t = lax.dot_general(wqb_ref[...], qn, NT_DIMS, preferred_element_type=F32)
    gq = jnp.concatenate([gq_ref[...]] * (tm // LANES), axis=1)
    for h in range(n_heads):
        blk = rope_rows(q_t[h * HEAD_PAD:(h + 1) * HEAD_PAD])
        ssq = jnp.sum(blk * blk, axis=0, keepdims=True)
        r = lax.rsqrt(ssq * (1.0 / qk_dim) + EPS)
        qT_ref[0, h * HEAD_PAD:(h + 1) * HEAD_PAD, :] = (blk * r * gq).astype(BF16)

    kv = jnp.dot(kvn, wkvb_ref[...], preferred_element_type=F32)
    lane = lax.broadcasted_iota(jnp.int32, (tm, HEAD_PAD), 1)
    gk = gk_ref[...]
    for h in range(n_heads):
        kb = jnp.where(lane < nope, kv[:, h * HEAD_PAD:(h + 1) * HEAD_PAD], 0.0) + kr
        ssq = jnp.sum(kb * kb, axis=1, keepdims=True)
        r = lax.rsqrt(ssq * (1.0 / qk_dim) + EPS)
        k_ref[:, h * HEAD_PAD:(h + 1) * HEAD_PAD] = (kb * r * gk).astype(BF16)

    vT_ref[0] = lax.dot_general(wv_ref[...], kvn, NT_DIMS, preferred_element_type=F32).astype(BF16)


def _mla_proj(x2d, mod_l, nmix, positions, w_in, g_qa, w_qb, g_kva, w_kvb, g_q, g_k, batch, seq):
    t, d = x2d.shape
    q_lora = g_qa.shape[0]
    kv_lora = g_kva.shape[0]
    qk_dim = g_q.shape[0]
    n_heads = w_qb.shape[1] // qk_dim
    nope = qk_dim - MLA_ROPE
    v_dim = w_kvb.shape[1] // n_heads - nope
    assert nope + v_dim == HEAD_PAD and qk_dim <= HEAD_PAD
    half = MLA_ROPE // 2
    tm = 512
    tpb = seq // tm

    w_lat = w_in[:, :q_lora + kv_lora].astype(BF16)
    w_kr = jnp.zeros((HEAD_PAD, d), F32).at[nope:qk_dim].set(w_in[:, q_lora + kv_lora:].T).astype(BF16)
    w_qb_t = jnp.pad(w_qb.T.reshape(n_heads, qk_dim, q_lora), ((0, 0), (0, HEAD_PAD - qk_dim), (0, 0)))
    w_qb_t = w_qb_t.reshape(n_heads * HEAD_PAD, q_lora).astype(BF16)
    w_kvb_b = w_kvb.astype(BF16)
    w_v_t = w_kvb.reshape(kv_lora, n_heads, nope + v_dim)[:, :, nope:].reshape(kv_lora, n_heads * v_dim).T.astype(BF16)
    q_gain = jnp.pad(g_q * (qk_dim ** -0.5 * LOG2E), (0, HEAD_PAD - qk_dim))
    gq_col = jnp.broadcast_to(q_gain[:, None], (HEAD_PAD, LANES))
    gk_row = jnp.pad(g_k, (0, HEAD_PAD - qk_dim))[None, :]
    inv = 1.0 / (ROPE_THETA ** (jnp.arange(half, dtype=F32) / half))
    inv_tab = jnp.broadcast_to(inv[:, None], (half, LANES))
    pos_row = positions.reshape(1, t)

    full = lambda shape: pl.BlockSpec(shape, lambda i: (0,) * len(shape))
    kern = functools.partial(_mla_proj_kernel, n_heads=n_heads, q_lora=q_lora, qk_dim=qk_dim)
    return pl.pallas_call(
        kern,
        grid=(t // tm,),
        in_specs=[
            pl.BlockSpec((tm, d), lambda i: (i, 0)),
            pl.BlockSpec((1, N_MOD, d), lambda i: (i // tpb, 0, 0)),
            full((1, d)),
            pl.BlockSpec((1, tm), lambda i: (0, i)),
            full((half, LANES)),
            full(w_lat.shape), full(w_kr.shape), full((1, q_lora)), full((1, kv_lora)),
            full(w_qb_t.shape), full(w_kvb_b.shape), full(w_v_t.shape),
            full((HEAD_PAD, LANES)), full((1, HEAD_PAD)),
        ],
        out_specs=[
            pl.BlockSpec((1, n_heads * HEAD_PAD, tm), lambda i: (i // tpb, 0, i % tpb)),
            pl.BlockSpec((tm, n_heads * HEAD_PAD), lambda i: (i, 0)),
            pl.BlockSpec((1, n_heads * v_dim, tm), lambda i: (i // tpb, 0, i % tpb)),
        ],
        out_shape=[
            jax.ShapeDtypeStruct((batch, n_heads * HEAD_PAD, seq), BF16),
            jax.ShapeDtypeStruct((t, n_heads * HEAD_PAD), BF16),
            jax.ShapeDtypeStruct((batch, n_heads * v_dim, seq), BF16),
        ],
        compiler_params=_params(("arbitrary",)),
        name="mla_proj",
    )(x2d, mod_l, nmix[None, :], pos_row, inv_tab, w_lat, w_kr, g_qa[None, :], g_kva[None, :],
      w_qb_t, w_kvb_b, w_v_t, gq_col, gk_row)


ONES_ROWS = 16


def _mla_attn_kernel(qT_ref, k_ref, vT_ref, oT_ref, s_ref, *, tq, tn, kc):
    k = k_ref[0]
    seq = k.shape[0]
    v_dim = vT_ref.shape[1]
    v_ext = jnp.concatenate([vT_ref[0], jnp.ones((ONES_ROWS, seq), BF16)], axis=0)
    n_tiles = seq // tq

    def scores(i):
        s_ref[i % 2] = jnp.dot(k, qT_ref[0, :, i * tq:(i + 1) * tq], preferred_element_type=F32)

    scores(0)
    for i in range(n_tiles):
        if i + 1 < n_tiles:
            scores(i + 1)
        for j in range(tq // tn):
            m = jnp.full((1, tn), MASKED, F32)
            acc = jnp.zeros((v_dim + ONES_ROWS, tn), F32)
            for c in range(seq // kc):
                s_c = s_ref[i % 2, c * kc:(c + 1) * kc, j * tn:(j + 1) * tn]
                m_new = jnp.maximum(m, jnp.max(s_c, axis=0, keepdims=True))
                p = jnp.exp2(s_c - m_new).astype(BF16)
                acc = acc * jnp.exp2(m - m_new) + jnp.dot(v_ext[:, c * kc:(c + 1) * kc], p,
                                                          preferred_element_type=F32)
                m = m_new
            o = acc[:v_dim] * (1.0 / acc[v_dim:v_dim + 1])
            oT_ref[0, :, i * tq + j * tn:i * tq + (j + 1) * tn] = o.astype(BF16)


def _mla_attn(q_t, k_pad, v_t, n_heads):
    batch, _, seq = q_t.shape
    v_dim = v_t.shape[1] // n_heads
    k3 = k_pad.reshape(batch, seq, n_heads * HEAD_PAD)
    tq = 512
    return pl.pallas_call(
        functools.partial(_mla_attn_kernel, tq=tq, tn=256, kc=256),
        grid=(batch, n_heads),
        in_specs=[
            pl.BlockSpec((1, HEAD_PAD, seq), lambda b, h: (b, h, 0)),
            pl.BlockSpec((1, seq, HEAD_PAD), lambda b, h: (b, 0, h)),
            pl.BlockSpec((1, v_dim, seq), lambda b, h: (b, h, 0)),
        ],
        out_specs=pl.BlockSpec((1, v_dim, seq), lambda b, h: (b, h, 0)),
        out_shape=jax.ShapeDtypeStruct((batch, n_heads * v_dim, seq), BF16),
        scratch_shapes=[pltpu.VMEM((2, seq, tq), F32)],
        compiler_params=_params(("arbitrary", "arbitrary")),
        name="mla_attn",
    )(q_t, k3, v_t)


def _attn_out_kernel(oT_ref, wo_ref, x_ref, mod_ref, nmlp_ref, x1_ref, h2_ref):
    y = lax.dot_general(oT_ref[0], wo_ref[...], TN_DIMS, preferred_element_type=F32)
    m = mod_ref[0]
    x1 = x_ref[...] + m[2:3] * y
    x1_ref[...] = x1
    h2_ref[...] = _ada_norm(x1, nmlp_ref[...], m[4:5], m[3:4]).astype(BF16)


def _attn_out(o_t, w_o, x2d, mod_l, nmlp, seq):
    t, d = x2d.shape
    tm = 512
    tpb = seq // tm
    c = o_t.shape[1]
    return pl.pallas_call(
        _attn_out_kernel,
        grid=(t // tm,),
        in_specs=[
            pl.BlockSpec((1, c, tm), lambda i: (i // tpb, 0, i % tpb)),
            pl.BlockSpec((c, d), lambda i: (0, 0)),
            pl.BlockSpec((tm, d), lambda i: (i, 0)),
            pl.BlockSpec((1, N_MOD, d), lambda i: (i // tpb, 0, 0)),
            pl.BlockSpec((1, d), lambda i: (0, 0)),
        ],
        out_specs=[pl.BlockSpec((tm, d), lambda i: (i, 0)), pl.BlockSpec((tm, d), lambda i: (i, 0))],
        out_shape=[jax.ShapeDtypeStruct((t, d), F32), jax.ShapeDtypeStruct((t, d), BF16)],
        compiler_params=_params(("arbitrary",)),
        name="attn_out",
    )(o_t, w_o.astype(BF16), x2d, mod_l, nmlp[None, :])


def _mlp_kernel(x_ref, h_ref, w1_ref, w2_ref, mod_ref, nnext_ref, modn_ref, x2_ref, hn_ref, acc_ref):
    j = pl.program_id(1)

    @pl.when(j == 0)
    def _():
        acc_ref[...] = jnp.zeros_like(acc_ref)

    u = jnp.dot(h_ref[...], w1_ref[...], preferred_element_type=F32)
    u = jnp.square(jnp.maximum(u, 0.0)).astype(BF16)
    acc_ref[...] += jnp.dot(u, w2_ref[...], preferred_element_type=F32)

    @pl.when(j == pl.num_programs(1) - 1)
    def _():
        x2 = x_ref[...] + mod_ref[0][5:6] * acc_ref[...]
        x2_ref[...] = x2
        mn = modn_ref[0]
        hn_ref[...] = _ada_norm(x2, nnext_ref[...], mn[1:2], mn[0:1]).astype(BF16)


def _mlp(x1, h2, w1, w2, mod_l, nnext, mod_next, seq):
    t, d = x1.shape
    ff = w1.shape[1]
    tm, tf = 1024, 1024
    tpb = seq // tm
    return pl.pallas_call(
        _mlp_kernel,
        grid=(t // tm, ff // tf),
        in_specs=[
            pl.BlockSpec((tm, d), lambda i, j: (i, 0)),
            pl.BlockSpec((tm, d), lambda i, j: (i, 0)),
            pl.BlockSpec((d, tf), lambda i, j: (0, j)),
            pl.BlockSpec((tf, d), lambda i, j: (j, 0)),
            pl.BlockSpec((1, N_MOD, d), lambda i, j: (i // tpb, 0, 0)),
            pl.BlockSpec((1, d), lambda i, j: (0, 0)),
            pl.BlockSpec((1, N_MOD, d), lambda i, j: (i // tpb, 0, 0)),
        ],
        out_specs=[pl.BlockSpec((tm, d), lambda i, j: (i, 0)), pl.BlockSpec((tm, d), lambda i, j: (i, 0))],
        out_shape=[jax.ShapeDtypeStruct((t, d), F32), jax.ShapeDtypeStruct((t, d), BF16)],
        scratch_shapes=[pltpu.VMEM((tm, d), F32)],
        compiler_params=_params(("arbitrary", "arbitrary")),
        name="mlp",
    )(x1, h2, w1.astype(BF16), w2.astype(BF16), mod_l, nnext[None, :], mod_next)


DIL_TQ = 128
DIL_HALF = 64
assert all(w // (2 * d) == DIL_HALF for w, d in DIL_GROUPS)


def _dil_geometry(seq, dilation):
    length = seq // dilation
    kwin = min(2 * DIL_TQ, length)
    return length, kwin, length // DIL_TQ


def _t5_bucket(rel):
    nb = N_BUCKETS // 2
    max_exact = nb // 2
    ret = jnp.where(rel > 0, nb, 0)
    n = jnp.abs(rel)
    nf = jnp.maximum(n, 1).astype(F32)
    large = max_exact + (jnp.log(nf / max_exact) / math.log(MAX_DISTANCE / max_exact)
                         * (nb - max_exact)).astype(jnp.int32)
    large = jnp.minimum(large, nb - 1)
    return ret + jnp.where(n < max_exact, n, large)


def _dil_bias_kernel(tab_ref, o_ref, *, group, dilation, kwin, n_var):
    var = pl.program_id(0)
    pair = pl.program_id(1)
    kk = lax.broadcasted_iota(jnp.int32, (kwin, DIL_TQ), 0)
    qi = lax.broadcasted_iota(jnp.int32, (kwin, DIL_TQ), 1)
    rel_a = kk - qi - (DIL_HALF if n_var > 1 else 0)
    valid = jnp.abs(rel_a) <= DIL_HALF
    if n_var > 1:
        crossing = jnp.where((qi < DIL_HALF) == (kk < DIL_TQ), 0, var)
        valid = jnp.where(valid, 1, 0) - crossing == 1
    bucket = _t5_bucket(rel_a * dilation)
    for side in range(2):
        col = group * DIL_HEADS + 2 * pair + side
        acc = jnp.zeros((kwin, DIL_TQ), F32)
        for b in range(N_BUCKETS):
            acc = jnp.where(bucket == b, tab_ref[b, col], acc)
        o_ref[0, 0, :, side * DIL_TQ:(side + 1) * DIL_TQ] = jnp.where(valid, acc * LOG2E, MASKED)


def _dil_bias(rel_bias, group, seq):
    _, dilation = DIL_GROUPS[group]
    _, kwin, n_tiles = _dil_geometry(seq, dilation)
    n_var = 1 if n_tiles == 1 else 2
    n_pairs = DIL_HEADS // 2
    kern = functools.partial(_dil_bias_kernel, group=group, dilation=dilation, kwin=kwin, n_var=n_var)
    return pl.pallas_call(
        kern,
        grid=(n_var, n_pairs),
        in_specs=[pl.BlockSpec(memory_space=pltpu.SMEM)],
        out_specs=pl.BlockSpec((1, 1, kwin, 2 * DIL_TQ), lambda v, p: (v, p, 0, 0)),
        out_shape=jax.ShapeDtypeStruct((n_var, n_pairs, kwin, 2 * DIL_TQ), F32),
        compiler_params=_params(("arbitrary", "arbitrary")),
        name=f"dil_bias{group}",
    )(rel_bias)


DIL_PROJ_ROWS = 512


def _dil_proj_kernel(h_ref, wq_ref, wk_ref, wv_ref, gq_ref, gk_ref, qT_ref, k_ref, vT_ref, *scratch,
                     dilation, rb, ra):
    hd = DIL_HEAD_DIM
    d_model = h_ref.shape[2]
    n_slabs = d_model // LANES
    mt = rb * ra
    if dilation == 1:
        hcat = h_ref[0]
    else:
        (scr,) = scratch
        sub = pl.program_id(2)

        @pl.when(sub == 0)
        def _():
            for s in range(n_slabs):
                scr[s] = h_ref[0, :, s * LANES:(s + 1) * LANES].astype(F32)

        rows = []
        for j in range(rb):
            res = sub * rb + j
            rows.append(jnp.concatenate(
                [scr[s, pl.ds(res, ra, stride=dilation), :] for s in range(n_slabs)], axis=1))
        hcat = jnp.concatenate(rows, axis=0).astype(BF16)

    q_t = lax.dot_general(wq_ref[...], hcat, NT_DIMS, preferred_element_type=F32)
    gq = jnp.concatenate([gq_ref[...]] * (mt // LANES), axis=1)
    for h in range(DIL_HEADS):
        blk = q_t[h * hd:(h + 1) * hd]
        r = lax.rsqrt(jnp.sum(blk * blk, axis=0, keepdims=True) * (1.0 / hd) + EPS)
        qb = (blk * r * gq).astype(BF16)
        for j in range(rb):
            for s in range(ra // LANES):
                lo = j * ra + s * LANES
                qT_ref[0, j, s, h * hd:(h + 1) * hd, :] = qb[:, lo:lo + LANES]

    kf = jnp.dot(hcat, wk_ref[...], preferred_element_type=F32)
    lane = lax.broadcasted_iota(jnp.int32, (mt, LANES), 1)
    low = lane < hd
    for c in range(DIL_HEADS * hd // LANES):
        y = kf[:, c * LANES:(c + 1) * LANES]
        y2 = y * y
        s_lo = jnp.sum(jnp.where(low, y2, 0.0), axis=1, keepdims=True)
        s_hi = jnp.sum(jnp.where(low, 0.0, y2), axis=1, keepdims=True)
        r = lax.rsqrt(jnp.where(low, s_lo, s_hi) * (1.0 / hd) + EPS)
        kb = (y * r * gk_ref[...]).astype(BF16)
        for j in range(rb):
            k_ref[0, j, :, c * LANES:(c + 1) * LANES] = kb[j * ra:(j + 1) * ra]

    v_t = lax.dot_general(wv_ref[...], hcat, NT_DIMS, preferred_element_type=F32).astype(BF16)
    for j in range(rb):
        for s in range(ra // LANES):
            lo = j * ra + s * LANES
            vT_ref[0, j, s] = v_t[:, lo:lo + LANES]


def _dil_proj(hn3, w_in_g, g_q, g_k, group):
    batch, seq, d_model = hn3.shape
    _, dilation = DIL_GROUPS[group]
    length = seq // dilation
    c = DIL_HEADS * DIL_HEAD_DIM
    ra = DIL_PROJ_ROWS if dilation == 1 else DIL_TQ
    rb = DIL_PROJ_ROWS // ra
    tok = ra * dilation
    n_sub = dilation // rb
    wq_t = w_in_g[:, :c].T.astype(BF16)
    wk = w_in_g[:, c:2 * c].astype(BF16)
    wv_t = w_in_g[:, 2 * c:].T.astype(BF16)
    gq_col = jnp.broadcast_to((g_q * (DIL_HEAD_DIM ** -0.5 * LOG2E))[:, None], (DIL_HEAD_DIM, LANES))
    gk_row = jnp.tile(g_k, LANES // DIL_HEAD_DIM)[None, :]
    kern = functools.partial(_dil_proj_kernel, dilation=dilation, rb=rb, ra=ra)
    const = lambda shape: pl.BlockSpec(shape, lambda b, t, s: (0,) * len(shape))
    scratch = [] if dilation == 1 else [pltpu.VMEM((d_model // LANES, tok, LANES), F32)]
    return pl.pallas_call(
        kern,
        grid=(batch, seq // tok, n_sub),
        in_specs=[
            pl.BlockSpec((1, tok, d_model), lambda b, t, s: (b, t, 0)),
            const((c, d_model)), const((d_model, c)), const((c, d_model)),
            const((DIL_HEAD_DIM, LANES)), const((1, LANES)),
        ],
        out_specs=[
            pl.BlockSpec((1, rb, ra // LANES, c, LANES), lambda b, t, s: (b, s, t, 0, 0)),
            pl.BlockSpec((1, rb, ra, c), lambda b, t, s: (b, s, t, 0)),
            pl.BlockSpec((1, rb, ra // LANES, c, LANES), lambda b, t, s: (b, s, t, 0, 0)),
        ],
        out_shape=[
            jax.ShapeDtypeStruct((batch, dilation, length // LANES, c, LANES), BF16),
            jax.ShapeDtypeStruct((batch, dilation, length, c), BF16),
            jax.ShapeDtypeStruct((batch, dilation, length // LANES, c, LANES), BF16),
        ],
        scratch_shapes=scratch,
        compiler_params=_params(("arbitrary", "arbitrary", "arbitrary")),
        name=f"dil_proj{group}",
    )(hn3, wq_t, wk, wv_t, gq_col, gk_row)


def _dil_attn_kernel(qT_ref, k_ref, vT_ref, bm_ref, o_ref, st_ref, s_ref, *, n_tiles, rb):
    tq = DIL_TQ
    hd = DIL_HEAD_DIM
    n_pairs = DIL_HEADS // 2
    c = DIL_HEADS * hd
    shifted = n_tiles > 1
    low = lax.broadcasted_iota(jnp.int32, (LANES, tq), 0) < hd
    ones = jnp.ones((ONES_ROWS, 2 * tq if shifted else tq), BF16)
    tiles = [(j, t) for j in range(rb) for t in range(n_tiles)]

    def score_tile(i):
        j, t = tiles[i]
        t1 = (t + 1) % n_tiles
        for p in range(n_pairs):
            rows = slice(p * LANES, (p + 1) * LANES)
            if shifted:
                q2 = jnp.concatenate([qT_ref[0, j, t, rows, DIL_HALF:], qT_ref[0, j, t1, rows, :DIL_HALF]], axis=1)
                k2 = jnp.concatenate([k_ref[0, j, t * tq:(t + 1) * tq, rows],
                                      k_ref[0, j, t1 * tq:(t1 + 1) * tq, rows]], axis=0)
                bias = bm_ref[1 if t + 1 == n_tiles else 0, p]
            else:
                q2 = qT_ref[0, j, 0, rows, :]
                k2 = k_ref[0, j, :, rows]
                bias = bm_ref[0, p]
            zero = jnp.zeros_like(q2)
            bd = jnp.concatenate([jnp.where(low, q2, zero), jnp.where(low, zero, q2)], axis=1)
            s_ref[i % 2, p] = jnp.dot(k2, bd, preferred_element_type=F32) + bias

    def attend_tile(i):
        j, t = tiles[i]
        t1 = (t + 1) % n_tiles

        def store(ref, lanes, val):
            if shifted:
                ref[0, t * tq + DIL_HALF:(t + 1) * tq, lanes] = val[:DIL_HALF]
                ref[0, t1 * tq:t1 * tq + DIL_HALF, lanes] = val[DIL_HALF:]
            else:
                ref[0, :, lanes] = val

        stats = []
        for p in range(n_pairs):
            rows = slice(p * LANES, (p + 1) * LANES)
            s_t = s_ref[i % 2, p]
            m = jnp.max(s_t, axis=0, keepdims=True)
            pb = jnp.exp2(s_t - m).astype(BF16)
            if shifted:
                v2 = jnp.concatenate([vT_ref[0, j, t, rows, :], vT_ref[0, j, t1, rows, :]], axis=1)
            else:
                v2 = vT_ref[0, j, 0, rows, :]
            res = jnp.dot(jnp.concatenate([v2, ones], axis=0), pb, preferred_element_type=F32)
            den = res[LANES:LANES + 1]
            rinv = 1.0 / den
            o_pair = jnp.concatenate([res[:hd, :tq] * rinv[:, :tq], res[hd:LANES, tq:] * rinv[:, tq:]], axis=0)
            store(o_ref, slice(j * c + p * LANES, j * c + (p + 1) * LANES), o_pair.T.astype(BF16))
            lse = (m + jnp.log2(den)) * LN2
            stats += [lse[:, :tq], lse[:, tq:]]
        st = jnp.concatenate(stats + [jnp.zeros((LANES - DIL_HEADS, tq), F32)], axis=0)
        store(st_ref, slice(j * LANES, (j + 1) * LANES), st.T)

    score_tile(0)
    for i in range(len(tiles)):
        if i + 1 < len(tiles):
            score_tile(i + 1)
        attend_tile(i)


def _dil_attn(q_t, k, v_t, bias, group, seq):
    batch = q_t.shape[0]
    _, dilation = DIL_GROUPS[group]
    length, kwin, n_tiles = _dil_geometry(seq, dilation)
    c = DIL_HEADS * DIL_HEAD_DIM
    n_slabs = length // LANES
    rb = 1 if n_tiles > 1 else min(dilation, 4)
    kern = functools.partial(_dil_attn_kernel, n_tiles=n_tiles, rb=rb)
    return pl.pallas_call(
        kern,
        grid=(batch, dilation // rb),
        in_specs=[
            pl.BlockSpec((1, rb, n_slabs, c, LANES), lambda b, r: (b, r, 0, 0, 0)),
            pl.BlockSpec((1, rb, length, c), lambda b, r: (b, r, 0, 0)),
            pl.BlockSpec((1, rb, n_slabs, c, LANES), lambda b, r: (b, r, 0, 0, 0)),
            pl.BlockSpec(bias.shape, lambda b, r: (0, 0, 0, 0)),
        ],
        out_specs=[
            pl.BlockSpec((1, length, rb * c), lambda b, r: (b, 0, r)),
            pl.BlockSpec((1, length, rb * LANES), lambda b, r: (b, 0, r)),
        ],
        out_shape=[
            jax.ShapeDtypeStruct((batch, length, dilation * c), BF16),
            jax.ShapeDtypeStruct((batch, length, dilation * LANES), F32),
        ],
        scratch_shapes=[pltpu.VMEM((2, DIL_HEADS // 2, kwin, 2 * DIL_TQ), F32)],
        compiler_params=_params(("arbitrary", "arbitrary")),
        name=f"dil_attn{group}",
    )(q_t, k, v_t, bias)


def _dil_out_kernel(o0_ref, o1_ref, o2_ref, s0_ref, s1_ref, s2_ref, e_ref, wo_ref, x_ref, mod_ref, nmlp_ref,
                    x1_ref, h2_ref, *scratch):
    tm = x_ref.shape[0]
    c = wo_ref.shape[0]

    def token_major(o_ref, s_ref, oscr, sscr, dilation):
        ra = tm // dilation
        for r in range(dilation):
            for s in range(c // LANES):
                lo = r * c + s * LANES
                oscr[s, pl.ds(r, ra, stride=dilation), :] = o_ref[0, :, lo:lo + LANES].astype(F32)
            sscr[pl.ds(r, ra, stride=dilation), :] = s_ref[0, :, r * LANES:(r + 1) * LANES]
        return jnp.concatenate([oscr[s] for s in range(c // LANES)], axis=1), sscr[...]

    o0, l0 = o0_ref[0].astype(F32), s0_ref[0]
    o1, l1 = token_major(o1_ref, s1_ref, scratch[0], scratch[1], DIL_GROUPS[1][1])
    o2, l2 = token_major(o2_ref, s2_ref, scratch[2], scratch[3], DIL_GROUPS[2][1])
    mx = jnp.maximum(jnp.maximum(l0, l1), l2)
    e0, e1, e2 = jnp.exp(l0 - mx), jnp.exp(l1 - mx), jnp.exp(l2 - mx)
    rden = 1.0 / (e0 + e1 + e2)
    e = e_ref[...]

    def expand(w):
        hi = w.astype(BF16)
        lo = (w - hi.astype(F32)).astype(BF16)
        return jnp.dot(hi, e, preferred_element_type=F32) + jnp.dot(lo, e, preferred_element_type=F32)

    o = expand(e0 * rden) * o0 + expand(e1 * rden) * o1 + expand(e2 * rden) * o2
    y = jnp.dot(o.astype(BF16), wo_ref[...], preferred_element_type=F32)
    m = mod_ref[0]
    x1 = x_ref[...] + m[2:3] * y
    x1_ref[...] = x1
    h2_ref[...] = _ada_norm(x1, nmlp_ref[...], m[4:5], m[3:4]).astype(BF16)


def _dil_out(outs, stats, w_o, x2d, mod_l, nmlp, seq):
    t, d = x2d.shape
    c = w_o.shape[0]
    tm = 512
    tpb = seq // tm
    expand = jnp.repeat(jnp.eye(DIL_HEADS, dtype=F32), DIL_HEAD_DIM, axis=1)
    expand = jnp.pad(expand, ((0, LANES - DIL_HEADS), (0, 0))).astype(BF16)
    row = lambda w: pl.BlockSpec((tm, w), lambda i: (i, 0))
    grp = lambda g, w: pl.BlockSpec((1, tm // DIL_GROUPS[g][1], DIL_GROUPS[g][1] * w),
                                    lambda i: (i // tpb, i % tpb, 0))
    scratch = []
    for g in (1, 2):
        scratch += [pltpu.VMEM((c // LANES, tm, LANES), F32), pltpu.VMEM((tm, LANES), F32)]
    return pl.pallas_call(
        _dil_out_kernel,
        grid=(t // tm,),
        in_specs=[grp(0, c), grp(1, c), grp(2, c), grp(0, LANES), grp(1, LANES), grp(2, LANES),
                  pl.BlockSpec((LANES, c), lambda i: (0, 0)),
                  pl.BlockSpec((c, d), lambda i: (0, 0)),
                  row(d),
                  pl.BlockSpec((1, N_MOD, d), lambda i: (i // tpb, 0, 0)),
                  pl.BlockSpec((1, d), lambda i: (0, 0))],
        out_specs=[row(d), row(d)],
        out_shape=[jax.ShapeDtypeStruct((t, d), F32), jax.ShapeDtypeStruct((t, d), BF16)],
        scratch_shapes=scratch,
        compiler_params=_params(("arbitrary",)),
        name="dil_out",
    )(*outs, *stats, expand, w_o.astype(BF16), x2d, mod_l, nmlp[None, :])


def kernel(x, c, positions, ada_w, ada_b, norm_mix, norm_mlp, mlp_w1, mlp_w2, mla_w_in, mla_g_qa, mla_w_qb,
           mla_g_kva, mla_w_kvb, mla_g_q, mla_g_k, mla_w_o, dil_w_in, dil_g_q, dil_g_k, dil_w_o, rel_bias):
    batch, seq, d = x.shape
    depth = ada_w.shape[0]
    assert depth == 2 and len(DIL_GROUPS) == 3
    mod = _mod(c, ada_w, ada_b).reshape(depth, batch, N_MOD, d)
    x2d = x.reshape(batch * seq, d)

    n_heads = mla_w_qb.shape[2] // mla_g_q.shape[1]
    q_t, k_pad, v_t = _mla_proj(x2d, mod[0], norm_mix[0], positions, mla_w_in[0], mla_g_qa[0], mla_w_qb[0],
                                mla_g_kva[0], mla_w_kvb[0], mla_g_q[0], mla_g_k[0], batch, seq)
    o_t = _mla_attn(q_t, k_pad, v_t, n_heads)
    x1, h2 = _attn_out(o_t, mla_w_o[0], x2d, mod[0], norm_mlp[0], seq)
    x2, hn = _mlp(x1, h2, mlp_w1[0], mlp_w2[0], mod[0], norm_mix[1], mod[1], seq)

    c_grp = 3 * DIL_HEADS * DIL_HEAD_DIM
    outs, stats = [], []
    hn3 = hn.reshape(batch, seq, d)
    for g in range(len(DIL_GROUPS)):
        w_g = dil_w_in[0][:, g * c_grp:(g + 1) * c_grp]
        q_g, k_g, v_g = _dil_proj(hn3, w_g, dil_g_q[0, g], dil_g_k[0, g], g)
        o_g, st_g = _dil_attn(q_g, k_g, v_g, _dil_bias(rel_bias, g, seq), g, seq)
        outs.append(o_g)
        stats.append(st_g)
    x3, h4 = _dil_out(outs, stats, dil_w_o[0], x2, mod[1], norm_mlp[1], seq)
    x4, _ = _mlp(x3, h4, mlp_w1[1], mlp_w2[1], mod[1], norm_mix[1], mod[1], seq)
    return x4.reshape(batch, seq, d)
```

```python
import functools
import math

import jax
import jax.numpy as jnp
from jax import lax
from jax.experimental import pallas as pl
from jax.experimental.pallas import tpu as pltpu

F32 = jnp.float32
BF16 = jnp.bfloat16

EPS = 1e-6
LOG2E = 1.4426950408889634
LN2 = 0.6931471805599453
MASKED = -1e30

N_MOD = 6
ROPE_THETA = 10000.0
MLA_ROPE = 32
DIL_GROUPS = ((128, 1), (512, 4), (2048, 16))
DIL_HEADS = 16
DIL_HEAD_DIM = 64
N_BUCKETS = 32
MAX_DISTANCE = 1024

LANES = 128
HEAD_PAD = 128

VMEM_LIMIT = 56 * 1024 * 1024

NT_DIMS = (((1,), (1,)), ((), ()))
TN_DIMS = (((0,), (0,)), ((), ()))


def _params(sem):
    return pltpu.CompilerParams(dimension_semantics=sem, vmem_limit_bytes=VMEM_LIMIT)


def _ada_norm(x, g, scale, shift):
    ms = jnp.mean(x * x, axis=-1, keepdims=True)
    return (x * lax.rsqrt(ms + EPS) * g) * (1.0 + scale) + shift


def _rms(x, g):
    ms = jnp.mean(x * x, axis=-1, keepdims=True)
    return x * lax.rsqrt(ms + EPS) * g


def _mod_kernel(c_ref, w_ref, b_ref, o_ref):
    c = c_ref[...]
    cond = c / (1.0 + jnp.exp(-c))
    o_ref[0] = (
        jnp.dot(cond.astype(BF16), w_ref[0].astype(BF16), preferred_element_type=F32) + b_ref[0]
    )


def _mod(c, ada_w, ada_b):
    depth, d, n = ada_w.shape
    b = c.shape[0]
    tn = 1536
    return pl.pallas_call(
        _mod_kernel,
        grid=(depth, n // tn),
        in_specs=[
            pl.BlockSpec((b, d), lambda l, j: (0, 0)),
            pl.BlockSpec((1, d, tn), lambda l, j: (l, 0, j)),
            pl.BlockSpec((1, 1, tn), lambda l, j: (l, 0, j)),
        ],
        out_specs=pl.BlockSpec((1, b, tn), lambda l, j: (l, 0, j)),
        out_shape=jax.ShapeDtypeStruct((depth, b, n), F32),
        compiler_params=_params(("arbitrary", "arbitrary")),
        name="mod",
    )(c, ada_w, ada_b.reshape(depth, 1, n))


def _mla_proj_kernel(x_ref, mod_ref, nmix_ref, pos_ref, inv_ref, wlat_ref, wkr_ref, gqa_ref, gkva_ref,
                     wqb_ref, wkn_ref, wv_ref, gq_ref, qT_ref, k_ref, vT_ref, *, n_heads, q_lora, qk_dim):
    x = x_ref[...]
    tm = x.shape[0]
    m = mod_ref[0]
    hb = _ada_norm(x, nmix_ref[...], m[1:2], m[0:1]).astype(BF16)
    lat = jnp.dot(hb, wlat_ref[...], preferred_element_type=F32)
    qn = _rms(lat[:, :q_lora], gqa_ref[...]).astype(BF16)
    kvn = _rms(lat[:, q_lora:], gkva_ref[...]).astype(BF16)

    half = MLA_ROPE // 2
    nope = qk_dim - MLA_ROPE
    pos = pos_ref[...].astype(F32)
    ang = jnp.concatenate([inv_ref[...]] * (tm // LANES), axis=1) * pos
    cos_t = jnp.cos(ang)
    sin_t = jnp.sin(ang)

    def rope_rows(blk):
        x1 = blk[nope:nope + half]
        x2 = blk[nope + half:qk_dim]
        return jnp.concatenate(
            [blk[:nope], x1 * cos_t - x2 * sin_t, x2 * cos_t + x1 * sin_t, blk[qk_dim:]], axis=0)

    kr_t = lax.dot_general(wkr_ref[...], hb, NT_DIMS, preferred_element_type=F32)
    kr = rope_rows(kr_t).T

    q_t = lax.dot_general(wqb_ref[...], qn, NT_DIMS, preferred_element_type=F32)
    gq = jnp.concatenate([gq_ref[...]] * (tm // LANES), axis=1)
    for h in range(n_heads):
        blk = rope_rows(q_t[h * HEAD_PAD:(h + 1) * HEAD_PAD])
        ssq = jnp.sum(blk * blk, axis=0, keepdims=True)
        r = lax.rsqrt(ssq * (1.0 / qk_dim) + EPS)
        qT_ref[0, h * HEAD_PAD:(h + 1) * HEAD_PAD, :] = (blk * r * gq).astype(BF16)

    kn = jnp.dot(kvn, wkn_ref[...], preferred_element_type=F32)
    for h in range(n_heads):
        kb = kn[:, h * HEAD_PAD:(h + 1) * HEAD_PAD] + kr
        ssq = jnp.sum(kb * kb, axis=1, keepdims=True)
        r = lax.rsqrt(ssq * (1.0 / qk_dim) + EPS)
        k_ref[:, h * HEAD_PAD:(h + 1) * HEAD_PAD] = (kb * r).astype(BF16)

    vT_ref[0] = lax.dot_general(wv_ref[...], kvn, NT_DIMS, preferred_element_type=F32).astype(BF16)


def _mla_proj(x2d, mod_l, nmix, positions, w_in, g_qa, w_qb, g_kva, w_kvb, g_q, g_k, batch, seq):
    t, d = x2d.shape
    q_lora = g_qa.shape[0]
    kv_lora = g_kva.shape[0]
    qk_dim = g_q.shape[0]
    n_heads = w_qb.shape[1] // qk_dim
    nope = qk_dim - MLA_ROPE
    v_dim = w_kvb.shape[1] // n_heads - nope
    assert nope + v_dim == HEAD_PAD and qk_dim <= HEAD_PAD
    half = MLA_ROPE // 2
    tm = 512
    tpb = seq // tm

    w_lat = w_in[:, :q_lora + kv_lora].astype(BF16)
    w_kr = jnp.zeros((HEAD_PAD, d), F32).at[nope:qk_dim].set(w_in[:, q_lora + kv_lora:].T).astype(BF16)
    w_qb_t = jnp.pad(w_qb.T.reshape(n_heads, qk_dim, q_lora), ((0, 0), (0, HEAD_PAD - qk_dim), (0, 0)))
    w_qb_t = w_qb_t.reshape(n_heads * HEAD_PAD, q_lora).astype(BF16)
    w_kvb3 = w_kvb.reshape(kv_lora, n_heads, nope + v_dim)
    w_kn = w_kvb3.at[:, :, nope:].set(0.0).reshape(kv_lora, n_heads * HEAD_PAD).astype(BF16)
    w_v_t = w_kvb3[:, :, nope:].reshape(kv_lora, n_heads * v_dim).T.astype(BF16)
    q_gain = jnp.pad(g_q * g_k * (qk_dim ** -0.5 * LOG2E), (0, HEAD_PAD - qk_dim))
    gq_col = jnp.broadcast_to(q_gain[:, None], (HEAD_PAD, LANES))
    inv = 1.0 / (ROPE_THETA ** (jnp.arange(half, dtype=F32) / half))
    inv_tab = jnp.broadcast_to(inv[:, None], (half, LANES))
    pos_row = positions.reshape(1, t)

    full = lambda shape: pl.BlockSpec(shape, lambda i: (0,) * len(shape))
    kern = functools.partial(_mla_proj_kernel, n_heads=n_heads, q_lora=q_lora, qk_dim=qk_dim)
    return pl.pallas_call(
        kern,
        grid=(t // tm,),
        in_specs=[
            pl.BlockSpec((tm, d), lambda i: (i, 0)),
            pl.BlockSpec((1, N_MOD, d), lambda i: (i // tpb, 0, 0)),
            full((1, d)),
            pl.BlockSpec((1, tm), lambda i: (0, i)),
            full((half, LANES)),
            full(w_lat.shape), full(w_kr.shape), full((1, q_lora)), full((1, kv_lora)),
            full(w_qb_t.shape), full(w_kn.shape), full(w_v_t.shape),
            full((HEAD_PAD, LANES)),
        ],
        out_specs=[
            pl.BlockSpec((1, n_heads * HEAD_PAD, tm), lambda i: (i // tpb, 0, i % tpb)),
            pl.BlockSpec((tm, n_heads * HEAD_PAD), lambda i: (i, 0)),
            pl.BlockSpec((1, n_heads * v_dim, tm), lambda i: (i // tpb, 0, i % tpb)),
        ],
        out_shape=[
            jax.ShapeDtypeStruct((batch, n_heads * HEAD_PAD, seq), BF16),
            jax.ShapeDtypeStruct((t, n_heads * HEAD_PAD), BF16),
            jax.ShapeDtypeStruct((batch, n_heads * v_dim, seq), BF16),
        ],
        compiler_params=_params(("arbitrary",)),
        name="mla_proj",
    )(x2d, mod_l, nmix[None, :], pos_row, inv_tab, w_lat, w_kr, g_qa[None, :], g_kva[None, :],
      w_qb_t, w_kn, w_v_t, gq_col)


ONES_ROWS = 16


def _mla_attn_kernel(qT_ref, k_ref, vT_ref, oT_ref, s_ref, *, tq, tn, kc):
    k = k_ref[0]
    seq = k.shape[0]
    v_dim = vT_ref.shape[1]
    v_ext = jnp.concatenate([vT_ref[0], jnp.ones((ONES_ROWS, seq), BF16)], axis=0)
    n_tiles = seq // tq

    n_chunks = seq // kc
    n_cols = tq // tn

    def scores(i, c):
        s_ref[i % 2, c * kc:(c + 1) * kc, :] = jnp.dot(
            k[c * kc:(c + 1) * kc], qT_ref[0, :, i * tq:(i + 1) * tq], preferred_element_type=F32)

    for c in range(n_chunks):
        scores(0, c)
    for i in range(n_tiles):
        for j in range(n_cols):
            m = jnp.full((1, tn), MASKED, F32)
            acc = jnp.zeros((v_dim + ONES_ROWS, tn), F32)
            for c in range(n_chunks):
                step = j * n_chunks + c
                if i + 1 < n_tiles and step % n_cols == 0:
                    scores(i + 1, step // n_cols)
                s_c = s_ref[i % 2, c * kc:(c + 1) * kc, j * tn:(j + 1) * tn]
                m_new = jnp.maximum(m, jnp.max(s_c, axis=0, keepdims=True))
                p = jnp.exp2(s_c - m_new).astype(BF16)
                acc = acc * jnp.exp2(m - m_new) + jnp.dot(v_ext[:, c * kc:(c + 1) * kc], p,
                                                          preferred_element_type=F32)
                m = m_new
            o = acc[:v_dim] * (1.0 / acc[v_dim:v_dim + 1])
            oT_ref[0, :, i * tq + j * tn:i * tq + (j + 1) * tn] = o.astype(BF16)


def _mla_attn(q_t, k_pad, v_t, n_heads):
    batch, _, seq = q_t.shape
    v_dim = v_t.shape[1] // n_heads
    k3 = k_pad.reshape(batch, seq, n_heads * HEAD_PAD)
    tq, kc = 512, 256
    return pl.pallas_call(
        functools.partial(_mla_attn_kernel, tq=tq, tn=256, kc=kc),
        grid=(batch, n_heads),
        in_specs=[
            pl.BlockSpec((1, HEAD_PAD, seq), lambda b, h: (b, h, 0)),
            pl.BlockSpec((1, seq, HEAD_PAD), lambda b, h: (b, 0, h)),
            pl.BlockSpec((1, v_dim, seq), lambda b, h: (b, h, 0)),
        ],
        out_specs=pl.BlockSpec((1, v_dim, seq), lambda b, h: (b, h, 0)),
        out_shape=jax.ShapeDtypeStruct((batch, n_heads * v_dim, seq), BF16),
        scratch_shapes=[pltpu.VMEM((2, seq, tq), F32)],
        compiler_params=_params(("arbitrary", "arbitrary")),
        name="mla_attn",
    )(q_t, k3, v_t)


def _attn_out_kernel(oT_ref, wo_ref, x_ref, mod_ref, nmlp_ref, x1_ref, h2_ref):
    y = lax.dot_general(oT_ref[0], wo_ref[...], TN_DIMS, preferred_element_type=F32)
    m = mod_ref[0]
    x1 = x_ref[...] + m[2:3] * y
    x1_ref[...] = x1
    h2_ref[...] = _ada_norm(x1, nmlp_ref[...], m[4:5], m[3:4]).astype(BF16)


def _attn_out(o_t, w_o, x2d, mod_l, nmlp, seq):
    t, d = x2d.shape
    tm = 512
    tpb = seq // tm
    c = o_t.shape[1]
    return pl.pallas_call(
        _attn_out_kernel,
        grid=(t // tm,),
        in_specs=[
            pl.BlockSpec((1, c, tm), lambda i: (i // tpb, 0, i % tpb)),
            pl.BlockSpec((c, d), lambda i: (0, 0)),
            pl.BlockSpec((tm, d), lambda i: (i, 0)),
            pl.BlockSpec((1, N_MOD, d), lambda i: (i // tpb, 0, 0)),
            pl.BlockSpec((1, d), lambda i: (0, 0)),
        ],
        out_specs=[pl.BlockSpec((tm, d), lambda i: (i, 0)), pl.BlockSpec((tm, d), lambda i: (i, 0))],
        out_shape=[jax.ShapeDtypeStruct((t, d), F32), jax.ShapeDtypeStruct((t, d), BF16)],
        compiler_params=_params(("arbitrary",)),
        name="attn_out",
    )(o_t, w_o.astype(BF16), x2d, mod_l, nmlp[None, :])


def _mlp_kernel(x_ref, h_ref, w1_ref, w2_ref, mod_ref, *rest, with_next):
    if with_next:
        nnext_ref, modn_ref, x2_ref, hn_ref, acc_ref = rest
    else:
        x2_ref, acc_ref = rest
    j = pl.program_id(1)

    @pl.when(j == 0)
    def _():
        acc_ref[...] = jnp.zeros_like(acc_ref)

    u = jnp.dot(h_ref[...], w1_ref[0].astype(BF16), preferred_element_type=F32)
    u = jnp.square(jnp.maximum(u, 0.0)).astype(BF16)
    acc_ref[...] += jnp.dot(u, w2_ref[0].astype(BF16), preferred_element_type=F32)

    @pl.when(j == pl.num_programs(1) - 1)
    def _():
        x2 = x_ref[...] + mod_ref[0][5:6] * acc_ref[...]
        x2_ref[...] = x2
        if with_next:
            mn = modn_ref[0]
            hn_ref[...] = _ada_norm(x2, nnext_ref[...], mn[1:2], mn[0:1]).astype(BF16)


def _mlp(x1, h2, w1, w2, layer, mod_l, seq, nnext=None, mod_next=None):
    t, d = x1.shape
    ff = w1.shape[2]
    tm, tf = 1024, 1024
    tpb = seq // tm
    with_next = nnext is not None
    row = pl.BlockSpec((tm, d), lambda i, j: (i, 0))
    mod_spec = pl.BlockSpec((1, N_MOD, d), lambda i, j: (i // tpb, 0, 0))
    in_specs = [row, row, pl.BlockSpec((1, d, tf), lambda i, j: (layer, 0, j)),
                pl.BlockSpec((1, tf, d), lambda i, j: (layer, j, 0)), mod_spec]
    args = [x1, h2, w1, w2, mod_l]
    out_specs = [row]
    out_shape = [jax.ShapeDtypeStruct((t, d), F32)]
    if with_next:
        in_specs += [pl.BlockSpec((1, d), lambda i, j: (0, 0)), mod_spec]
        args += [nnext[None, :], mod_next]
        out_specs.append(row)
        out_shape.append(jax.ShapeDtypeStruct((t, d), BF16))
    return pl.pallas_call(
        functools.partial(_mlp_kernel, with_next=with_next),
        grid=(t // tm, ff // tf),
        in_specs=in_specs,
        out_specs=out_specs,
        out_shape=out_shape,
        scratch_shapes=[pltpu.VMEM((tm, d), F32)],
        compiler_params=_params(("arbitrary", "arbitrary")),
        name="mlp",
    )(*args)


DIL_TQ = 128
DIL_HALF = 64
assert all(w // (2 * d) == DIL_HALF for w, d in DIL_GROUPS)


def _dil_geometry(seq, dilation):
    length = seq // dilation
    kwin = min(2 * DIL_TQ, length)
    return length, kwin, length // DIL_TQ


def _t5_bucket(rel):
    nb = N_BUCKETS // 2
    max_exact = nb // 2
    ret = jnp.where(rel > 0, nb, 0)
    n = jnp.abs(rel)
    nf = jnp.maximum(n, 1).astype(F32)
    large = max_exact + (jnp.log(nf / max_exact) / math.log(MAX_DISTANCE / max_exact)
                         * (nb - max_exact)).astype(jnp.int32)
    large = jnp.minimum(large, nb - 1)
    return ret + jnp.where(n < max_exact, n, large)


def _dil_bias_kernel(tab_ref, o_ref, *, group, dilation, kwin, n_var):
    var = pl.program_id(0)
    pair = pl.program_id(1)
    kk = lax.broadcasted_iota(jnp.int32, (kwin, DIL_TQ), 0)
    qi = lax.broadcasted_iota(jnp.int32, (kwin, DIL_TQ), 1)
    rel_a = kk - qi - (DIL_HALF if n_var > 1 else 0)
    valid = jnp.abs(rel_a) <= DIL_HALF
    if n_var > 1:
        crossing = jnp.where((qi < DIL_HALF) == (kk < DIL_TQ), 0, var)
        valid = jnp.where(valid, 1, 0) - crossing == 1
    bucket = _t5_bucket(rel_a * dilation)
    for side in range(2):
        col = group * DIL_HEADS + 2 * pair + side
        acc = jnp.zeros((kwin, DIL_TQ), F32)
        for b in range(N_BUCKETS):
            acc = jnp.where(bucket == b, tab_ref[b, col], acc)
        o_ref[0, 0, :, side * DIL_TQ:(side + 1) * DIL_TQ] = jnp.where(valid, acc * LOG2E, MASKED)


def _dil_bias(rel_bias, group, seq):
    _, dilation = DIL_GROUPS[group]
    _, kwin, n_tiles = _dil_geometry(seq, dilation)
    n_var = 1 if n_tiles == 1 else 2
    n_pairs = DIL_HEADS // 2
    kern = functools.partial(_dil_bias_kernel, group=group, dilation=dilation, kwin=kwin, n_var=n_var)
    return pl.pallas_call(
        kern,
        grid=(n_var, n_pairs),
        in_specs=[pl.BlockSpec(memory_space=pltpu.SMEM)],
        out_specs=pl.BlockSpec((1, 1, kwin, 2 * DIL_TQ), lambda v, p: (v, p, 0, 0)),
        out_shape=jax.ShapeDtypeStruct((n_var, n_pairs, kwin, 2 * DIL_TQ), F32),
        compiler_params=_params(("arbitrary", "arbitrary")),
        name=f"dil_bias{group}",
    )(rel_bias)


DIL_PROJ_ROWS = 512


def _dil_proj_kernel(h_ref, wq_ref, wk_ref, wv_ref, gq_ref, qT_ref, k_ref, vT_ref, *scratch,
                     dilation, rb, ra):
    hd = DIL_HEAD_DIM
    d_model = h_ref.shape[2]
    n_slabs = d_model // LANES
    mt = rb * ra
    if dilation == 1:
        hcat = h_ref[0]
    else:
        (scr,) = scratch
        sub = pl.program_id(2)

        @pl.when(sub == 0)
        def _():
            for s in range(n_slabs):
                scr[s] = h_ref[0, :, s * LANES:(s + 1) * LANES].astype(F32)

        rows = []
        for j in range(rb):
            res = sub * rb + j
            rows.append(jnp.concatenate(
                [scr[s, pl.ds(res, ra, stride=dilation), :] for s in range(n_slabs)], axis=1))
        hcat = jnp.concatenate(rows, axis=0).astype(BF16)

    q_t = lax.dot_general(wq_ref[...], hcat, NT_DIMS, preferred_element_type=F32)
    gq = jnp.concatenate([gq_ref[...]] * (mt // LANES), axis=1)
    for h in range(DIL_HEADS):
        blk = q_t[h * hd:(h + 1) * hd]
        r = lax.rsqrt(jnp.sum(blk * blk, axis=0, keepdims=True) * (1.0 / hd) + EPS)
        qb = (blk * r * gq).astype(BF16)
        for j in range(rb):
            for s in range(ra // LANES):
                lo = j * ra + s * LANES
                qT_ref[0, j, s, h * hd:(h + 1) * hd, :] = qb[:, lo:lo + LANES]

    kf = jnp.dot(hcat, wk_ref[...], preferred_element_type=F32)
    lane = lax.broadcasted_iota(jnp.int32, (mt, LANES), 1)
    low = lane < hd
    for c in range(DIL_HEADS * hd // LANES):
        y = kf[:, c * LANES:(c + 1) * LANES]
        y2 = y * y
        s_lo = jnp.sum(jnp.where(low, y2, 0.0), axis=1, keepdims=True)
        s_hi = jnp.sum(jnp.where(low, 0.0, y2), axis=1, keepdims=True)
        r = lax.rsqrt(jnp.where(low, s_lo, s_hi) * (1.0 / hd) + EPS)
        kb = (y * r).astype(BF16)
        for j in range(rb):
            k_ref[0, j, :, c * LANES:(c + 1) * LANES] = kb[j * ra:(j + 1) * ra]

    v_t = lax.dot_general(wv_ref[...], hcat, NT_DIMS, preferred_element_type=F32).astype(BF16)
    for j in range(rb):
        for s in range(ra // LANES):
            lo = j * ra + s * LANES
            vT_ref[0, j, s] = v_t[:, lo:lo + LANES]


def _dil_proj(hn3, w_in_g, g_q, g_k, group):
    batch, seq, d_model = hn3.shape
    _, dilation = DIL_GROUPS[group]
    length = seq // dilation
    c = DIL_HEADS * DIL_HEAD_DIM
    ra = DIL_PROJ_ROWS if dilation == 1 else DIL_TQ
    rb = DIL_PROJ_ROWS // ra
    tok = ra * dilation
    n_sub = dilation // rb
    wq_t = w_in_g[:, :c].T.astype(BF16)
    wk = w_in_g[:, c:2 * c].astype(BF16)
    wv_t = w_in_g[:, 2 * c:].T.astype(BF16)
    gq_col = jnp.broadcast_to((g_q * g_k * (DIL_HEAD_DIM ** -0.5 * LOG2E))[:, None], (DIL_HEAD_DIM, LANES))
    kern = functools.partial(_dil_proj_kernel, dilation=dilation, rb=rb, ra=ra)
    const = lambda shape: pl.BlockSpec(shape, lambda b, t, s: (0,) * len(shape))
    scratch = [] if dilation == 1 else [pltpu.VMEM((d_model // LANES, tok, LANES), F32)]
    return pl.pallas_call(
        kern,
        grid=(batch, seq // tok, n_sub),
        in_specs=[
            pl.BlockSpec((1, tok, d_model), lambda b, t, s: (b, t, 0)),
            const((c, d_model)), const((d_model, c)), const((c, d_model)),
            const((DIL_HEAD_DIM, LANES)),
        ],
        out_specs=[
            pl.BlockSpec((1, rb, ra // LANES, c, LANES), lambda b, t, s: (b, s, t, 0, 0)),
            pl.BlockSpec((1, rb, ra, c), lambda b, t, s: (b, s, t, 0)),
            pl.BlockSpec((1, rb, ra // LANES, c, LANES), lambda b, t, s: (b, s, t, 0, 0)),
        ],
        out_shape=[
            jax.ShapeDtypeStruct((batch, dilation, length // LANES, c, LANES), BF16),
            jax.ShapeDtypeStruct((batch, dilation, length, c), BF16),
            jax.ShapeDtypeStruct((batch, dilation, length // LANES, c, LANES), BF16),
        ],
        scratch_shapes=scratch,
        compiler_params=_params(("arbitrary", "arbitrary", "arbitrary")),
        name=f"dil_proj{group}",
    )(hn3, wq_t, wk, wv_t, gq_col)


def _dil_attn_kernel(qT_ref, k_ref, vT_ref, bm_ref, o_ref, st_ref, s_ref, *, n_tiles, rb):
    tq = DIL_TQ
    hd = DIL_HEAD_DIM
    n_pairs = DIL_HEADS // 2
    c = DIL_HEADS * hd
    shifted = n_tiles > 1
    low = lax.broadcasted_iota(jnp.int32, (LANES, tq), 0) < hd
    ones = jnp.ones((ONES_ROWS, 2 * tq if shifted else tq), BF16)
    tiles = [(j, t) for j in range(rb) for t in range(n_tiles)]

    def score_tile(i):
        j, t = tiles[i]
        t1 = (t + 1) % n_tiles
        for p in range(n_pairs):
            rows = slice(p * LANES, (p + 1) * LANES)
            if shifted:
                q2 = jnp.concatenate([qT_ref[0, j, t, rows, DIL_HALF:], qT_ref[0, j, t1, rows, :DIL_HALF]], axis=1)
                k2 = jnp.concatenate([k_ref[0, j, t * tq:(t + 1) * tq, rows],
                                      k_ref[0, j, t1 * tq:(t1 + 1) * tq, rows]], axis=0)
                bias = bm_ref[1 if t + 1 == n_tiles else 0, p]
            else:
                q2 = qT_ref[0, j, 0, rows, :]
                k2 = k_ref[0, j, :, rows]
                bias = bm_ref[0, p]
            zero = jnp.zeros_like(q2)
            bd = jnp.concatenate([jnp.where(low, q2, zero), jnp.where(low, zero, q2)], axis=1)
            s_ref[i % 2, p] = jnp.dot(k2, bd, preferred_element_type=F32) + bias

    def attend_tile(i):
        j, t = tiles[i]
        t1 = (t + 1) % n_tiles

        def store(ref, lanes, val):
            if shifted:
                ref[0, t * tq + DIL_HALF:(t + 1) * tq, lanes] = val[:DIL_HALF]
                ref[0, t1 * tq:t1 * tq + DIL_HALF, lanes] = val[DIL_HALF:]
            else:
                ref[0, :, lanes] = val

        stats = []
        for p in range(n_pairs):
            rows = slice(p * LANES, (p + 1) * LANES)
            s_t = s_ref[i % 2, p]
            m = jnp.max(s_t, axis=0, keepdims=True)
            pb = jnp.exp2(s_t - m).astype(BF16)
            if shifted:
                v2 = jnp.concatenate([vT_ref[0, j, t, rows, :], vT_ref[0, j, t1, rows, :]], axis=1)
            else:
                v2 = vT_ref[0, j, 0, rows, :]
            res = jnp.dot(jnp.concatenate([v2, ones], axis=0), pb, preferred_element_type=F32)
            den = res[LANES:LANES + 1]
            rinv = 1.0 / den
            o_pair = jnp.concatenate([res[:hd, :tq] * rinv[:, :tq], res[hd:LANES, tq:] * rinv[:, tq:]], axis=0)
            store(o_ref, slice(j * c + p * LANES, j * c + (p + 1) * LANES), o_pair.T.astype(BF16))
            lse = (m + jnp.log2(den)) * LN2
            stats += [lse[:, :tq], lse[:, tq:]]
        st = jnp.concatenate(stats + [jnp.zeros((LANES - DIL_HEADS, tq), F32)], axis=0)
        store(st_ref, slice(j * LANES, (j + 1) * LANES), st.T)

    score_tile(0)
    for i in range(len(tiles)):
        if i + 1 < len(tiles):
            score_tile(i + 1)
        attend_tile(i)


def _dil_attn(q_t, k, v_t, bias, group, seq):
    batch = q_t.shape[0]
    _, dilation = DIL_GROUPS[group]
    length, kwin, n_tiles = _dil_geometry(seq, dilation)
    c = DIL_HEADS * DIL_HEAD_DIM
    n_slabs = length // LANES
    rb = 1 if n_tiles > 1 else min(dilation, 4)
    kern = functools.partial(_dil_attn_kernel, n_tiles=n_tiles, rb=rb)
    return pl.pallas_call(
        kern,
        grid=(batch, dilation // rb),
        in_specs=[
            pl.BlockSpec((1, rb, n_slabs, c, LANES), lambda b, r: (b, r, 0, 0, 0)),
            pl.BlockSpec((1, rb, length, c), lambda b, r: (b, r, 0, 0)),
            pl.BlockSpec((1, rb, n_slabs, c, LANES), lambda b, r: (b, r, 0, 0, 0)),
            pl.BlockSpec(bias.shape, lambda b, r: (0, 0, 0, 0)),
        ],
        out_specs=[
            pl.BlockSpec((1, length, rb * c), lambda b, r: (b, 0, r)),
            pl.BlockSpec((1, length, rb * LANES), lambda b, r: (b, 0, r)),
        ],
        out_shape=[
            jax.ShapeDtypeStruct((batch, length, dilation * c), BF16),
            jax.ShapeDtypeStruct((batch, length, dilation * LANES), F32),
        ],
        scratch_shapes=[pltpu.VMEM((2, DIL_HEADS // 2, kwin, 2 * DIL_TQ), F32)],
        compiler_params=_params(("arbitrary", "arbitrary")),
        name=f"dil_attn{group}",
    )(q_t, k, v_t, bias)


def _dil_out_kernel(o0_ref, o1_ref, o2_ref, s0_ref, s1_ref, s2_ref, e_ref, wo_ref, x_ref, mod_ref, nmlp_ref,
                    x1_ref, h2_ref, *scratch):
    tm = x_ref.shape[0]
    c = wo_ref.shape[0]

    def token_major(o_ref, s_ref, oscr, sscr, dilation):
        ra = tm // dilation
        for r in range(dilation):
            for s in range(c // LANES):
                lo = r * c + s * LANES
                oscr[s, pl.ds(r, ra, stride=dilation), :] = o_ref[0, :, lo:lo + LANES].astype(F32)
            sscr[pl.ds(r, ra, stride=dilation), :] = s_ref[0, :, r * LANES:(r + 1) * LANES]
        return jnp.concatenate([oscr[s] for s in range(c // LANES)], axis=1), sscr[...]

    o0, l0 = o0_ref[0].astype(F32), s0_ref[0]
    o1, l1 = token_major(o1_ref, s1_ref, scratch[0], scratch[1], DIL_GROUPS[1][1])
    o2, l2 = token_major(o2_ref, s2_ref, scratch[2], scratch[3], DIL_GROUPS[2][1])
    mx = jnp.maximum(jnp.maximum(l0, l1), l2)
    e0, e1, e2 = jnp.exp(l0 - mx), jnp.exp(l1 - mx), jnp.exp(l2 - mx)
    rden = 1.0 / (e0 + e1 + e2)
    e = e_ref[...]

    def expand(w):
        return jnp.dot(w.astype(BF16), e, preferred_element_type=F32)

    o = expand(e0 * rden) * o0 + expand(e1 * rden) * o1 + expand(e2 * rden) * o2
    y = jnp.dot(o.astype(BF16), wo_ref[...], preferred_element_type=F32)
    m = mod_ref[0]
    x1 = x_ref[...] + m[2:3] * y
    x1_ref[...] = x1
    h2_ref[...] = _ada_norm(x1, nmlp_ref[...], m[4:5], m[3:4]).astype(BF16)


def _dil_out(outs, stats, w_o, x2d, mod_l, nmlp, seq):
    t, d = x2d.shape
    c = w_o.shape[0]
    tm = 512
    tpb = seq // tm
    expand = jnp.repeat(jnp.eye(DIL_HEADS, dtype=F32), DIL_HEAD_DIM, axis=1)
    expand = jnp.pad(expand, ((0, LANES - DIL_HEADS), (0, 0))).astype(BF16)
    row = lambda w: pl.BlockSpec((tm, w), lambda i: (i, 0))
    grp = lambda g, w: pl.BlockSpec((1, tm // DIL_GROUPS[g][1], DIL_GROUPS[g][1] * w),
                                    lambda i: (i // tpb, i % tpb, 0))
    scratch = []
    for g in (1, 2):
        scratch += [pltpu.VMEM((c // LANES, tm, LANES), F32), pltpu.VMEM((tm, LANES), F32)]
    return pl.pallas_call(
        _dil_out_kernel,
        grid=(t // tm,),
        in_specs=[grp(0, c), grp(1, c), grp(2, c), grp(0, LANES), grp(1, LANES), grp(2, LANES),
                  pl.BlockSpec((LANES, c), lambda i: (0, 0)),
                  pl.BlockSpec((c, d), lambda i: (0, 0)),
                  row(d),
                  pl.BlockSpec((1, N_MOD, d), lambda i: (i // tpb, 0, 0)),
                  pl.BlockSpec((1, d), lambda i: (0, 0))],
        out_specs=[row(d), row(d)],
        out_shape=[jax.ShapeDtypeStruct((t, d), F32), jax.ShapeDtypeStruct((t, d), BF16)],
        scratch_shapes=scratch,
        compiler_params=_params(("arbitrary",)),
        name="dil_out",
    )(*outs, *stats, expand, w_o.astype(BF16), x2d, mod_l, nmlp[None, :])


def kernel(x, c, positions, ada_w, ada_b, norm_mix, norm_mlp, mlp_w1, mlp_w2, mla_w_in, mla_g_qa, mla_w_qb,
           mla_g_kva, mla_w_kvb, mla_g_q, mla_g_k, mla_w_o, dil_w_in, dil_g_q, dil_g_k, dil_w_o, rel_bias):
    batch, seq, d = x.shape
    depth = ada_w.shape[0]
    assert depth == 2 and len(DIL_GROUPS) == 3
    mod = _mod(c, ada_w, ada_b).reshape(depth, batch, N_MOD, d)
    x2d = x.reshape(batch * seq, d)

    n_heads = mla_w_qb.shape[2] // mla_g_q.shape[1]
    q_t, k_pad, v_t = _mla_proj(x2d, mod[0], norm_mix[0], positions, mla_w_in[0], mla_g_qa[0], mla_w_qb[0],
                                mla_g_kva[0], mla_w_kvb[0], mla_g_q[0], mla_g_k[0], batch, seq)
    o_t = _mla_attn(q_t, k_pad, v_t, n_heads)
    x1, h2 = _attn_out(o_t, mla_w_o[0], x2d, mod[0], norm_mlp[0], seq)
    x2, hn = _mlp(x1, h2, mlp_w1, mlp_w2, 0, mod[0], seq, norm_mix[1], mod[1])

    c_grp = 3 * DIL_HEADS * DIL_HEAD_DIM
    outs, stats = [], []
    hn3 = hn.reshape(batch, seq, d)
    for g in range(len(DIL_GROUPS)):
        w_g = dil_w_in[0][:, g * c_grp:(g + 1) * c_grp]
        q_g, k_g, v_g = _dil_proj(hn3, w_g, dil_g_q[0, g], dil_g_k[0, g], g)
        o_g, st_g = _dil_attn(q_g, k_g, v_g, _dil_bias(rel_bias, g, seq), g, seq)
        outs.append(o_g)
        stats.append(st_g)
    x3, h4 = _dil_out(outs, stats, dil_w_o[0], x2, mod[1], norm_mlp[1], seq)
    (x4,) = _mlp(x3, h4, mlp_w1, mlp_w2, 1, mod[1], seq)
    return x4.reshape(batch, seq, d)
```

```python
import functools
import math

import jax
import jax.numpy as jnp
from jax import lax
from jax.experimental import pallas as pl
from jax.experimental.pallas import tpu as pltpu

F32 = jnp.float32
BF16 = jnp.bfloat16

EPS = 1e-6
LOG2E = 1.4426950408889634
LN2 = 0.6931471805599453
MASKED = -1e30

N_MOD = 6
ROPE_THETA = 10000.0
MLA_ROPE = 32
DIL_GROUPS = ((128, 1), (512, 4), (2048, 16))
DIL_HEADS = 16
DIL_HEAD_DIM = 64
N_BUCKETS = 32
MAX_DISTANCE = 1024

LANES = 128
HEAD_PAD = 128

VMEM_LIMIT = 56 * 1024 * 1024

NT_DIMS = (((1,), (1,)), ((), ()))
TN_DIMS = (((0,), (0,)), ((), ()))


def _params(sem):
    return pltpu.CompilerParams(dimension_semantics=sem, vmem_limit_bytes=VMEM_LIMIT)


def _ada_norm(x, g, scale, shift):
    ms = jnp.mean(x * x, axis=-1, keepdims=True)
    return (x * lax.rsqrt(ms + EPS) * g) * (1.0 + scale) + shift


def _rms(x, g):
    ms = jnp.mean(x * x, axis=-1, keepdims=True)
    return x * lax.rsqrt(ms + EPS) * g


def _mod_kernel(c_ref, w_ref, b_ref, o_ref):
    c = c_ref[...]
    cond = c / (1.0 + jnp.exp(-c))
    o_ref[0] = (
        jnp.dot(cond.astype(BF16), w_ref[0].astype(BF16), preferred_element_type=F32) + b_ref[0]
    )


def _mod(c, ada_w, ada_b):
    depth, d, n = ada_w.shape
    b = c.shape[0]
    tn = 1536
    return pl.pallas_call(
        _mod_kernel,
        grid=(depth, n // tn),
        in_specs=[
            pl.BlockSpec((b, d), lambda l, j: (0, 0)),
            pl.BlockSpec((1, d, tn), lambda l, j: (l, 0, j)),
            pl.BlockSpec((1, 1, tn), lambda l, j: (l, 0, j)),
        ],
        out_specs=pl.BlockSpec((1, b, tn), lambda l, j: (l, 0, j)),
        out_shape=jax.ShapeDtypeStruct((depth, b, n), F32),
        compiler_params=_params(("arbitrary", "arbitrary")),
        name="mod",
    )(c, ada_w, ada_b.reshape(depth, 1, n))


def _mla_proj_kernel(x_ref, mod_ref, nmix_ref, pos_ref, inv_ref, wlat_ref, wkr_ref, gqa_ref, gkva_ref,
                     wqb_ref, wkn_ref, wv_ref, gq_ref, qT_ref, k_ref, vT_ref, *, n_heads, q_lora, qk_dim):
    x = x_ref[...]
    tm = x.shape[0]
    m = mod_ref[0]
    hb = _ada_norm(x, nmix_ref[...], m[1:2], m[0:1]).astype(BF16)
    lat = jnp.dot(hb, wlat_ref[...], preferred_element_type=F32)
    qn = _rms(lat[:, :q_lora], gqa_ref[...]).astype(BF16)
    kvn = _rms(lat[:, q_lora:], gkva_ref[...]).astype(BF16)

    half = MLA_ROPE // 2
    nope = qk_dim - MLA_ROPE
    pos = pos_ref[...].astype(F32)
    ang = jnp.concatenate([inv_ref[...]] * (tm // LANES), axis=1) * pos
    cos_t = jnp.cos(ang)
    sin_t = jnp.sin(ang)

    def rope_rows(blk):
        x1 = blk[nope:nope + half]
        x2 = blk[nope + half:qk_dim]
        return jnp.concatenate(
            [blk[:nope], x1 * cos_t - x2 * sin_t, x2 * cos_t + x1 * sin_t, blk[qk_dim:]], axis=0)

    kr_t = lax.dot_general(wkr_ref[...], hb, NT_DIMS, preferred_element_type=F32)
    kr = rope_rows(kr_t).T

    q_t = lax.dot_general(wqb_ref[...], qn, NT_DIMS, preferred_element_type=F32)
    gq = jnp.concatenate([gq_ref[...]] * (tm // LANES), axis=1)
    for h in range(n_heads):
        blk = rope_rows(q_t[h * HEAD_PAD:(h + 1) * HEAD_PAD])
        ssq = jnp.sum(blk * blk, axis=0, keepdims=True)
        r = lax.rsqrt(ssq * (1.0 / qk_dim) + EPS)
        qT_ref[0, h * HEAD_PAD:(h + 1) * HEAD_PAD, :] = (blk * r * gq).astype(BF16)

    kn = jnp.dot(kvn, wkn_ref[...], preferred_element_type=F32)
    for h in range(n_heads):
        kb = kn[:, h * HEAD_PAD:(h + 1) * HEAD_PAD] + kr
        ssq = jnp.sum(kb * kb, axis=1, keepdims=True)
        r = lax.rsqrt(ssq * (1.0 / qk_dim) + EPS)
        k_ref[:, h * HEAD_PAD:(h + 1) * HEAD_PAD] = (kb * r).astype(BF16)

    vT_ref[0] = lax.dot_general(wv_ref[...], kvn, NT_DIMS, preferred_element_type=F32).astype(BF16)


def _mla_proj(x2d, mod_l, nmix, positions, w_in, g_qa, w_qb, g_kva, w_kvb, g_q, g_k, batch, seq):
    t, d = x2d.shape
    q_lora = g_qa.shape[0]
    kv_lora = g_kva.shape[0]
    qk_dim = g_q.shape[0]
    n_heads = w_qb.shape[1] // qk_dim
    nope = qk_dim - MLA_ROPE
    v_dim = w_kvb.shape[1] // n_heads - nope
    assert nope + v_dim == HEAD_PAD and qk_dim <= HEAD_PAD
    half = MLA_ROPE // 2
    tm = 512
    tpb = seq // tm

    w_lat = w_in[:, :q_lora + kv_lora].astype(BF16)
    w_kr = jnp.zeros((HEAD_PAD, d), F32).at[nope:qk_dim].set(w_in[:, q_lora + kv_lora:].T).astype(BF16)
    w_qb_t = jnp.pad(w_qb.T.reshape(n_heads, qk_dim, q_lora), ((0, 0), (0, HEAD_PAD - qk_dim), (0, 0)))
    w_qb_t = w_qb_t.reshape(n_heads * HEAD_PAD, q_lora).astype(BF16)
    w_kvb3 = w_kvb.reshape(kv_lora, n_heads, nope + v_dim)
    w_kn = jnp.where(jnp.arange(nope + v_dim) < nope, w_kvb3, 0.0).reshape(kv_lora, n_heads * HEAD_PAD).astype(BF16)
    w_v_t = w_kvb3[:, :, nope:].reshape(kv_lora, n_heads * v_dim).T.astype(BF16)
    q_gain = jnp.pad(g_q * g_k * (qk_dim ** -0.5 * LOG2E), (0, HEAD_PAD - qk_dim))
    gq_col = jnp.broadcast_to(q_gain[:, None], (HEAD_PAD, LANES))
    inv = 1.0 / (ROPE_THETA ** (jnp.arange(half, dtype=F32) / half))
    inv_tab = jnp.broadcast_to(inv[:, None], (half, LANES))
    pos_row = positions.reshape(1, t)

    full = lambda shape: pl.BlockSpec(shape, lambda i: (0,) * len(shape))
    kern = functools.partial(_mla_proj_kernel, n_heads=n_heads, q_lora=q_lora, qk_dim=qk_dim)
    return pl.pallas_call(
        kern,
        grid=(t // tm,),
        in_specs=[
            pl.BlockSpec((tm, d), lambda i: (i, 0)),
            pl.BlockSpec((1, N_MOD, d), lambda i: (i // tpb, 0, 0)),
            full((1, d)),
            pl.BlockSpec((1, tm), lambda i: (0, i)),
            full((half, LANES)),
            full(w_lat.shape), full(w_kr.shape), full((1, q_lora)), full((1, kv_lora)),
            full(w_qb_t.shape), full(w_kn.shape), full(w_v_t.shape),
            full((HEAD_PAD, LANES)),
        ],
        out_specs=[
            pl.BlockSpec((1, n_heads * HEAD_PAD, tm), lambda i: (i // tpb, 0, i % tpb)),
            pl.BlockSpec((tm, n_heads * HEAD_PAD), lambda i: (i, 0)),
            pl.BlockSpec((1, n_heads * v_dim, tm), lambda i: (i // tpb, 0, i % tpb)),
        ],
        out_shape=[
            jax.ShapeDtypeStruct((batch, n_heads * HEAD_PAD, seq), BF16),
            jax.ShapeDtypeStruct((t, n_heads * HEAD_PAD), BF16),
            jax.ShapeDtypeStruct((batch, n_heads * v_dim, seq), BF16),
        ],
        compiler_params=_params(("arbitrary",)),
        name="mla_proj",
    )(x2d, mod_l, nmix[None, :], pos_row, inv_tab, w_lat, w_kr, g_qa[None, :], g_kva[None, :],
      w_qb_t, w_kn, w_v_t, gq_col)


ONES_ROWS = 16


def _mla_attn_kernel(qT_ref, k_ref, vT_ref, oT_ref, s_ref, *, hb, tq, tn, kc):
    seq = k_ref.shape[1]
    v_dim = vT_ref.shape[1] // hb
    ones = jnp.ones((ONES_ROWS, seq), BF16)
    n_chunks = seq // kc
    n_cols = tq // tn
    tiles = [(h, i) for h in range(hb) for i in range(seq // tq)]

    def scores(t, c):
        h, i = tiles[t]
        s_ref[t % 2, c * kc:(c + 1) * kc, :] = jnp.dot(
            k_ref[0, c * kc:(c + 1) * kc, h * HEAD_PAD:(h + 1) * HEAD_PAD],
            qT_ref[0, h * HEAD_PAD:(h + 1) * HEAD_PAD, i * tq:(i + 1) * tq], preferred_element_type=F32)

    for c in range(n_chunks):
        scores(0, c)
    for t, (h, i) in enumerate(tiles):
        v_ext = jnp.concatenate([vT_ref[0, h * v_dim:(h + 1) * v_dim, :], ones], axis=0)
        for j in range(n_cols):
            m = jnp.full((1, tn), MASKED, F32)
            acc = jnp.zeros((v_dim + ONES_ROWS, tn), F32)
            for c in range(n_chunks):
                step = j * n_chunks + c
                if t + 1 < len(tiles) and step % n_cols == 0:
                    scores(t + 1, step // n_cols)
                s_c = s_ref[t % 2, c * kc:(c + 1) * kc, j * tn:(j + 1) * tn]
                m_new = jnp.maximum(m, jnp.max(s_c, axis=0, keepdims=True))
                p = jnp.exp2(s_c - m_new).astype(BF16)
                acc = acc * jnp.exp2(m - m_new) + jnp.dot(v_ext[:, c * kc:(c + 1) * kc], p,
                                                          preferred_element_type=F32)
                m = m_new
            o = acc[:v_dim] * (1.0 / acc[v_dim:v_dim + 1])
            oT_ref[0, h * v_dim:(h + 1) * v_dim, i * tq + j * tn:i * tq + (j + 1) * tn] = o.astype(BF16)


def _mla_attn(q_t, k_pad, v_t, n_heads):
    batch, _, seq = q_t.shape
    v_dim = v_t.shape[1] // n_heads
    k3 = k_pad.reshape(batch, seq, n_heads * HEAD_PAD)
    tq, kc = 512, 256
    hb = 2
    return pl.pallas_call(
        functools.partial(_mla_attn_kernel, hb=hb, tq=tq, tn=256, kc=kc),
        grid=(batch, n_heads // hb),
        in_specs=[
            pl.BlockSpec((1, hb * HEAD_PAD, seq), lambda b, h: (b, h, 0)),
            pl.BlockSpec((1, seq, hb * HEAD_PAD), lambda b, h: (b, 0, h)),
            pl.BlockSpec((1, hb * v_dim, seq), lambda b, h: (b, h, 0)),
        ],
        out_specs=pl.BlockSpec((1, hb * v_dim, seq), lambda b, h: (b, h, 0)),
        out_shape=jax.ShapeDtypeStruct((batch, n_heads * v_dim, seq), BF16),
        scratch_shapes=[pltpu.VMEM((2, seq, tq), F32)],
        compiler_params=_params(("arbitrary", "arbitrary")),
        name="mla_attn",
    )(q_t, k3, v_t)


def _attn_out_kernel(oT_ref, wo_ref, x_ref, mod_ref, nmlp_ref, x1_ref, h2_ref):
    y = lax.dot_general(oT_ref[0], wo_ref[...], TN_DIMS, preferred_element_type=F32)
    m = mod_ref[0]
    x1 = x_ref[...] + m[2:3] * y
    x1_ref[...] = x1
    h2_ref[...] = _ada_norm(x1, nmlp_ref[...], m[4:5], m[3:4]).astype(BF16)


def _attn_out(o_t, w_o, x2d, mod_l, nmlp, seq):
    t, d = x2d.shape
    tm = 512
    tpb = seq // tm
    c = o_t.shape[1]
    return pl.pallas_call(
        _attn_out_kernel,
        grid=(t // tm,),
        in_specs=[
            pl.BlockSpec((1, c, tm), lambda i: (i // tpb, 0, i % tpb)),
            pl.BlockSpec((c, d), lambda i: (0, 0)),
            pl.BlockSpec((tm, d), lambda i: (i, 0)),
            pl.BlockSpec((1, N_MOD, d), lambda i: (i // tpb, 0, 0)),
            pl.BlockSpec((1, d), lambda i: (0, 0)),
        ],
        out_specs=[pl.BlockSpec((tm, d), lambda i: (i, 0)), pl.BlockSpec((tm, d), lambda i: (i, 0))],
        out_shape=[jax.ShapeDtypeStruct((t, d), F32), jax.ShapeDtypeStruct((t, d), BF16)],
        compiler_params=_params(("arbitrary",)),
        name="attn_out",
    )(o_t, w_o.astype(BF16), x2d, mod_l, nmlp[None, :])


def _mlp_kernel(x_ref, h_ref, w1_ref, w2_ref, mod_ref, *rest, with_next):
    if with_next:
        nnext_ref, modn_ref, x2_ref, hn_ref, acc_ref = rest
    else:
        x2_ref, acc_ref = rest
    j = pl.program_id(1)

    @pl.when(j == 0)
    def _():
        acc_ref[...] = jnp.zeros_like(acc_ref)

    u = jnp.dot(h_ref[...], w1_ref[0].astype(BF16), preferred_element_type=F32)
    u = jnp.square(jnp.maximum(u, 0.0)).astype(BF16)
    acc_ref[...] += jnp.dot(u, w2_ref[0].astype(BF16), preferred_element_type=F32)

    @pl.when(j == pl.num_programs(1) - 1)
    def _():
        x2 = x_ref[...] + mod_ref[0][5:6] * acc_ref[...]
        x2_ref[...] = x2
        if with_next:
            mn = modn_ref[0]
            hn_ref[...] = _ada_norm(x2, nnext_ref[...], mn[1:2], mn[0:1]).astype(BF16)


def _mlp(x1, h2, w1, w2, layer, mod_l, seq, nnext=None, mod_next=None):
    t, d = x1.shape
    ff = w1.shape[2]
    tm, tf = 1024, 1024
    tpb = seq // tm
    with_next = nnext is not None
    row = pl.BlockSpec((tm, d), lambda i, j: (i, 0))
    mod_spec = pl.BlockSpec((1, N_MOD, d), lambda i, j: (i // tpb, 0, 0))
    in_specs = [row, row, pl.BlockSpec((1, d, tf), lambda i, j: (layer, 0, j)),
                pl.BlockSpec((1, tf, d), lambda i, j: (layer, j, 0)), mod_spec]
    args = [x1, h2, w1, w2, mod_l]
    out_specs = [row]
    out_shape = [jax.ShapeDtypeStruct((t, d), F32)]
    if with_next:
        in_specs += [pl.BlockSpec((1, d), lambda i, j: (0, 0)), mod_spec]
        args += [nnext[None, :], mod_next]
        out_specs.append(row)
        out_shape.append(jax.ShapeDtypeStruct((t, d), BF16))
    return pl.pallas_call(
        functools.partial(_mlp_kernel, with_next=with_next),
        grid=(t // tm, ff // tf),
        in_specs=in_specs,
        out_specs=out_specs,
        out_shape=out_shape,
        scratch_shapes=[pltpu.VMEM((tm, d), F32)],
        compiler_params=_params(("arbitrary", "arbitrary")),
        name="mlp",
    )(*args)


DIL_TQ = 128
DIL_HALF = 64
assert all(w // (2 * d) == DIL_HALF for w, d in DIL_GROUPS)


def _dil_geometry(seq, dilation):
    length = seq // dilation
    kwin = min(2 * DIL_TQ, length)
    return length, kwin, length // DIL_TQ


def _t5_bucket(rel):
    nb = N_BUCKETS // 2
    max_exact = nb // 2
    ret = jnp.where(rel > 0, nb, 0)
    n = jnp.abs(rel)
    nf = jnp.maximum(n, 1).astype(F32)
    large = max_exact + (jnp.log(nf / max_exact) / math.log(MAX_DISTANCE / max_exact)
                         * (nb - max_exact)).astype(jnp.int32)
    large = jnp.minimum(large, nb - 1)
    return ret + jnp.where(n < max_exact, n, large)


def _dil_bias_kernel(tab_ref, o_ref, *, group, dilation, kwin, n_var):
    var = pl.program_id(0)
    pair = pl.program_id(1)
    kk = lax.broadcasted_iota(jnp.int32, (kwin, DIL_TQ), 0)
    qi = lax.broadcasted_iota(jnp.int32, (kwin, DIL_TQ), 1)
    rel_a = kk - qi - (DIL_HALF if n_var > 1 else 0)
    valid = jnp.abs(rel_a) <= DIL_HALF
    if n_var > 1:
        crossing = jnp.where((qi < DIL_HALF) == (kk < DIL_TQ), 0, var)
        valid = jnp.where(valid, 1, 0) - crossing == 1
    bucket = _t5_bucket(rel_a * dilation)
    for side in range(2):
        col = group * DIL_HEADS + 2 * pair + side
        acc = jnp.zeros((kwin, DIL_TQ), F32)
        for b in range(N_BUCKETS):
            acc = jnp.where(bucket == b, tab_ref[b, col], acc)
        o_ref[0, 0, :, side * DIL_TQ:(side + 1) * DIL_TQ] = jnp.where(valid, acc * LOG2E, MASKED)


def _dil_bias(rel_bias, group, seq):
    _, dilation = DIL_GROUPS[group]
    _, kwin, n_tiles = _dil_geometry(seq, dilation)
    n_var = 1 if n_tiles == 1 else 2
    n_pairs = DIL_HEADS // 2
    kern = functools.partial(_dil_bias_kernel, group=group, dilation=dilation, kwin=kwin, n_var=n_var)
    return pl.pallas_call(
        kern,
        grid=(n_var, n_pairs),
        in_specs=[pl.BlockSpec(memory_space=pltpu.SMEM)],
        out_specs=pl.BlockSpec((1, 1, kwin, 2 * DIL_TQ), lambda v, p: (v, p, 0, 0)),
        out_shape=jax.ShapeDtypeStruct((n_var, n_pairs, kwin, 2 * DIL_TQ), F32),
        compiler_params=_params(("arbitrary", "arbitrary")),
        name=f"dil_bias{group}",
    )(rel_bias)


DIL_PROJ_ROWS = 512


def _dil_proj_kernel(h_ref, wqf_ref, wkf_ref, wvf_ref, gq_ref, qT_ref, k_ref, vT_ref, wq_ref, wk_ref, wv_ref,
                     *scratch, dilation, rb, ra):
    hd = DIL_HEAD_DIM
    d_model = h_ref.shape[2]
    n_slabs = d_model // LANES
    mt = rb * ra

    @pl.when((pl.program_id(0) == 0) & (pl.program_id(1) == 0) & (pl.program_id(2) == 0))
    def _():
        wq_ref[...] = wqf_ref[0].T.astype(BF16)
        wk_ref[...] = wkf_ref[0].astype(BF16)
        wv_ref[...] = wvf_ref[0].T.astype(BF16)

    if dilation == 1:
        hcat = h_ref[0]
    else:
        (scr,) = scratch
        sub = pl.program_id(2)

        @pl.when(sub == 0)
        def _():
            for s in range(n_slabs):
                scr[s] = h_ref[0, :, s * LANES:(s + 1) * LANES].astype(F32)

        rows = []
        for j in range(rb):
            res = sub * rb + j
            rows.append(jnp.concatenate(
                [scr[s, pl.ds(res, ra, stride=dilation), :] for s in range(n_slabs)], axis=1))
        hcat = jnp.concatenate(rows, axis=0).astype(BF16)

    q_t = lax.dot_general(wq_ref[...], hcat, NT_DIMS, preferred_element_type=F32)
    gq = jnp.concatenate([gq_ref[...]] * (mt // LANES), axis=1)
    for h in range(DIL_HEADS):
        blk = q_t[h * hd:(h + 1) * hd]
        r = lax.rsqrt(jnp.sum(blk * blk, axis=0, keepdims=True) * (1.0 / hd) + EPS)
        qb = (blk * r * gq).astype(BF16)
        for j in range(rb):
            for s in range(ra // LANES):
                lo = j * ra + s * LANES
                qT_ref[0, j, s, h * hd:(h + 1) * hd, :] = qb[:, lo:lo + LANES]

    kf = jnp.dot(hcat, wk_ref[...], preferred_element_type=F32)
    lane = lax.broadcasted_iota(jnp.int32, (mt, LANES), 1)
    low = lane < hd
    for c in range(DIL_HEADS * hd // LANES):
        y = kf[:, c * LANES:(c + 1) * LANES]
        y2 = y * y
        s_lo = jnp.sum(jnp.where(low, y2, 0.0), axis=1, keepdims=True)
        s_hi = jnp.sum(jnp.where(low, 0.0, y2), axis=1, keepdims=True)
        r = lax.rsqrt(jnp.where(low, s_lo, s_hi) * (1.0 / hd) + EPS)
        kb = (y * r).astype(BF16)
        for j in range(rb):
            k_ref[0, j, :, c * LANES:(c + 1) * LANES] = kb[j * ra:(j + 1) * ra]

    v_t = lax.dot_general(wv_ref[...], hcat, NT_DIMS, preferred_element_type=F32).astype(BF16)
    for j in range(rb):
        for s in range(ra // LANES):
            lo = j * ra + s * LANES
            vT_ref[0, j, s] = v_t[:, lo:lo + LANES]


def _dil_proj(hn3, w_in, g_q, g_k, group):
    batch, seq, d_model = hn3.shape
    _, dilation = DIL_GROUPS[group]
    length = seq // dilation
    c = DIL_HEADS * DIL_HEAD_DIM
    ra = DIL_PROJ_ROWS if dilation == 1 else DIL_TQ
    rb = DIL_PROJ_ROWS // ra
    tok = ra * dilation
    n_sub = dilation // rb
    gq_col = jnp.broadcast_to((g_q * g_k * (DIL_HEAD_DIM ** -0.5 * LOG2E))[:, None], (DIL_HEAD_DIM, LANES))
    kern = functools.partial(_dil_proj_kernel, dilation=dilation, rb=rb, ra=ra)
    const = lambda shape: pl.BlockSpec(shape, lambda b, t, s: (0,) * len(shape))
    w_blk = lambda j: pl.BlockSpec((1, d_model, c), lambda b, t, s: (0, 0, 3 * group + j))
    scratch = [pltpu.VMEM((c, d_model), BF16), pltpu.VMEM((d_model, c), BF16), pltpu.VMEM((c, d_model), BF16)]
    if dilation > 1:
        scratch.append(pltpu.VMEM((d_model // LANES, tok, LANES), F32))
    return pl.pallas_call(
        kern,
        grid=(batch, seq // tok, n_sub),
        in_specs=[
            pl.BlockSpec((1, tok, d_model), lambda b, t, s: (b, t, 0)),
            w_blk(0), w_blk(1), w_blk(2),
            const((DIL_HEAD_DIM, LANES)),
        ],
        out_specs=[
            pl.BlockSpec((1, rb, ra // LANES, c, LANES), lambda b, t, s: (b, s, t, 0, 0)),
            pl.BlockSpec((1, rb, ra, c), lambda b, t, s: (b, s, t, 0)),
            pl.BlockSpec((1, rb, ra // LANES, c, LANES), lambda b, t, s: (b, s, t, 0, 0)),
        ],
        out_shape=[
            jax.ShapeDtypeStruct((batch, dilation, length // LANES, c, LANES), BF16),
            jax.ShapeDtypeStruct((batch, dilation, length, c), BF16),
            jax.ShapeDtypeStruct((batch, dilation, length // LANES, c, LANES), BF16),
        ],
        scratch_shapes=scratch,
        compiler_params=_params(("arbitrary", "arbitrary", "arbitrary")),
        name=f"dil_proj{group}",
    )(hn3, w_in, w_in, w_in, gq_col)


def _dil_attn_kernel(qT_ref, k_ref, vT_ref, bm_ref, o_ref, st_ref, s_ref, *, n_tiles, rb):
    tq = DIL_TQ
    hd = DIL_HEAD_DIM
    n_pairs = DIL_HEADS // 2
    c = DIL_HEADS * hd
    shifted = n_tiles > 1
    low = lax.broadcasted_iota(jnp.int32, (LANES, tq), 0) < hd
    ones = jnp.ones((ONES_ROWS, 2 * tq if shifted else tq), BF16)
    tiles = [(j, t) for j in range(rb) for t in range(n_tiles)]

    def score_tile(i):
        j, t = tiles[i]
        t1 = (t + 1) % n_tiles
        for p in range(n_pairs):
            rows = slice(p * LANES, (p + 1) * LANES)
            if shifted:
                q2 = jnp.concatenate([qT_ref[0, j, t, rows, DIL_HALF:], qT_ref[0, j, t1, rows, :DIL_HALF]], axis=1)
                k2 = jnp.concatenate([k_ref[0, j, t * tq:(t + 1) * tq, rows],
                                      k_ref[0, j, t1 * tq:(t1 + 1) * tq, rows]], axis=0)
                bias = bm_ref[1 if t + 1 == n_tiles else 0, p]
            else:
                q2 = qT_ref[0, j, 0, rows, :]
                k2 = k_ref[0, j, :, rows]
                bias = bm_ref[0, p]
            zero = jnp.zeros_like(q2)
            bd = jnp.concatenate([jnp.where(low, q2, zero), jnp.where(low, zero, q2)], axis=1)
            s_ref[i % 2, p] = jnp.dot(k2, bd, preferred_element_type=F32) + bias

    def attend_tile(i):
        j, t = tiles[i]
        t1 = (t + 1) % n_tiles

        def store(ref, lanes, val):
            if shifted:
                ref[0, t * tq + DIL_HALF:(t + 1) * tq, lanes] = val[:DIL_HALF]
                ref[0, t1 * tq:t1 * tq + DIL_HALF, lanes] = val[DIL_HALF:]
            else:
                ref[0, :, lanes] = val

        stats = []
        for p in range(n_pairs):
            rows = slice(p * LANES, (p + 1) * LANES)
            s_t = s_ref[i % 2, p]
            m = jnp.max(s_t, axis=0, keepdims=True)
            pb = jnp.exp2(s_t - m).astype(BF16)
            if shifted:
                v2 = jnp.concatenate([vT_ref[0, j, t, rows, :], vT_ref[0, j, t1, rows, :]], axis=1)
            else:
                v2 = vT_ref[0, j, 0, rows, :]
            res = jnp.dot(jnp.concatenate([v2, ones], axis=0), pb, preferred_element_type=F32)
            den = res[LANES:LANES + 1]
            rinv = 1.0 / den
            o_pair = jnp.concatenate([res[:hd, :tq] * rinv[:, :tq], res[hd:LANES, tq:] * rinv[:, tq:]], axis=0)
            store(o_ref, slice(j * c + p * LANES, j * c + (p + 1) * LANES), o_pair.T.astype(BF16))
            lse = (m + jnp.log2(den)) * LN2
            stats += [lse[:, :tq], lse[:, tq:]]
        st = jnp.concatenate(stats + [jnp.zeros((LANES - DIL_HEADS, tq), F32)], axis=0)
        store(st_ref, slice(j * LANES, (j + 1) * LANES), st.T)

    score_tile(0)
    for i in range(len(tiles)):
        if i + 1 < len(tiles):
            score_tile(i + 1)
        attend_tile(i)


def _dil_attn(q_t, k, v_t, bias, group, seq):
    batch = q_t.shape[0]
    _, dilation = DIL_GROUPS[group]
    length, kwin, n_tiles = _dil_geometry(seq, dilation)
    c = DIL_HEADS * DIL_HEAD_DIM
    n_slabs = length // LANES
    rb = 1 if n_tiles > 1 else min(dilation, 4)
    kern = functools.partial(_dil_attn_kernel, n_tiles=n_tiles, rb=rb)
    return pl.pallas_call(
        kern,
        grid=(batch, dilation // rb),
        in_specs=[
            pl.BlockSpec((1, rb, n_slabs, c, LANES), lambda b, r: (b, r, 0, 0, 0)),
            pl.BlockSpec((1, rb, length, c), lambda b, r: (b, r, 0, 0)),
            pl.BlockSpec((1, rb, n_slabs, c, LANES), lambda b, r: (b, r, 0, 0, 0)),
            pl.BlockSpec(bias.shape, lambda b, r: (0, 0, 0, 0)),
        ],
        out_specs=[
            pl.BlockSpec((1, length, rb * c), lambda b, r: (b, 0, r)),
            pl.BlockSpec((1, length, rb * LANES), lambda b, r: (b, 0, r)),
        ],
        out_shape=[
            jax.ShapeDtypeStruct((batch, length, dilation * c), BF16),
            jax.ShapeDtypeStruct((batch, length, dilation * LANES), F32),
        ],
        scratch_shapes=[pltpu.VMEM((2, DIL_HEADS // 2, kwin, 2 * DIL_TQ), F32)],
        compiler_params=_params(("arbitrary", "arbitrary")),
        name=f"dil_attn{group}",
    )(q_t, k, v_t, bias)


def _dil_out_kernel(o0_ref, o1_ref, o2_ref, s0_ref, s1_ref, s2_ref, e_ref, wo_ref, x_ref, mod_ref, nmlp_ref,
                    x1_ref, h2_ref, *scratch):
    tm = x_ref.shape[0]
    c = wo_ref.shape[0]

    def token_major(o_ref, s_ref, oscr, sscr, dilation):
        ra = tm // dilation
        for r in range(dilation):
            for s in range(c // LANES):
                lo = r * c + s * LANES
                oscr[s, pl.ds(r, ra, stride=dilation), :] = o_ref[0, :, lo:lo + LANES].astype(F32)
            sscr[pl.ds(r, ra, stride=dilation), :] = s_ref[0, :, r * LANES:(r + 1) * LANES]
        return jnp.concatenate([oscr[s] for s in range(c // LANES)], axis=1), sscr[...]

    o0, l0 = o0_ref[0].astype(F32), s0_ref[0]
    o1, l1 = token_major(o1_ref, s1_ref, scratch[0], scratch[1], DIL_GROUPS[1][1])
    o2, l2 = token_major(o2_ref, s2_ref, scratch[2], scratch[3], DIL_GROUPS[2][1])
    mx = jnp.maximum(jnp.maximum(l0, l1), l2)
    e0, e1, e2 = jnp.exp(l0 - mx), jnp.exp(l1 - mx), jnp.exp(l2 - mx)
    rden = 1.0 / (e0 + e1 + e2)
    e = e_ref[...]

    def expand(w):
        return jnp.dot(w.astype(BF16), e, preferred_element_type=F32)

    o = expand(e0 * rden) * o0 + expand(e1 * rden) * o1 + expand(e2 * rden) * o2
    y = jnp.dot(o.astype(BF16), wo_ref[...], preferred_element_type=F32)
    m = mod_ref[0]
    x1 = x_ref[...] + m[2:3] * y
    x1_ref[...] = x1
    h2_ref[...] = _ada_norm(x1, nmlp_ref[...], m[4:5], m[3:4]).astype(BF16)


def _dil_out(outs, stats, w_o, x2d, mod_l, nmlp, seq):
    t, d = x2d.shape
    c = w_o.shape[0]
    tm = 512
    tpb = seq // tm
    expand = jnp.repeat(jnp.eye(DIL_HEADS, dtype=F32), DIL_HEAD_DIM, axis=1)
    expand = jnp.pad(expand, ((0, LANES - DIL_HEADS), (0, 0))).astype(BF16)
    row = lambda w: pl.BlockSpec((tm, w), lambda i: (i, 0))
    grp = lambda g, w: pl.BlockSpec((1, tm // DIL_GROUPS[g][1], DIL_GROUPS[g][1] * w),
                                    lambda i: (i // tpb, i % tpb, 0))
    scratch = []
    for g in (1, 2):
        scratch += [pltpu.VMEM((c // LANES, tm, LANES), F32), pltpu.VMEM((tm, LANES), F32)]
    return pl.pallas_call(
        _dil_out_kernel,
        grid=(t // tm,),
        in_specs=[grp(0, c), grp(1, c), grp(2, c), grp(0, LANES), grp(1, LANES), grp(2, LANES),
                  pl.BlockSpec((LANES, c), lambda i: (0, 0)),
                  pl.BlockSpec((c, d), lambda i: (0, 0)),
                  row(d),
                  pl.BlockSpec((1, N_MOD, d), lambda i: (i // tpb, 0, 0)),
                  pl.BlockSpec((1, d), lambda i: (0, 0))],
        out_specs=[row(d), row(d)],
        out_shape=[jax.ShapeDtypeStruct((t, d), F32), jax.ShapeDtypeStruct((t, d), BF16)],
        scratch_shapes=scratch,
        compiler_params=_params(("arbitrary",)),
        name="dil_out",
    )(*outs, *stats, expand, w_o.astype(BF16), x2d, mod_l, nmlp[None, :])


def kernel(x, c, positions, ada_w, ada_b, norm_mix, norm_mlp, mlp_w1, mlp_w2, mla_w_in, mla_g_qa, mla_w_qb,
           mla_g_kva, mla_w_kvb, mla_g_q, mla_g_k, mla_w_o, dil_w_in, dil_g_q, dil_g_k, dil_w_o, rel_bias):
    batch, seq, d = x.shape
    depth = ada_w.shape[0]
    assert depth == 2 and len(DIL_GROUPS) == 3
    mod = _mod(c, ada_w, ada_b).reshape(depth, batch, N_MOD, d)
    x2d = x.reshape(batch * seq, d)

    n_heads = mla_w_qb.shape[2] // mla_g_q.shape[1]
    q_t, k_pad, v_t = _mla_proj(x2d, mod[0], norm_mix[0], positions, mla_w_in[0], mla_g_qa[0], mla_w_qb[0],
                                mla_g_kva[0], mla_w_kvb[0], mla_g_q[0], mla_g_k[0], batch, seq)
    o_t = _mla_attn(q_t, k_pad, v_t, n_heads)
    x1, h2 = _attn_out(o_t, mla_w_o[0], x2d, mod[0], norm_mlp[0], seq)
    x2, hn = _mlp(x1, h2, mlp_w1, mlp_w2, 0, mod[0], seq, norm_mix[1], mod[1])

    outs, stats = [], []
    hn3 = hn.reshape(batch, seq, d)
    for g in range(len(DIL_GROUPS)):
        q_g, k_g, v_g = _dil_proj(hn3, dil_w_in, dil_g_q[0, g], dil_g_k[0, g], g)
        o_g, st_g = _dil_attn(q_g, k_g, v_g, _dil_bias(rel_bias, g, seq), g, seq)
        outs.append(o_g)
        stats.append(st_g)
    x3, h4 = _dil_out(outs, stats, dil_w_o[0], x2, mod[1], norm_mlp[1], seq)
    (x4,) = _mlp(x3, h4, mlp_w1, mlp_w2, 1, mod[1], seq)
    return x4.reshape(batch, seq, d)
```

```python
import functools
import math

import jax
import jax.numpy as jnp
from jax import lax
from jax.experimental import pallas as pl
from jax.experimental.pallas import tpu as pltpu

F32 = jnp.float32
BF16 = jnp.bfloat16

EPS = 1e-6
LOG2E = 1.4426950408889634
LN2 = 0.6931471805599453
MASKED = -1e30

N_MOD = 6
ROPE_THETA = 10000.0
MLA_ROPE = 32
DIL_GROUPS = ((128, 1), (512, 4), (2048, 16))
DIL_HEADS = 16
DIL_HEAD_DIM = 64
N_BUCKETS = 32
MAX_DISTANCE = 1024

LANES = 128
HEAD_PAD = 128

VMEM_LIMIT = 56 * 1024 * 1024

NT_DIMS = (((1,), (1,)), ((), ()))
TN_DIMS = (((0,), (0,)), ((), ()))


def _params(sem):
    return pltpu.CompilerParams(dimension_semantics=sem, vmem_limit_bytes=VMEM_LIMIT)


def _ada_norm(x, g, scale, shift):
    ms = jnp.mean(x * x, axis=-1, keepdims=True)
    return (x * lax.rsqrt(ms + EPS) * g) * (1.0 + scale) + shift


def _rms(x, g):
    ms = jnp.mean(x * x, axis=-1, keepdims=True)
    return x * lax.rsqrt(ms + EPS) * g


def _mod_kernel(c_ref, w_ref, b_ref, o_ref):
    c = c_ref[...]
    cond = c / (1.0 + jnp.exp(-c))
    o_ref[0] = (
        jnp.dot(cond.astype(BF16), w_ref[0].astype(BF16), preferred_element_type=F32) + b_ref[0]
    )


def _mod(c, ada_w, ada_b):
    depth, d, n = ada_w.shape
    b = c.shape[0]
    tn = 1536
    return pl.pallas_call(
        _mod_kernel,
        grid=(depth, n // tn),
        in_specs=[
            pl.BlockSpec((b, d), lambda l, j: (0, 0)),
            pl.BlockSpec((1, d, tn), lambda l, j: (l, 0, j)),
            pl.BlockSpec((1, 1, tn), lambda l, j: (l, 0, j)),
        ],
        out_specs=pl.BlockSpec((1, b, tn), lambda l, j: (l, 0, j)),
        out_shape=jax.ShapeDtypeStruct((depth, b, n), F32),
        compiler_params=_params(("arbitrary", "arbitrary")),
        name="mod",
    )(c, ada_w, ada_b.reshape(depth, 1, n))


def _mla_proj_kernel(x_ref, mod_ref, nmix_ref, pos_ref, inv_ref, wlat_ref, wkr_ref, gqa_ref, gkva_ref,
                     wqb_ref, wkn_ref, wv_ref, gq_ref, qT_ref, k_ref, vT_ref, *, n_heads, q_lora, qk_dim):
    x = x_ref[...]
    tm = x.shape[0]
    m = mod_ref[0]
    hb = _ada_norm(x, nmix_ref[...], m[1:2], m[0:1]).astype(BF16)
    lat = jnp.dot(hb, wlat_ref[...], preferred_element_type=F32)
    qn = _rms(lat[:, :q_lora], gqa_ref[...]).astype(BF16)
    kvn = _rms(lat[:, q_lora:], gkva_ref[...]).astype(BF16)

    half = MLA_ROPE // 2
    nope = qk_dim - MLA_ROPE
    pos = pos_ref[...].astype(F32)
    ang = jnp.concatenate([inv_ref[...]] * (tm // LANES), axis=1) * pos
    cos_t = jnp.cos(ang)
    sin_t = jnp.sin(ang)

    def rope_rows(blk):
        x1 = blk[nope:nope + half]
        x2 = blk[nope + half:qk_dim]
        return jnp.concatenate(
            [blk[:nope], x1 * cos_t - x2 * sin_t, x2 * cos_t + x1 * sin_t, blk[qk_dim:]], axis=0)

    kr_t = lax.dot_general(wkr_ref[...], hb, NT_DIMS, preferred_element_type=F32)
    kr = rope_rows(kr_t).T

    q_t = lax.dot_general(wqb_ref[...], qn, NT_DIMS, preferred_element_type=F32)
    gq = jnp.concatenate([gq_ref[...]] * (tm // LANES), axis=1)
    for h in range(n_heads):
        blk = rope_rows(q_t[h * HEAD_PAD:(h + 1) * HEAD_PAD])
        ssq = jnp.sum(blk * blk, axis=0, keepdims=True)
        r = lax.rsqrt(ssq * (1.0 / qk_dim) + EPS)
        qT_ref[0, h * HEAD_PAD:(h + 1) * HEAD_PAD, :] = (blk * r * gq).astype(BF16)

    kn = jnp.dot(kvn, wkn_ref[...], preferred_element_type=F32)
    for h in range(n_heads):
        kb = kn[:, h * HEAD_PAD:(h + 1) * HEAD_PAD] + kr
        ssq = jnp.sum(kb * kb, axis=1, keepdims=True)
        r = lax.rsqrt(ssq * (1.0 / qk_dim) + EPS)
        k_ref[:, h * HEAD_PAD:(h + 1) * HEAD_PAD] = (kb * r).astype(BF16)

    vT_ref[0] = lax.dot_general(wv_ref[...], kvn, NT_DIMS, preferred_element_type=F32).astype(BF16)


def _mla_proj(x2d, mod_l, nmix, positions, w_in, g_qa, w_qb, g_kva, w_kvb, g_q, g_k, batch, seq):
    t, d = x2d.shape
    q_lora = g_qa.shape[0]
    kv_lora = g_kva.shape[0]
    qk_dim = g_q.shape[0]
    n_heads = w_qb.shape[1] // qk_dim
    nope = qk_dim - MLA_ROPE
    v_dim = w_kvb.shape[1] // n_heads - nope
    assert nope + v_dim == HEAD_PAD and qk_dim <= HEAD_PAD
    half = MLA_ROPE // 2
    tm = 512
    tpb = seq // tm

    w_lat = w_in[:, :q_lora + kv_lora].astype(BF16)
    w_kr = jnp.zeros((HEAD_PAD, d), F32).at[nope:qk_dim].set(w_in[:, q_lora + kv_lora:].T).astype(BF16)
    w_qb_t = jnp.pad(w_qb.T.reshape(n_heads, qk_dim, q_lora), ((0, 0), (0, HEAD_PAD - qk_dim), (0, 0)))
    w_qb_t = w_qb_t.reshape(n_heads * HEAD_PAD, q_lora).astype(BF16)
    w_kvb3 = w_kvb.reshape(kv_lora, n_heads, nope + v_dim)
    w_kn = jnp.where(jnp.arange(nope + v_dim) < nope, w_kvb3, 0.0).reshape(kv_lora, n_heads * HEAD_PAD).astype(BF16)
    w_v_t = w_kvb3[:, :, nope:].reshape(kv_lora, n_heads * v_dim).T.astype(BF16)
    q_gain = jnp.pad(g_q * g_k * (qk_dim ** -0.5 * LOG2E), (0, HEAD_PAD - qk_dim))
    gq_col = jnp.broadcast_to(q_gain[:, None], (HEAD_PAD, LANES))
    inv = 1.0 / (ROPE_THETA ** (jnp.arange(half, dtype=F32) / half))
    inv_tab = jnp.broadcast_to(inv[:, None], (half, LANES))
    pos_row = positions.reshape(1, t)

    full = lambda shape: pl.BlockSpec(shape, lambda i: (0,) * len(shape))
    kern = functools.partial(_mla_proj_kernel, n_heads=n_heads, q_lora=q_lora, qk_dim=qk_dim)
    return pl.pallas_call(
        kern,
        grid=(t // tm,),
        in_specs=[
            pl.BlockSpec((tm, d), lambda i: (i, 0)),
            pl.BlockSpec((1, N_MOD, d), lambda i: (i // tpb, 0, 0)),
            full((1, d)),
            pl.BlockSpec((1, tm), lambda i: (0, i)),
            full((half, LANES)),
            full(w_lat.shape), full(w_kr.shape), full((1, q_lora)), full((1, kv_lora)),
            full(w_qb_t.shape), full(w_kn.shape), full(w_v_t.shape),
            full((HEAD_PAD, LANES)),
        ],
        out_specs=[
            pl.BlockSpec((1, n_heads * HEAD_PAD, tm), lambda i: (i // tpb, 0, i % tpb)),
            pl.BlockSpec((tm, n_heads * HEAD_PAD), lambda i: (i, 0)),
            pl.BlockSpec((1, n_heads * v_dim, tm), lambda i: (i // tpb, 0, i % tpb)),
        ],
        out_shape=[
            jax.ShapeDtypeStruct((batch, n_heads * HEAD_PAD, seq), BF16),
            jax.ShapeDtypeStruct((t, n_heads * HEAD_PAD), BF16),
            jax.ShapeDtypeStruct((batch, n_heads * v_dim, seq), BF16),
        ],
        compiler_params=_params(("arbitrary",)),
        name="mla_proj",
    )(x2d, mod_l, nmix[None, :], pos_row, inv_tab, w_lat, w_kr, g_qa[None, :], g_kva[None, :],
      w_qb_t, w_kn, w_v_t, gq_col)


ONES_ROWS = 16


def _mla_attn_kernel(qT_ref, k_ref, vT_ref, oT_ref, s_ref, *, hb, tq, tn, kc):
    seq = k_ref.shape[1]
    v_dim = vT_ref.shape[1] // hb
    ones = jnp.ones((ONES_ROWS, seq), BF16)
    n_chunks = seq // kc
    n_cols = tq // tn
    tiles = [(h, i) for h in range(hb) for i in range(seq // tq)]

    def scores(t, c):
        h, i = tiles[t]
        s_ref[t % 2, c * kc:(c + 1) * kc, :] = jnp.dot(
            k_ref[0, c * kc:(c + 1) * kc, h * HEAD_PAD:(h + 1) * HEAD_PAD],
            qT_ref[0, h * HEAD_PAD:(h + 1) * HEAD_PAD, i * tq:(i + 1) * tq], preferred_element_type=F32)

    for c in range(n_chunks):
        scores(0, c)
    for t, (h, i) in enumerate(tiles):
        v_ext = jnp.concatenate([vT_ref[0, h * v_dim:(h + 1) * v_dim, :], ones], axis=0)
        for j in range(n_cols):
            m = jnp.full((1, tn), MASKED, F32)
            acc = jnp.zeros((v_dim + ONES_ROWS, tn), F32)
            for c in range(n_chunks):
                step = j * n_chunks + c
                if t + 1 < len(tiles) and step % n_cols == 0:
                    scores(t + 1, step // n_cols)
                s_c = s_ref[t % 2, c * kc:(c + 1) * kc, j * tn:(j + 1) * tn]
                m_new = jnp.maximum(m, jnp.max(s_c, axis=0, keepdims=True))
                p = jnp.exp2(s_c - m_new).astype(BF16)
                acc = acc * jnp.exp2(m - m_new) + jnp.dot(v_ext[:, c * kc:(c + 1) * kc], p,
                                                          preferred_element_type=F32)
                m = m_new
            o = acc[:v_dim] * (1.0 / acc[v_dim:v_dim + 1])
            oT_ref[0, h * v_dim:(h + 1) * v_dim, i * tq + j * tn:i * tq + (j + 1) * tn] = o.astype(BF16)


def _mla_attn(q_t, k_pad, v_t, n_heads):
    batch, _, seq = q_t.shape
    v_dim = v_t.shape[1] // n_heads
    k3 = k_pad.reshape(batch, seq, n_heads * HEAD_PAD)
    tq, kc = 512, 256
    hb = 2
    return pl.pallas_call(
        functools.partial(_mla_attn_kernel, hb=hb, tq=tq, tn=256, kc=kc),
        grid=(batch, n_heads // hb),
        in_specs=[
            pl.BlockSpec((1, hb * HEAD_PAD, seq), lambda b, h: (b, h, 0)),
            pl.BlockSpec((1, seq, hb * HEAD_PAD), lambda b, h: (b, 0, h)),
            pl.BlockSpec((1, hb * v_dim, seq), lambda b, h: (b, h, 0)),
        ],
        out_specs=pl.BlockSpec((1, hb * v_dim, seq), lambda b, h: (b, h, 0)),
        out_shape=jax.ShapeDtypeStruct((batch, n_heads * v_dim, seq), BF16),
        scratch_shapes=[pltpu.VMEM((2, seq, tq), F32)],
        compiler_params=_params(("arbitrary", "arbitrary")),
        name="mla_attn",
    )(q_t, k3, v_t)


def _attn_out_kernel(oT_ref, wo_ref, x_ref, mod_ref, nmlp_ref, x1_ref, h2_ref):
    m = mod_ref[0]
    tm = x_ref.shape[0]
    n_sub = 2
    blocks = [slice(i * (tm // n_sub), (i + 1) * (tm // n_sub)) for i in range(n_sub)]
    project = lambda rows: lax.dot_general(oT_ref[0, :, rows], wo_ref[...], TN_DIMS, preferred_element_type=F32)
    y = project(blocks[0])
    for i, rows in enumerate(blocks):
        y_next = project(blocks[i + 1]) if i + 1 < n_sub else None
        x1 = x_ref[rows, :] + m[2:3] * y
        x1_ref[rows, :] = x1
        h2_ref[rows, :] = _ada_norm(x1, nmlp_ref[...], m[4:5], m[3:4]).astype(BF16)
        y = y_next


def _attn_out(o_t, w_o, x2d, mod_l, nmlp, seq):
    t, d = x2d.shape
    tm = 512
    tpb = seq // tm
    c = o_t.shape[1]
    return pl.pallas_call(
        _attn_out_kernel,
        grid=(t // tm,),
        in_specs=[
            pl.BlockSpec((1, c, tm), lambda i: (i // tpb, 0, i % tpb)),
            pl.BlockSpec((c, d), lambda i: (0, 0)),
            pl.BlockSpec((tm, d), lambda i: (i, 0)),
            pl.BlockSpec((1, N_MOD, d), lambda i: (i // tpb, 0, 0)),
            pl.BlockSpec((1, d), lambda i: (0, 0)),
        ],
        out_specs=[pl.BlockSpec((tm, d), lambda i: (i, 0)), pl.BlockSpec((tm, d), lambda i: (i, 0))],
        out_shape=[jax.ShapeDtypeStruct((t, d), F32), jax.ShapeDtypeStruct((t, d), BF16)],
        compiler_params=_params(("arbitrary",)),
        name="attn_out",
    )(o_t, w_o.astype(BF16), x2d, mod_l, nmlp[None, :])


def _mlp_kernel(x_ref, h_ref, w1_ref, w2_ref, mod_ref, *rest, with_next):
    if with_next:
        nnext_ref, modn_ref, x2_ref, hn_ref, acc_ref = rest
    else:
        x2_ref, acc_ref = rest
    j = pl.program_id(1)

    @pl.when(j == 0)
    def _():
        acc_ref[...] = jnp.zeros_like(acc_ref)

    u = jnp.dot(h_ref[...], w1_ref[0].astype(BF16), preferred_element_type=F32)
    u = jnp.square(jnp.maximum(u, 0.0)).astype(BF16)
    acc_ref[...] += jnp.dot(u, w2_ref[0].astype(BF16), preferred_element_type=F32)

    @pl.when(j == pl.num_programs(1) - 1)
    def _():
        x2 = x_ref[...] + mod_ref[0][5:6] * acc_ref[...]
        x2_ref[...] = x2
        if with_next:
            mn = modn_ref[0]
            hn_ref[...] = _ada_norm(x2, nnext_ref[...], mn[1:2], mn[0:1]).astype(BF16)


def _mlp(x1, h2, w1, w2, layer, mod_l, seq, nnext=None, mod_next=None):
    t, d = x1.shape
    ff = w1.shape[2]
    tm, tf = 1024, 1024
    tpb = seq // tm
    with_next = nnext is not None
    row = pl.BlockSpec((tm, d), lambda i, j: (i, 0))
    mod_spec = pl.BlockSpec((1, N_MOD, d), lambda i, j: (i // tpb, 0, 0))
    in_specs = [row, row, pl.BlockSpec((1, d, tf), lambda i, j: (layer, 0, j)),
                pl.BlockSpec((1, tf, d), lambda i, j: (layer, j, 0)), mod_spec]
    args = [x1, h2, w1, w2, mod_l]
    out_specs = [row]
    out_shape = [jax.ShapeDtypeStruct((t, d), F32)]
    if with_next:
        in_specs += [pl.BlockSpec((1, d), lambda i, j: (0, 0)), mod_spec]
        args += [nnext[None, :], mod_next]
        out_specs.append(row)
        out_shape.append(jax.ShapeDtypeStruct((t, d), BF16))
    return pl.pallas_call(
        functools.partial(_mlp_kernel, with_next=with_next),
        grid=(t // tm, ff // tf),
        in_specs=in_specs,
        out_specs=out_specs,
        out_shape=out_shape,
        scratch_shapes=[pltpu.VMEM((tm, d), F32)],
        compiler_params=_params(("arbitrary", "arbitrary")),
        name="mlp",
    )(*args)


DIL_TQ = 128
DIL_HALF = 64
assert all(w // (2 * d) == DIL_HALF for w, d in DIL_GROUPS)


def _dil_geometry(seq, dilation):
    length = seq // dilation
    kwin = min(2 * DIL_TQ, length)
    return length, kwin, length // DIL_TQ


def _t5_log_thresholds():
    nb = N_BUCKETS // 2
    max_exact = nb // 2
    steps = nb - max_exact
    thresholds = []
    for k in range(1, steps):
        n = max_exact
        while math.floor(math.log(n / max_exact) / math.log(MAX_DISTANCE / max_exact) * steps) < k:
            n += 1
        thresholds.append(n)
    return thresholds


def _t5_bucket(rel):
    nb = N_BUCKETS // 2
    max_exact = nb // 2
    n = jnp.abs(rel)
    large = max_exact
    for thr in _t5_log_thresholds():
        large = large + jnp.where(n >= thr, 1, 0)
    return jnp.where(rel > 0, nb, 0) + jnp.where(n < max_exact, n, large)


BIAS_ROWS = 16


def _dil_bias_kernel(tab_ref, o_ref, *, group, dilation, kwin, n_var):
    var = pl.program_id(0)

    def rows(step, carry):
        r0 = pl.multiple_of(step * BIAS_ROWS, BIAS_ROWS)
        kk = lax.broadcasted_iota(jnp.int32, (BIAS_ROWS, DIL_TQ), 0) + r0
        qi = lax.broadcasted_iota(jnp.int32, (BIAS_ROWS, DIL_TQ), 1)
        rel_a = kk - qi - (DIL_HALF if n_var > 1 else 0)
        valid = jnp.abs(rel_a) <= DIL_HALF
        if n_var > 1:
            crossing = jnp.where((qi < DIL_HALF) == (kk < DIL_TQ), 0, var)
            valid = jnp.where(valid, 1, 0) - crossing == 1
        bucket = _t5_bucket(rel_a * dilation)
        accs = [jnp.zeros((BIAS_ROWS, DIL_TQ), F32)] * DIL_HEADS
        for b in range(N_BUCKETS):
            hit = bucket == b
            accs = [jnp.where(hit, tab_ref[b, group * DIL_HEADS + h], a) for h, a in enumerate(accs)]
        for h, a in enumerate(accs):
            o_ref[0, h // 2, pl.ds(r0, BIAS_ROWS), (h % 2) * DIL_TQ:(h % 2 + 1) * DIL_TQ] = jnp.where(
                valid, a * LOG2E, MASKED)
        return carry

    lax.fori_loop(0, kwin // BIAS_ROWS, rows, 0)


def _dil_bias(rel_bias, group, seq):
    _, dilation = DIL_GROUPS[group]
    _, kwin, n_tiles = _dil_geometry(seq, dilation)
    n_var = 1 if n_tiles == 1 else 2
    n_pairs = DIL_HEADS // 2
    kern = functools.partial(_dil_bias_kernel, group=group, dilation=dilation, kwin=kwin, n_var=n_var)
    return pl.pallas_call(
        kern,
        grid=(n_var,),
        in_specs=[pl.BlockSpec(memory_space=pltpu.SMEM)],
        out_specs=pl.BlockSpec((1, n_pairs, kwin, 2 * DIL_TQ), lambda v: (v, 0, 0, 0)),
        out_shape=jax.ShapeDtypeStruct((n_var, n_pairs, kwin, 2 * DIL_TQ), F32),
        compiler_params=_params(("arbitrary",)),
        name=f"dil_bias{group}",
    )(rel_bias)


DIL_PROJ_ROWS = 512


def _dil_proj_kernel(h_ref, wqf_ref, wkf_ref, wvf_ref, gq_ref, qT_ref, k_ref, vT_ref, wq_ref, wk_ref, wv_ref,
                     *scratch, dilation, rb, ra):
    hd = DIL_HEAD_DIM
    d_model = h_ref.shape[2]
    n_slabs = d_model // LANES
    mt = rb * ra

    @pl.when((pl.program_id(0) == 0) & (pl.program_id(1) == 0) & (pl.program_id(2) == 0))
    def _():
        wq_ref[...] = wqf_ref[0].T.astype(BF16)
        wk_ref[...] = wkf_ref[0].astype(BF16)
        wv_ref[...] = wvf_ref[0].T.astype(BF16)

    if dilation == 1:
        hcat = h_ref[0]
    else:
        (scr,) = scratch
        sub = pl.program_id(2)

        @pl.when(sub == 0)
        def _():
            for s in range(n_slabs):
                scr[s] = h_ref[0, :, s * LANES:(s + 1) * LANES].astype(F32)

        rows = []
        for j in range(rb):
            res = sub * rb + j
            rows.append(jnp.concatenate(
                [scr[s, pl.ds(res, ra, stride=dilation), :] for s in range(n_slabs)], axis=1))
        hcat = jnp.concatenate(rows, axis=0).astype(BF16)

    q_t = lax.dot_general(wq_ref[...], hcat, NT_DIMS, preferred_element_type=F32)
    gq = jnp.concatenate([gq_ref[...]] * (mt // LANES), axis=1)
    for h in range(DIL_HEADS):
        blk = q_t[h * hd:(h + 1) * hd]
        r = lax.rsqrt(jnp.sum(blk * blk, axis=0, keepdims=True) * (1.0 / hd) + EPS)
        qb = (blk * r * gq).astype(BF16)
        for j in range(rb):
            for s in range(ra // LANES):
                lo = j * ra + s * LANES
                qT_ref[0, j, s, h * hd:(h + 1) * hd, :] = qb[:, lo:lo + LANES]

    kf = jnp.dot(hcat, wk_ref[...], preferred_element_type=F32)
    lane = lax.broadcasted_iota(jnp.int32, (mt, LANES), 1)
    low = lane < hd
    for c in range(DIL_HEADS * hd // LANES):
        y = kf[:, c * LANES:(c + 1) * LANES]
        y2 = y * y
        s_lo = jnp.sum(jnp.where(low, y2, 0.0), axis=1, keepdims=True)
        s_hi = jnp.sum(jnp.where(low, 0.0, y2), axis=1, keepdims=True)
        r = lax.rsqrt(jnp.where(low, s_lo, s_hi) * (1.0 / hd) + EPS)
        kb = (y * r).astype(BF16)
        for j in range(rb):
            k_ref[0, j, :, c * LANES:(c + 1) * LANES] = kb[j * ra:(j + 1) * ra]

    v_t = lax.dot_general(wv_ref[...], hcat, NT_DIMS, preferred_element_type=F32).astype(BF16)
    for j in range(rb):
        for s in range(ra // LANES):
            lo = j * ra + s * LANES
            vT_ref[0, j, s] = v_t[:, lo:lo + LANES]


def _dil_proj(hn3, w_in, g_q, g_k, group):
    batch, seq, d_model = hn3.shape
    _, dilation = DIL_GROUPS[group]
    length = seq // dilation
    c = DIL_HEADS * DIL_HEAD_DIM
    ra = DIL_PROJ_ROWS if dilation == 1 else DIL_TQ
    rb = DIL_PROJ_ROWS // ra
    tok = ra * dilation
    n_sub = dilation // rb
    gq_col = jnp.broadcast_to((g_q * g_k * (DIL_HEAD_DIM ** -0.5 * LOG2E))[:, None], (DIL_HEAD_DIM, LANES))
    kern = functools.partial(_dil_proj_kernel, dilation=dilation, rb=rb, ra=ra)
    const = lambda shape: pl.BlockSpec(shape, lambda b, t, s: (0,) * len(shape))
    w_blk = lambda j: pl.BlockSpec((1, d_model, c), lambda b, t, s: (0, 0, 3 * group + j))
    scratch = [pltpu.VMEM((c, d_model), BF16), pltpu.VMEM((d_model, c), BF16), pltpu.VMEM((c, d_model), BF16)]
    if dilation > 1:
        scratch.append(pltpu.VMEM((d_model // LANES, tok, LANES), F32))
    return pl.pallas_call(
        kern,
        grid=(batch, seq // tok, n_sub),
        in_specs=[
            pl.BlockSpec((1, tok, d_model), lambda b, t, s: (b, t, 0)),
            w_blk(0), w_blk(1), w_blk(2),
            const((DIL_HEAD_DIM, LANES)),
        ],
        out_specs=[
            pl.BlockSpec((1, rb, ra // LANES, c, LANES), lambda b, t, s: (b, s, t, 0, 0)),
            pl.BlockSpec((1, rb, ra, c), lambda b, t, s: (b, s, t, 0)),
            pl.BlockSpec((1, rb, ra // LANES, c, LANES), lambda b, t, s: (b, s, t, 0, 0)),
        ],
        out_shape=[
            jax.ShapeDtypeStruct((batch, dilation, length // LANES, c, LANES), BF16),
            jax.ShapeDtypeStruct((batch, dilation, length, c), BF16),
            jax.ShapeDtypeStruct((batch, dilation, length // LANES, c, LANES), BF16),
        ],
        scratch_shapes=scratch,
        compiler_params=_params(("arbitrary", "arbitrary", "arbitrary")),
        name=f"dil_proj{group}",
    )(hn3, w_in, w_in, w_in, gq_col)


def _dil_attn_kernel(qT_ref, k_ref, vT_ref, bm_ref, o_ref, st_ref, s_ref, *, n_tiles, rb):
    tq = DIL_TQ
    hd = DIL_HEAD_DIM
    n_pairs = DIL_HEADS // 2
    c = DIL_HEADS * hd
    shifted = n_tiles > 1
    low = lax.broadcasted_iota(jnp.int32, (LANES, tq), 0) < hd
    ones = jnp.ones((ONES_ROWS, 2 * tq if shifted else tq), BF16)
    tiles = [(j, t) for j in range(rb) for t in range(n_tiles)]

    def score_tile(i):
        j, t = tiles[i]
        t1 = (t + 1) % n_tiles
        for p in range(n_pairs):
            rows = slice(p * LANES, (p + 1) * LANES)
            if shifted:
                q2 = jnp.concatenate([qT_ref[0, j, t, rows, DIL_HALF:], qT_ref[0, j, t1, rows, :DIL_HALF]], axis=1)
                k2 = jnp.concatenate([k_ref[0, j, t * tq:(t + 1) * tq, rows],
                                      k_ref[0, j, t1 * tq:(t1 + 1) * tq, rows]], axis=0)
            else:
                q2 = qT_ref[0, j, 0, rows, :]
                k2 = k_ref[0, j, :, rows]
            zero = jnp.zeros_like(q2)
            bd = jnp.concatenate([jnp.where(low, q2, zero), jnp.where(low, zero, q2)], axis=1)
            s_ref[i % 2, p] = jnp.dot(k2, bd, preferred_element_type=F32)

    def attend_tile(i):
        j, t = tiles[i]
        t1 = (t + 1) % n_tiles

        def store(ref, lanes, val):
            if shifted:
                ref[0, t * tq + DIL_HALF:(t + 1) * tq, lanes] = val[:DIL_HALF]
                ref[0, t1 * tq:t1 * tq + DIL_HALF, lanes] = val[DIL_HALF:]
            else:
                ref[0, :, lanes] = val

        stats = []
        for p in range(n_pairs):
            rows = slice(p * LANES, (p + 1) * LANES)
            var = 1 if shifted and t + 1 == n_tiles else 0
            ms, ps = [], []
            for side in range(2):
                cols = slice(side * tq, (side + 1) * tq)
                s_h = s_ref[i % 2, p, :, cols] + bm_ref[var, p, :, cols]
                ms.append(jnp.max(s_h, axis=0, keepdims=True))
                ps.append(jnp.exp2(s_h - ms[-1]).astype(BF16))
            m = jnp.concatenate(ms, axis=1)
            pb = jnp.concatenate(ps, axis=1)
            if shifted:
                v2 = jnp.concatenate([vT_ref[0, j, t, rows, :], vT_ref[0, j, t1, rows, :]], axis=1)
            else:
                v2 = vT_ref[0, j, 0, rows, :]
            res = jnp.dot(jnp.concatenate([v2, ones], axis=0), pb, preferred_element_type=F32)
            den = res[LANES:LANES + 1]
            rinv = 1.0 / den
            o_pair = jnp.concatenate([res[:hd, :tq] * rinv[:, :tq], res[hd:LANES, tq:] * rinv[:, tq:]], axis=0)
            store(o_ref, slice(j * c + p * LANES, j * c + (p + 1) * LANES), o_pair.T.astype(BF16))
            lse = (m + jnp.log2(den)) * LN2
            stats += [lse[:, :tq], lse[:, tq:]]
        st = jnp.concatenate(stats + [jnp.zeros((LANES - DIL_HEADS, tq), F32)], axis=0)
        store(st_ref, slice(j * LANES, (j + 1) * LANES), st.T)

    score_tile(0)
    for i in range(len(tiles)):
        if i + 1 < len(tiles):
            score_tile(i + 1)
        attend_tile(i)


def _dil_attn(q_t, k, v_t, bias, group, seq):
    batch = q_t.shape[0]
    _, dilation = DIL_GROUPS[group]
    length, kwin, n_tiles = _dil_geometry(seq, dilation)
    c = DIL_HEADS * DIL_HEAD_DIM
    n_slabs = length // LANES
    rb = 1 if n_tiles > 1 else min(dilation, 4)
    kern = functools.partial(_dil_attn_kernel, n_tiles=n_tiles, rb=rb)
    return pl.pallas_call(
        kern,
        grid=(batch, dilation // rb),
        in_specs=[
            pl.BlockSpec((1, rb, n_slabs, c, LANES), lambda b, r: (b, r, 0, 0, 0)),
            pl.BlockSpec((1, rb, length, c), lambda b, r: (b, r, 0, 0)),
            pl.BlockSpec((1, rb, n_slabs, c, LANES), lambda b, r: (b, r, 0, 0, 0)),
            pl.BlockSpec(bias.shape, lambda b, r: (0, 0, 0, 0)),
        ],
        out_specs=[
            pl.BlockSpec((1, length, rb * c), lambda b, r: (b, 0, r)),
            pl.BlockSpec((1, length, rb * LANES), lambda b, r: (b, 0, r)),
        ],
        out_shape=[
            jax.ShapeDtypeStruct((batch, length, dilation * c), BF16),
            jax.ShapeDtypeStruct((batch, length, dilation * LANES), F32),
        ],
        scratch_shapes=[pltpu.VMEM((2, DIL_HEADS // 2, kwin, 2 * DIL_TQ), F32)],
        compiler_params=_params(("arbitrary", "arbitrary")),
        name=f"dil_attn{group}",
    )(q_t, k, v_t, bias)


def _dil_out_kernel(o0_ref, o1_ref, o2_ref, s0_ref, s1_ref, s2_ref, e_ref, wo_ref, x_ref, mod_ref, nmlp_ref,
                    x1_ref, h2_ref, *scratch):
    tm = x_ref.shape[0]
    c = wo_ref.shape[0]

    def token_major(o_ref, s_ref, oscr, sscr, dilation):
        ra = tm // dilation
        for r in range(dilation):
            for s in range(c // LANES):
                lo = r * c + s * LANES
                oscr[s, pl.ds(r, ra, stride=dilation), :] = o_ref[0, :, lo:lo + LANES].astype(F32)
            sscr[pl.ds(r, ra, stride=dilation), :] = s_ref[0, :, r * LANES:(r + 1) * LANES]

    token_major(o1_ref, s1_ref, scratch[0], scratch[1], DIL_GROUPS[1][1])
    token_major(o2_ref, s2_ref, scratch[2], scratch[3], DIL_GROUPS[2][1])
    e = e_ref[...]
    m = mod_ref[0]

    def expand(w):
        return jnp.dot(w.astype(BF16), e, preferred_element_type=F32)

    def merged(rows):
        gather = lambda oscr: jnp.concatenate([oscr[s, rows, :] for s in range(c // LANES)], axis=1)
        o0, l0 = o0_ref[0, rows, :].astype(F32), s0_ref[0, rows, :]
        o1, l1 = gather(scratch[0]), scratch[1][rows, :]
        o2, l2 = gather(scratch[2]), scratch[3][rows, :]
        mx = jnp.maximum(jnp.maximum(l0, l1), l2)
        e0, e1, e2 = jnp.exp(l0 - mx), jnp.exp(l1 - mx), jnp.exp(l2 - mx)
        rden = 1.0 / (e0 + e1 + e2)
        return (expand(e0 * rden) * o0 + expand(e1 * rden) * o1 + expand(e2 * rden) * o2).astype(BF16)

    def project(rows, o):
        y = jnp.dot(o, wo_ref[...], preferred_element_type=F32)
        x1 = x_ref[rows, :] + m[2:3] * y
        x1_ref[rows, :] = x1
        h2_ref[rows, :] = _ada_norm(x1, nmlp_ref[...], m[4:5], m[3:4]).astype(BF16)

    n_sub = 2
    blocks = [slice(i * (tm // n_sub), (i + 1) * (tm // n_sub)) for i in range(n_sub)]
    o = merged(blocks[0])
    for i, rows in enumerate(blocks):
        o_next = merged(blocks[i + 1]) if i + 1 < n_sub else None
        project(rows, o)
        o = o_next


def _dil_out(outs, stats, w_o, x2d, mod_l, nmlp, seq):
    t, d = x2d.shape
    c = w_o.shape[0]
    tm = 512
    tpb = seq // tm
    expand = jnp.repeat(jnp.eye(DIL_HEADS, dtype=F32), DIL_HEAD_DIM, axis=1)
    expand = jnp.pad(expand, ((0, LANES - DIL_HEADS), (0, 0))).astype(BF16)
    row = lambda w: pl.BlockSpec((tm, w), lambda i: (i, 0))
    grp = lambda g, w: pl.BlockSpec((1, tm // DIL_GROUPS[g][1], DIL_GROUPS[g][1] * w),
                                    lambda i: (i // tpb, i % tpb, 0))
    scratch = []
    for g in (1, 2):
        scratch += [pltpu.VMEM((c // LANES, tm, LANES), F32), pltpu.VMEM((tm, LANES), F32)]
    return pl.pallas_call(
        _dil_out_kernel,
        grid=(t // tm,),
        in_specs=[grp(0, c), grp(1, c), grp(2, c), grp(0, LANES), grp(1, LANES), grp(2, LANES),
                  pl.BlockSpec((LANES, c), lambda i: (0, 0)),
                  pl.BlockSpec((c, d), lambda i: (0, 0)),
                  row(d),
                  pl.BlockSpec((1, N_MOD, d), lambda i: (i // tpb, 0, 0)),
                  pl.BlockSpec((1, d), lambda i: (0, 0))],
        out_specs=[row(d), row(d)],
        out_shape=[jax.ShapeDtypeStruct((t, d), F32), jax.ShapeDtypeStruct((t, d), BF16)],
        scratch_shapes=scratch,
        compiler_params=_params(("arbitrary",)),
        name="dil_out",
    )(*outs, *stats, expand, w_o.astype(BF16), x2d, mod_l, nmlp[None, :])


def kernel(x, c, positions, ada_w, ada_b, norm_mix, norm_mlp, mlp_w1, mlp_w2, mla_w_in, mla_g_qa, mla_w_qb,
           mla_g_kva, mla_w_kvb, mla_g_q, mla_g_k, mla_w_o, dil_w_in, dil_g_q, dil_g_k, dil_w_o, rel_bias):
    batch, seq, d = x.shape
    depth = ada_w.shape[0]
    assert depth == 2 and len(DIL_GROUPS) == 3
    mod = _mod(c, ada_w, ada_b).reshape(depth, batch, N_MOD, d)
    x2d = x.reshape(batch * seq, d)

    n_heads = mla_w_qb.shape[2] // mla_g_q.shape[1]
    q_t, k_pad, v_t = _mla_proj(x2d, mod[0], norm_mix[0], positions, mla_w_in[0], mla_g_qa[0], mla_w_qb[0],
                                mla_g_kva[0], mla_w_kvb[0], mla_g_q[0], mla_g_k[0], batch, seq)
    o_t = _mla_attn(q_t, k_pad, v_t, n_heads)
    x1, h2 = _attn_out(o_t, mla_w_o[0], x2d, mod[0], norm_mlp[0], seq)
    x2, hn = _mlp(x1, h2, mlp_w1, mlp_w2, 0, mod[0], seq, norm_mix[1], mod[1])

    outs, stats = [], []
    hn3 = hn.reshape(batch, seq, d)
    for g in range(len(DIL_GROUPS)):
        q_g, k_g, v_g = _dil_proj(hn3, dil_w_in, dil_g_q[0, g], dil_g_k[0, g], g)
        o_g, st_g = _dil_attn(q_g, k_g, v_g, _dil_bias(rel_bias, g, seq), g, seq)
        outs.append(o_g)
        stats.append(st_g)
    x3, h4 = _dil_out(outs, stats, dil_w_o[0], x2, mod[1], norm_mlp[1], seq)
    (x4,) = _mlp(x3, h4, mlp_w1, mlp_w2, 1, mod[1], seq)
    return x4.reshape(batch, seq, d)
```

```python
import functools
import math

import jax
import jax.numpy as jnp
from jax import lax
from jax.experimental import pallas as pl
from jax.experimental.pallas import tpu as pltpu

F32 = jnp.float32
BF16 = jnp.bfloat16

EPS = 1e-6
LOG2E = 1.4426950408889634
LN2 = 0.6931471805599453
MASKED = -1e30

N_MOD = 6
ROPE_THETA = 10000.0
MLA_ROPE = 32
DIL_GROUPS = ((128, 1), (512, 4), (2048, 16))
DIL_HEADS = 16
DIL_HEAD_DIM = 64
N_BUCKETS = 32
MAX_DISTANCE = 1024

LANES = 128
HEAD_PAD = 128

VMEM_LIMIT = 56 * 1024 * 1024

NT_DIMS = (((1,), (1,)), ((), ()))
TN_DIMS = (((0,), (0,)), ((), ()))


def _params(sem):
    return pltpu.CompilerParams(dimension_semantics=sem, vmem_limit_bytes=VMEM_LIMIT)


def _ada_norm(x, g, scale, shift):
    ms = jnp.mean(x * x, axis=-1, keepdims=True)
    return (x * lax.rsqrt(ms + EPS) * g) * (1.0 + scale) + shift


def _rms(x, g):
    ms = jnp.mean(x * x, axis=-1, keepdims=True)
    return x * lax.rsqrt(ms + EPS) * g


def _mod_kernel(c_ref, w_ref, b_ref, o_ref):
    c = c_ref[...]
    cond = c / (1.0 + jnp.exp(-c))
    o_ref[0] = (
        jnp.dot(cond.astype(BF16), w_ref[0].astype(BF16), preferred_element_type=F32) + b_ref[0]
    )


def _mod(c, ada_w, ada_b):
    depth, d, n = ada_w.shape
    b = c.shape[0]
    tn = 1536
    return pl.pallas_call(
        _mod_kernel,
        grid=(depth, n // tn),
        in_specs=[
            pl.BlockSpec((b, d), lambda l, j: (0, 0)),
            pl.BlockSpec((1, d, tn), lambda l, j: (l, 0, j)),
            pl.BlockSpec((1, 1, tn), lambda l, j: (l, 0, j)),
        ],
        out_specs=pl.BlockSpec((1, b, tn), lambda l, j: (l, 0, j)),
        out_shape=jax.ShapeDtypeStruct((depth, b, n), F32),
        compiler_params=_params(("arbitrary", "arbitrary")),
        name="mod",
    )(c, ada_w, ada_b.reshape(depth, 1, n))


def _mla_proj_kernel(x_ref, mod_ref, nmix_ref, pos_ref, inv_ref, wlat_ref, wkr_ref, gqa_ref, gkva_ref,
                     wqb_ref, wkn_ref, wv_ref, gq_ref, qT_ref, k_ref, vT_ref, *, n_heads, q_lora, qk_dim):
    x = x_ref[...]
    tm = x.shape[0]
    m = mod_ref[0]
    hb = _ada_norm(x, nmix_ref[...], m[1:2], m[0:1]).astype(BF16)
    lat = jnp.dot(hb, wlat_ref[...], preferred_element_type=F32)
    qn = _rms(lat[:, :q_lora], gqa_ref[...]).astype(BF16)
    kvn = _rms(lat[:, q_lora:], gkva_ref[...]).astype(BF16)

    half = MLA_ROPE // 2
    nope = qk_dim - MLA_ROPE
    pos = pos_ref[...].astype(F32)
    ang = jnp.concatenate([inv_ref[...]] * (tm // LANES), axis=1) * pos
    cos_t = jnp.cos(ang)
    sin_t = jnp.sin(ang)

    def rope_rows(blk):
        x1 = blk[nope:nope + half]
        x2 = blk[nope + half:qk_dim]
        return jnp.concatenate(
            [blk[:nope], x1 * cos_t - x2 * sin_t, x2 * cos_t + x1 * sin_t, blk[qk_dim:]], axis=0)

    kr_t = lax.dot_general(wkr_ref[...], hb, NT_DIMS, preferred_element_type=F32)
    kr = rope_rows(kr_t).T

    q_t = lax.dot_general(wqb_ref[...], qn, NT_DIMS, preferred_element_type=F32)
    gq = jnp.concatenate([gq_ref[...]] * (tm // LANES), axis=1)
    for h in range(n_heads):
        blk = rope_rows(q_t[h * HEAD_PAD:(h + 1) * HEAD_PAD])
        ssq = jnp.sum(blk * blk, axis=0, keepdims=True)
        r = lax.rsqrt(ssq * (1.0 / qk_dim) + EPS)
        qT_ref[0, 0, h * HEAD_PAD:(h + 1) * HEAD_PAD, :] = (blk * r * gq).astype(BF16)

    kn = jnp.dot(kvn, wkn_ref[...], preferred_element_type=F32)
    for h in range(n_heads):
        kb = kn[:, h * HEAD_PAD:(h + 1) * HEAD_PAD] + kr
        ssq = jnp.sum(kb * kb, axis=1, keepdims=True)
        r = lax.rsqrt(ssq * (1.0 / qk_dim) + EPS)
        k_ref[:, h * HEAD_PAD:(h + 1) * HEAD_PAD] = (kb * r).astype(BF16)

    vT_ref[0] = lax.dot_general(wv_ref[...], kvn, NT_DIMS, preferred_element_type=F32).astype(BF16)


def _mla_proj(x2d, mod_l, nmix, positions, w_in, g_qa, w_qb, g_kva, w_kvb, g_q, g_k, batch, seq):
    t, d = x2d.shape
    q_lora = g_qa.shape[0]
    kv_lora = g_kva.shape[0]
    qk_dim = g_q.shape[0]
    n_heads = w_qb.shape[1] // qk_dim
    nope = qk_dim - MLA_ROPE
    v_dim = w_kvb.shape[1] // n_heads - nope
    assert nope + v_dim == HEAD_PAD and qk_dim <= HEAD_PAD
    half = MLA_ROPE // 2
    tm = 512
    tpb = seq // tm

    w_lat = w_in[:, :q_lora + kv_lora].astype(BF16)
    w_kr = jnp.zeros((HEAD_PAD, d), F32).at[nope:qk_dim].set(w_in[:, q_lora + kv_lora:].T).astype(BF16)
    w_qb_t = jnp.pad(w_qb.T.reshape(n_heads, qk_dim, q_lora), ((0, 0), (0, HEAD_PAD - qk_dim), (0, 0)))
    w_qb_t = w_qb_t.reshape(n_heads * HEAD_PAD, q_lora).astype(BF16)
    w_kvb3 = w_kvb.reshape(kv_lora, n_heads, nope + v_dim)
    w_kn = jnp.where(jnp.arange(nope + v_dim) < nope, w_kvb3, 0.0).reshape(kv_lora, n_heads * HEAD_PAD).astype(BF16)
    w_v_t = w_kvb3[:, :, nope:].reshape(kv_lora, n_heads * v_dim).T.astype(BF16)
    q_gain = jnp.pad(g_q * g_k * (qk_dim ** -0.5 * LOG2E), (0, HEAD_PAD - qk_dim))
    gq_col = jnp.broadcast_to(q_gain[:, None], (HEAD_PAD, LANES))
    inv = 1.0 / (ROPE_THETA ** (jnp.arange(half, dtype=F32) / half))
    inv_tab = jnp.broadcast_to(inv[:, None], (half, LANES))
    pos_row = positions.reshape(1, t)

    full = lambda shape: pl.BlockSpec(shape, lambda i: (0,) * len(shape))
    kern = functools.partial(_mla_proj_kernel, n_heads=n_heads, q_lora=q_lora, qk_dim=qk_dim)
    return pl.pallas_call(
        kern,
        grid=(t // tm,),
        in_specs=[
            pl.BlockSpec((tm, d), lambda i: (i, 0)),
            pl.BlockSpec((1, N_MOD, d), lambda i: (i // tpb, 0, 0)),
            full((1, d)),
            pl.BlockSpec((1, tm), lambda i: (0, i)),
            full((half, LANES)),
            full(w_lat.shape), full(w_kr.shape), full((1, q_lora)), full((1, kv_lora)),
            full(w_qb_t.shape), full(w_kn.shape), full(w_v_t.shape),
            full((HEAD_PAD, LANES)),
        ],
        out_specs=[
            pl.BlockSpec((1, 1, n_heads * HEAD_PAD, tm), lambda i: (i // tpb, i % tpb, 0, 0)),
            pl.BlockSpec((tm, n_heads * HEAD_PAD), lambda i: (i, 0)),
            pl.BlockSpec((1, n_heads * v_dim, tm), lambda i: (i // tpb, 0, i % tpb)),
        ],
        out_shape=[
            jax.ShapeDtypeStruct((batch, tpb, n_heads * HEAD_PAD, tm), BF16),
            jax.ShapeDtypeStruct((t, n_heads * HEAD_PAD), BF16),
            jax.ShapeDtypeStruct((batch, n_heads * v_dim, seq), BF16),
        ],
        compiler_params=_params(("arbitrary",)),
        name="mla_proj",
    )(x2d, mod_l, nmix[None, :], pos_row, inv_tab, w_lat, w_kr, g_qa[None, :], g_kva[None, :],
      w_qb_t, w_kn, w_v_t, gq_col)


ONES_ROWS = 16


SAFE_EXPONENT = 60.0


def _score_bound_is_safe(gain_product, head_dim, bias=None):
    bound = 1.02 * math.sqrt(head_dim) * LOG2E * jnp.max(jnp.abs(gain_product))
    if bias is not None:
        bound = bound + LOG2E * jnp.max(jnp.abs(bias))
    return (bound <= SAFE_EXPONENT).astype(jnp.int32).reshape(1)


def _mla_attn_kernel(safe_ref, qT_ref, k_ref, vT_ref, oT_ref, s_ref, *, hb, tn, kc):
    seq = k_ref.shape[1]
    n_tiles, _, tq = qT_ref.shape[1:]
    v_dim = vT_ref.shape[1] // hb
    ones = jnp.ones((ONES_ROWS, seq), BF16)
    n_chunks = seq // kc
    n_cols = tq // tn
    hrows = lambda h: slice(h * HEAD_PAD, (h + 1) * HEAD_PAD)
    vrows = lambda h: slice(h * v_dim, (h + 1) * v_dim)
    v_ext = lambda h: jnp.concatenate([vT_ref[0, vrows(h), :], ones], axis=0)

    @pl.when(safe_ref[0] == 1)
    def _():
        tiles = [(h, i) for h in range(hb) for i in range(n_tiles)]

        def scores(t, c):
            h, i = tiles[t]
            s_ref[t % 2, c * kc:(c + 1) * kc, :] = jnp.dot(
                k_ref[0, c * kc:(c + 1) * kc, hrows(h)], qT_ref[0, i, hrows(h), :], preferred_element_type=F32)

        for c in range(n_chunks):
            scores(0, c)
        for t, (h, i) in enumerate(tiles):
            ve = v_ext(h)
            for j in range(n_cols):
                acc = jnp.zeros((v_dim + ONES_ROWS, tn), F32)
                for c in range(n_chunks):
                    step = j * n_chunks + c
                    if t + 1 < len(tiles) and step % n_cols == 0:
                        scores(t + 1, step // n_cols)
                    p = jnp.exp2(s_ref[t % 2, c * kc:(c + 1) * kc, j * tn:(j + 1) * tn]).astype(BF16)
                    acc = acc + jnp.dot(ve[:, c * kc:(c + 1) * kc], p, preferred_element_type=F32)
                o = acc[:v_dim] * (1.0 / acc[v_dim:v_dim + 1])
                oT_ref[0, i, vrows(h), j * tn:(j + 1) * tn] = o.astype(BF16)

    @pl.when(safe_ref[0] != 1)
    def _():
        for h in range(hb):
            ve = v_ext(h)
            k_h = k_ref[0, :, hrows(h)]

            def tile(i, carry):
                s_t = jnp.dot(k_h, qT_ref[0, i, hrows(h), :], preferred_element_type=F32)
                p = jnp.exp2(s_t - jnp.max(s_t, axis=0, keepdims=True)).astype(BF16)
                r = jnp.dot(ve, p, preferred_element_type=F32)
                oT_ref[0, i, vrows(h), :] = (r[:v_dim] * (1.0 / r[v_dim:v_dim + 1])).astype(BF16)
                return carry

            lax.fori_loop(0, n_tiles, tile, 0)


def _mla_attn(q_t, k_pad, v_t, safe, n_heads):
    batch, n_tiles, _, tq = q_t.shape
    seq = n_tiles * tq
    v_dim = v_t.shape[1] // n_heads
    k3 = k_pad.reshape(batch, seq, n_heads * HEAD_PAD)
    hb = 2
    return pl.pallas_call(
        functools.partial(_mla_attn_kernel, hb=hb, tn=256, kc=256),
        grid=(batch, n_heads // hb),
        in_specs=[
            pl.BlockSpec(memory_space=pltpu.SMEM),
            pl.BlockSpec((1, n_tiles, hb * HEAD_PAD, tq), lambda b, h: (b, 0, h, 0)),
            pl.BlockSpec((1, seq, hb * HEAD_PAD), lambda b, h: (b, 0, h)),
            pl.BlockSpec((1, hb * v_dim, seq), lambda b, h: (b, h, 0)),
        ],
        out_specs=pl.BlockSpec((1, n_tiles, hb * v_dim, tq), lambda b, h: (b, 0, h, 0)),
        out_shape=jax.ShapeDtypeStruct((batch, n_tiles, n_heads * v_dim, tq), BF16),
        scratch_shapes=[pltpu.VMEM((2, seq, tq), F32)],
        compiler_params=_params(("arbitrary", "arbitrary")),
        name="mla_attn",
    )(safe, q_t, k3, v_t)


def _attn_out_kernel(oT_ref, wo_ref, x_ref, mod_ref, nmlp_ref, x1_ref, h2_ref):
    m = mod_ref[0]
    tm = x_ref.shape[0]
    n_sub = 2
    blocks = [slice(i * (tm // n_sub), (i + 1) * (tm // n_sub)) for i in range(n_sub)]
    project = lambda rows: lax.dot_general(oT_ref[0, 0, :, rows], wo_ref[...], TN_DIMS, preferred_element_type=F32)
    y = project(blocks[0])
    for i, rows in enumerate(blocks):
        y_next = project(blocks[i + 1]) if i + 1 < n_sub else None
        x1 = x_ref[rows, :] + m[2:3] * y
        x1_ref[rows, :] = x1
        h2_ref[rows, :] = _ada_norm(x1, nmlp_ref[...], m[4:5], m[3:4]).astype(BF16)
        y = y_next


def _attn_out(o_t, w_o, x2d, mod_l, nmlp, seq):
    t, d = x2d.shape
    tm = 512
    tpb = seq // tm
    c = o_t.shape[2]
    return pl.pallas_call(
        _attn_out_kernel,
        grid=(t // tm,),
        in_specs=[
            pl.BlockSpec((1, 1, c, tm), lambda i: (i // tpb, i % tpb, 0, 0)),
            pl.BlockSpec((c, d), lambda i: (0, 0)),
            pl.BlockSpec((tm, d), lambda i: (i, 0)),
            pl.BlockSpec((1, N_MOD, d), lambda i: (i // tpb, 0, 0)),
            pl.BlockSpec((1, d), lambda i: (0, 0)),
        ],
        out_specs=[pl.BlockSpec((tm, d), lambda i: (i, 0)), pl.BlockSpec((tm, d), lambda i: (i, 0))],
        out_shape=[jax.ShapeDtypeStruct((t, d), F32), jax.ShapeDtypeStruct((t, d), BF16)],
        compiler_params=_params(("arbitrary",)),
        name="attn_out",
    )(o_t, w_o.astype(BF16), x2d, mod_l, nmlp[None, :])


def _mlp_kernel(x_ref, h_ref, w1_ref, w2_ref, mod_ref, *rest, with_next):
    if with_next:
        nnext_ref, modn_ref, x2_ref, hn_ref, acc_ref = rest
    else:
        x2_ref, acc_ref = rest
    j = pl.program_id(1)

    @pl.when(j == 0)
    def _():
        acc_ref[...] = jnp.zeros_like(acc_ref)

    u = jnp.dot(h_ref[...], w1_ref[0].astype(BF16), preferred_element_type=F32)
    u = jnp.square(jnp.maximum(u, 0.0)).astype(BF16)
    acc_ref[...] += jnp.dot(u, w2_ref[0].astype(BF16), preferred_element_type=F32)

    @pl.when(j == pl.num_programs(1) - 1)
    def _():
        x2 = x_ref[...] + mod_ref[0][5:6] * acc_ref[...]
        x2_ref[...] = x2
        if with_next:
            mn = modn_ref[0]
            hn_ref[...] = _ada_norm(x2, nnext_ref[...], mn[1:2], mn[0:1]).astype(BF16)


def _mlp(x1, h2, w1, w2, layer, mod_l, seq, nnext=None, mod_next=None):
    t, d = x1.shape
    ff = w1.shape[2]
    tm, tf = 1024, 1024
    tpb = seq // tm
    with_next = nnext is not None
    row = pl.BlockSpec((tm, d), lambda i, j: (i, 0))
    mod_spec = pl.BlockSpec((1, N_MOD, d), lambda i, j: (i // tpb, 0, 0))
    in_specs = [row, row, pl.BlockSpec((1, d, tf), lambda i, j: (layer, 0, j)),
                pl.BlockSpec((1, tf, d), lambda i, j: (layer, j, 0)), mod_spec]
    args = [x1, h2, w1, w2, mod_l]
    out_specs = [row]
    out_shape = [jax.ShapeDtypeStruct((t, d), F32)]
    if with_next:
        in_specs += [pl.BlockSpec((1, d), lambda i, j: (0, 0)), mod_spec]
        args += [nnext[None, :], mod_next]
        out_specs.append(row)
        out_shape.append(jax.ShapeDtypeStruct((t, d), BF16))
    return pl.pallas_call(
        functools.partial(_mlp_kernel, with_next=with_next),
        grid=(t // tm, ff // tf),
        in_specs=in_specs,
        out_specs=out_specs,
        out_shape=out_shape,
        scratch_shapes=[pltpu.VMEM((tm, d), F32)],
        compiler_params=_params(("arbitrary", "arbitrary")),
        name="mlp",
    )(*args)


DIL_TQ = 128
DIL_HALF = 64
assert all(w // (2 * d) == DIL_HALF for w, d in DIL_GROUPS)


def _dil_geometry(seq, dilation):
    length = seq // dilation
    kwin = min(2 * DIL_TQ, length)
    return length, kwin, length // DIL_TQ


def _t5_log_thresholds():
    nb = N_BUCKETS // 2
    max_exact = nb // 2
    steps = nb - max_exact
    thresholds = []
    for k in range(1, steps):
        n = max_exact
        while math.floor(math.log(n / max_exact) / math.log(MAX_DISTANCE / max_exact) * steps) < k:
            n += 1
        thresholds.append(n)
    return thresholds


def _t5_bucket(rel):
    nb = N_BUCKETS // 2
    max_exact = nb // 2
    n = jnp.abs(rel)
    large = max_exact
    for thr in _t5_log_thresholds():
        large = large + jnp.where(n >= thr, 1, 0)
    return jnp.where(rel > 0, nb, 0) + jnp.where(n < max_exact, n, large)


BIAS_ROWS = 16


def _dil_bias_kernel(tab_ref, o_ref, *, group, dilation, kwin, n_var):
    var = pl.program_id(0)

    def rows(step, carry):
        r0 = pl.multiple_of(step * BIAS_ROWS, BIAS_ROWS)
        kk = lax.broadcasted_iota(jnp.int32, (BIAS_ROWS, DIL_TQ), 0) + r0
        qi = lax.broadcasted_iota(jnp.int32, (BIAS_ROWS, DIL_TQ), 1)
        rel_a = kk - qi - (DIL_HALF if n_var > 1 else 0)
        valid = jnp.abs(rel_a) <= DIL_HALF
        if n_var > 1:
            crossing = jnp.where((qi < DIL_HALF) == (kk < DIL_TQ), 0, var)
            valid = jnp.where(valid, 1, 0) - crossing == 1
        bucket = _t5_bucket(rel_a * dilation)
        accs = [jnp.zeros((BIAS_ROWS, DIL_TQ), F32)] * DIL_HEADS
        for b in range(N_BUCKETS):
            hit = bucket == b
            accs = [jnp.where(hit, tab_ref[b, group * DIL_HEADS + h], a) for h, a in enumerate(accs)]
        for h, a in enumerate(accs):
            o_ref[0, h // 2, pl.ds(r0, BIAS_ROWS), (h % 2) * DIL_TQ:(h % 2 + 1) * DIL_TQ] = jnp.where(
                valid, a * LOG2E, MASKED)
        return carry

    lax.fori_loop(0, kwin // BIAS_ROWS, rows, 0)


def _dil_bias(rel_bias, group, seq):
    _, dilation = DIL_GROUPS[group]
    _, kwin, n_tiles = _dil_geometry(seq, dilation)
    n_var = 1 if n_tiles == 1 else 2
    n_pairs = DIL_HEADS // 2
    kern = functools.partial(_dil_bias_kernel, group=group, dilation=dilation, kwin=kwin, n_var=n_var)
    return pl.pallas_call(
        kern,
        grid=(n_var,),
        in_specs=[pl.BlockSpec(memory_space=pltpu.SMEM)],
        out_specs=pl.BlockSpec((1, n_pairs, kwin, 2 * DIL_TQ), lambda v: (v, 0, 0, 0)),
        out_shape=jax.ShapeDtypeStruct((n_var, n_pairs, kwin, 2 * DIL_TQ), F32),
        compiler_params=_params(("arbitrary",)),
        name=f"dil_bias{group}",
    )(rel_bias)


DIL_PROJ_ROWS = 512


def _dil_proj_kernel(h_ref, wqf_ref, wkf_ref, wvf_ref, gq_ref, qT_ref, k_ref, vT_ref, wq_ref, wk_ref, wv_ref,
                     *scratch, dilation, rb, ra):
    hd = DIL_HEAD_DIM
    d_model = h_ref.shape[2]
    n_slabs = d_model // LANES
    mt = rb * ra

    @pl.when((pl.program_id(0) == 0) & (pl.program_id(1) == 0) & (pl.program_id(2) == 0))
    def _():
        wq_ref[...] = wqf_ref[0].T.astype(BF16)
        wk_ref[...] = wkf_ref[0].astype(BF16)
        wv_ref[...] = wvf_ref[0].T.astype(BF16)

    if dilation == 1:
        hcat = h_ref[0]
    else:
        (scr,) = scratch
        sub = pl.program_id(2)

        @pl.when(sub == 0)
        def _():
            for s in range(n_slabs):
                scr[s] = h_ref[0, :, s * LANES:(s + 1) * LANES].astype(F32)

        rows = []
        for j in range(rb):
            res = sub * rb + j
            rows.append(jnp.concatenate(
                [scr[s, pl.ds(res, ra, stride=dilation), :] for s in range(n_slabs)], axis=1))
        hcat = jnp.concatenate(rows, axis=0).astype(BF16)

    q_t = lax.dot_general(wq_ref[...], hcat, NT_DIMS, preferred_element_type=F32)
    gq = jnp.concatenate([gq_ref[...]] * (mt // LANES), axis=1)
    for h in range(DIL_HEADS):
        blk = q_t[h * hd:(h + 1) * hd]
        r = lax.rsqrt(jnp.sum(blk * blk, axis=0, keepdims=True) * (1.0 / hd) + EPS)
        qb = (blk * r * gq).astype(BF16)
        for j in range(rb):
            for s in range(ra // LANES):
                lo = j * ra + s * LANES
                qT_ref[0, j, s, h * hd:(h + 1) * hd, :] = qb[:, lo:lo + LANES]

    kf = jnp.dot(hcat, wk_ref[...], preferred_element_type=F32)
    lane = lax.broadcasted_iota(jnp.int32, (mt, LANES), 1)
    low = lane < hd
    for c in range(DIL_HEADS * hd // LANES):
        y = kf[:, c * LANES:(c + 1) * LANES]
        y2 = y * y
        s_lo = jnp.sum(jnp.where(low, y2, 0.0), axis=1, keepdims=True)
        s_hi = jnp.sum(jnp.where(low, 0.0, y2), axis=1, keepdims=True)
        r = lax.rsqrt(jnp.where(low, s_lo, s_hi) * (1.0 / hd) + EPS)
        kb = (y * r).astype(BF16)
        for j in range(rb):
            k_ref[0, j, :, c * LANES:(c + 1) * LANES] = kb[j * ra:(j + 1) * ra]

    v_t = lax.dot_general(wv_ref[...], hcat, NT_DIMS, preferred_element_type=F32).astype(BF16)
    for j in range(rb):
        for s in range(ra // LANES):
            lo = j * ra + s * LANES
            vT_ref[0, j, s] = v_t[:, lo:lo + LANES]


def _dil_proj(hn3, w_in, g_q, g_k, group):
    batch, seq, d_model = hn3.shape
    _, dilation = DIL_GROUPS[group]
    length = seq // dilation
    c = DIL_HEADS * DIL_HEAD_DIM
    ra = DIL_PROJ_ROWS if dilation == 1 else DIL_TQ
    rb = DIL_PROJ_ROWS // ra
    tok = ra * dilation
    n_sub = dilation // rb
    gq_col = jnp.broadcast_to((g_q * g_k * (DIL_HEAD_DIM ** -0.5 * LOG2E))[:, None], (DIL_HEAD_DIM, LANES))
    kern = functools.partial(_dil_proj_kernel, dilation=dilation, rb=rb, ra=ra)
    const = lambda shape: pl.BlockSpec(shape, lambda b, t, s: (0,) * len(shape))
    w_blk = lambda j: pl.BlockSpec((1, d_model, c), lambda b, t, s: (0, 0, 3 * group + j))
    scratch = [pltpu.VMEM((c, d_model), BF16), pltpu.VMEM((d_model, c), BF16), pltpu.VMEM((c, d_model), BF16)]
    if dilation > 1:
        scratch.append(pltpu.VMEM((d_model // LANES, tok, LANES), F32))
    return pl.pallas_call(
        kern,
        grid=(batch, seq // tok, n_sub),
        in_specs=[
            pl.BlockSpec((1, tok, d_model), lambda b, t, s: (b, t, 0)),
            w_blk(0), w_blk(1), w_blk(2),
            const((DIL_HEAD_DIM, LANES)),
        ],
        out_specs=[
            pl.BlockSpec((1, rb, ra // LANES, c, LANES), lambda b, t, s: (b, s, t, 0, 0)),
            pl.BlockSpec((1, rb, ra, c), lambda b, t, s: (b, s, t, 0)),
            pl.BlockSpec((1, rb, ra // LANES, c, LANES), lambda b, t, s: (b, s, t, 0, 0)),
        ],
        out_shape=[
            jax.ShapeDtypeStruct((batch, dilation, length // LANES, c, LANES), BF16),
            jax.ShapeDtypeStruct((batch, dilation, length, c), BF16),
            jax.ShapeDtypeStruct((batch, dilation, length // LANES, c, LANES), BF16),
        ],
        scratch_shapes=scratch,
        compiler_params=_params(("arbitrary", "arbitrary", "arbitrary")),
        name=f"dil_proj{group}",
    )(hn3, w_in, w_in, w_in, gq_col)


def _dil_attn_kernel(qT_ref, k_ref, vT_ref, bm_ref, o_ref, st_ref, s_ref, *, n_tiles, rb):
    tq = DIL_TQ
    hd = DIL_HEAD_DIM
    n_pairs = DIL_HEADS // 2
    c = DIL_HEADS * hd
    shifted = n_tiles > 1
    low = lax.broadcasted_iota(jnp.int32, (LANES, tq), 0) < hd
    ones = jnp.ones((ONES_ROWS, 2 * tq if shifted else tq), BF16)
    tiles = [(j, t) for j in range(rb) for t in range(n_tiles)]

    def score_tile(i):
        j, t = tiles[i]
        t1 = (t + 1) % n_tiles
        for p in range(n_pairs):
            rows = slice(p * LANES, (p + 1) * LANES)
            if shifted:
                q2 = jnp.concatenate([qT_ref[0, j, t, rows, DIL_HALF:], qT_ref[0, j, t1, rows, :DIL_HALF]], axis=1)
                k2 = jnp.concatenate([k_ref[0, j, t * tq:(t + 1) * tq, rows],
                                      k_ref[0, j, t1 * tq:(t1 + 1) * tq, rows]], axis=0)
            else:
                q2 = qT_ref[0, j, 0, rows, :]
                k2 = k_ref[0, j, :, rows]
            zero = jnp.zeros_like(q2)
            bd = jnp.concatenate([jnp.where(low, q2, zero), jnp.where(low, zero, q2)], axis=1)
            s_ref[i % 2, p] = jnp.dot(k2, bd, preferred_element_type=F32)

    def attend_tile(i):
        j, t = tiles[i]
        t1 = (t + 1) % n_tiles

        def store(ref, lanes, val):
            if shifted:
                ref[0, t * tq + DIL_HALF:(t + 1) * tq, lanes] = val[:DIL_HALF]
                ref[0, t1 * tq:t1 * tq + DIL_HALF, lanes] = val[DIL_HALF:]
            else:
                ref[0, :, lanes] = val

        stats = []
        for p in range(n_pairs):
            rows = slice(p * LANES, (p + 1) * LANES)
            var = 1 if shifted and t + 1 == n_tiles else 0
            ms, ps = [], []
            for side in range(2):
                cols = slice(side * tq, (side + 1) * tq)
                s_h = s_ref[i % 2, p, :, cols] + bm_ref[var, p, :, cols]
                ms.append(jnp.max(s_h, axis=0, keepdims=True))
                ps.append(jnp.exp2(s_h - ms[-1]).astype(BF16))
            m = jnp.concatenate(ms, axis=1)
            pb = jnp.concatenate(ps, axis=1)
            if shifted:
                v2 = jnp.concatenate([vT_ref[0, j, t, rows, :], vT_ref[0, j, t1, rows, :]], axis=1)
            else:
                v2 = vT_ref[0, j, 0, rows, :]
            res = jnp.dot(jnp.concatenate([v2, ones], axis=0), pb, preferred_element_type=F32)
            den = res[LANES:LANES + 1]
            rinv = 1.0 / den
            o_pair = jnp.concatenate([res[:hd, :tq] * rinv[:, :tq], res[hd:LANES, tq:] * rinv[:, tq:]], axis=0)
            store(o_ref, slice(j * c + p * LANES, j * c + (p + 1) * LANES), o_pair.T.astype(BF16))
            lse = (m + jnp.log2(den)) * LN2
            stats += [lse[:, :tq], lse[:, tq:]]
        st = jnp.concatenate(stats + [jnp.zeros((LANES - DIL_HEADS, tq), F32)], axis=0)
        store(st_ref, slice(j * LANES, (j + 1) * LANES), st.T)

    score_tile(0)
    for i in range(len(tiles)):
        if i + 1 < len(tiles):
            score_tile(i + 1)
        attend_tile(i)


def _dil_attn(q_t, k, v_t, bias, group, seq):
    batch = q_t.shape[0]
    _, dilation = DIL_GROUPS[group]
    length, kwin, n_tiles = _dil_geometry(seq, dilation)
    c = DIL_HEADS * DIL_HEAD_DIM
    n_slabs = length // LANES
    rb = 1 if n_tiles > 1 else min(dilation, 4)
    kern = functools.partial(_dil_attn_kernel, n_tiles=n_tiles, rb=rb)
    return pl.pallas_call(
        kern,
        grid=(batch, dilation // rb),
        in_specs=[
            pl.BlockSpec((1, rb, n_slabs, c, LANES), lambda b, r: (b, r, 0, 0, 0)),
            pl.BlockSpec((1, rb, length, c), lambda b, r: (b, r, 0, 0)),
            pl.BlockSpec((1, rb, n_slabs, c, LANES), lambda b, r: (b, r, 0, 0, 0)),
            pl.BlockSpec(bias.shape, lambda b, r: (0, 0, 0, 0)),
        ],
        out_specs=[
            pl.BlockSpec((1, length, rb * c), lambda b, r: (b, 0, r)),
            pl.BlockSpec((1, length, rb * LANES), lambda b, r: (b, 0, r)),
        ],
        out_shape=[
            jax.ShapeDtypeStruct((batch, length, dilation * c), BF16),
            jax.ShapeDtypeStruct((batch, length, dilation * LANES), F32),
        ],
        scratch_shapes=[pltpu.VMEM((2, DIL_HEADS // 2, kwin, 2 * DIL_TQ), F32)],
        compiler_params=_params(("arbitrary", "arbitrary")),
        name=f"dil_attn{group}",
    )(q_t, k, v_t, bias)


def _dil_out_kernel(o0_ref, o1_ref, o2_ref, s0_ref, s1_ref, s2_ref, e_ref, wo_ref, x_ref, mod_ref, nmlp_ref,
                    x1_ref, h2_ref, *scratch):
    tm = x_ref.shape[0]
    c = wo_ref.shape[0]

    def token_major(o_ref, s_ref, oscr, sscr, dilation):
        ra = tm // dilation
        for r in range(dilation):
            for s in range(c // LANES):
                lo = r * c + s * LANES
                oscr[s, pl.ds(r, ra, stride=dilation), :] = o_ref[0, :, lo:lo + LANES].astype(F32)
            sscr[pl.ds(r, ra, stride=dilation), :] = s_ref[0, :, r * LANES:(r + 1) * LANES]

    token_major(o1_ref, s1_ref, scratch[0], scratch[1], DIL_GROUPS[1][1])
    token_major(o2_ref, s2_ref, scratch[2], scratch[3], DIL_GROUPS[2][1])
    e = e_ref[...]
    m = mod_ref[0]

    def expand(w):
        return jnp.dot(w.astype(BF16), e, preferred_element_type=F32)

    def merged(rows):
        gather = lambda oscr: jnp.concatenate([oscr[s, rows, :] for s in range(c // LANES)], axis=1)
        o0, l0 = o0_ref[0, rows, :].astype(F32), s0_ref[0, rows, :]
        o1, l1 = gather(scratch[0]), scratch[1][rows, :]
        o2, l2 = gather(scratch[2]), scratch[3][rows, :]
        mx = jnp.maximum(jnp.maximum(l0, l1), l2)
        e0, e1, e2 = jnp.exp(l0 - mx), jnp.exp(l1 - mx), jnp.exp(l2 - mx)
        rden = 1.0 / (e0 + e1 + e2)
        return (expand(e0 * rden) * o0 + expand(e1 * rden) * o1 + expand(e2 * rden) * o2).astype(BF16)

    def project(rows, o):
        y = jnp.dot(o, wo_ref[...], preferred_element_type=F32)
        x1 = x_ref[rows, :] + m[2:3] * y
        x1_ref[rows, :] = x1
        h2_ref[rows, :] = _ada_norm(x1, nmlp_ref[...], m[4:5], m[3:4]).astype(BF16)

    n_sub = 2
    blocks = [slice(i * (tm // n_sub), (i + 1) * (tm // n_sub)) for i in range(n_sub)]
    o = merged(blocks[0])
    for i, rows in enumerate(blocks):
        o_next = merged(blocks[i + 1]) if i + 1 < n_sub else None
        project(rows, o)
        o = o_next


def _dil_out(outs, stats, w_o, x2d, mod_l, nmlp, seq):
    t, d = x2d.shape
    c = w_o.shape[0]
    tm = 512
    tpb = seq // tm
    expand = jnp.repeat(jnp.eye(DIL_HEADS, dtype=F32), DIL_HEAD_DIM, axis=1)
    expand = jnp.pad(expand, ((0, LANES - DIL_HEADS), (0, 0))).astype(BF16)
    row = lambda w: pl.BlockSpec((tm, w), lambda i: (i, 0))
    grp = lambda g, w: pl.BlockSpec((1, tm // DIL_GROUPS[g][1], DIL_GROUPS[g][1] * w),
                                    lambda i: (i // tpb, i % tpb, 0))
    scratch = []
    for g in (1, 2):
        scratch += [pltpu.VMEM((c // LANES, tm, LANES), F32), pltpu.VMEM((tm, LANES), F32)]
    return pl.pallas_call(
        _dil_out_kernel,
        grid=(t // tm,),
        in_specs=[grp(0, c), grp(1, c), grp(2, c), grp(0, LANES), grp(1, LANES), grp(2, LANES),
                  pl.BlockSpec((LANES, c), lambda i: (0, 0)),
                  pl.BlockSpec((c, d), lambda i: (0, 0)),
                  row(d),
                  pl.BlockSpec((1, N_MOD, d), lambda i: (i // tpb, 0, 0)),
                  pl.BlockSpec((1, d), lambda i: (0, 0))],
        out_specs=[row(d), row(d)],
        out_shape=[jax.ShapeDtypeStruct((t, d), F32), jax.ShapeDtypeStruct((t, d), BF16)],
        scratch_shapes=scratch,
        compiler_params=_params(("arbitrary",)),
        name="dil_out",
    )(*outs, *stats, expand, w_o.astype(BF16), x2d, mod_l, nmlp[None, :])


def kernel(x, c, positions, ada_w, ada_b, norm_mix, norm_mlp, mlp_w1, mlp_w2, mla_w_in, mla_g_qa, mla_w_qb,
           mla_g_kva, mla_w_kvb, mla_g_q, mla_g_k, mla_w_o, dil_w_in, dil_g_q, dil_g_k, dil_w_o, rel_bias):
    batch, seq, d = x.shape
    depth = ada_w.shape[0]
    assert depth == 2 and len(DIL_GROUPS) == 3
    mod = _mod(c, ada_w, ada_b).reshape(depth, batch, N_MOD, d)
    x2d = x.reshape(batch * seq, d)

    n_heads = mla_w_qb.shape[2] // mla_g_q.shape[1]
    q_t, k_pad, v_t = _mla_proj(x2d, mod[0], norm_mix[0], positions, mla_w_in[0], mla_g_qa[0], mla_w_qb[0],
                                mla_g_kva[0], mla_w_kvb[0], mla_g_q[0], mla_g_k[0], batch, seq)
    o_t = _mla_attn(q_t, k_pad, v_t, _score_bound_is_safe(mla_g_q[0] * mla_g_k[0], mla_g_q.shape[1]), n_heads)
    x1, h2 = _attn_out(o_t, mla_w_o[0], x2d, mod[0], norm_mlp[0], seq)
    x2, hn = _mlp(x1, h2, mlp_w1, mlp_w2, 0, mod[0], seq, norm_mix[1], mod[1])

    outs, stats = [], []
    hn3 = hn.reshape(batch, seq, d)
    for g in range(len(DIL_GROUPS)):
        q_g, k_g, v_g = _dil_proj(hn3, dil_w_in, dil_g_q[0, g], dil_g_k[0, g], g)
        o_g, st_g = _dil_attn(q_g, k_g, v_g, _dil_bias(rel_bias, g, seq), g, seq)
        outs.append(o_g)
        stats.append(st_g)
    x3, h4 = _dil_out(outs, stats, dil_w_o[0], x2, mod[1], norm_mlp[1], seq)
    (x4,) = _mlp(x3, h4, mlp_w1, mlp_w2, 1, mod[1], seq)
    return x4.reshape(batch, seq, d)
```

```python
import functools
import math

import jax
import jax.numpy as jnp
from jax import lax
from jax.experimental import pallas as pl
from jax.experimental.pallas import tpu as pltpu

F32 = jnp.float32
BF16 = jnp.bfloat16

EPS = 1e-6
LOG2E = 1.4426950408889634
LN2 = 0.6931471805599453
MASKED = -1e30

N_MOD = 6
ROPE_THETA = 10000.0
MLA_ROPE = 32
DIL_GROUPS = ((128, 1), (512, 4), (2048, 16))
DIL_HEADS = 16
DIL_HEAD_DIM = 64
N_BUCKETS = 32
MAX_DISTANCE = 1024

LANES = 128
HEAD_PAD = 128

VMEM_LIMIT = 56 * 1024 * 1024

NT_DIMS = (((1,), (1,)), ((), ()))
TN_DIMS = (((0,), (0,)), ((), ()))


def _params(sem):
    return pltpu.CompilerParams(dimension_semantics=sem, vmem_limit_bytes=VMEM_LIMIT)


def _ada_norm(x, g, scale, shift):
    ms = jnp.mean(x * x, axis=-1, keepdims=True)
    return (x * lax.rsqrt(ms + EPS) * g) * (1.0 + scale) + shift


def _rms(x, g):
    ms = jnp.mean(x * x, axis=-1, keepdims=True)
    return x * lax.rsqrt(ms + EPS) * g


def _mod_kernel(c_ref, w_ref, b_ref, o_ref):
    c = c_ref[...]
    cond = c / (1.0 + jnp.exp(-c))
    o_ref[0] = (
        jnp.dot(cond.astype(BF16), w_ref[0].astype(BF16), preferred_element_type=F32) + b_ref[0]
    )


def _mod(c, ada_w, ada_b):
    depth, d, n = ada_w.shape
    b = c.shape[0]
    tn = 1536
    return pl.pallas_call(
        _mod_kernel,
        grid=(depth, n // tn),
        in_specs=[
            pl.BlockSpec((b, d), lambda l, j: (0, 0)),
            pl.BlockSpec((1, d, tn), lambda l, j: (l, 0, j)),
            pl.BlockSpec((1, 1, tn), lambda l, j: (l, 0, j)),
        ],
        out_specs=pl.BlockSpec((1, b, tn), lambda l, j: (l, 0, j)),
        out_shape=jax.ShapeDtypeStruct((depth, b, n), F32),
        compiler_params=_params(("arbitrary", "arbitrary")),
        name="mod",
    )(c, ada_w, ada_b.reshape(depth, 1, n))


def _mla_proj_kernel(x_ref, mod_ref, nmix_ref, pos_ref, inv_ref, wlat_ref, wkr_ref, gqa_ref, gkva_ref,
                     wqb_ref, wkn_ref, wv_ref, gq_ref, qT_ref, k_ref, vT_ref, *, n_heads, q_lora, qk_dim):
    x = x_ref[...]
    tm = x.shape[0]
    m = mod_ref[0]
    hb = _ada_norm(x, nmix_ref[...], m[1:2], m[0:1]).astype(BF16)
    lat = jnp.dot(hb, wlat_ref[...], preferred_element_type=F32)
    qn = _rms(lat[:, :q_lora], gqa_ref[...]).astype(BF16)
    kvn = _rms(lat[:, q_lora:], gkva_ref[...]).astype(BF16)

    half = MLA_ROPE // 2
    nope = qk_dim - MLA_ROPE
    pos = pos_ref[...].astype(F32)
    ang = jnp.concatenate([inv_ref[...]] * (tm // LANES), axis=1) * pos
    cos_t = jnp.cos(ang)
    sin_t = jnp.sin(ang)

    def rope_rows(blk):
        x1 = blk[nope:nope + half]
        x2 = blk[nope + half:qk_dim]
        return jnp.concatenate(
            [blk[:nope], x1 * cos_t - x2 * sin_t, x2 * cos_t + x1 * sin_t, blk[qk_dim:]], axis=0)

    kr_t = lax.dot_general(wkr_ref[...], hb, NT_DIMS, preferred_element_type=F32)
    kr = rope_rows(kr_t).T

    q_t = lax.dot_general(wqb_ref[...], qn, NT_DIMS, preferred_element_type=F32)
    gq = jnp.concatenate([gq_ref[...]] * (tm // LANES), axis=1)
    for h in range(n_heads):
        blk = rope_rows(q_t[h * HEAD_PAD:(h + 1) * HEAD_PAD])
        ssq = jnp.sum(blk * blk, axis=0, keepdims=True)
        r = lax.rsqrt(ssq * (1.0 / qk_dim) + EPS)
        qT_ref[0, 0, h * HEAD_PAD:(h + 1) * HEAD_PAD, :] = (blk * r * gq).astype(BF16)

    kn = jnp.dot(kvn, wkn_ref[...], preferred_element_type=F32)
    for h in range(n_heads):
        kb = kn[:, h * HEAD_PAD:(h + 1) * HEAD_PAD] + kr
        ssq = jnp.sum(kb * kb, axis=1, keepdims=True)
        r = lax.rsqrt(ssq * (1.0 / qk_dim) + EPS)
        k_ref[:, h * HEAD_PAD:(h + 1) * HEAD_PAD] = (kb * r).astype(BF16)

    vT_ref[0] = lax.dot_general(wv_ref[...], kvn, NT_DIMS, preferred_element_type=F32).astype(BF16)


def _mla_proj(x2d, mod_l, nmix, positions, w_in, g_qa, w_qb, g_kva, w_kvb, g_q, g_k, batch, seq):
    t, d = x2d.shape
    q_lora = g_qa.shape[0]
    kv_lora = g_kva.shape[0]
    qk_dim = g_q.shape[0]
    n_heads = w_qb.shape[1] // qk_dim
    nope = qk_dim - MLA_ROPE
    v_dim = w_kvb.shape[1] // n_heads - nope
    assert nope + v_dim == HEAD_PAD and qk_dim <= HEAD_PAD
    half = MLA_ROPE // 2
    tm = 512
    tpb = seq // tm

    w_lat = w_in[:, :q_lora + kv_lora].astype(BF16)
    w_kr = jnp.zeros((HEAD_PAD, d), F32).at[nope:qk_dim].set(w_in[:, q_lora + kv_lora:].T).astype(BF16)
    w_qb_t = jnp.pad(w_qb.T.reshape(n_heads, qk_dim, q_lora), ((0, 0), (0, HEAD_PAD - qk_dim), (0, 0)))
    w_qb_t = w_qb_t.reshape(n_heads * HEAD_PAD, q_lora).astype(BF16)
    w_kvb3 = w_kvb.reshape(kv_lora, n_heads, nope + v_dim)
    w_kn = jnp.where(jnp.arange(nope + v_dim) < nope, w_kvb3, 0.0).reshape(kv_lora, n_heads * HEAD_PAD).astype(BF16)
    w_v_t = w_kvb3[:, :, nope:].reshape(kv_lora, n_heads * v_dim).T.astype(BF16)
    q_gain = jnp.pad(g_q * g_k * (qk_dim ** -0.5 * LOG2E), (0, HEAD_PAD - qk_dim))
    gq_col = jnp.broadcast_to(q_gain[:, None], (HEAD_PAD, LANES))
    inv = 1.0 / (ROPE_THETA ** (jnp.arange(half, dtype=F32) / half))
    inv_tab = jnp.broadcast_to(inv[:, None], (half, LANES))
    pos_row = positions.reshape(1, t)

    full = lambda shape: pl.BlockSpec(shape, lambda i: (0,) * len(shape))
    kern = functools.partial(_mla_proj_kernel, n_heads=n_heads, q_lora=q_lora, qk_dim=qk_dim)
    return pl.pallas_call(
        kern,
        grid=(t // tm,),
        in_specs=[
            pl.BlockSpec((tm, d), lambda i: (i, 0)),
            pl.BlockSpec((1, N_MOD, d), lambda i: (i // tpb, 0, 0)),
            full((1, d)),
            pl.BlockSpec((1, tm), lambda i: (0, i)),
            full((half, LANES)),
            full(w_lat.shape), full(w_kr.shape), full((1, q_lora)), full((1, kv_lora)),
            full(w_qb_t.shape), full(w_kn.shape), full(w_v_t.shape),
            full((HEAD_PAD, LANES)),
        ],
        out_specs=[
            pl.BlockSpec((1, 1, n_heads * HEAD_PAD, tm), lambda i: (i // tpb, i % tpb, 0, 0)),
            pl.BlockSpec((tm, n_heads * HEAD_PAD), lambda i: (i, 0)),
            pl.BlockSpec((1, n_heads * v_dim, tm), lambda i: (i // tpb, 0, i % tpb)),
        ],
        out_shape=[
            jax.ShapeDtypeStruct((batch, tpb, n_heads * HEAD_PAD, tm), BF16),
            jax.ShapeDtypeStruct((t, n_heads * HEAD_PAD), BF16),
            jax.ShapeDtypeStruct((batch, n_heads * v_dim, seq), BF16),
        ],
        compiler_params=_params(("arbitrary",)),
        name="mla_proj",
    )(x2d, mod_l, nmix[None, :], pos_row, inv_tab, w_lat, w_kr, g_qa[None, :], g_kva[None, :],
      w_qb_t, w_kn, w_v_t, gq_col)


ONES_ROWS = 16


SAFE_EXPONENT = 60.0


def _score_bound_is_safe(gain_product, head_dim, bias=None):
    bound = 1.02 * math.sqrt(head_dim) * LOG2E * jnp.max(jnp.abs(gain_product))
    if bias is not None:
        bound = bound + LOG2E * jnp.max(jnp.abs(bias))
    return (bound <= SAFE_EXPONENT).astype(jnp.int32).reshape(1)


def _mla_attn_kernel(safe_ref, qT_ref, k_ref, vT_ref, oT_ref, s_ref, *, hb, tn, kc):
    seq = k_ref.shape[1]
    n_tiles, _, tq = qT_ref.shape[1:]
    v_dim = vT_ref.shape[1] // hb
    ones = jnp.ones((ONES_ROWS, seq), BF16)
    n_chunks = seq // kc
    n_cols = tq // tn
    hrows = lambda h: slice(h * HEAD_PAD, (h + 1) * HEAD_PAD)
    vrows = lambda h: slice(h * v_dim, (h + 1) * v_dim)
    v_ext = lambda h: jnp.concatenate([vT_ref[0, vrows(h), :], ones], axis=0)

    @pl.when(safe_ref[0] == 1)
    def _():
        tiles = [(h, i) for h in range(hb) for i in range(n_tiles)]

        def scores(t, c):
            h, i = tiles[t]
            s_ref[t % 2, c * kc:(c + 1) * kc, :] = jnp.dot(
                k_ref[0, c * kc:(c + 1) * kc, hrows(h)], qT_ref[0, i, hrows(h), :], preferred_element_type=F32)

        for c in range(n_chunks):
            scores(0, c)
        for t, (h, i) in enumerate(tiles):
            ve = v_ext(h)
            for j in range(n_cols):
                acc = jnp.zeros((v_dim + ONES_ROWS, tn), F32)
                for c in range(n_chunks):
                    step = j * n_chunks + c
                    if t + 1 < len(tiles) and step % n_cols == 0:
                        scores(t + 1, step // n_cols)
                    p = jnp.exp2(s_ref[t % 2, c * kc:(c + 1) * kc, j * tn:(j + 1) * tn]).astype(BF16)
                    acc = acc + jnp.dot(ve[:, c * kc:(c + 1) * kc], p, preferred_element_type=F32)
                o = acc[:v_dim] * (1.0 / acc[v_dim:v_dim + 1])
                oT_ref[0, i, vrows(h), j * tn:(j + 1) * tn] = o.astype(BF16)

    @pl.when(safe_ref[0] != 1)
    def _():
        for h in range(hb):
            ve = v_ext(h)
            k_h = k_ref[0, :, hrows(h)]

            def tile(i, carry):
                s_t = jnp.dot(k_h, qT_ref[0, i, hrows(h), :], preferred_element_type=F32)
                p = jnp.exp2(s_t - jnp.max(s_t, axis=0, keepdims=True)).astype(BF16)
                r = jnp.dot(ve, p, preferred_element_type=F32)
                oT_ref[0, i, vrows(h), :] = (r[:v_dim] * (1.0 / r[v_dim:v_dim + 1])).astype(BF16)
                return carry

            lax.fori_loop(0, n_tiles, tile, 0)


def _mla_attn(q_t, k_pad, v_t, safe, n_heads):
    batch, n_tiles, _, tq = q_t.shape
    seq = n_tiles * tq
    v_dim = v_t.shape[1] // n_heads
    k3 = k_pad.reshape(batch, seq, n_heads * HEAD_PAD)
    hb = 2
    return pl.pallas_call(
        functools.partial(_mla_attn_kernel, hb=hb, tn=256, kc=256),
        grid=(batch, n_heads // hb),
        in_specs=[
            pl.BlockSpec(memory_space=pltpu.SMEM),
            pl.BlockSpec((1, n_tiles, hb * HEAD_PAD, tq), lambda b, h: (b, 0, h, 0)),
            pl.BlockSpec((1, seq, hb * HEAD_PAD), lambda b, h: (b, 0, h)),
            pl.BlockSpec((1, hb * v_dim, seq), lambda b, h: (b, h, 0)),
        ],
        out_specs=pl.BlockSpec((1, n_tiles, hb * v_dim, tq), lambda b, h: (b, 0, h, 0)),
        out_shape=jax.ShapeDtypeStruct((batch, n_tiles, n_heads * v_dim, tq), BF16),
        scratch_shapes=[pltpu.VMEM((2, seq, tq), F32)],
        compiler_params=_params(("arbitrary", "arbitrary")),
        name="mla_attn",
    )(safe, q_t, k3, v_t)


def _attn_out_kernel(oT_ref, wo_ref, x_ref, mod_ref, nmlp_ref, x1_ref, h2_ref):
    m = mod_ref[0]
    tm = x_ref.shape[0]
    n_sub = 2
    blocks = [slice(i * (tm // n_sub), (i + 1) * (tm // n_sub)) for i in range(n_sub)]
    project = lambda rows: lax.dot_general(oT_ref[0, 0, :, rows], wo_ref[...], TN_DIMS, preferred_element_type=F32)
    y = project(blocks[0])
    for i, rows in enumerate(blocks):
        y_next = project(blocks[i + 1]) if i + 1 < n_sub else None
        x1 = x_ref[rows, :] + m[2:3] * y
        x1_ref[rows, :] = x1
        h2_ref[rows, :] = _ada_norm(x1, nmlp_ref[...], m[4:5], m[3:4]).astype(BF16)
        y = y_next


def _attn_out(o_t, w_o, x2d, mod_l, nmlp, seq):
    t, d = x2d.shape
    tm = 512
    tpb = seq // tm
    c = o_t.shape[2]
    return pl.pallas_call(
        _attn_out_kernel,
        grid=(t // tm,),
        in_specs=[
            pl.BlockSpec((1, 1, c, tm), lambda i: (i // tpb, i % tpb, 0, 0)),
            pl.BlockSpec((c, d), lambda i: (0, 0)),
            pl.BlockSpec((tm, d), lambda i: (i, 0)),
            pl.BlockSpec((1, N_MOD, d), lambda i: (i // tpb, 0, 0)),
            pl.BlockSpec((1, d), lambda i: (0, 0)),
        ],
        out_specs=[pl.BlockSpec((tm, d), lambda i: (i, 0)), pl.BlockSpec((tm, d), lambda i: (i, 0))],
        out_shape=[jax.ShapeDtypeStruct((t, d), F32), jax.ShapeDtypeStruct((t, d), BF16)],
        compiler_params=_params(("arbitrary",)),
        name="attn_out",
    )(o_t, w_o.astype(BF16), x2d, mod_l, nmlp[None, :])


def _mlp_kernel(x_ref, h_ref, w1_ref, w2_ref, mod_ref, *rest, with_next):
    if with_next:
        nnext_ref, modn_ref, x2_ref, hn_ref, acc_ref = rest
    else:
        x2_ref, acc_ref = rest
    j = pl.program_id(1)

    @pl.when(j == 0)
    def _():
        acc_ref[...] = jnp.zeros_like(acc_ref)

    u = jnp.dot(h_ref[...], w1_ref[0].astype(BF16), preferred_element_type=F32)
    u = jnp.square(jnp.maximum(u, 0.0)).astype(BF16)
    acc_ref[...] += jnp.dot(u, w2_ref[0].astype(BF16), preferred_element_type=F32)

    @pl.when(j == pl.num_programs(1) - 1)
    def _():
        x2 = x_ref[...] + mod_ref[0][5:6] * acc_ref[...]
        x2_ref[...] = x2
        if with_next:
            mn = modn_ref[0]
            hn_ref[...] = _ada_norm(x2, nnext_ref[...], mn[1:2], mn[0:1]).astype(BF16)


def _mlp(x1, h2, w1, w2, layer, mod_l, seq, nnext=None, mod_next=None):
    t, d = x1.shape
    ff = w1.shape[2]
    tm, tf = 1024, 1024
    tpb = seq // tm
    with_next = nnext is not None
    row = pl.BlockSpec((tm, d), lambda i, j: (i, 0))
    mod_spec = pl.BlockSpec((1, N_MOD, d), lambda i, j: (i // tpb, 0, 0))
    in_specs = [row, row, pl.BlockSpec((1, d, tf), lambda i, j: (layer, 0, j)),
                pl.BlockSpec((1, tf, d), lambda i, j: (layer, j, 0)), mod_spec]
    args = [x1, h2, w1, w2, mod_l]
    out_specs = [row]
    out_shape = [jax.ShapeDtypeStruct((t, d), F32)]
    if with_next:
        in_specs += [pl.BlockSpec((1, d), lambda i, j: (0, 0)), mod_spec]
        args += [nnext[None, :], mod_next]
        out_specs.append(row)
        out_shape.append(jax.ShapeDtypeStruct((t, d), BF16))
    return pl.pallas_call(
        functools.partial(_mlp_kernel, with_next=with_next),
        grid=(t // tm, ff // tf),
        in_specs=in_specs,
        out_specs=out_specs,
        out_shape=out_shape,
        scratch_shapes=[pltpu.VMEM((tm, d), F32)],
        compiler_params=_params(("arbitrary", "arbitrary")),
        name="mlp",
    )(*args)


DIL_TQ = 128
DIL_HALF = 64
assert all(w // (2 * d) == DIL_HALF for w, d in DIL_GROUPS)


def _dil_geometry(seq, dilation):
    length = seq // dilation
    kwin = min(2 * DIL_TQ, length)
    return length, kwin, length // DIL_TQ


def _t5_log_thresholds():
    nb = N_BUCKETS // 2
    max_exact = nb // 2
    steps = nb - max_exact
    thresholds = []
    for k in range(1, steps):
        n = max_exact
        while math.floor(math.log(n / max_exact) / math.log(MAX_DISTANCE / max_exact) * steps) < k:
            n += 1
        thresholds.append(n)
    return thresholds


def _t5_bucket(rel):
    nb = N_BUCKETS // 2
    max_exact = nb // 2
    n = jnp.abs(rel)
    large = max_exact
    for thr in _t5_log_thresholds():
        large = large + jnp.where(n >= thr, 1, 0)
    return jnp.where(rel > 0, nb, 0) + jnp.where(n < max_exact, n, large)


BIAS_ROWS = 16


def _dil_bias_kernel(tab_ref, o_ref, *, group, dilation, kwin, n_var):
    var = pl.program_id(0)

    def rows(step, carry):
        r0 = pl.multiple_of(step * BIAS_ROWS, BIAS_ROWS)
        kk = lax.broadcasted_iota(jnp.int32, (BIAS_ROWS, DIL_TQ), 0) + r0
        qi = lax.broadcasted_iota(jnp.int32, (BIAS_ROWS, DIL_TQ), 1)
        rel_a = kk - qi - (DIL_HALF if n_var > 1 else 0)
        valid = jnp.abs(rel_a) <= DIL_HALF
        if n_var > 1:
            crossing = jnp.where((qi < DIL_HALF) == (kk < DIL_TQ), 0, var)
            valid = jnp.where(valid, 1, 0) - crossing == 1
        bucket = _t5_bucket(rel_a * dilation)
        accs = [jnp.zeros((BIAS_ROWS, DIL_TQ), F32)] * DIL_HEADS
        for b in range(N_BUCKETS):
            hit = bucket == b
            accs = [jnp.where(hit, tab_ref[b, group * DIL_HEADS + h], a) for h, a in enumerate(accs)]
        for h, a in enumerate(accs):
            o_ref[0, h // 2, pl.ds(r0, BIAS_ROWS), (h % 2) * DIL_TQ:(h % 2 + 1) * DIL_TQ] = jnp.where(
                valid, a * LOG2E, MASKED)
        return carry

    lax.fori_loop(0, kwin // BIAS_ROWS, rows, 0)


def _dil_bias(rel_bias, group, seq):
    _, dilation = DIL_GROUPS[group]
    _, kwin, n_tiles = _dil_geometry(seq, dilation)
    n_var = 1 if n_tiles == 1 else 2
    n_pairs = DIL_HEADS // 2
    kern = functools.partial(_dil_bias_kernel, group=group, dilation=dilation, kwin=kwin, n_var=n_var)
    return pl.pallas_call(
        kern,
        grid=(n_var,),
        in_specs=[pl.BlockSpec(memory_space=pltpu.SMEM)],
        out_specs=pl.BlockSpec((1, n_pairs, kwin, 2 * DIL_TQ), lambda v: (v, 0, 0, 0)),
        out_shape=jax.ShapeDtypeStruct((n_var, n_pairs, kwin, 2 * DIL_TQ), F32),
        compiler_params=_params(("arbitrary",)),
        name=f"dil_bias{group}",
    )(rel_bias)


DIL_PROJ_ROWS = 512


def _dil_proj_kernel(h_ref, wqf_ref, wkf_ref, wvf_ref, gq_ref, qT_ref, k_ref, vT_ref, wq_ref, wk_ref, wv_ref,
                     *scratch, dilation, rb, ra):
    hd = DIL_HEAD_DIM
    d_model = h_ref.shape[2]
    n_slabs = d_model // LANES
    mt = rb * ra

    @pl.when((pl.program_id(0) == 0) & (pl.program_id(1) == 0) & (pl.program_id(2) == 0))
    def _():
        wq_ref[...] = wqf_ref[0].T.astype(BF16)
        wk_ref[...] = wkf_ref[0].astype(BF16)
        wv_ref[...] = wvf_ref[0].T.astype(BF16)

    if dilation == 1:
        hcat = h_ref[0]
    else:
        (scr,) = scratch
        sub = pl.program_id(2)

        @pl.when(sub == 0)
        def _():
            for s in range(n_slabs):
                scr[s] = h_ref[0, :, s * LANES:(s + 1) * LANES].astype(F32)

        rows = []
        for j in range(rb):
            res = sub * rb + j
            rows.append(jnp.concatenate(
                [scr[s, pl.ds(res, ra, stride=dilation), :] for s in range(n_slabs)], axis=1))
        hcat = jnp.concatenate(rows, axis=0).astype(BF16)

    q_t = lax.dot_general(wq_ref[...], hcat, NT_DIMS, preferred_element_type=F32)
    gq = jnp.concatenate([gq_ref[...]] * (mt // LANES), axis=1)
    for h in range(DIL_HEADS):
        blk = q_t[h * hd:(h + 1) * hd]
        r = lax.rsqrt(jnp.sum(blk * blk, axis=0, keepdims=True) * (1.0 / hd) + EPS)
        qb = (blk * r * gq).astype(BF16)
        for j in range(rb):
            for s in range(ra // LANES):
                lo = j * ra + s * LANES
                qT_ref[0, j, s, h * hd:(h + 1) * hd, :] = qb[:, lo:lo + LANES]

    kf = jnp.dot(hcat, wk_ref[...], preferred_element_type=F32)
    lane = lax.broadcasted_iota(jnp.int32, (mt, LANES), 1)
    low = lane < hd
    for c in range(DIL_HEADS * hd // LANES):
        y = kf[:, c * LANES:(c + 1) * LANES]
        y2 = y * y
        s_lo = jnp.sum(jnp.where(low, y2, 0.0), axis=1, keepdims=True)
        s_hi = jnp.sum(jnp.where(low, 0.0, y2), axis=1, keepdims=True)
        r = lax.rsqrt(jnp.where(low, s_lo, s_hi) * (1.0 / hd) + EPS)
        kb = (y * r).astype(BF16)
        for j in range(rb):
            k_ref[0, j, :, c * LANES:(c + 1) * LANES] = kb[j * ra:(j + 1) * ra]

    v_t = lax.dot_general(wv_ref[...], hcat, NT_DIMS, preferred_element_type=F32).astype(BF16)
    for j in range(rb):
        for s in range(ra // LANES):
            lo = j * ra + s * LANES
            vT_ref[0, j, s] = v_t[:, lo:lo + LANES]


def _dil_proj(hn3, w_in, g_q, g_k, group):
    batch, seq, d_model = hn3.shape
    _, dilation = DIL_GROUPS[group]
    length = seq // dilation
    c = DIL_HEADS * DIL_HEAD_DIM
    ra = DIL_PROJ_ROWS if dilation == 1 else DIL_TQ
    rb = DIL_PROJ_ROWS // ra
    tok = ra * dilation
    n_sub = dilation // rb
    gq_col = jnp.broadcast_to((g_q * g_k * (DIL_HEAD_DIM ** -0.5 * LOG2E))[:, None], (DIL_HEAD_DIM, LANES))
    kern = functools.partial(_dil_proj_kernel, dilation=dilation, rb=rb, ra=ra)
    const = lambda shape: pl.BlockSpec(shape, lambda b, t, s: (0,) * len(shape))
    w_blk = lambda j: pl.BlockSpec((1, d_model, c), lambda b, t, s: (0, 0, 3 * group + j))
    scratch = [pltpu.VMEM((c, d_model), BF16), pltpu.VMEM((d_model, c), BF16), pltpu.VMEM((c, d_model), BF16)]
    if dilation > 1:
        scratch.append(pltpu.VMEM((d_model // LANES, tok, LANES), F32))
    return pl.pallas_call(
        kern,
        grid=(batch, seq // tok, n_sub),
        in_specs=[
            pl.BlockSpec((1, tok, d_model), lambda b, t, s: (b, t, 0)),
            w_blk(0), w_blk(1), w_blk(2),
            const((DIL_HEAD_DIM, LANES)),
        ],
        out_specs=[
            pl.BlockSpec((1, rb, ra // LANES, c, LANES), lambda b, t, s: (b, s, t, 0, 0)),
            pl.BlockSpec((1, rb, ra, c), lambda b, t, s: (b, s, t, 0)),
            pl.BlockSpec((1, rb, ra // LANES, c, LANES), lambda b, t, s: (b, s, t, 0, 0)),
        ],
        out_shape=[
            jax.ShapeDtypeStruct((batch, dilation, length // LANES, c, LANES), BF16),
            jax.ShapeDtypeStruct((batch, dilation, length, c), BF16),
            jax.ShapeDtypeStruct((batch, dilation, length // LANES, c, LANES), BF16),
        ],
        scratch_shapes=scratch,
        compiler_params=_params(("arbitrary", "arbitrary", "arbitrary")),
        name=f"dil_proj{group}",
    )(hn3, w_in, w_in, w_in, gq_col)


def _dil_attn_kernel(safe_ref, qT_ref, k_ref, vT_ref, bm_ref, o_ref, st_ref, s_ref, *, n_tiles, rb):
    tq = DIL_TQ
    hd = DIL_HEAD_DIM
    n_pairs = DIL_HEADS // 2
    c = DIL_HEADS * hd
    shifted = n_tiles > 1
    low = lax.broadcasted_iota(jnp.int32, (LANES, tq), 0) < hd
    ones = jnp.ones((ONES_ROWS, 2 * tq if shifted else tq), BF16)
    prow = lambda p: slice(p * LANES, (p + 1) * LANES)

    def block_diag(q2):
        zero = jnp.zeros_like(q2)
        return jnp.concatenate([jnp.where(low, q2, zero), jnp.where(low, zero, q2)], axis=1)

    def attend(s_t, v2, use_max):
        if use_max:
            m = jnp.max(s_t, axis=0, keepdims=True)
            pb = jnp.exp2(s_t - m).astype(BF16)
        else:
            pb = jnp.exp2(s_t).astype(BF16)
        res = jnp.dot(jnp.concatenate([v2, ones], axis=0), pb, preferred_element_type=F32)
        den = res[LANES:LANES + 1]
        rinv = 1.0 / den
        o_pair = jnp.concatenate([res[:hd, :tq] * rinv[:, :tq], res[hd:LANES, tq:] * rinv[:, tq:]], axis=0)
        lse = jnp.log2(den) + m if use_max else jnp.log2(den)
        lse = lse * LN2
        return o_pair.T.astype(BF16), [lse[:, :tq], lse[:, tq:]]

    def stats_rows(stats):
        return jnp.concatenate(stats + [jnp.zeros((LANES - DIL_HEADS, tq), F32)], axis=0).T

    tiles = [(j, t) for j in range(rb) for t in range(n_tiles)]

    @pl.when(safe_ref[0] == 1)
    def _():
        def score_tile(i):
            j, t = tiles[i]
            t1 = (t + 1) % n_tiles
            for p in range(n_pairs):
                if shifted:
                    q2 = jnp.concatenate(
                        [qT_ref[0, j, t, prow(p), DIL_HALF:], qT_ref[0, j, t1, prow(p), :DIL_HALF]], axis=1)
                    k2 = jnp.concatenate([k_ref[0, j, t * tq:(t + 1) * tq, prow(p)],
                                          k_ref[0, j, t1 * tq:(t1 + 1) * tq, prow(p)]], axis=0)
                else:
                    q2 = qT_ref[0, j, 0, prow(p), :]
                    k2 = k_ref[0, j, :, prow(p)]
                s_ref[i % 2, p] = jnp.dot(k2, block_diag(q2), preferred_element_type=F32)

        def attend_tile(i):
            j, t = tiles[i]
            t1 = (t + 1) % n_tiles
            var = 1 if shifted and t + 1 == n_tiles else 0

            def store(ref, lanes, val):
                if shifted:
                    ref[0, t * tq + DIL_HALF:(t + 1) * tq, lanes] = val[:DIL_HALF]
                    ref[0, t1 * tq:t1 * tq + DIL_HALF, lanes] = val[DIL_HALF:]
                else:
                    ref[0, :, lanes] = val

            stats = []
            for p in range(n_pairs):
                if shifted:
                    v2 = jnp.concatenate([vT_ref[0, j, t, prow(p), :], vT_ref[0, j, t1, prow(p), :]], axis=1)
                else:
                    v2 = vT_ref[0, j, 0, prow(p), :]
                o_rows, lses = attend(s_ref[i % 2, p] + bm_ref[var, p], v2, use_max=False)
                store(o_ref, slice(j * c + p * LANES, j * c + (p + 1) * LANES), o_rows)
                stats += lses
            store(st_ref, slice(j * LANES, (j + 1) * LANES), stats_rows(stats))

        score_tile(0)
        for i in range(len(tiles)):
            if i + 1 < len(tiles):
                score_tile(i + 1)
            attend_tile(i)

    @pl.when(safe_ref[0] != 1)
    def _():
        def tile(j, t):
            if shifted:
                last = t + 1 == n_tiles
                t1 = jnp.where(last, 0, t + 1)
                var = jnp.where(last, 1, 0)
                r0 = pl.multiple_of(t * tq, tq)
                r1 = pl.multiple_of(t1 * tq, tq)

            def store(ref, lanes, val):
                if shifted:
                    ref[0, pl.ds(r0 + DIL_HALF, DIL_HALF), lanes] = val[:DIL_HALF]
                    ref[0, pl.ds(r1, DIL_HALF), lanes] = val[DIL_HALF:]
                else:
                    ref[0, :, lanes] = val

            stats = []
            for p in range(n_pairs):
                if shifted:
                    q2 = jnp.concatenate(
                        [qT_ref[0, j, t, prow(p), DIL_HALF:], qT_ref[0, j, t1, prow(p), :DIL_HALF]], axis=1)
                    k2 = jnp.concatenate(
                        [k_ref[0, j, pl.ds(r0, tq), prow(p)], k_ref[0, j, pl.ds(r1, tq), prow(p)]], axis=0)
                    v2 = jnp.concatenate([vT_ref[0, j, t, prow(p), :], vT_ref[0, j, t1, prow(p), :]], axis=1)
                    bias = bm_ref[var, p]
                else:
                    q2 = qT_ref[0, j, 0, prow(p), :]
                    k2 = k_ref[0, j, :, prow(p)]
                    v2 = vT_ref[0, j, 0, prow(p), :]
                    bias = bm_ref[0, p]
                s_t = jnp.dot(k2, block_diag(q2), preferred_element_type=F32) + bias
                o_rows, lses = attend(s_t, v2, use_max=True)
                store(o_ref, slice(j * c + p * LANES, j * c + (p + 1) * LANES), o_rows)
                stats += lses
            store(st_ref, slice(j * LANES, (j + 1) * LANES), stats_rows(stats))

        for j in range(rb):
            if shifted:
                lax.fori_loop(0, n_tiles, lambda t, carry, j=j: (tile(j, t), carry)[1], 0)
            else:
                tile(j, 0)


def _dil_attn(q_t, k, v_t, bias, safe, group, seq):
    batch = q_t.shape[0]
    _, dilation = DIL_GROUPS[group]
    length, kwin, n_tiles = _dil_geometry(seq, dilation)
    c = DIL_HEADS * DIL_HEAD_DIM
    n_slabs = length // LANES
    rb = 1 if n_tiles > 1 else min(dilation, 4)
    kern = functools.partial(_dil_attn_kernel, n_tiles=n_tiles, rb=rb)
    return pl.pallas_call(
        kern,
        grid=(batch, dilation // rb),
        in_specs=[
            pl.BlockSpec(memory_space=pltpu.SMEM),
            pl.BlockSpec((1, rb, n_slabs, c, LANES), lambda b, r: (b, r, 0, 0, 0)),
            pl.BlockSpec((1, rb, length, c), lambda b, r: (b, r, 0, 0)),
            pl.BlockSpec((1, rb, n_slabs, c, LANES), lambda b, r: (b, r, 0, 0, 0)),
            pl.BlockSpec(bias.shape, lambda b, r: (0, 0, 0, 0)),
        ],
        out_specs=[
            pl.BlockSpec((1, length, rb * c), lambda b, r: (b, 0, r)),
            pl.BlockSpec((1, length, rb * LANES), lambda b, r: (b, 0, r)),
        ],
        out_shape=[
            jax.ShapeDtypeStruct((batch, length, dilation * c), BF16),
            jax.ShapeDtypeStruct((batch, length, dilation * LANES), F32),
        ],
        scratch_shapes=[pltpu.VMEM((2, DIL_HEADS // 2, kwin, 2 * DIL_TQ), F32)],
        compiler_params=_params(("arbitrary", "arbitrary")),
        name=f"dil_attn{group}",
    )(safe, q_t, k, v_t, bias)


def _dil_out_kernel(o0_ref, o1_ref, o2_ref, s0_ref, s1_ref, s2_ref, e_ref, wo_ref, x_ref, mod_ref, nmlp_ref,
                    x1_ref, h2_ref, *scratch):
    tm = x_ref.shape[0]
    c = wo_ref.shape[0]

    def token_major(o_ref, s_ref, oscr, sscr, dilation):
        ra = tm // dilation
        for r in range(dilation):
            for s in range(c // LANES):
                lo = r * c + s * LANES
                oscr[s, pl.ds(r, ra, stride=dilation), :] = o_ref[0, :, lo:lo + LANES].astype(F32)
            sscr[pl.ds(r, ra, stride=dilation), :] = s_ref[0, :, r * LANES:(r + 1) * LANES]

    token_major(o1_ref, s1_ref, scratch[0], scratch[1], DIL_GROUPS[1][1])
    token_major(o2_ref, s2_ref, scratch[2], scratch[3], DIL_GROUPS[2][1])
    e = e_ref[...]
    m = mod_ref[0]

    def expand(w):
        return jnp.dot(w.astype(BF16), e, preferred_element_type=F32)

    def merged(rows):
        gather = lambda oscr: jnp.concatenate([oscr[s, rows, :] for s in range(c // LANES)], axis=1)
        o0, l0 = o0_ref[0, rows, :].astype(F32), s0_ref[0, rows, :]
        o1, l1 = gather(scratch[0]), scratch[1][rows, :]
        o2, l2 = gather(scratch[2]), scratch[3][rows, :]
        mx = jnp.maximum(jnp.maximum(l0, l1), l2)
        e0, e1, e2 = jnp.exp(l0 - mx), jnp.exp(l1 - mx), jnp.exp(l2 - mx)
        rden = 1.0 / (e0 + e1 + e2)
        return (expand(e0 * rden) * o0 + expand(e1 * rden) * o1 + expand(e2 * rden) * o2).astype(BF16)

    def project(rows, o):
        y = jnp.dot(o, wo_ref[...], preferred_element_type=F32)
        x1 = x_ref[rows, :] + m[2:3] * y
        x1_ref[rows, :] = x1
        h2_ref[rows, :] = _ada_norm(x1, nmlp_ref[...], m[4:5], m[3:4]).astype(BF16)

    n_sub = 2
    blocks = [slice(i * (tm // n_sub), (i + 1) * (tm // n_sub)) for i in range(n_sub)]
    o = merged(blocks[0])
    for i, rows in enumerate(blocks):
        o_next = merged(blocks[i + 1]) if i + 1 < n_sub else None
        project(rows, o)
        o = o_next


def _dil_out(outs, stats, w_o, x2d, mod_l, nmlp, seq):
    t, d = x2d.shape
    c = w_o.shape[0]
    tm = 512
    tpb = seq // tm
    expand = jnp.repeat(jnp.eye(DIL_HEADS, dtype=F32), DIL_HEAD_DIM, axis=1)
    expand = jnp.pad(expand, ((0, LANES - DIL_HEADS), (0, 0))).astype(BF16)
    row = lambda w: pl.BlockSpec((tm, w), lambda i: (i, 0))
    grp = lambda g, w: pl.BlockSpec((1, tm // DIL_GROUPS[g][1], DIL_GROUPS[g][1] * w),
                                    lambda i: (i // tpb, i % tpb, 0))
    scratch = []
    for g in (1, 2):
        scratch += [pltpu.VMEM((c // LANES, tm, LANES), F32), pltpu.VMEM((tm, LANES), F32)]
    return pl.pallas_call(
        _dil_out_kernel,
        grid=(t // tm,),
        in_specs=[grp(0, c), grp(1, c), grp(2, c), grp(0, LANES), grp(1, LANES), grp(2, LANES),
                  pl.BlockSpec((LANES, c), lambda i: (0, 0)),
                  pl.BlockSpec((c, d), lambda i: (0, 0)),
                  row(d),
                  pl.BlockSpec((1, N_MOD, d), lambda i: (i // tpb, 0, 0)),
                  pl.BlockSpec((1, d), lambda i: (0, 0))],
        out_specs=[row(d), row(d)],
        out_shape=[jax.ShapeDtypeStruct((t, d), F32), jax.ShapeDtypeStruct((t, d), BF16)],
        scratch_shapes=scratch,
        compiler_params=_params(("arbitrary",)),
        name="dil_out",
    )(*outs, *stats, expand, w_o.astype(BF16), x2d, mod_l, nmlp[None, :])


def kernel(x, c, positions, ada_w, ada_b, norm_mix, norm_mlp, mlp_w1, mlp_w2, mla_w_in, mla_g_qa, mla_w_qb,
           mla_g_kva, mla_w_kvb, mla_g_q, mla_g_k, mla_w_o, dil_w_in, dil_g_q, dil_g_k, dil_w_o, rel_bias):
    batch, seq, d = x.shape
    depth = ada_w.shape[0]
    assert depth == 2 and len(DIL_GROUPS) == 3
    mod = _mod(c, ada_w, ada_b).reshape(depth, batch, N_MOD, d)
    x2d = x.reshape(batch * seq, d)

    n_heads = mla_w_qb.shape[2] // mla_g_q.shape[1]
    q_t, k_pad, v_t = _mla_proj(x2d, mod[0], norm_mix[0], positions, mla_w_in[0], mla_g_qa[0], mla_w_qb[0],
                                mla_g_kva[0], mla_w_kvb[0], mla_g_q[0], mla_g_k[0], batch, seq)
    o_t = _mla_attn(q_t, k_pad, v_t, _score_bound_is_safe(mla_g_q[0] * mla_g_k[0], mla_g_q.shape[1]), n_heads)
    x1, h2 = _attn_out(o_t, mla_w_o[0], x2d, mod[0], norm_mlp[0], seq)
    x2, hn = _mlp(x1, h2, mlp_w1, mlp_w2, 0, mod[0], seq, norm_mix[1], mod[1])

    outs, stats = [], []
    hn3 = hn.reshape(batch, seq, d)
    for g in range(len(DIL_GROUPS)):
        q_g, k_g, v_g = _dil_proj(hn3, dil_w_in, dil_g_q[0, g], dil_g_k[0, g], g)
        safe = _score_bound_is_safe(dil_g_q[0, g] * dil_g_k[0, g], DIL_HEAD_DIM,
                                    rel_bias[:, g * DIL_HEADS:(g + 1) * DIL_HEADS])
        o_g, st_g = _dil_attn(q_g, k_g, v_g, _dil_bias(rel_bias, g, seq), safe, g, seq)
        outs.append(o_g)
        stats.append(st_g)
    x3, h4 = _dil_out(outs, stats, dil_w_o[0], x2, mod[1], norm_mlp[1], seq)
    (x4,) = _mlp(x3, h4, mlp_w1, mlp_w2, 1, mod[1], seq)
    return x4.reshape(batch, seq, d)
```

```python
import functools
import math

import jax
import jax.numpy as jnp
from jax import lax
from jax.experimental import pallas as pl
from jax.experimental.pallas import tpu as pltpu

F32 = jnp.float32
BF16 = jnp.bfloat16

EPS = 1e-6
LOG2E = 1.4426950408889634
LN2 = 0.6931471805599453
MASKED = -1e30

N_MOD = 6
ROPE_THETA = 10000.0
MLA_ROPE = 32
DIL_GROUPS = ((128, 1), (512, 4), (2048, 16))
DIL_HEADS = 16
DIL_HEAD_DIM = 64
N_BUCKETS = 32
MAX_DISTANCE = 1024

LANES = 128
HEAD_PAD = 128

VMEM_LIMIT = 56 * 1024 * 1024

NT_DIMS = (((1,), (1,)), ((), ()))
TN_DIMS = (((0,), (0,)), ((), ()))


def _params(sem):
    return pltpu.CompilerParams(dimension_semantics=sem, vmem_limit_bytes=VMEM_LIMIT)


def _ada_norm(x, g, scale, shift):
    ms = jnp.mean(x * x, axis=-1, keepdims=True)
    return (x * lax.rsqrt(ms + EPS) * g) * (1.0 + scale) + shift


def _rms(x, g):
    ms = jnp.mean(x * x, axis=-1, keepdims=True)
    return x * lax.rsqrt(ms + EPS) * g


def _mod_kernel(c_ref, w_ref, b_ref, o_ref):
    c = c_ref[...]
    cond = c / (1.0 + jnp.exp(-c))
    o_ref[0] = (
        jnp.dot(cond.astype(BF16), w_ref[0].astype(BF16), preferred_element_type=F32) + b_ref[0]
    )


def _mod(c, ada_w, ada_b):
    depth, d, n = ada_w.shape
    b = c.shape[0]
    tn = 1536
    return pl.pallas_call(
        _mod_kernel,
        grid=(depth, n // tn),
        in_specs=[
            pl.BlockSpec((b, d), lambda l, j: (0, 0)),
            pl.BlockSpec((1, d, tn), lambda l, j: (l, 0, j)),
            pl.BlockSpec((1, 1, tn), lambda l, j: (l, 0, j)),
        ],
        out_specs=pl.BlockSpec((1, b, tn), lambda l, j: (l, 0, j)),
        out_shape=jax.ShapeDtypeStruct((depth, b, n), F32),
        compiler_params=_params(("arbitrary", "arbitrary")),
        name="mod",
    )(c, ada_w, ada_b.reshape(depth, 1, n))


def _mla_proj_kernel(x_ref, mod_ref, nmix_ref, pos_ref, inv_ref, wlat_ref, wkr_ref, gqa_ref, gkva_ref,
                     wqb_ref, wkn_ref, wv_ref, gq_ref, qT_ref, k_ref, vT_ref, *, n_heads, q_lora, qk_dim):
    x = x_ref[...]
    tm = x.shape[0]
    m = mod_ref[0]
    hb = _ada_norm(x, nmix_ref[...], m[1:2], m[0:1]).astype(BF16)
    lat = jnp.dot(hb, wlat_ref[...], preferred_element_type=F32)
    qn = _rms(lat[:, :q_lora], gqa_ref[...]).astype(BF16)
    kvn = _rms(lat[:, q_lora:], gkva_ref[...]).astype(BF16)

    half = MLA_ROPE // 2
    nope = qk_dim - MLA_ROPE
    pos = pos_ref[...].astype(F32)
    ang = jnp.concatenate([inv_ref[...]] * (tm // LANES), axis=1) * pos
    cos_t = jnp.cos(ang)
    sin_t = jnp.sin(ang)

    def rope_rows(blk):
        x1 = blk[nope:nope + half]
        x2 = blk[nope + half:qk_dim]
        return jnp.concatenate(
            [blk[:nope], x1 * cos_t - x2 * sin_t, x2 * cos_t + x1 * sin_t, blk[qk_dim:]], axis=0)

    kr_t = lax.dot_general(wkr_ref[...], hb, NT_DIMS, preferred_element_type=F32)
    kr = rope_rows(kr_t).T

    q_t = lax.dot_general(wqb_ref[...], qn, NT_DIMS, preferred_element_type=F32)
    gq = jnp.concatenate([gq_ref[...]] * (tm // LANES), axis=1)
    for h in range(n_heads):
        blk = rope_rows(q_t[h * HEAD_PAD:(h + 1) * HEAD_PAD])
        ssq = jnp.sum(blk * blk, axis=0, keepdims=True)
        r = lax.rsqrt(ssq * (1.0 / qk_dim) + EPS)
        qT_ref[0, 0, h * HEAD_PAD:(h + 1) * HEAD_PAD, :] = (blk * r * gq).astype(BF16)

    kn = jnp.dot(kvn, wkn_ref[...], preferred_element_type=F32)
    for h in range(n_heads):
        kb = kn[:, h * HEAD_PAD:(h + 1) * HEAD_PAD] + kr
        ssq = jnp.sum(kb * kb, axis=1, keepdims=True)
        r = lax.rsqrt(ssq * (1.0 / qk_dim) + EPS)
        k_ref[:, h * HEAD_PAD:(h + 1) * HEAD_PAD] = (kb * r).astype(BF16)

    vT_ref[0] = lax.dot_general(wv_ref[...], kvn, NT_DIMS, preferred_element_type=F32).astype(BF16)


def _mla_proj(x2d, mod_l, nmix, positions, w_in, g_qa, w_qb, g_kva, w_kvb, g_q, g_k, batch, seq):
    t, d = x2d.shape
    q_lora = g_qa.shape[0]
    kv_lora = g_kva.shape[0]
    qk_dim = g_q.shape[0]
    n_heads = w_qb.shape[1] // qk_dim
    nope = qk_dim - MLA_ROPE
    v_dim = w_kvb.shape[1] // n_heads - nope
    assert nope + v_dim == HEAD_PAD and qk_dim <= HEAD_PAD
    half = MLA_ROPE // 2
    tm = 512
    tpb = seq // tm

    w_lat = w_in[:, :q_lora + kv_lora].astype(BF16)
    w_kr = jnp.zeros((HEAD_PAD, d), F32).at[nope:qk_dim].set(w_in[:, q_lora + kv_lora:].T).astype(BF16)
    w_qb_t = jnp.pad(w_qb.T.reshape(n_heads, qk_dim, q_lora), ((0, 0), (0, HEAD_PAD - qk_dim), (0, 0)))
    w_qb_t = w_qb_t.reshape(n_heads * HEAD_PAD, q_lora).astype(BF16)
    w_kvb3 = w_kvb.reshape(kv_lora, n_heads, nope + v_dim)
    w_kn = jnp.where(jnp.arange(nope + v_dim) < nope, w_kvb3, 0.0).reshape(kv_lora, n_heads * HEAD_PAD).astype(BF16)
    w_v_t = w_kvb3[:, :, nope:].reshape(kv_lora, n_heads * v_dim).T.astype(BF16)
    q_gain = jnp.pad(g_q * g_k * (qk_dim ** -0.5 * LOG2E), (0, HEAD_PAD - qk_dim))
    gq_col = jnp.broadcast_to(q_gain[:, None], (HEAD_PAD, LANES))
    inv = 1.0 / (ROPE_THETA ** (jnp.arange(half, dtype=F32) / half))
    inv_tab = jnp.broadcast_to(inv[:, None], (half, LANES))
    pos_row = positions.reshape(1, t)

    full = lambda shape: pl.BlockSpec(shape, lambda i: (0,) * len(shape))
    kern = functools.partial(_mla_proj_kernel, n_heads=n_heads, q_lora=q_lora, qk_dim=qk_dim)
    return pl.pallas_call(
        kern,
        grid=(t // tm,),
        in_specs=[
            pl.BlockSpec((tm, d), lambda i: (i, 0)),
            pl.BlockSpec((1, N_MOD, d), lambda i: (i // tpb, 0, 0)),
            full((1, d)),
            pl.BlockSpec((1, tm), lambda i: (0, i)),
            full((half, LANES)),
            full(w_lat.shape), full(w_kr.shape), full((1, q_lora)), full((1, kv_lora)),
            full(w_qb_t.shape), full(w_kn.shape), full(w_v_t.shape),
            full((HEAD_PAD, LANES)),
        ],
        out_specs=[
            pl.BlockSpec((1, 1, n_heads * HEAD_PAD, tm), lambda i: (i // tpb, i % tpb, 0, 0)),
            pl.BlockSpec((tm, n_heads * HEAD_PAD), lambda i: (i, 0)),
            pl.BlockSpec((1, n_heads * v_dim, tm), lambda i: (i // tpb, 0, i % tpb)),
        ],
        out_shape=[
            jax.ShapeDtypeStruct((batch, tpb, n_heads * HEAD_PAD, tm), BF16),
            jax.ShapeDtypeStruct((t, n_heads * HEAD_PAD), BF16),
            jax.ShapeDtypeStruct((batch, n_heads * v_dim, seq), BF16),
        ],
        compiler_params=_params(("arbitrary",)),
        name="mla_proj",
    )(x2d, mod_l, nmix[None, :], pos_row, inv_tab, w_lat, w_kr, g_qa[None, :], g_kva[None, :],
      w_qb_t, w_kn, w_v_t, gq_col)


ONES_ROWS = 16


SAFE_EXPONENT = 60.0


def _score_bound_is_safe(gain_product, head_dim, bias=None):
    bound = 1.02 * math.sqrt(head_dim) * LOG2E * jnp.max(jnp.abs(gain_product))
    if bias is not None:
        bound = bound + LOG2E * jnp.max(jnp.abs(bias))
    return (bound <= SAFE_EXPONENT).astype(jnp.int32).reshape(1)


def _mla_attn_kernel(safe_ref, qT_ref, k_ref, vT_ref, oT_ref, s_ref, *, hb, tn, kc):
    seq = k_ref.shape[1]
    n_tiles, _, tq = qT_ref.shape[1:]
    v_dim = vT_ref.shape[1] // hb
    ones = jnp.ones((ONES_ROWS, seq), BF16)
    n_chunks = seq // kc
    n_cols = tq // tn
    hrows = lambda h: slice(h * HEAD_PAD, (h + 1) * HEAD_PAD)
    vrows = lambda h: slice(h * v_dim, (h + 1) * v_dim)
    v_ext = lambda h: jnp.concatenate([vT_ref[0, vrows(h), :], ones], axis=0)

    @pl.when(safe_ref[0] == 1)
    def _():
        tiles = [(h, i) for h in range(hb) for i in range(n_tiles)]

        def scores(t, c):
            h, i = tiles[t]
            s_ref[t % 2, c * kc:(c + 1) * kc, :] = jnp.dot(
                k_ref[0, c * kc:(c + 1) * kc, hrows(h)], qT_ref[0, i, hrows(h), :], preferred_element_type=F32)

        for c in range(n_chunks):
            scores(0, c)
        for t, (h, i) in enumerate(tiles):
            ve = v_ext(h)
            for j in range(n_cols):
                acc = jnp.zeros((v_dim + ONES_ROWS, tn), F32)
                for c in range(n_chunks):
                    step = j * n_chunks + c
                    if t + 1 < len(tiles) and step % n_cols == 0:
                        scores(t + 1, step // n_cols)
                    p = jnp.exp2(s_ref[t % 2, c * kc:(c + 1) * kc, j * tn:(j + 1) * tn]).astype(BF16)
                    acc = acc + jnp.dot(ve[:, c * kc:(c + 1) * kc], p, preferred_element_type=F32)
                o = acc[:v_dim] * (1.0 / acc[v_dim:v_dim + 1])
                oT_ref[0, i, vrows(h), j * tn:(j + 1) * tn] = o.astype(BF16)

    @pl.when(safe_ref[0] != 1)
    def _():
        for h in range(hb):
            ve = v_ext(h)
            k_h = k_ref[0, :, hrows(h)]

            def tile(i, carry):
                s_t = jnp.dot(k_h, qT_ref[0, i, hrows(h), :], preferred_element_type=F32)
                p = jnp.exp2(s_t - jnp.max(s_t, axis=0, keepdims=True)).astype(BF16)
                r = jnp.dot(ve, p, preferred_element_type=F32)
                oT_ref[0, i, vrows(h), :] = (r[:v_dim] * (1.0 / r[v_dim:v_dim + 1])).astype(BF16)
                return carry

            lax.fori_loop(0, n_tiles, tile, 0)


def _mla_attn(q_t, k_pad, v_t, safe, n_heads):
    batch, n_tiles, _, tq = q_t.shape
    seq = n_tiles * tq
    v_dim = v_t.shape[1] // n_heads
    k3 = k_pad.reshape(batch, seq, n_heads * HEAD_PAD)
    hb = 2
    return pl.pallas_call(
        functools.partial(_mla_attn_kernel, hb=hb, tn=256, kc=256),
        grid=(batch, n_heads // hb),
        in_specs=[
            pl.BlockSpec(memory_space=pltpu.SMEM),
            pl.BlockSpec((1, n_tiles, hb * HEAD_PAD, tq), lambda b, h: (b, 0, h, 0)),
            pl.BlockSpec((1, seq, hb * HEAD_PAD), lambda b, h: (b, 0, h)),
            pl.BlockSpec((1, hb * v_dim, seq), lambda b, h: (b, h, 0)),
        ],
        out_specs=pl.BlockSpec((1, n_tiles, hb * v_dim, tq), lambda b, h: (b, 0, h, 0)),
        out_shape=jax.ShapeDtypeStruct((batch, n_tiles, n_heads * v_dim, tq), BF16),
        scratch_shapes=[pltpu.VMEM((2, seq, tq), F32)],
        compiler_params=_params(("arbitrary", "arbitrary")),
        name="mla_attn",
    )(safe, q_t, k3, v_t)


MIX_ROWS = 512
MIX_SUB = 256
FF_CHUNK = 2048


def _load_bf16(src, dst, stage, sem):
    chunk = stage.shape[1]
    n = dst.shape[0] // chunk

    def copy(i):
        return pltpu.make_async_copy(src.at[pl.ds(i * chunk, chunk), :], stage.at[i % 2], sem.at[i % 2])

    copy(0).start()
    for i in range(n):
        if i + 1 < n:
            copy(i + 1).start()
        copy(i).wait()
        dst[i * chunk:(i + 1) * chunk, :] = stage[i % 2].astype(BF16)


def _mlp_rows(h2, w1_ref, w2_ref):
    z = None
    for f in range(w1_ref.shape[1] // FF_CHUNK):
        cols = slice(f * FF_CHUNK, (f + 1) * FF_CHUNK)
        u = jnp.dot(h2, w1_ref[:, cols], preferred_element_type=F32)
        u = jnp.square(jnp.maximum(u, 0.0)).astype(BF16)
        part = jnp.dot(u, w2_ref[cols, :], preferred_element_type=F32)
        z = part if z is None else z + part
    return z


def _mix_mlp_body(mixer_out, x_ref, mod_ref, nmlp_ref, next_refs, x2_ref, hn_ref, wo_ref, w1_ref, w2_ref):
    m = mod_ref[0]
    tm = x_ref.shape[0]
    blocks = [slice(i * MIX_SUB, (i + 1) * MIX_SUB) for i in range(tm // MIX_SUB)]

    def mix(rows):
        x1 = x_ref[rows, :] + m[2:3] * mixer_out(rows, wo_ref)
        return x1, _ada_norm(x1, nmlp_ref[...], m[4:5], m[3:4]).astype(BF16)

    cur = mix(blocks[0])
    for i, rows in enumerate(blocks):
        nxt = mix(blocks[i + 1]) if i + 1 < len(blocks) else None
        x1, h2 = cur
        x2 = x1 + m[5:6] * _mlp_rows(h2, w1_ref, w2_ref)
        x2_ref[rows, :] = x2
        if next_refs is not None:
            nnext_ref, modn_ref = next_refs
            mn = modn_ref[0]
            hn_ref[rows, :] = _ada_norm(x2, nnext_ref[...], mn[1:2], mn[0:1]).astype(BF16)
        cur = nxt


def _load_mix_weights(wo_hbm, w1_hbm, w2_hbm, layer, wo_ref, w1_ref, w2_ref, stage_w, stage_n, sem):
    _load_bf16(wo_hbm, wo_ref, stage_n, sem)
    _load_bf16(w1_hbm.at[layer], w1_ref, stage_w, sem)
    _load_bf16(w2_hbm.at[layer], w2_ref, stage_n, sem)


def _mla_mlp_kernel(oT_ref, x_ref, mod_ref, nmlp_ref, nnext_ref, modn_ref, wo_hbm, w1_hbm, w2_hbm, x2_ref, hn_ref,
                    wo_ref, w1_ref, w2_ref, stage_w, stage_n, sem, *, layer):
    @pl.when(pl.program_id(0) == 0)
    def _():
        _load_mix_weights(wo_hbm, w1_hbm, w2_hbm, layer, wo_ref, w1_ref, w2_ref, stage_w, stage_n, sem)

    def mixer_out(rows, wo):
        return lax.dot_general(oT_ref[0, 0, :, rows], wo[...], TN_DIMS, preferred_element_type=F32)

    _mix_mlp_body(mixer_out, x_ref, mod_ref, nmlp_ref, (nnext_ref, modn_ref), x2_ref, hn_ref, wo_ref, w1_ref, w2_ref)


def _dil_mlp_kernel(o0_ref, o1_ref, o2_ref, s0_ref, s1_ref, s2_ref, e_ref, x_ref, mod_ref, nmlp_ref,
                    wo_hbm, w1_hbm, w2_hbm, x2_ref, wo_ref, w1_ref, w2_ref, stage_w, stage_n, sem,
                    oscr1, sscr1, oscr2, sscr2, *, layer):
    @pl.when(pl.program_id(0) == 0)
    def _():
        _load_mix_weights(wo_hbm, w1_hbm, w2_hbm, layer, wo_ref, w1_ref, w2_ref, stage_w, stage_n, sem)

    tm = x_ref.shape[0]
    c = wo_ref.shape[0]

    def token_major(o_ref, s_ref, oscr, sscr, dilation):
        ra = tm // dilation
        for r in range(dilation):
            for s in range(c // LANES):
                lo = r * c + s * LANES
                oscr[s, pl.ds(r, ra, stride=dilation), :] = o_ref[0, :, lo:lo + LANES].astype(F32)
            sscr[pl.ds(r, ra, stride=dilation), :] = s_ref[0, :, r * LANES:(r + 1) * LANES]

    token_major(o1_ref, s1_ref, oscr1, sscr1, DIL_GROUPS[1][1])
    token_major(o2_ref, s2_ref, oscr2, sscr2, DIL_GROUPS[2][1])
    e = e_ref[...]

    def expand(w):
        return jnp.dot(w.astype(BF16), e, preferred_element_type=F32)

    def mixer_out(rows, wo):
        gather = lambda oscr: jnp.concatenate([oscr[s, rows, :] for s in range(c // LANES)], axis=1)
        o0, l0 = o0_ref[0, rows, :].astype(F32), s0_ref[0, rows, :]
        o1, l1 = gather(oscr1), sscr1[rows, :]
        o2, l2 = gather(oscr2), sscr2[rows, :]
        mx = jnp.maximum(jnp.maximum(l0, l1), l2)
        e0, e1, e2 = jnp.exp(l0 - mx), jnp.exp(l1 - mx), jnp.exp(l2 - mx)
        rden = 1.0 / (e0 + e1 + e2)
        o = expand(e0 * rden) * o0 + expand(e1 * rden) * o1 + expand(e2 * rden) * o2
        return jnp.dot(o.astype(BF16), wo[...], preferred_element_type=F32)

    _mix_mlp_body(mixer_out, x_ref, mod_ref, nmlp_ref, None, x2_ref, None, wo_ref, w1_ref, w2_ref)


def _mix_mlp_call(kern, name, lead_specs, lead_args, w_o, w1, w2, x2d, mod_l, nmlp, seq, next_args, extra_scratch):
    t, d = x2d.shape
    ff = w1.shape[2]
    c = w_o.shape[0]
    tm = MIX_ROWS
    tpb = seq // tm
    row = pl.BlockSpec((tm, d), lambda i: (i, 0))
    mod_spec = pl.BlockSpec((1, N_MOD, d), lambda i: (i // tpb, 0, 0))
    vec = pl.BlockSpec((1, d), lambda i: (0, 0))
    hbm = pl.BlockSpec(memory_space=pl.ANY)
    in_specs = lead_specs + [row, mod_spec, vec]
    args = lead_args + [x2d, mod_l, nmlp[None, :]]
    out_specs, out_shape = [row], [jax.ShapeDtypeStruct((t, d), F32)]
    if next_args is not None:
        nnext, mod_next = next_args
        in_specs += [vec, mod_spec]
        args += [nnext[None, :], mod_next]
        out_specs.append(row)
        out_shape.append(jax.ShapeDtypeStruct((t, d), BF16))
    in_specs += [hbm, hbm, hbm]
    args += [w_o, w1, w2]
    scratch = [pltpu.VMEM((c, d), BF16), pltpu.VMEM((d, ff), BF16), pltpu.VMEM((ff, d), BF16),
               pltpu.VMEM((2, 64, ff), F32), pltpu.VMEM((2, 256, d), F32), pltpu.SemaphoreType.DMA((2,))]
    return pl.pallas_call(
        kern,
        grid=(t // tm,),
        in_specs=in_specs,
        out_specs=out_specs,
        out_shape=out_shape,
        scratch_shapes=scratch + extra_scratch,
        compiler_params=_params(("arbitrary",)),
        name=name,
    )(*args)


def _mla_mlp(o_t, w_o, w1, w2, layer, x2d, mod_l, nmlp, nnext, mod_next, seq):
    tm = MIX_ROWS
    tpb = seq // tm
    c = o_t.shape[2]
    assert o_t.shape[3] == tm
    lead = [pl.BlockSpec((1, 1, c, tm), lambda i: (i // tpb, i % tpb, 0, 0))]
    return _mix_mlp_call(functools.partial(_mla_mlp_kernel, layer=layer), "mla_mlp", lead, [o_t], w_o, w1, w2, x2d,
                         mod_l, nmlp, seq, (nnext, mod_next), [])


def _dil_mlp(outs, stats, w_o, w1, w2, layer, x2d, mod_l, nmlp, seq):
    tm = MIX_ROWS
    tpb = seq // tm
    c = w_o.shape[0]
    expand = jnp.repeat(jnp.eye(DIL_HEADS, dtype=F32), DIL_HEAD_DIM, axis=1)
    expand = jnp.pad(expand, ((0, LANES - DIL_HEADS), (0, 0))).astype(BF16)
    grp = lambda g, w: pl.BlockSpec((1, tm // DIL_GROUPS[g][1], DIL_GROUPS[g][1] * w),
                                    lambda i: (i // tpb, i % tpb, 0))
    lead = [grp(0, c), grp(1, c), grp(2, c), grp(0, LANES), grp(1, LANES), grp(2, LANES),
            pl.BlockSpec((LANES, c), lambda i: (0, 0))]
    extra = []
    for g in (1, 2):
        extra += [pltpu.VMEM((c // LANES, tm, LANES), F32), pltpu.VMEM((tm, LANES), F32)]
    (x2,) = _mix_mlp_call(functools.partial(_dil_mlp_kernel, layer=layer), "dil_mlp", lead, [*outs, *stats, expand],
                          w_o, w1, w2, x2d, mod_l, nmlp, seq, None, extra)
    return x2


def _attn_out_kernel(oT_ref, wo_ref, x_ref, mod_ref, nmlp_ref, x1_ref, h2_ref):
    m = mod_ref[0]
    tm = x_ref.shape[0]
    n_sub = 2
    blocks = [slice(i * (tm // n_sub), (i + 1) * (tm // n_sub)) for i in range(n_sub)]
    project = lambda rows: lax.dot_general(oT_ref[0, 0, :, rows], wo_ref[...], TN_DIMS, preferred_element_type=F32)
    y = project(blocks[0])
    for i, rows in enumerate(blocks):
        y_next = project(blocks[i + 1]) if i + 1 < n_sub else None
        x1 = x_ref[rows, :] + m[2:3] * y
        x1_ref[rows, :] = x1
        h2_ref[rows, :] = _ada_norm(x1, nmlp_ref[...], m[4:5], m[3:4]).astype(BF16)
        y = y_next


def _attn_out(o_t, w_o, x2d, mod_l, nmlp, seq):
    t, d = x2d.shape
    tm = 512
    tpb = seq // tm
    c = o_t.shape[2]
    return pl.pallas_call(
        _attn_out_kernel,
        grid=(t // tm,),
        in_specs=[
            pl.BlockSpec((1, 1, c, tm), lambda i: (i // tpb, i % tpb, 0, 0)),
            pl.BlockSpec((c, d), lambda i: (0, 0)),
            pl.BlockSpec((tm, d), lambda i: (i, 0)),
            pl.BlockSpec((1, N_MOD, d), lambda i: (i // tpb, 0, 0)),
            pl.BlockSpec((1, d), lambda i: (0, 0)),
        ],
        out_specs=[pl.BlockSpec((tm, d), lambda i: (i, 0)), pl.BlockSpec((tm, d), lambda i: (i, 0))],
        out_shape=[jax.ShapeDtypeStruct((t, d), F32), jax.ShapeDtypeStruct((t, d), BF16)],
        compiler_params=_params(("arbitrary",)),
        name="attn_out",
    )(o_t, w_o.astype(BF16), x2d, mod_l, nmlp[None, :])


def _mlp_kernel(x_ref, h_ref, w1_ref, w2_ref, mod_ref, *rest, with_next):
    if with_next:
        nnext_ref, modn_ref, x2_ref, hn_ref, acc_ref = rest
    else:
        x2_ref, acc_ref = rest
    j = pl.program_id(1)

    @pl.when(j == 0)
    def _():
        acc_ref[...] = jnp.zeros_like(acc_ref)

    u = jnp.dot(h_ref[...], w1_ref[0].astype(BF16), preferred_element_type=F32)
    u = jnp.square(jnp.maximum(u, 0.0)).astype(BF16)
    acc_ref[...] += jnp.dot(u, w2_ref[0].astype(BF16), preferred_element_type=F32)

    @pl.when(j == pl.num_programs(1) - 1)
    def _():
        x2 = x_ref[...] + mod_ref[0][5:6] * acc_ref[...]
        x2_ref[...] = x2
        if with_next:
            mn = modn_ref[0]
            hn_ref[...] = _ada_norm(x2, nnext_ref[...], mn[1:2], mn[0:1]).astype(BF16)


def _mlp(x1, h2, w1, w2, layer, mod_l, seq, nnext=None, mod_next=None):
    t, d = x1.shape
    ff = w1.shape[2]
    tm, tf = 1024, 1024
    tpb = seq // tm
    with_next = nnext is not None
    row = pl.BlockSpec((tm, d), lambda i, j: (i, 0))
    mod_spec = pl.BlockSpec((1, N_MOD, d), lambda i, j: (i // tpb, 0, 0))
    in_specs = [row, row, pl.BlockSpec((1, d, tf), lambda i, j: (layer, 0, j)),
                pl.BlockSpec((1, tf, d), lambda i, j: (layer, j, 0)), mod_spec]
    args = [x1, h2, w1, w2, mod_l]
    out_specs = [row]
    out_shape = [jax.ShapeDtypeStruct((t, d), F32)]
    if with_next:
        in_specs += [pl.BlockSpec((1, d), lambda i, j: (0, 0)), mod_spec]
        args += [nnext[None, :], mod_next]
        out_specs.append(row)
        out_shape.append(jax.ShapeDtypeStruct((t, d), BF16))
    return pl.pallas_call(
        functools.partial(_mlp_kernel, with_next=with_next),
        grid=(t // tm, ff // tf),
        in_specs=in_specs,
        out_specs=out_specs,
        out_shape=out_shape,
        scratch_shapes=[pltpu.VMEM((tm, d), F32)],
        compiler_params=_params(("arbitrary", "arbitrary")),
        name="mlp",
    )(*args)


DIL_TQ = 128
DIL_HALF = 64
assert all(w // (2 * d) == DIL_HALF for w, d in DIL_GROUPS)


def _dil_geometry(seq, dilation):
    length = seq // dilation
    kwin = min(2 * DIL_TQ, length)
    return length, kwin, length // DIL_TQ


def _t5_log_thresholds():
    nb = N_BUCKETS // 2
    max_exact = nb // 2
    steps = nb - max_exact
    thresholds = []
    for k in range(1, steps):
        n = max_exact
        while math.floor(math.log(n / max_exact) / math.log(MAX_DISTANCE / max_exact) * steps) < k:
            n += 1
        thresholds.append(n)
    return thresholds


def _t5_bucket(rel):
    nb = N_BUCKETS // 2
    max_exact = nb // 2
    n = jnp.abs(rel)
    large = max_exact
    for thr in _t5_log_thresholds():
        large = large + jnp.where(n >= thr, 1, 0)
    return jnp.where(rel > 0, nb, 0) + jnp.where(n < max_exact, n, large)


BIAS_ROWS = 16


def _dil_bias_kernel(tab_ref, o_ref, *, group, dilation, kwin, n_var):
    var = pl.program_id(0)

    def rows(step, carry):
        r0 = pl.multiple_of(step * BIAS_ROWS, BIAS_ROWS)
        kk = lax.broadcasted_iota(jnp.int32, (BIAS_ROWS, DIL_TQ), 0) + r0
        qi = lax.broadcasted_iota(jnp.int32, (BIAS_ROWS, DIL_TQ), 1)
        rel_a = kk - qi - (DIL_HALF if n_var > 1 else 0)
        valid = jnp.abs(rel_a) <= DIL_HALF
        if n_var > 1:
            crossing = jnp.where((qi < DIL_HALF) == (kk < DIL_TQ), 0, var)
            valid = jnp.where(valid, 1, 0) - crossing == 1
        bucket = _t5_bucket(rel_a * dilation)
        accs = [jnp.zeros((BIAS_ROWS, DIL_TQ), F32)] * DIL_HEADS
        for b in range(N_BUCKETS):
            hit = bucket == b
            accs = [jnp.where(hit, tab_ref[b, group * DIL_HEADS + h], a) for h, a in enumerate(accs)]
        for h, a in enumerate(accs):
            o_ref[0, h // 2, pl.ds(r0, BIAS_ROWS), (h % 2) * DIL_TQ:(h % 2 + 1) * DIL_TQ] = jnp.where(
                valid, a * LOG2E, MASKED)
        return carry

    lax.fori_loop(0, kwin // BIAS_ROWS, rows, 0)


def _dil_bias(rel_bias, group, seq):
    _, dilation = DIL_GROUPS[group]
    _, kwin, n_tiles = _dil_geometry(seq, dilation)
    n_var = 1 if n_tiles == 1 else 2
    n_pairs = DIL_HEADS // 2
    kern = functools.partial(_dil_bias_kernel, group=group, dilation=dilation, kwin=kwin, n_var=n_var)
    return pl.pallas_call(
        kern,
        grid=(n_var,),
        in_specs=[pl.BlockSpec(memory_space=pltpu.SMEM)],
        out_specs=pl.BlockSpec((1, n_pairs, kwin, 2 * DIL_TQ), lambda v: (v, 0, 0, 0)),
        out_shape=jax.ShapeDtypeStruct((n_var, n_pairs, kwin, 2 * DIL_TQ), F32),
        compiler_params=_params(("arbitrary",)),
        name=f"dil_bias{group}",
    )(rel_bias)


DIL_PROJ_ROWS = 512


def _dil_proj_kernel(h_ref, wqf_ref, wkf_ref, wvf_ref, gq_ref, qT_ref, k_ref, vT_ref, wq_ref, wk_ref, wv_ref,
                     *scratch, dilation, rb, ra):
    hd = DIL_HEAD_DIM
    d_model = h_ref.shape[2]
    n_slabs = d_model // LANES
    mt = rb * ra

    @pl.when((pl.program_id(0) == 0) & (pl.program_id(1) == 0) & (pl.program_id(2) == 0))
    def _():
        wq_ref[...] = wqf_ref[0].T.astype(BF16)
        wk_ref[...] = wkf_ref[0].astype(BF16)
        wv_ref[...] = wvf_ref[0].T.astype(BF16)

    if dilation == 1:
        hcat = h_ref[0]
    else:
        (scr,) = scratch
        sub = pl.program_id(2)

        @pl.when(sub == 0)
        def _():
            for s in range(n_slabs):
                scr[s] = h_ref[0, :, s * LANES:(s + 1) * LANES].astype(F32)

        rows = []
        for j in range(rb):
            res = sub * rb + j
            rows.append(jnp.concatenate(
                [scr[s, pl.ds(res, ra, stride=dilation), :] for s in range(n_slabs)], axis=1))
        hcat = jnp.concatenate(rows, axis=0).astype(BF16)

    q_t = lax.dot_general(wq_ref[...], hcat, NT_DIMS, preferred_element_type=F32)
    gq = jnp.concatenate([gq_ref[...]] * (mt // LANES), axis=1)
    for h in range(DIL_HEADS):
        blk = q_t[h * hd:(h + 1) * hd]
        r = lax.rsqrt(jnp.sum(blk * blk, axis=0, keepdims=True) * (1.0 / hd) + EPS)
        qb = (blk * r * gq).astype(BF16)
        for j in range(rb):
            for s in range(ra // LANES):
                lo = j * ra + s * LANES
                qT_ref[0, j, s, h * hd:(h + 1) * hd, :] = qb[:, lo:lo + LANES]

    kf = jnp.dot(hcat, wk_ref[...], preferred_element_type=F32)
    lane = lax.broadcasted_iota(jnp.int32, (mt, LANES), 1)
    low = lane < hd
    for c in range(DIL_HEADS * hd // LANES):
        y = kf[:, c * LANES:(c + 1) * LANES]
        y2 = y * y
        s_lo = jnp.sum(jnp.where(low, y2, 0.0), axis=1, keepdims=True)
        s_hi = jnp.sum(jnp.where(low, 0.0, y2), axis=1, keepdims=True)
        r = lax.rsqrt(jnp.where(low, s_lo, s_hi) * (1.0 / hd) + EPS)
        kb = (y * r).astype(BF16)
        for j in range(rb):
            k_ref[0, j, :, c * LANES:(c + 1) * LANES] = kb[j * ra:(j + 1) * ra]

    v_t = lax.dot_general(wv_ref[...], hcat, NT_DIMS, preferred_element_type=F32).astype(BF16)
    for j in range(rb):
        for s in range(ra // LANES):
            lo = j * ra + s * LANES
            vT_ref[0, j, s] = v_t[:, lo:lo + LANES]


def _dil_proj(hn3, w_in, g_q, g_k, group):
    batch, seq, d_model = hn3.shape
    _, dilation = DIL_GROUPS[group]
    length = seq // dilation
    c = DIL_HEADS * DIL_HEAD_DIM
    ra = DIL_PROJ_ROWS if dilation == 1 else DIL_TQ
    rb = DIL_PROJ_ROWS // ra
    tok = ra * dilation
    n_sub = dilation // rb
    gq_col = jnp.broadcast_to((g_q * g_k * (DIL_HEAD_DIM ** -0.5 * LOG2E))[:, None], (DIL_HEAD_DIM, LANES))
    kern = functools.partial(_dil_proj_kernel, dilation=dilation, rb=rb, ra=ra)
    const = lambda shape: pl.BlockSpec(shape, lambda b, t, s: (0,) * len(shape))
    w_blk = lambda j: pl.BlockSpec((1, d_model, c), lambda b, t, s: (0, 0, 3 * group + j))
    scratch = [pltpu.VMEM((c, d_model), BF16), pltpu.VMEM((d_model, c), BF16), pltpu.VMEM((c, d_model), BF16)]
    if dilation > 1:
        scratch.append(pltpu.VMEM((d_model // LANES, tok, LANES), F32))
    return pl.pallas_call(
        kern,
        grid=(batch, seq // tok, n_sub),
        in_specs=[
            pl.BlockSpec((1, tok, d_model), lambda b, t, s: (b, t, 0)),
            w_blk(0), w_blk(1), w_blk(2),
            const((DIL_HEAD_DIM, LANES)),
        ],
        out_specs=[
            pl.BlockSpec((1, rb, ra // LANES, c, LANES), lambda b, t, s: (b, s, t, 0, 0)),
            pl.BlockSpec((1, rb, ra, c), lambda b, t, s: (b, s, t, 0)),
            pl.BlockSpec((1, rb, ra // LANES, c, LANES), lambda b, t, s: (b, s, t, 0, 0)),
        ],
        out_shape=[
            jax.ShapeDtypeStruct((batch, dilation, length // LANES, c, LANES), BF16),
            jax.ShapeDtypeStruct((batch, dilation, length, c), BF16),
            jax.ShapeDtypeStruct((batch, dilation, length // LANES, c, LANES), BF16),
        ],
        scratch_shapes=scratch,
        compiler_params=_params(("arbitrary", "arbitrary", "arbitrary")),
        name=f"dil_proj{group}",
    )(hn3, w_in, w_in, w_in, gq_col)


def _dil_attn_kernel(safe_ref, qT_ref, k_ref, vT_ref, bm_ref, o_ref, st_ref, s_ref, *, n_tiles, rb):
    tq = DIL_TQ
    hd = DIL_HEAD_DIM
    n_pairs = DIL_HEADS // 2
    c = DIL_HEADS * hd
    shifted = n_tiles > 1
    low = lax.broadcasted_iota(jnp.int32, (LANES, tq), 0) < hd
    ones = jnp.ones((ONES_ROWS, 2 * tq if shifted else tq), BF16)
    prow = lambda p: slice(p * LANES, (p + 1) * LANES)

    def block_diag(q2):
        zero = jnp.zeros_like(q2)
        return jnp.concatenate([jnp.where(low, q2, zero), jnp.where(low, zero, q2)], axis=1)

    def attend(s_t, v2, use_max):
        if use_max:
            m = jnp.max(s_t, axis=0, keepdims=True)
            pb = jnp.exp2(s_t - m).astype(BF16)
        else:
            pb = jnp.exp2(s_t).astype(BF16)
        res = jnp.dot(jnp.concatenate([v2, ones], axis=0), pb, preferred_element_type=F32)
        den = res[LANES:LANES + 1]
        rinv = 1.0 / den
        o_pair = jnp.concatenate([res[:hd, :tq] * rinv[:, :tq], res[hd:LANES, tq:] * rinv[:, tq:]], axis=0)
        lse = jnp.log2(den) + m if use_max else jnp.log2(den)
        lse = lse * LN2
        return o_pair.T.astype(BF16), [lse[:, :tq], lse[:, tq:]]

    def stats_rows(stats):
        return jnp.concatenate(stats + [jnp.zeros((LANES - DIL_HEADS, tq), F32)], axis=0).T

    tiles = [(j, t) for j in range(rb) for t in range(n_tiles)]

    @pl.when(safe_ref[0] == 1)
    def _():
        def score_tile(i):
            j, t = tiles[i]
            t1 = (t + 1) % n_tiles
            for p in range(n_pairs):
                if shifted:
                    q2 = jnp.concatenate(
                        [qT_ref[0, j, t, prow(p), DIL_HALF:], qT_ref[0, j, t1, prow(p), :DIL_HALF]], axis=1)
                    k2 = jnp.concatenate([k_ref[0, j, t * tq:(t + 1) * tq, prow(p)],
                                          k_ref[0, j, t1 * tq:(t1 + 1) * tq, prow(p)]], axis=0)
                else:
                    q2 = qT_ref[0, j, 0, prow(p), :]
                    k2 = k_ref[0, j, :, prow(p)]
                s_ref[i % 2, p] = jnp.dot(k2, block_diag(q2), preferred_element_type=F32)

        def attend_tile(i):
            j, t = tiles[i]
            t1 = (t + 1) % n_tiles
            var = 1 if shifted and t + 1 == n_tiles else 0

            def store(ref, lanes, val):
                if shifted:
                    ref[0, t * tq + DIL_HALF:(t + 1) * tq, lanes] = val[:DIL_HALF]
                    ref[0, t1 * tq:t1 * tq + DIL_HALF, lanes] = val[DIL_HALF:]
                else:
                    ref[0, :, lanes] = val

            stats = []
            for p in range(n_pairs):
                if shifted:
                    v2 = jnp.concatenate([vT_ref[0, j, t, prow(p), :], vT_ref[0, j, t1, prow(p), :]], axis=1)
                else:
                    v2 = vT_ref[0, j, 0, prow(p), :]
                o_rows, lses = attend(s_ref[i % 2, p] + bm_ref[var, p], v2, use_max=False)
                store(o_ref, slice(j * c + p * LANES, j * c + (p + 1) * LANES), o_rows)
                stats += lses
            store(st_ref, slice(j * LANES, (j + 1) * LANES), stats_rows(stats))

        score_tile(0)
        for i in range(len(tiles)):
            if i + 1 < len(tiles):
                score_tile(i + 1)
            attend_tile(i)

    @pl.when(safe_ref[0] != 1)
    def _():
        def tile(j, t):
            if shifted:
                last = t + 1 == n_tiles
                t1 = jnp.where(last, 0, t + 1)
                var = jnp.where(last, 1, 0)
                r0 = pl.multiple_of(t * tq, tq)
                r1 = pl.multiple_of(t1 * tq, tq)

            def store(ref, lanes, val):
                if shifted:
                    ref[0, pl.ds(r0 + DIL_HALF, DIL_HALF), lanes] = val[:DIL_HALF]
                    ref[0, pl.ds(r1, DIL_HALF), lanes] = val[DIL_HALF:]
                else:
                    ref[0, :, lanes] = val

            stats = []
            for p in range(n_pairs):
                if shifted:
                    q2 = jnp.concatenate(
                        [qT_ref[0, j, t, prow(p), DIL_HALF:], qT_ref[0, j, t1, prow(p), :DIL_HALF]], axis=1)
                    k2 = jnp.concatenate(
                        [k_ref[0, j, pl.ds(r0, tq), prow(p)], k_ref[0, j, pl.ds(r1, tq), prow(p)]], axis=0)
                    v2 = jnp.concatenate([vT_ref[0, j, t, prow(p), :], vT_ref[0, j, t1, prow(p), :]], axis=1)
                    bias = bm_ref[var, p]
                else:
                    q2 = qT_ref[0, j, 0, prow(p), :]
                    k2 = k_ref[0, j, :, prow(p)]
                    v2 = vT_ref[0, j, 0, prow(p), :]
                    bias = bm_ref[0, p]
                s_t = jnp.dot(k2, block_diag(q2), preferred_element_type=F32) + bias
                o_rows, lses = attend(s_t, v2, use_max=True)
                store(o_ref, slice(j * c + p * LANES, j * c + (p + 1) * LANES), o_rows)
                stats += lses
            store(st_ref, slice(j * LANES, (j + 1) * LANES), stats_rows(stats))

        for j in range(rb):
            if shifted:
                lax.fori_loop(0, n_tiles, lambda t, carry, j=j: (tile(j, t), carry)[1], 0)
            else:
                tile(j, 0)


def _dil_attn(q_t, k, v_t, bias, safe, group, seq):
    batch = q_t.shape[0]
    _, dilation = DIL_GROUPS[group]
    length, kwin, n_tiles = _dil_geometry(seq, dilation)
    c = DIL_HEADS * DIL_HEAD_DIM
    n_slabs = length // LANES
    rb = 1 if n_tiles > 1 else min(dilation, 4)
    kern = functools.partial(_dil_attn_kernel, n_tiles=n_tiles, rb=rb)
    return pl.pallas_call(
        kern,
        grid=(batch, dilation // rb),
        in_specs=[
            pl.BlockSpec(memory_space=pltpu.SMEM),
            pl.BlockSpec((1, rb, n_slabs, c, LANES), lambda b, r: (b, r, 0, 0, 0)),
            pl.BlockSpec((1, rb, length, c), lambda b, r: (b, r, 0, 0)),
            pl.BlockSpec((1, rb, n_slabs, c, LANES), lambda b, r: (b, r, 0, 0, 0)),
            pl.BlockSpec(bias.shape, lambda b, r: (0, 0, 0, 0)),
        ],
        out_specs=[
            pl.BlockSpec((1, length, rb * c), lambda b, r: (b, 0, r)),
            pl.BlockSpec((1, length, rb * LANES), lambda b, r: (b, 0, r)),
        ],
        out_shape=[
            jax.ShapeDtypeStruct((batch, length, dilation * c), BF16),
            jax.ShapeDtypeStruct((batch, length, dilation * LANES), F32),
        ],
        scratch_shapes=[pltpu.VMEM((2, DIL_HEADS // 2, kwin, 2 * DIL_TQ), F32)],
        compiler_params=_params(("arbitrary", "arbitrary")),
        name=f"dil_attn{group}",
    )(safe, q_t, k, v_t, bias)


def _dil_out_kernel(o0_ref, o1_ref, o2_ref, s0_ref, s1_ref, s2_ref, e_ref, wo_ref, x_ref, mod_ref, nmlp_ref,
                    x1_ref, h2_ref, *scratch):
    tm = x_ref.shape[0]
    c = wo_ref.shape[0]

    def token_major(o_ref, s_ref, oscr, sscr, dilation):
        ra = tm // dilation
        for r in range(dilation):
            for s in range(c // LANES):
                lo = r * c + s * LANES
                oscr[s, pl.ds(r, ra, stride=dilation), :] = o_ref[0, :, lo:lo + LANES].astype(F32)
            sscr[pl.ds(r, ra, stride=dilation), :] = s_ref[0, :, r * LANES:(r + 1) * LANES]

    token_major(o1_ref, s1_ref, scratch[0], scratch[1], DIL_GROUPS[1][1])
    token_major(o2_ref, s2_ref, scratch[2], scratch[3], DIL_GROUPS[2][1])
    e = e_ref[...]
    m = mod_ref[0]

    def expand(w):
        return jnp.dot(w.astype(BF16), e, preferred_element_type=F32)

    def merged(rows):
        gather = lambda oscr: jnp.concatenate([oscr[s, rows, :] for s in range(c // LANES)], axis=1)
        o0, l0 = o0_ref[0, rows, :].astype(F32), s0_ref[0, rows, :]
        o1, l1 = gather(scratch[0]), scratch[1][rows, :]
        o2, l2 = gather(scratch[2]), scratch[3][rows, :]
        mx = jnp.maximum(jnp.maximum(l0, l1), l2)
        e0, e1, e2 = jnp.exp(l0 - mx), jnp.exp(l1 - mx), jnp.exp(l2 - mx)
        rden = 1.0 / (e0 + e1 + e2)
        return (expand(e0 * rden) * o0 + expand(e1 * rden) * o1 + expand(e2 * rden) * o2).astype(BF16)

    def project(rows, o):
        y = jnp.dot(o, wo_ref[...], preferred_element_type=F32)
        x1 = x_ref[rows, :] + m[2:3] * y
        x1_ref[rows, :] = x1
        h2_ref[rows, :] = _ada_norm(x1, nmlp_ref[...], m[4:5], m[3:4]).astype(BF16)

    n_sub = 2
    blocks = [slice(i * (tm // n_sub), (i + 1) * (tm // n_sub)) for i in range(n_sub)]
    o = merged(blocks[0])
    for i, rows in enumerate(blocks):
        o_next = merged(blocks[i + 1]) if i + 1 < n_sub else None
        project(rows, o)
        o = o_next


def _dil_out(outs, stats, w_o, x2d, mod_l, nmlp, seq):
    t, d = x2d.shape
    c = w_o.shape[0]
    tm = 512
    tpb = seq // tm
    expand = jnp.repeat(jnp.eye(DIL_HEADS, dtype=F32), DIL_HEAD_DIM, axis=1)
    expand = jnp.pad(expand, ((0, LANES - DIL_HEADS), (0, 0))).astype(BF16)
    row = lambda w: pl.BlockSpec((tm, w), lambda i: (i, 0))
    grp = lambda g, w: pl.BlockSpec((1, tm // DIL_GROUPS[g][1], DIL_GROUPS[g][1] * w),
                                    lambda i: (i // tpb, i % tpb, 0))
    scratch = []
    for g in (1, 2):
        scratch += [pltpu.VMEM((c // LANES, tm, LANES), F32), pltpu.VMEM((tm, LANES), F32)]
    return pl.pallas_call(
        _dil_out_kernel,
        grid=(t // tm,),
        in_specs=[grp(0, c), grp(1, c), grp(2, c), grp(0, LANES), grp(1, LANES), grp(2, LANES),
                  pl.BlockSpec((LANES, c), lambda i: (0, 0)),
                  pl.BlockSpec((c, d), lambda i: (0, 0)),
                  row(d),
                  pl.BlockSpec((1, N_MOD, d), lambda i: (i // tpb, 0, 0)),
                  pl.BlockSpec((1, d), lambda i: (0, 0))],
        out_specs=[row(d), row(d)],
        out_shape=[jax.ShapeDtypeStruct((t, d), F32), jax.ShapeDtypeStruct((t, d), BF16)],
        scratch_shapes=scratch,
        compiler_params=_params(("arbitrary",)),
        name="dil_out",
    )(*outs, *stats, expand, w_o.astype(BF16), x2d, mod_l, nmlp[None, :])


def kernel(x, c, positions, ada_w, ada_b, norm_mix, norm_mlp, mlp_w1, mlp_w2, mla_w_in, mla_g_qa, mla_w_qb,
           mla_g_kva, mla_w_kvb, mla_g_q, mla_g_k, mla_w_o, dil_w_in, dil_g_q, dil_g_k, dil_w_o, rel_bias):
    batch, seq, d = x.shape
    depth = ada_w.shape[0]
    assert depth == 2 and len(DIL_GROUPS) == 3
    mod = _mod(c, ada_w, ada_b).reshape(depth, batch, N_MOD, d)
    x2d = x.reshape(batch * seq, d)

    n_heads = mla_w_qb.shape[2] // mla_g_q.shape[1]
    q_t, k_pad, v_t = _mla_proj(x2d, mod[0], norm_mix[0], positions, mla_w_in[0], mla_g_qa[0], mla_w_qb[0],
                                mla_g_kva[0], mla_w_kvb[0], mla_g_q[0], mla_g_k[0], batch, seq)
    o_t = _mla_attn(q_t, k_pad, v_t, _score_bound_is_safe(mla_g_q[0] * mla_g_k[0], mla_g_q.shape[1]), n_heads)
    x2, hn = _mla_mlp(o_t, mla_w_o[0], mlp_w1, mlp_w2, 0, x2d, mod[0], norm_mlp[0], norm_mix[1], mod[1], seq)

    outs, stats = [], []
    hn3 = hn.reshape(batch, seq, d)
    for g in range(len(DIL_GROUPS)):
        q_g, k_g, v_g = _dil_proj(hn3, dil_w_in, dil_g_q[0, g], dil_g_k[0, g], g)
        safe = _score_bound_is_safe(dil_g_q[0, g] * dil_g_k[0, g], DIL_HEAD_DIM,
                                    rel_bias[:, g * DIL_HEADS:(g + 1) * DIL_HEADS])
        o_g, st_g = _dil_attn(q_g, k_g, v_g, _dil_bias(rel_bias, g, seq), safe, g, seq)
        outs.append(o_g)
        stats.append(st_g)
    x4 = _dil_mlp(outs, stats, dil_w_o[0], mlp_w1, mlp_w2, 1, x2, mod[1], norm_mlp[1], seq)
    return x4.reshape(batch, seq, d)
```

```python
import functools
import math

import jax
import jax.numpy as jnp
from jax import lax
from jax.experimental import pallas as pl
from jax.experimental.pallas import tpu as pltpu

F32 = jnp.float32
BF16 = jnp.bfloat16

EPS = 1e-6
LOG2E = 1.4426950408889634
LN2 = 0.6931471805599453
MASKED = -1e30

N_MOD = 6
ROPE_THETA = 10000.0
MLA_ROPE = 32
DIL_GROUPS = ((128, 1), (512, 4), (2048, 16))
DIL_HEADS = 16
DIL_HEAD_DIM = 64
N_BUCKETS = 32
MAX_DISTANCE = 1024

LANES = 128
HEAD_PAD = 128

VMEM_LIMIT = 56 * 1024 * 1024

NT_DIMS = (((1,), (1,)), ((), ()))
TN_DIMS = (((0,), (0,)), ((), ()))


def _params(sem):
    return pltpu.CompilerParams(dimension_semantics=sem, vmem_limit_bytes=VMEM_LIMIT)


def _ada_norm(x, g, scale, shift):
    ms = jnp.mean(x * x, axis=-1, keepdims=True)
    return (x * lax.rsqrt(ms + EPS) * g) * (1.0 + scale) + shift


def _rms(x, g):
    ms = jnp.mean(x * x, axis=-1, keepdims=True)
    return x * lax.rsqrt(ms + EPS) * g


def _mod_kernel(c_ref, w_ref, b_ref, o_ref):
    c = c_ref[...]
    cond = c / (1.0 + jnp.exp(-c))
    o_ref[0] = (
        jnp.dot(cond.astype(BF16), w_ref[0].astype(BF16), preferred_element_type=F32) + b_ref[0]
    )


def _mod(c, ada_w, ada_b):
    depth, d, n = ada_w.shape
    b = c.shape[0]
    tn = 1536
    return pl.pallas_call(
        _mod_kernel,
        grid=(depth, n // tn),
        in_specs=[
            pl.BlockSpec((b, d), lambda l, j: (0, 0)),
            pl.BlockSpec((1, d, tn), lambda l, j: (l, 0, j)),
            pl.BlockSpec((1, 1, tn), lambda l, j: (l, 0, j)),
        ],
        out_specs=pl.BlockSpec((1, b, tn), lambda l, j: (l, 0, j)),
        out_shape=jax.ShapeDtypeStruct((depth, b, n), F32),
        compiler_params=_params(("arbitrary", "arbitrary")),
        name="mod",
    )(c, ada_w, ada_b.reshape(depth, 1, n))


def _mla_proj_kernel(x_ref, mod_ref, nmix_ref, pos_ref, inv_ref, wlat_ref, wkr_ref, gqa_ref, gkva_ref,
                     wqb_ref, wkn_ref, wv_ref, gq_ref, qT_ref, k_ref, vT_ref, *, n_heads, q_lora, qk_dim):
    x = x_ref[...]
    tm = x.shape[0]
    m = mod_ref[0]
    hb = _ada_norm(x, nmix_ref[...], m[1:2], m[0:1]).astype(BF16)
    lat = jnp.dot(hb, wlat_ref[...], preferred_element_type=F32)
    qn = _rms(lat[:, :q_lora], gqa_ref[...]).astype(BF16)
    kvn = _rms(lat[:, q_lora:], gkva_ref[...]).astype(BF16)

    half = MLA_ROPE // 2
    nope = qk_dim - MLA_ROPE
    pos = pos_ref[...].astype(F32)
    ang = jnp.concatenate([inv_ref[...]] * (tm // LANES), axis=1) * pos
    cos_t = jnp.cos(ang)
    sin_t = jnp.sin(ang)

    def rope_rows(blk):
        x1 = blk[nope:nope + half]
        x2 = blk[nope + half:qk_dim]
        return jnp.concatenate(
            [blk[:nope], x1 * cos_t - x2 * sin_t, x2 * cos_t + x1 * sin_t, blk[qk_dim:]], axis=0)

    kr_t = lax.dot_general(wkr_ref[...], hb, NT_DIMS, preferred_element_type=F32)
    kr = rope_rows(kr_t).T

    q_t = lax.dot_general(wqb_ref[...], qn, NT_DIMS, preferred_element_type=F32)
    gq = jnp.concatenate([gq_ref[...]] * (tm // LANES), axis=1)
    for h in range(n_heads):
        blk = rope_rows(q_t[h * HEAD_PAD:(h + 1) * HEAD_PAD])
        ssq = jnp.sum(blk * blk, axis=0, keepdims=True)
        r = lax.rsqrt(ssq * (1.0 / qk_dim) + EPS)
        qT_ref[0, 0, h * HEAD_PAD:(h + 1) * HEAD_PAD, :] = (blk * r * gq).astype(BF16)

    kn = jnp.dot(kvn, wkn_ref[...], preferred_element_type=F32)
    for h in range(n_heads):
        kb = kn[:, h * HEAD_PAD:(h + 1) * HEAD_PAD] + kr
        ssq = jnp.sum(kb * kb, axis=1, keepdims=True)
        r = lax.rsqrt(ssq * (1.0 / qk_dim) + EPS)
        k_ref[:, h * HEAD_PAD:(h + 1) * HEAD_PAD] = (kb * r).astype(BF16)

    vT_ref[0] = lax.dot_general(wv_ref[...], kvn, NT_DIMS, preferred_element_type=F32).astype(BF16)


def _mla_proj(x2d, mod_l, nmix, positions, w_in, g_qa, w_qb, g_kva, w_kvb, g_q, g_k, batch, seq):
    t, d = x2d.shape
    q_lora = g_qa.shape[0]
    kv_lora = g_kva.shape[0]
    qk_dim = g_q.shape[0]
    n_heads = w_qb.shape[1] // qk_dim
    nope = qk_dim - MLA_ROPE
    v_dim = w_kvb.shape[1] // n_heads - nope
    assert nope + v_dim == HEAD_PAD and qk_dim <= HEAD_PAD
    half = MLA_ROPE // 2
    tm = 512
    tpb = seq // tm

    w_lat = w_in[:, :q_lora + kv_lora].astype(BF16)
    w_kr = jnp.zeros((HEAD_PAD, d), F32).at[nope:qk_dim].set(w_in[:, q_lora + kv_lora:].T).astype(BF16)
    w_qb_t = jnp.pad(w_qb.T.reshape(n_heads, qk_dim, q_lora), ((0, 0), (0, HEAD_PAD - qk_dim), (0, 0)))
    w_qb_t = w_qb_t.reshape(n_heads * HEAD_PAD, q_lora).astype(BF16)
    w_kvb3 = w_kvb.reshape(kv_lora, n_heads, nope + v_dim)
    w_kn = jnp.where(jnp.arange(nope + v_dim) < nope, w_kvb3, 0.0).reshape(kv_lora, n_heads * HEAD_PAD).astype(BF16)
    w_v_t = w_kvb3[:, :, nope:].reshape(kv_lora, n_heads * v_dim).T.astype(BF16)
    q_gain = jnp.pad(g_q * g_k * (qk_dim ** -0.5 * LOG2E), (0, HEAD_PAD - qk_dim))
    gq_col = jnp.broadcast_to(q_gain[:, None], (HEAD_PAD, LANES))
    inv = 1.0 / (ROPE_THETA ** (jnp.arange(half, dtype=F32) / half))
    inv_tab = jnp.broadcast_to(inv[:, None], (half, LANES))
    pos_row = positions.reshape(1, t)

    full = lambda shape: pl.BlockSpec(shape, lambda i: (0,) * len(shape))
    kern = functools.partial(_mla_proj_kernel, n_heads=n_heads, q_lora=q_lora, qk_dim=qk_dim)
    return pl.pallas_call(
        kern,
        grid=(t // tm,),
        in_specs=[
            pl.BlockSpec((tm, d), lambda i: (i, 0)),
            pl.BlockSpec((1, N_MOD, d), lambda i: (i // tpb, 0, 0)),
            full((1, d)),
            pl.BlockSpec((1, tm), lambda i: (0, i)),
            full((half, LANES)),
            full(w_lat.shape), full(w_kr.shape), full((1, q_lora)), full((1, kv_lora)),
            full(w_qb_t.shape), full(w_kn.shape), full(w_v_t.shape),
            full((HEAD_PAD, LANES)),
        ],
        out_specs=[
            pl.BlockSpec((1, 1, n_heads * HEAD_PAD, tm), lambda i: (i // tpb, i % tpb, 0, 0)),
            pl.BlockSpec((tm, n_heads * HEAD_PAD), lambda i: (i, 0)),
            pl.BlockSpec((1, n_heads * v_dim, tm), lambda i: (i // tpb, 0, i % tpb)),
        ],
        out_shape=[
            jax.ShapeDtypeStruct((batch, tpb, n_heads * HEAD_PAD, tm), BF16),
            jax.ShapeDtypeStruct((t, n_heads * HEAD_PAD), BF16),
            jax.ShapeDtypeStruct((batch, n_heads * v_dim, seq), BF16),
        ],
        compiler_params=_params(("arbitrary",)),
        name="mla_proj",
    )(x2d, mod_l, nmix[None, :], pos_row, inv_tab, w_lat, w_kr, g_qa[None, :], g_kva[None, :],
      w_qb_t, w_kn, w_v_t, gq_col)


ONES_ROWS = 16


SAFE_EXPONENT = 60.0


def _score_bound_is_safe(gain_product, head_dim, bias=None):
    bound = 1.02 * math.sqrt(head_dim) * LOG2E * jnp.max(jnp.abs(gain_product))
    if bias is not None:
        bound = bound + LOG2E * jnp.max(jnp.abs(bias))
    return (bound <= SAFE_EXPONENT).astype(jnp.int32).reshape(1)


def _mla_attn_kernel(safe_ref, qT_ref, k_ref, vT_ref, oT_ref, s_ref, *, hb, tn, kc):
    seq = k_ref.shape[1]
    n_tiles, _, tq = qT_ref.shape[1:]
    v_dim = vT_ref.shape[1] // hb
    ones = jnp.ones((ONES_ROWS, seq), BF16)
    n_chunks = seq // kc
    n_cols = tq // tn
    hrows = lambda h: slice(h * HEAD_PAD, (h + 1) * HEAD_PAD)
    vrows = lambda h: slice(h * v_dim, (h + 1) * v_dim)
    v_ext = lambda h: jnp.concatenate([vT_ref[0, vrows(h), :], ones], axis=0)

    @pl.when(safe_ref[0] == 1)
    def _():
        tiles = [(h, i) for h in range(hb) for i in range(n_tiles)]

        def scores(t, c):
            h, i = tiles[t]
            s_ref[t % 2, c * kc:(c + 1) * kc, :] = jnp.dot(
                k_ref[0, c * kc:(c + 1) * kc, hrows(h)], qT_ref[0, i, hrows(h), :], preferred_element_type=F32)

        for c in range(n_chunks):
            scores(0, c)
        for t, (h, i) in enumerate(tiles):
            ve = v_ext(h)
            for j in range(n_cols):
                acc = jnp.zeros((v_dim + ONES_ROWS, tn), F32)
                for c in range(n_chunks):
                    step = j * n_chunks + c
                    if t + 1 < len(tiles) and step % n_cols == 0:
                        scores(t + 1, step // n_cols)
                    p = jnp.exp2(s_ref[t % 2, c * kc:(c + 1) * kc, j * tn:(j + 1) * tn]).astype(BF16)
                    acc = acc + jnp.dot(ve[:, c * kc:(c + 1) * kc], p, preferred_element_type=F32)
                o = acc[:v_dim] * (1.0 / acc[v_dim:v_dim + 1])
                oT_ref[0, i, vrows(h), j * tn:(j + 1) * tn] = o.astype(BF16)

    @pl.when(safe_ref[0] != 1)
    def _():
        for h in range(hb):
            ve = v_ext(h)
            k_h = k_ref[0, :, hrows(h)]

            def tile(i, carry):
                s_t = jnp.dot(k_h, qT_ref[0, i, hrows(h), :], preferred_element_type=F32)
                p = jnp.exp2(s_t - jnp.max(s_t, axis=0, keepdims=True)).astype(BF16)
                r = jnp.dot(ve, p, preferred_element_type=F32)
                oT_ref[0, i, vrows(h), :] = (r[:v_dim] * (1.0 / r[v_dim:v_dim + 1])).astype(BF16)
                return carry

            lax.fori_loop(0, n_tiles, tile, 0)


def _mla_attn(q_t, k_pad, v_t, safe, n_heads):
    batch, n_tiles, _, tq = q_t.shape
    seq = n_tiles * tq
    v_dim = v_t.shape[1] // n_heads
    k3 = k_pad.reshape(batch, seq, n_heads * HEAD_PAD)
    hb = 2
    return pl.pallas_call(
        functools.partial(_mla_attn_kernel, hb=hb, tn=256, kc=256),
        grid=(batch, n_heads // hb),
        in_specs=[
            pl.BlockSpec(memory_space=pltpu.SMEM),
            pl.BlockSpec((1, n_tiles, hb * HEAD_PAD, tq), lambda b, h: (b, 0, h, 0)),
            pl.BlockSpec((1, seq, hb * HEAD_PAD), lambda b, h: (b, 0, h)),
            pl.BlockSpec((1, hb * v_dim, seq), lambda b, h: (b, h, 0)),
        ],
        out_specs=pl.BlockSpec((1, n_tiles, hb * v_dim, tq), lambda b, h: (b, 0, h, 0)),
        out_shape=jax.ShapeDtypeStruct((batch, n_tiles, n_heads * v_dim, tq), BF16),
        scratch_shapes=[pltpu.VMEM((2, seq, tq), F32)],
        compiler_params=_params(("arbitrary", "arbitrary")),
        name="mla_attn",
    )(safe, q_t, k3, v_t)


MIX_ROWS = 512
MIX_SUB = 256
FF_CHUNK = 2048
LOAD_SLOTS = 4


def _load_bf16(src, dst, stage, sem):
    slots, chunk = stage.shape[:2]
    n = dst.shape[0] // chunk

    def copy(i):
        return pltpu.make_async_copy(src.at[pl.ds(i * chunk, chunk), :], stage.at[i % slots], sem.at[i % slots])

    for i in range(min(slots - 1, n)):
        copy(i).start()
    for i in range(n):
        if i + slots - 1 < n:
            copy(i + slots - 1).start()
        copy(i).wait()
        dst[i * chunk:(i + 1) * chunk, :] = stage[i % slots].astype(BF16)


def _mlp_rows(h2, w1_ref, w2_ref):
    z = None
    for f in range(w1_ref.shape[1] // FF_CHUNK):
        cols = slice(f * FF_CHUNK, (f + 1) * FF_CHUNK)
        u = jnp.dot(h2, w1_ref[:, cols], preferred_element_type=F32)
        u = jnp.square(jnp.maximum(u, 0.0)).astype(BF16)
        part = jnp.dot(u, w2_ref[cols, :], preferred_element_type=F32)
        z = part if z is None else z + part
    return z


def _mix_mlp_body(mixer_out, x_ref, mod_ref, nmlp_ref, next_refs, x2_ref, hn_ref, wo_ref, w1_ref, w2_ref):
    m = mod_ref[0]
    tm = x_ref.shape[0]
    blocks = [slice(i * MIX_SUB, (i + 1) * MIX_SUB) for i in range(tm // MIX_SUB)]

    def mix(rows):
        x1 = x_ref[rows, :] + m[2:3] * mixer_out(rows, wo_ref)
        return x1, _ada_norm(x1, nmlp_ref[...], m[4:5], m[3:4]).astype(BF16)

    cur = mix(blocks[0])
    for i, rows in enumerate(blocks):
        nxt = mix(blocks[i + 1]) if i + 1 < len(blocks) else None
        x1, h2 = cur
        x2 = x1 + m[5:6] * _mlp_rows(h2, w1_ref, w2_ref)
        x2_ref[rows, :] = x2
        if next_refs is not None:
            nnext_ref, modn_ref = next_refs
            mn = modn_ref[0]
            hn_ref[rows, :] = _ada_norm(x2, nnext_ref[...], mn[1:2], mn[0:1]).astype(BF16)
        cur = nxt


def _load_mix_weights(wo_hbm, w1_hbm, w2_hbm, layer, wo_ref, w1_ref, w2_ref, stage_w, stage_n, sem):
    _load_bf16(wo_hbm, wo_ref, stage_n, sem)
    _load_bf16(w1_hbm.at[layer], w1_ref, stage_w, sem)
    _load_bf16(w2_hbm.at[layer], w2_ref, stage_n, sem)


def _mla_mlp_kernel(oT_ref, x_ref, mod_ref, nmlp_ref, nnext_ref, modn_ref, wo_hbm, w1_hbm, w2_hbm, x2_ref, hn_ref,
                    wo_ref, w1_ref, w2_ref, stage_w, stage_n, sem, *, layer):
    @pl.when(pl.program_id(0) == 0)
    def _():
        _load_mix_weights(wo_hbm, w1_hbm, w2_hbm, layer, wo_ref, w1_ref, w2_ref, stage_w, stage_n, sem)

    def mixer_out(rows, wo):
        return lax.dot_general(oT_ref[0, 0, :, rows], wo[...], TN_DIMS, preferred_element_type=F32)

    _mix_mlp_body(mixer_out, x_ref, mod_ref, nmlp_ref, (nnext_ref, modn_ref), x2_ref, hn_ref, wo_ref, w1_ref, w2_ref)


def _dil_mlp_kernel(o0_ref, o1_ref, o2_ref, s0_ref, s1_ref, s2_ref, e_ref, x_ref, mod_ref, nmlp_ref,
                    wo_hbm, w1_hbm, w2_hbm, x2_ref, wo_ref, w1_ref, w2_ref, stage_w, stage_n, sem,
                    oscr1, sscr1, oscr2, sscr2, *, layer):
    @pl.when(pl.program_id(0) == 0)
    def _():
        _load_mix_weights(wo_hbm, w1_hbm, w2_hbm, layer, wo_ref, w1_ref, w2_ref, stage_w, stage_n, sem)

    tm = x_ref.shape[0]
    c = wo_ref.shape[0]

    def token_major(o_ref, s_ref, oscr, sscr, dilation):
        ra = tm // dilation
        for r in range(dilation):
            for s in range(c // LANES):
                lo = r * c + s * LANES
                oscr[s, pl.ds(r, ra, stride=dilation), :] = o_ref[0, :, lo:lo + LANES].astype(F32)
            sscr[pl.ds(r, ra, stride=dilation), :] = s_ref[0, :, r * LANES:(r + 1) * LANES]

    token_major(o1_ref, s1_ref, oscr1, sscr1, DIL_GROUPS[1][1])
    token_major(o2_ref, s2_ref, oscr2, sscr2, DIL_GROUPS[2][1])
    e = e_ref[...]

    def expand(w):
        return jnp.dot(w.astype(BF16), e, preferred_element_type=F32)

    def mixer_out(rows, wo):
        gather = lambda oscr: jnp.concatenate([oscr[s, rows, :] for s in range(c // LANES)], axis=1)
        o0, l0 = o0_ref[0, rows, :].astype(F32), s0_ref[0, rows, :]
        o1, l1 = gather(oscr1), sscr1[rows, :]
        o2, l2 = gather(oscr2), sscr2[rows, :]
        mx = jnp.maximum(jnp.maximum(l0, l1), l2)
        e0, e1, e2 = jnp.exp(l0 - mx), jnp.exp(l1 - mx), jnp.exp(l2 - mx)
        rden = 1.0 / (e0 + e1 + e2)
        o = expand(e0 * rden) * o0 + expand(e1 * rden) * o1 + expand(e2 * rden) * o2
        return jnp.dot(o.astype(BF16), wo[...], preferred_element_type=F32)

    _mix_mlp_body(mixer_out, x_ref, mod_ref, nmlp_ref, None, x2_ref, None, wo_ref, w1_ref, w2_ref)


def _mix_mlp_call(kern, name, lead_specs, lead_args, w_o, w1, w2, x2d, mod_l, nmlp, seq, next_args, extra_scratch):
    t, d = x2d.shape
    ff = w1.shape[2]
    c = w_o.shape[0]
    tm = MIX_ROWS
    tpb = seq // tm
    row = pl.BlockSpec((tm, d), lambda i: (i, 0))
    mod_spec = pl.BlockSpec((1, N_MOD, d), lambda i: (i // tpb, 0, 0))
    vec = pl.BlockSpec((1, d), lambda i: (0, 0))
    hbm = pl.BlockSpec(memory_space=pl.ANY)
    in_specs = lead_specs + [row, mod_spec, vec]
    args = lead_args + [x2d, mod_l, nmlp[None, :]]
    out_specs, out_shape = [row], [jax.ShapeDtypeStruct((t, d), F32)]
    if next_args is not None:
        nnext, mod_next = next_args
        in_specs += [vec, mod_spec]
        args += [nnext[None, :], mod_next]
        out_specs.append(row)
        out_shape.append(jax.ShapeDtypeStruct((t, d), BF16))
    in_specs += [hbm, hbm, hbm]
    args += [w_o, w1, w2]
    scratch = [pltpu.VMEM((c, d), BF16), pltpu.VMEM((d, ff), BF16), pltpu.VMEM((ff, d), BF16),
               pltpu.VMEM((LOAD_SLOTS, 64, ff), F32), pltpu.VMEM((LOAD_SLOTS, 256, d), F32),
               pltpu.SemaphoreType.DMA((LOAD_SLOTS,))]
    return pl.pallas_call(
        kern,
        grid=(t // tm,),
        in_specs=in_specs,
        out_specs=out_specs,
        out_shape=out_shape,
        scratch_shapes=scratch + extra_scratch,
        compiler_params=_params(("arbitrary",)),
        name=name,
    )(*args)


def _mla_mlp(o_t, w_o, w1, w2, layer, x2d, mod_l, nmlp, nnext, mod_next, seq):
    tm = MIX_ROWS
    tpb = seq // tm
    c = o_t.shape[2]
    assert o_t.shape[3] == tm
    lead = [pl.BlockSpec((1, 1, c, tm), lambda i: (i // tpb, i % tpb, 0, 0))]
    return _mix_mlp_call(functools.partial(_mla_mlp_kernel, layer=layer), "mla_mlp", lead, [o_t], w_o, w1, w2, x2d,
                         mod_l, nmlp, seq, (nnext, mod_next), [])


def _dil_mlp(outs, stats, w_o, w1, w2, layer, x2d, mod_l, nmlp, seq):
    tm = MIX_ROWS
    tpb = seq // tm
    c = w_o.shape[0]
    expand = jnp.repeat(jnp.eye(DIL_HEADS, dtype=F32), DIL_HEAD_DIM, axis=1)
    expand = jnp.pad(expand, ((0, LANES - DIL_HEADS), (0, 0))).astype(BF16)
    grp = lambda g, w: pl.BlockSpec((1, tm // DIL_GROUPS[g][1], DIL_GROUPS[g][1] * w),
                                    lambda i: (i // tpb, i % tpb, 0))
    lead = [grp(0, c), grp(1, c), grp(2, c), grp(0, LANES), grp(1, LANES), grp(2, LANES),
            pl.BlockSpec((LANES, c), lambda i: (0, 0))]
    extra = []
    for g in (1, 2):
        extra += [pltpu.VMEM((c // LANES, tm, LANES), F32), pltpu.VMEM((tm, LANES), F32)]
    (x2,) = _mix_mlp_call(functools.partial(_dil_mlp_kernel, layer=layer), "dil_mlp", lead, [*outs, *stats, expand],
                          w_o, w1, w2, x2d, mod_l, nmlp, seq, None, extra)
    return x2


def _attn_out_kernel(oT_ref, wo_ref, x_ref, mod_ref, nmlp_ref, x1_ref, h2_ref):
    m = mod_ref[0]
    tm = x_ref.shape[0]
    n_sub = 2
    blocks = [slice(i * (tm // n_sub), (i + 1) * (tm // n_sub)) for i in range(n_sub)]
    project = lambda rows: lax.dot_general(oT_ref[0, 0, :, rows], wo_ref[...], TN_DIMS, preferred_element_type=F32)
    y = project(blocks[0])
    for i, rows in enumerate(blocks):
        y_next = project(blocks[i + 1]) if i + 1 < n_sub else None
        x1 = x_ref[rows, :] + m[2:3] * y
        x1_ref[rows, :] = x1
        h2_ref[rows, :] = _ada_norm(x1, nmlp_ref[...], m[4:5], m[3:4]).astype(BF16)
        y = y_next


def _attn_out(o_t, w_o, x2d, mod_l, nmlp, seq):
    t, d = x2d.shape
    tm = 512
    tpb = seq // tm
    c = o_t.shape[2]
    return pl.pallas_call(
        _attn_out_kernel,
        grid=(t // tm,),
        in_specs=[
            pl.BlockSpec((1, 1, c, tm), lambda i: (i // tpb, i % tpb, 0, 0)),
            pl.BlockSpec((c, d), lambda i: (0, 0)),
            pl.BlockSpec((tm, d), lambda i: (i, 0)),
            pl.BlockSpec((1, N_MOD, d), lambda i: (i // tpb, 0, 0)),
            pl.BlockSpec((1, d), lambda i: (0, 0)),
        ],
        out_specs=[pl.BlockSpec((tm, d), lambda i: (i, 0)), pl.BlockSpec((tm, d), lambda i: (i, 0))],
        out_shape=[jax.ShapeDtypeStruct((t, d), F32), jax.ShapeDtypeStruct((t, d), BF16)],
        compiler_params=_params(("arbitrary",)),
        name="attn_out",
    )(o_t, w_o.astype(BF16), x2d, mod_l, nmlp[None, :])


def _mlp_kernel(x_ref, h_ref, w1_ref, w2_ref, mod_ref, *rest, with_next):
    if with_next:
        nnext_ref, modn_ref, x2_ref, hn_ref, acc_ref = rest
    else:
        x2_ref, acc_ref = rest
    j = pl.program_id(1)

    @pl.when(j == 0)
    def _():
        acc_ref[...] = jnp.zeros_like(acc_ref)

    u = jnp.dot(h_ref[...], w1_ref[0].astype(BF16), preferred_element_type=F32)
    u = jnp.square(jnp.maximum(u, 0.0)).astype(BF16)
    acc_ref[...] += jnp.dot(u, w2_ref[0].astype(BF16), preferred_element_type=F32)

    @pl.when(j == pl.num_programs(1) - 1)
    def _():
        x2 = x_ref[...] + mod_ref[0][5:6] * acc_ref[...]
        x2_ref[...] = x2
        if with_next:
            mn = modn_ref[0]
            hn_ref[...] = _ada_norm(x2, nnext_ref[...], mn[1:2], mn[0:1]).astype(BF16)


def _mlp(x1, h2, w1, w2, layer, mod_l, seq, nnext=None, mod_next=None):
    t, d = x1.shape
    ff = w1.shape[2]
    tm, tf = 1024, 1024
    tpb = seq // tm
    with_next = nnext is not None
    row = pl.BlockSpec((tm, d), lambda i, j: (i, 0))
    mod_spec = pl.BlockSpec((1, N_MOD, d), lambda i, j: (i // tpb, 0, 0))
    in_specs = [row, row, pl.BlockSpec((1, d, tf), lambda i, j: (layer, 0, j)),
                pl.BlockSpec((1, tf, d), lambda i, j: (layer, j, 0)), mod_spec]
    args = [x1, h2, w1, w2, mod_l]
    out_specs = [row]
    out_shape = [jax.ShapeDtypeStruct((t, d), F32)]
    if with_next:
        in_specs += [pl.BlockSpec((1, d), lambda i, j: (0, 0)), mod_spec]
        args += [nnext[None, :], mod_next]
        out_specs.append(row)
        out_shape.append(jax.ShapeDtypeStruct((t, d), BF16))
    return pl.pallas_call(
        functools.partial(_mlp_kernel, with_next=with_next),
        grid=(t // tm, ff // tf),
        in_specs=in_specs,
        out_specs=out_specs,
        out_shape=out_shape,
        scratch_shapes=[pltpu.VMEM((tm, d), F32)],
        compiler_params=_params(("arbitrary", "arbitrary")),
        name="mlp",
    )(*args)


DIL_TQ = 128
DIL_HALF = 64
assert all(w // (2 * d) == DIL_HALF for w, d in DIL_GROUPS)


def _dil_geometry(seq, dilation):
    length = seq // dilation
    kwin = min(2 * DIL_TQ, length)
    return length, kwin, length // DIL_TQ


def _t5_log_thresholds():
    nb = N_BUCKETS // 2
    max_exact = nb // 2
    steps = nb - max_exact
    thresholds = []
    for k in range(1, steps):
        n = max_exact
        while math.floor(math.log(n / max_exact) / math.log(MAX_DISTANCE / max_exact) * steps) < k:
            n += 1
        thresholds.append(n)
    return thresholds


def _t5_bucket(rel):
    nb = N_BUCKETS // 2
    max_exact = nb // 2
    n = jnp.abs(rel)
    large = max_exact
    for thr in _t5_log_thresholds():
        large = large + jnp.where(n >= thr, 1, 0)
    return jnp.where(rel > 0, nb, 0) + jnp.where(n < max_exact, n, large)


BIAS_ROWS = 16


def _dil_bias_kernel(tab_ref, o_ref, *, group, dilation, kwin, n_var):
    var = pl.program_id(0)

    def rows(step, carry):
        r0 = pl.multiple_of(step * BIAS_ROWS, BIAS_ROWS)
        kk = lax.broadcasted_iota(jnp.int32, (BIAS_ROWS, DIL_TQ), 0) + r0
        qi = lax.broadcasted_iota(jnp.int32, (BIAS_ROWS, DIL_TQ), 1)
        rel_a = kk - qi - (DIL_HALF if n_var > 1 else 0)
        valid = jnp.abs(rel_a) <= DIL_HALF
        if n_var > 1:
            crossing = jnp.where((qi < DIL_HALF) == (kk < DIL_TQ), 0, var)
            valid = jnp.where(valid, 1, 0) - crossing == 1
        bucket = _t5_bucket(rel_a * dilation)
        accs = [jnp.zeros((BIAS_ROWS, DIL_TQ), F32)] * DIL_HEADS
        for b in range(N_BUCKETS):
            hit = bucket == b
            accs = [jnp.where(hit, tab_ref[b, group * DIL_HEADS + h], a) for h, a in enumerate(accs)]
        for h, a in enumerate(accs):
            o_ref[0, h // 2, pl.ds(r0, BIAS_ROWS), (h % 2) * DIL_TQ:(h % 2 + 1) * DIL_TQ] = jnp.where(
                valid, a * LOG2E, MASKED)
        return carry

    lax.fori_loop(0, kwin // BIAS_ROWS, rows, 0)


def _dil_bias(rel_bias, group, seq):
    _, dilation = DIL_GROUPS[group]
    _, kwin, n_tiles = _dil_geometry(seq, dilation)
    n_var = 1 if n_tiles == 1 else 2
    n_pairs = DIL_HEADS // 2
    kern = functools.partial(_dil_bias_kernel, group=group, dilation=dilation, kwin=kwin, n_var=n_var)
    return pl.pallas_call(
        kern,
        grid=(n_var,),
        in_specs=[pl.BlockSpec(memory_space=pltpu.SMEM)],
        out_specs=pl.BlockSpec((1, n_pairs, kwin, 2 * DIL_TQ), lambda v: (v, 0, 0, 0)),
        out_shape=jax.ShapeDtypeStruct((n_var, n_pairs, kwin, 2 * DIL_TQ), F32),
        compiler_params=_params(("arbitrary",)),
        name=f"dil_bias{group}",
    )(rel_bias)


DIL_PROJ_ROWS = 512


def _dil_proj_kernel(h_ref, wqf_ref, wkf_ref, wvf_ref, gq_ref, qT_ref, k_ref, vT_ref, wq_ref, wk_ref, wv_ref,
                     *scratch, dilation, rb, ra):
    hd = DIL_HEAD_DIM
    d_model = h_ref.shape[2]
    n_slabs = d_model // LANES
    mt = rb * ra

    @pl.when((pl.program_id(0) == 0) & (pl.program_id(1) == 0) & (pl.program_id(2) == 0))
    def _():
        wq_ref[...] = wqf_ref[0].T.astype(BF16)
        wk_ref[...] = wkf_ref[0].astype(BF16)
        wv_ref[...] = wvf_ref[0].T.astype(BF16)

    if dilation == 1:
        hcat = h_ref[0]
    else:
        (scr,) = scratch
        sub = pl.program_id(2)

        @pl.when(sub == 0)
        def _():
            for s in range(n_slabs):
                scr[s] = h_ref[0, :, s * LANES:(s + 1) * LANES].astype(F32)

        rows = []
        for j in range(rb):
            res = sub * rb + j
            rows.append(jnp.concatenate(
                [scr[s, pl.ds(res, ra, stride=dilation), :] for s in range(n_slabs)], axis=1))
        hcat = jnp.concatenate(rows, axis=0).astype(BF16)

    q_t = lax.dot_general(wq_ref[...], hcat, NT_DIMS, preferred_element_type=F32)
    gq = jnp.concatenate([gq_ref[...]] * (mt // LANES), axis=1)
    for h in range(DIL_HEADS):
        blk = q_t[h * hd:(h + 1) * hd]
        r = lax.rsqrt(jnp.sum(blk * blk, axis=0, keepdims=True) * (1.0 / hd) + EPS)
        qb = (blk * r * gq).astype(BF16)
        for j in range(rb):
            for s in range(ra // LANES):
                lo = j * ra + s * LANES
                qT_ref[0, j, s, h * hd:(h + 1) * hd, :] = qb[:, lo:lo + LANES]

    kf = jnp.dot(hcat, wk_ref[...], preferred_element_type=F32)
    lane = lax.broadcasted_iota(jnp.int32, (mt, LANES), 1)
    low = lane < hd
    for c in range(DIL_HEADS * hd // LANES):
        y = kf[:, c * LANES:(c + 1) * LANES]
        y2 = y * y
        s_lo = jnp.sum(jnp.where(low, y2, 0.0), axis=1, keepdims=True)
        s_hi = jnp.sum(jnp.where(low, 0.0, y2), axis=1, keepdims=True)
        r = lax.rsqrt(jnp.where(low, s_lo, s_hi) * (1.0 / hd) + EPS)
        kb = (y * r).astype(BF16)
        for j in range(rb):
            k_ref[0, j, :, c * LANES:(c + 1) * LANES] = kb[j * ra:(j + 1) * ra]

    v_t = lax.dot_general(wv_ref[...], hcat, NT_DIMS, preferred_element_type=F32).astype(BF16)
    for j in range(rb):
        for s in range(ra // LANES):
            lo = j * ra + s * LANES
            vT_ref[0, j, s] = v_t[:, lo:lo + LANES]


def _dil_proj(hn3, w_in, g_q, g_k, group):
    batch, seq, d_model = hn3.shape
    _, dilation = DIL_GROUPS[group]
    length = seq // dilation
    c = DIL_HEADS * DIL_HEAD_DIM
    rb = min(dilation, 4)
    rows = 1024 if dilation <= rb else DIL_PROJ_ROWS
    ra = rows // rb
    tok = ra * dilation
    n_sub = dilation // rb
    gq_col = jnp.broadcast_to((g_q * g_k * (DIL_HEAD_DIM ** -0.5 * LOG2E))[:, None], (DIL_HEAD_DIM, LANES))
    kern = functools.partial(_dil_proj_kernel, dilation=dilation, rb=rb, ra=ra)
    const = lambda shape: pl.BlockSpec(shape, lambda b, t, s: (0,) * len(shape))
    w_blk = lambda j: pl.BlockSpec((1, d_model, c), lambda b, t, s: (0, 0, 3 * group + j))
    scratch = [pltpu.VMEM((c, d_model), BF16), pltpu.VMEM((d_model, c), BF16), pltpu.VMEM((c, d_model), BF16)]
    if dilation > 1:
        scratch.append(pltpu.VMEM((d_model // LANES, tok, LANES), F32))
    return pl.pallas_call(
        kern,
        grid=(batch, seq // tok, n_sub),
        in_specs=[
            pl.BlockSpec((1, tok, d_model), lambda b, t, s: (b, t, 0)),
            w_blk(0), w_blk(1), w_blk(2),
            const((DIL_HEAD_DIM, LANES)),
        ],
        out_specs=[
            pl.BlockSpec((1, rb, ra // LANES, c, LANES), lambda b, t, s: (b, s, t, 0, 0)),
            pl.BlockSpec((1, rb, ra, c), lambda b, t, s: (b, s, t, 0)),
            pl.BlockSpec((1, rb, ra // LANES, c, LANES), lambda b, t, s: (b, s, t, 0, 0)),
        ],
        out_shape=[
            jax.ShapeDtypeStruct((batch, dilation, length // LANES, c, LANES), BF16),
            jax.ShapeDtypeStruct((batch, dilation, length, c), BF16),
            jax.ShapeDtypeStruct((batch, dilation, length // LANES, c, LANES), BF16),
        ],
        scratch_shapes=scratch,
        compiler_params=_params(("arbitrary", "arbitrary", "arbitrary")),
        name=f"dil_proj{group}",
    )(hn3, w_in, w_in, w_in, gq_col)


def _dil_attn_kernel(safe_ref, qT_ref, k_ref, vT_ref, bm_ref, o_ref, st_ref, s_ref, *, n_tiles, rb):
    tq = DIL_TQ
    hd = DIL_HEAD_DIM
    n_pairs = DIL_HEADS // 2
    c = DIL_HEADS * hd
    shifted = n_tiles > 1
    low = lax.broadcasted_iota(jnp.int32, (LANES, tq), 0) < hd
    ones = jnp.ones((ONES_ROWS, 2 * tq if shifted else tq), BF16)
    prow = lambda p: slice(p * LANES, (p + 1) * LANES)

    def block_diag(q2):
        zero = jnp.zeros_like(q2)
        return jnp.concatenate([jnp.where(low, q2, zero), jnp.where(low, zero, q2)], axis=1)

    def attend(s_t, v2, use_max):
        if use_max:
            m = jnp.max(s_t, axis=0, keepdims=True)
            pb = jnp.exp2(s_t - m).astype(BF16)
        else:
            pb = jnp.exp2(s_t).astype(BF16)
        res = jnp.dot(jnp.concatenate([v2, ones], axis=0), pb, preferred_element_type=F32)
        den = res[LANES:LANES + 1]
        rinv = 1.0 / den
        o_pair = jnp.concatenate([res[:hd, :tq] * rinv[:, :tq], res[hd:LANES, tq:] * rinv[:, tq:]], axis=0)
        lse = jnp.log2(den) + m if use_max else jnp.log2(den)
        lse = lse * LN2
        return o_pair.T.astype(BF16), [lse[:, :tq], lse[:, tq:]]

    def stats_rows(stats):
        return jnp.concatenate(stats + [jnp.zeros((LANES - DIL_HEADS, tq), F32)], axis=0).T

    tiles = [(j, t) for j in range(rb) for t in range(n_tiles)]

    @pl.when(safe_ref[0] == 1)
    def _():
        def score_tile(i):
            j, t = tiles[i]
            t1 = (t + 1) % n_tiles
            for p in range(n_pairs):
                if shifted:
                    q2 = jnp.concatenate(
                        [qT_ref[0, j, t, prow(p), DIL_HALF:], qT_ref[0, j, t1, prow(p), :DIL_HALF]], axis=1)
                    k2 = jnp.concatenate([k_ref[0, j, t * tq:(t + 1) * tq, prow(p)],
                                          k_ref[0, j, t1 * tq:(t1 + 1) * tq, prow(p)]], axis=0)
                else:
                    q2 = qT_ref[0, j, 0, prow(p), :]
                    k2 = k_ref[0, j, :, prow(p)]
                s_ref[i % 2, p] = jnp.dot(k2, block_diag(q2), preferred_element_type=F32)

        def attend_tile(i):
            j, t = tiles[i]
            t1 = (t + 1) % n_tiles
            var = 1 if shifted and t + 1 == n_tiles else 0

            def store(ref, lanes, val):
                if shifted:
                    ref[0, t * tq + DIL_HALF:(t + 1) * tq, lanes] = val[:DIL_HALF]
                    ref[0, t1 * tq:t1 * tq + DIL_HALF, lanes] = val[DIL_HALF:]
                else:
                    ref[0, :, lanes] = val

            stats = []
            for p in range(n_pairs):
                if shifted:
                    v2 = jnp.concatenate([vT_ref[0, j, t, prow(p), :], vT_ref[0, j, t1, prow(p), :]], axis=1)
                else:
                    v2 = vT_ref[0, j, 0, prow(p), :]
                o_rows, lses = attend(s_ref[i % 2, p] + bm_ref[var, p], v2, use_max=False)
                store(o_ref, slice(j * c + p * LANES, j * c + (p + 1) * LANES), o_rows)
                stats += lses
            store(st_ref, slice(j * LANES, (j + 1) * LANES), stats_rows(stats))

        score_tile(0)
        for i in range(len(tiles)):
            if i + 1 < len(tiles):
                score_tile(i + 1)
            attend_tile(i)

    @pl.when(safe_ref[0] != 1)
    def _():
        def tile(j, t):
            if shifted:
                last = t + 1 == n_tiles
                t1 = jnp.where(last, 0, t + 1)
                var = jnp.where(last, 1, 0)
                r0 = pl.multiple_of(t * tq, tq)
                r1 = pl.multiple_of(t1 * tq, tq)

            def store(ref, lanes, val):
                if shifted:
                    ref[0, pl.ds(r0 + DIL_HALF, DIL_HALF), lanes] = val[:DIL_HALF]
                    ref[0, pl.ds(r1, DIL_HALF), lanes] = val[DIL_HALF:]
                else:
                    ref[0, :, lanes] = val

            stats = []
            for p in range(n_pairs):
                if shifted:
                    q2 = jnp.concatenate(
                        [qT_ref[0, j, t, prow(p), DIL_HALF:], qT_ref[0, j, t1, prow(p), :DIL_HALF]], axis=1)
                    k2 = jnp.concatenate(
                        [k_ref[0, j, pl.ds(r0, tq), prow(p)], k_ref[0, j, pl.ds(r1, tq), prow(p)]], axis=0)
                    v2 = jnp.concatenate([vT_ref[0, j, t, prow(p), :], vT_ref[0, j, t1, prow(p), :]], axis=1)
                    bias = bm_ref[var, p]
                else:
                    q2 = qT_ref[0, j, 0, prow(p), :]
                    k2 = k_ref[0, j, :, prow(p)]
                    v2 = vT_ref[0, j, 0, prow(p), :]
                    bias = bm_ref[0, p]
                s_t = jnp.dot(k2, block_diag(q2), preferred_element_type=F32) + bias
                o_rows, lses = attend(s_t, v2, use_max=True)
                store(o_ref, slice(j * c + p * LANES, j * c + (p + 1) * LANES), o_rows)
                stats += lses
            store(st_ref, slice(j * LANES, (j + 1) * LANES), stats_rows(stats))

        for j in range(rb):
            if shifted:
                lax.fori_loop(0, n_tiles, lambda t, carry, j=j: (tile(j, t), carry)[1], 0)
            else:
                tile(j, 0)


def _dil_attn(q_t, k, v_t, bias, safe, group, seq):
    batch = q_t.shape[0]
    _, dilation = DIL_GROUPS[group]
    length, kwin, n_tiles = _dil_geometry(seq, dilation)
    c = DIL_HEADS * DIL_HEAD_DIM
    n_slabs = length // LANES
    rb = 1 if n_tiles > 1 else min(dilation, 4)
    kern = functools.partial(_dil_attn_kernel, n_tiles=n_tiles, rb=rb)
    return pl.pallas_call(
        kern,
        grid=(batch, dilation // rb),
        in_specs=[
            pl.BlockSpec(memory_space=pltpu.SMEM),
            pl.BlockSpec((1, rb, n_slabs, c, LANES), lambda b, r: (b, r, 0, 0, 0)),
            pl.BlockSpec((1, rb, length, c), lambda b, r: (b, r, 0, 0)),
            pl.BlockSpec((1, rb, n_slabs, c, LANES), lambda b, r: (b, r, 0, 0, 0)),
            pl.BlockSpec(bias.shape, lambda b, r: (0, 0, 0, 0)),
        ],
        out_specs=[
            pl.BlockSpec((1, length, rb * c), lambda b, r: (b, 0, r)),
            pl.BlockSpec((1, length, rb * LANES), lambda b, r: (b, 0, r)),
        ],
        out_shape=[
            jax.ShapeDtypeStruct((batch, length, dilation * c), BF16),
            jax.ShapeDtypeStruct((batch, length, dilation * LANES), F32),
        ],
        scratch_shapes=[pltpu.VMEM((2, DIL_HEADS // 2, kwin, 2 * DIL_TQ), F32)],
        compiler_params=_params(("arbitrary", "arbitrary")),
        name=f"dil_attn{group}",
    )(safe, q_t, k, v_t, bias)


def _dil_out_kernel(o0_ref, o1_ref, o2_ref, s0_ref, s1_ref, s2_ref, e_ref, wo_ref, x_ref, mod_ref, nmlp_ref,
                    x1_ref, h2_ref, *scratch):
    tm = x_ref.shape[0]
    c = wo_ref.shape[0]

    def token_major(o_ref, s_ref, oscr, sscr, dilation):
        ra = tm // dilation
        for r in range(dilation):
            for s in range(c // LANES):
                lo = r * c + s * LANES
                oscr[s, pl.ds(r, ra, stride=dilation), :] = o_ref[0, :, lo:lo + LANES].astype(F32)
            sscr[pl.ds(r, ra, stride=dilation), :] = s_ref[0, :, r * LANES:(r + 1) * LANES]

    token_major(o1_ref, s1_ref, scratch[0], scratch[1], DIL_GROUPS[1][1])
    token_major(o2_ref, s2_ref, scratch[2], scratch[3], DIL_GROUPS[2][1])
    e = e_ref[...]
    m = mod_ref[0]

    def expand(w):
        return jnp.dot(w.astype(BF16), e, preferred_element_type=F32)

    def merged(rows):
        gather = lambda oscr: jnp.concatenate([oscr[s, rows, :] for s in range(c // LANES)], axis=1)
        o0, l0 = o0_ref[0, rows, :].astype(F32), s0_ref[0, rows, :]
        o1, l1 = gather(scratch[0]), scratch[1][rows, :]
        o2, l2 = gather(scratch[2]), scratch[3][rows, :]
        mx = jnp.maximum(jnp.maximum(l0, l1), l2)
        e0, e1, e2 = jnp.exp(l0 - mx), jnp.exp(l1 - mx), jnp.exp(l2 - mx)
        rden = 1.0 / (e0 + e1 + e2)
        return (expand(e0 * rden) * o0 + expand(e1 * rden) * o1 + expand(e2 * rden) * o2).astype(BF16)

    def project(rows, o):
        y = jnp.dot(o, wo_ref[...], preferred_element_type=F32)
        x1 = x_ref[rows, :] + m[2:3] * y
        x1_ref[rows, :] = x1
        h2_ref[rows, :] = _ada_norm(x1, nmlp_ref[...], m[4:5], m[3:4]).astype(BF16)

    n_sub = 2
    blocks = [slice(i * (tm // n_sub), (i + 1) * (tm // n_sub)) for i in range(n_sub)]
    o = merged(blocks[0])
    for i, rows in enumerate(blocks):
        o_next = merged(blocks[i + 1]) if i + 1 < n_sub else None
        project(rows, o)
        o = o_next


def _dil_out(outs, stats, w_o, x2d, mod_l, nmlp, seq):
    t, d = x2d.shape
    c = w_o.shape[0]
    tm = 512
    tpb = seq // tm
    expand = jnp.repeat(jnp.eye(DIL_HEADS, dtype=F32), DIL_HEAD_DIM, axis=1)
    expand = jnp.pad(expand, ((0, LANES - DIL_HEADS), (0, 0))).astype(BF16)
    row = lambda w: pl.BlockSpec((tm, w), lambda i: (i, 0))
    grp = lambda g, w: pl.BlockSpec((1, tm // DIL_GROUPS[g][1], DIL_GROUPS[g][1] * w),
                                    lambda i: (i // tpb, i % tpb, 0))
    scratch = []
    for g in (1, 2):
        scratch += [pltpu.VMEM((c // LANES, tm, LANES), F32), pltpu.VMEM((tm, LANES), F32)]
    return pl.pallas_call(
        _dil_out_kernel,
        grid=(t // tm,),
        in_specs=[grp(0, c), grp(1, c), grp(2, c), grp(0, LANES), grp(1, LANES), grp(2, LANES),
                  pl.BlockSpec((LANES, c), lambda i: (0, 0)),
                  pl.BlockSpec((c, d), lambda i: (0, 0)),
                  row(d),
                  pl.BlockSpec((1, N_MOD, d), lambda i: (i // tpb, 0, 0)),
                  pl.BlockSpec((1, d), lambda i: (0, 0))],
        out_specs=[row(d), row(d)],
        out_shape=[jax.ShapeDtypeStruct((t, d), F32), jax.ShapeDtypeStruct((t, d), BF16)],
        scratch_shapes=scratch,
        compiler_params=_params(("arbitrary",)),
        name="dil_out",
    )(*outs, *stats, expand, w_o.astype(BF16), x2d, mod_l, nmlp[None, :])


def kernel(x, c, positions, ada_w, ada_b, norm_mix, norm_mlp, mlp_w1, mlp_w2, mla_w_in, mla_g_qa, mla_w_qb,
           mla_g_kva, mla_w_kvb, mla_g_q, mla_g_k, mla_w_o, dil_w_in, dil_g_q, dil_g_k, dil_w_o, rel_bias):
    batch, seq, d = x.shape
    depth = ada_w.shape[0]
    assert depth == 2 and len(DIL_GROUPS) == 3
    mod = _mod(c, ada_w, ada_b).reshape(depth, batch, N_MOD, d)
    x2d = x.reshape(batch * seq, d)

    n_heads = mla_w_qb.shape[2] // mla_g_q.shape[1]
    q_t, k_pad, v_t = _mla_proj(x2d, mod[0], norm_mix[0], positions, mla_w_in[0], mla_g_qa[0], mla_w_qb[0],
                                mla_g_kva[0], mla_w_kvb[0], mla_g_q[0], mla_g_k[0], batch, seq)
    o_t = _mla_attn(q_t, k_pad, v_t, _score_bound_is_safe(mla_g_q[0] * mla_g_k[0], mla_g_q.shape[1]), n_heads)
    x2, hn = _mla_mlp(o_t, mla_w_o[0], mlp_w1, mlp_w2, 0, x2d, mod[0], norm_mlp[0], norm_mix[1], mod[1], seq)

    outs, stats = [], []
    hn3 = hn.reshape(batch, seq, d)
    for g in range(len(DIL_GROUPS)):
        q_g, k_g, v_g = _dil_proj(hn3, dil_w_in, dil_g_q[0, g], dil_g_k[0, g], g)
        safe = _score_bound_is_safe(dil_g_q[0, g] * dil_g_k[0, g], DIL_HEAD_DIM,
                                    rel_bias[:, g * DIL_HEADS:(g + 1) * DIL_HEADS])
        o_g, st_g = _dil_attn(q_g, k_g, v_g, _dil_bias(rel_bias, g, seq), safe, g, seq)
        outs.append(o_g)
        stats.append(st_g)
    x4 = _dil_mlp(outs, stats, dil_w_o[0], mlp_w1, mlp_w2, 1, x2, mod[1], norm_mlp[1], seq)
    return x4.reshape(batch, seq, d)
```

```python
import functools
import math

import jax
import jax.numpy as jnp
from jax import lax
from jax.experimental import pallas as pl
from jax.experimental.pallas import tpu as pltpu

F32 = jnp.float32
BF16 = jnp.bfloat16

EPS = 1e-6
LOG2E = 1.4426950408889634
LN2 = 0.6931471805599453
MASKED = -1e30

N_MOD = 6
ROPE_THETA = 10000.0
MLA_ROPE = 32
DIL_GROUPS = ((128, 1), (512, 4), (2048, 16))
DIL_HEADS = 16
DIL_HEAD_DIM = 64
N_BUCKETS = 32
MAX_DISTANCE = 1024

LANES = 128
HEAD_PAD = 128

VMEM_LIMIT = 56 * 1024 * 1024

NT_DIMS = (((1,), (1,)), ((), ()))
TN_DIMS = (((0,), (0,)), ((), ()))


def _params(sem):
    return pltpu.CompilerParams(dimension_semantics=sem, vmem_limit_bytes=VMEM_LIMIT)


def _ada_norm(x, g, scale, shift):
    ms = jnp.mean(x * x, axis=-1, keepdims=True)
    return (x * lax.rsqrt(ms + EPS) * g) * (1.0 + scale) + shift


def _rms(x, g):
    ms = jnp.mean(x * x, axis=-1, keepdims=True)
    return x * lax.rsqrt(ms + EPS) * g


def _mod_kernel(c_ref, w_ref, b_ref, o_ref):
    c = c_ref[...]
    cond = c / (1.0 + jnp.exp(-c))
    o_ref[0] = (
        jnp.dot(cond.astype(BF16), w_ref[0].astype(BF16), preferred_element_type=F32) + b_ref[0]
    )


def _mod(c, ada_w, ada_b):
    depth, d, n = ada_w.shape
    b = c.shape[0]
    tn = 1536
    return pl.pallas_call(
        _mod_kernel,
        grid=(depth, n // tn),
        in_specs=[
            pl.BlockSpec((b, d), lambda l, j: (0, 0)),
            pl.BlockSpec((1, d, tn), lambda l, j: (l, 0, j)),
            pl.BlockSpec((1, 1, tn), lambda l, j: (l, 0, j)),
        ],
        out_specs=pl.BlockSpec((1, b, tn), lambda l, j: (l, 0, j)),
        out_shape=jax.ShapeDtypeStruct((depth, b, n), F32),
        compiler_params=_params(("arbitrary", "arbitrary")),
        name="mod",
    )(c, ada_w, ada_b.reshape(depth, 1, n))


def _mla_proj_kernel(x_ref, mod_ref, nmix_ref, pos_ref, inv_ref, wlat_ref, wkr_ref, gqa_ref, gkva_ref,
                     wqb_ref, wkn_ref, wv_ref, gq_ref, qT_ref, k_ref, vT_ref, *, n_heads, q_lora, qk_dim):
    x = x_ref[...]
    tm = x.shape[0]
    m = mod_ref[0]
    hb = _ada_norm(x, nmix_ref[...], m[1:2], m[0:1]).astype(BF16)
    lat = jnp.dot(hb, wlat_ref[...], preferred_element_type=F32)
    qn = _rms(lat[:, :q_lora], gqa_ref[...]).astype(BF16)
    kvn = _rms(lat[:, q_lora:], gkva_ref[...]).astype(BF16)

    half = MLA_ROPE // 2
    nope = qk_dim - MLA_ROPE
    pos = pos_ref[...].astype(F32)
    ang = jnp.concatenate([inv_ref[...]] * (tm // LANES), axis=1) * pos
    cos_t = jnp.cos(ang)
    sin_t = jnp.sin(ang)

    def rope_rows(blk):
        x1 = blk[nope:nope + half]
        x2 = blk[nope + half:qk_dim]
        return jnp.concatenate(
            [blk[:nope], x1 * cos_t - x2 * sin_t, x2 * cos_t + x1 * sin_t, blk[qk_dim:]], axis=0)

    kr_t = lax.dot_general(wkr_ref[...], hb, NT_DIMS, preferred_element_type=F32)
    kr = rope_rows(kr_t).T

    q_t = lax.dot_general(wqb_ref[...], qn, NT_DIMS, preferred_element_type=F32)
    gq = jnp.concatenate([gq_ref[...]] * (tm // LANES), axis=1)
    for h in range(n_heads):
        blk = rope_rows(q_t[h * HEAD_PAD:(h + 1) * HEAD_PAD])
        ssq = jnp.sum(blk * blk, axis=0, keepdims=True)
        r = lax.rsqrt(ssq * (1.0 / qk_dim) + EPS)
        qT_ref[0, 0, h * HEAD_PAD:(h + 1) * HEAD_PAD, :] = (blk * r * gq).astype(BF16)

    kn = jnp.dot(kvn, wkn_ref[...], preferred_element_type=F32)
    for h in range(n_heads):
        kb = kn[:, h * HEAD_PAD:(h + 1) * HEAD_PAD] + kr
        ssq = jnp.sum(kb * kb, axis=1, keepdims=True)
        r = lax.rsqrt(ssq * (1.0 / qk_dim) + EPS)
        k_ref[:, h * HEAD_PAD:(h + 1) * HEAD_PAD] = (kb * r).astype(BF16)

    vT_ref[0] = lax.dot_general(wv_ref[...], kvn, NT_DIMS, preferred_element_type=F32).astype(BF16)


def _mla_proj(x2d, mod_l, nmix, positions, w_in, g_qa, w_qb, g_kva, w_kvb, g_q, g_k, batch, seq):
    t, d = x2d.shape
    q_lora = g_qa.shape[0]
    kv_lora = g_kva.shape[0]
    qk_dim = g_q.shape[0]
    n_heads = w_qb.shape[1] // qk_dim
    nope = qk_dim - MLA_ROPE
    v_dim = w_kvb.shape[1] // n_heads - nope
    assert nope + v_dim == HEAD_PAD and qk_dim <= HEAD_PAD
    half = MLA_ROPE // 2
    tm = 512
    tpb = seq // tm

    w_lat = w_in[:, :q_lora + kv_lora].astype(BF16)
    w_kr = jnp.zeros((HEAD_PAD, d), F32).at[nope:qk_dim].set(w_in[:, q_lora + kv_lora:].T).astype(BF16)
    w_qb_t = jnp.pad(w_qb.T.reshape(n_heads, qk_dim, q_lora), ((0, 0), (0, HEAD_PAD - qk_dim), (0, 0)))
    w_qb_t = w_qb_t.reshape(n_heads * HEAD_PAD, q_lora).astype(BF16)
    w_kvb3 = w_kvb.reshape(kv_lora, n_heads, nope + v_dim)
    w_kn = jnp.where(jnp.arange(nope + v_dim) < nope, w_kvb3, 0.0).reshape(kv_lora, n_heads * HEAD_PAD).astype(BF16)
    w_v_t = w_kvb3[:, :, nope:].reshape(kv_lora, n_heads * v_dim).T.astype(BF16)
    q_gain = jnp.pad(g_q * g_k * (qk_dim ** -0.5 * LOG2E), (0, HEAD_PAD - qk_dim))
    gq_col = jnp.broadcast_to(q_gain[:, None], (HEAD_PAD, LANES))
    inv = 1.0 / (ROPE_THETA ** (jnp.arange(half, dtype=F32) / half))
    inv_tab = jnp.broadcast_to(inv[:, None], (half, LANES))
    pos_row = positions.reshape(1, t)

    full = lambda shape: pl.BlockSpec(shape, lambda i: (0,) * len(shape))
    kern = functools.partial(_mla_proj_kernel, n_heads=n_heads, q_lora=q_lora, qk_dim=qk_dim)
    return pl.pallas_call(
        kern,
        grid=(t // tm,),
        in_specs=[
            pl.BlockSpec((tm, d), lambda i: (i, 0)),
            pl.BlockSpec((1, N_MOD, d), lambda i: (i // tpb, 0, 0)),
            full((1, d)),
            pl.BlockSpec((1, tm), lambda i: (0, i)),
            full((half, LANES)),
            full(w_lat.shape), full(w_kr.shape), full((1, q_lora)), full((1, kv_lora)),
            full(w_qb_t.shape), full(w_kn.shape), full(w_v_t.shape),
            full((HEAD_PAD, LANES)),
        ],
        out_specs=[
            pl.BlockSpec((1, 1, n_heads * HEAD_PAD, tm), lambda i: (i // tpb, i % tpb, 0, 0)),
            pl.BlockSpec((tm, n_heads * HEAD_PAD), lambda i: (i, 0)),
            pl.BlockSpec((1, n_heads * v_dim, tm), lambda i: (i // tpb, 0, i % tpb)),
        ],
        out_shape=[
            jax.ShapeDtypeStruct((batch, tpb, n_heads * HEAD_PAD, tm), BF16),
            jax.ShapeDtypeStruct((t, n_heads * HEAD_PAD), BF16),
            jax.ShapeDtypeStruct((batch, n_heads * v_dim, seq), BF16),
        ],
        compiler_params=_params(("arbitrary",)),
        name="mla_proj",
    )(x2d, mod_l, nmix[None, :], pos_row, inv_tab, w_lat, w_kr, g_qa[None, :], g_kva[None, :],
      w_qb_t, w_kn, w_v_t, gq_col)


ONES_ROWS = 16


SAFE_EXPONENT = 60.0


def _score_bound_is_safe(gain_product, head_dim, bias=None):
    bound = 1.02 * math.sqrt(head_dim) * LOG2E * jnp.max(jnp.abs(gain_product))
    if bias is not None:
        bound = bound + LOG2E * jnp.max(jnp.abs(bias))
    return (bound <= SAFE_EXPONENT).astype(jnp.int32).reshape(1)


def _mla_attn_kernel(safe_ref, qT_ref, k_ref, vT_ref, oT_ref, s_ref, *, hb, tn, kc):
    seq = k_ref.shape[1]
    n_tiles, _, tq = qT_ref.shape[1:]
    v_dim = vT_ref.shape[1] // hb
    ones = jnp.ones((ONES_ROWS, seq), BF16)
    n_chunks = seq // kc
    n_cols = tq // tn
    hrows = lambda h: slice(h * HEAD_PAD, (h + 1) * HEAD_PAD)
    vrows = lambda h: slice(h * v_dim, (h + 1) * v_dim)
    v_ext = lambda h: jnp.concatenate([vT_ref[0, vrows(h), :], ones], axis=0)

    @pl.when(safe_ref[0] == 1)
    def _():
        tiles = [(h, i) for h in range(hb) for i in range(n_tiles)]

        def scores(t, c):
            h, i = tiles[t]
            s_ref[t % 2, c * kc:(c + 1) * kc, :] = jnp.dot(
                k_ref[0, c * kc:(c + 1) * kc, hrows(h)], qT_ref[0, i, hrows(h), :], preferred_element_type=F32)

        for c in range(n_chunks):
            scores(0, c)
        for t, (h, i) in enumerate(tiles):
            ve = v_ext(h)
            for j in range(n_cols):
                acc = jnp.zeros((v_dim + ONES_ROWS, tn), F32)
                for c in range(n_chunks):
                    step = j * n_chunks + c
                    if t + 1 < len(tiles) and step % n_cols == 0:
                        scores(t + 1, step // n_cols)
                    p = jnp.exp2(s_ref[t % 2, c * kc:(c + 1) * kc, j * tn:(j + 1) * tn]).astype(BF16)
                    acc = acc + jnp.dot(ve[:, c * kc:(c + 1) * kc], p, preferred_element_type=F32)
                o = acc[:v_dim] * (1.0 / acc[v_dim:v_dim + 1])
                oT_ref[0, i, vrows(h), j * tn:(j + 1) * tn] = o.astype(BF16)

    @pl.when(safe_ref[0] != 1)
    def _():
        for h in range(hb):
            ve = v_ext(h)
            k_h = k_ref[0, :, hrows(h)]

            def tile(i, carry):
                s_t = jnp.dot(k_h, qT_ref[0, i, hrows(h), :], preferred_element_type=F32)
                p = jnp.exp2(s_t - jnp.max(s_t, axis=0, keepdims=True)).astype(BF16)
                r = jnp.dot(ve, p, preferred_element_type=F32)
                oT_ref[0, i, vrows(h), :] = (r[:v_dim] * (1.0 / r[v_dim:v_dim + 1])).astype(BF16)
                return carry

            lax.fori_loop(0, n_tiles, tile, 0)


def _mla_attn(q_t, k_pad, v_t, safe, n_heads):
    batch, n_tiles, _, tq = q_t.shape
    seq = n_tiles * tq
    v_dim = v_t.shape[1] // n_heads
    k3 = k_pad.reshape(batch, seq, n_heads * HEAD_PAD)
    hb = 2
    return pl.pallas_call(
        functools.partial(_mla_attn_kernel, hb=hb, tn=256, kc=256),
        grid=(batch, n_heads // hb),
        in_specs=[
            pl.BlockSpec(memory_space=pltpu.SMEM),
            pl.BlockSpec((1, n_tiles, hb * HEAD_PAD, tq), lambda b, h: (b, 0, h, 0)),
            pl.BlockSpec((1, seq, hb * HEAD_PAD), lambda b, h: (b, 0, h)),
            pl.BlockSpec((1, hb * v_dim, seq), lambda b, h: (b, h, 0)),
        ],
        out_specs=pl.BlockSpec((1, n_tiles, hb * v_dim, tq), lambda b, h: (b, 0, h, 0)),
        out_shape=jax.ShapeDtypeStruct((batch, n_tiles, n_heads * v_dim, tq), BF16),
        scratch_shapes=[pltpu.VMEM((2, seq, tq), F32)],
        compiler_params=_params(("arbitrary", "arbitrary")),
        name="mla_attn",
    )(safe, q_t, k3, v_t)


MIX_ROWS = 512
MIX_SUB = 256
FF_CHUNK = 2048
LOAD_SLOTS = 4


def _load_bf16(src, dst, stage, sem):
    slots, chunk = stage.shape[:2]
    n = dst.shape[0] // chunk

    def copy(i):
        return pltpu.make_async_copy(src.at[pl.ds(i * chunk, chunk), :], stage.at[i % slots], sem.at[i % slots])

    for i in range(min(slots - 1, n)):
        copy(i).start()
    for i in range(n):
        if i + slots - 1 < n:
            copy(i + slots - 1).start()
        copy(i).wait()
        dst[i * chunk:(i + 1) * chunk, :] = stage[i % slots].astype(BF16)


def _mlp_rows(h2, w1_ref, w2_ref):
    z = None
    for f in range(w1_ref.shape[1] // FF_CHUNK):
        cols = slice(f * FF_CHUNK, (f + 1) * FF_CHUNK)
        u = jnp.dot(h2, w1_ref[:, cols], preferred_element_type=F32)
        u = jnp.square(jnp.maximum(u, 0.0)).astype(BF16)
        part = jnp.dot(u, w2_ref[cols, :], preferred_element_type=F32)
        z = part if z is None else z + part
    return z


def _mix_mlp_body(mixer_out, x_ref, mod_ref, nmlp_ref, next_refs, x2_ref, hn_ref, wo_ref, w1_ref, w2_ref):
    m = mod_ref[0]
    tm = x_ref.shape[0]
    blocks = [slice(i * MIX_SUB, (i + 1) * MIX_SUB) for i in range(tm // MIX_SUB)]

    def mix(rows):
        x1 = x_ref[rows, :] + m[2:3] * mixer_out(rows, wo_ref)
        return x1, _ada_norm(x1, nmlp_ref[...], m[4:5], m[3:4]).astype(BF16)

    cur = mix(blocks[0])
    for i, rows in enumerate(blocks):
        nxt = mix(blocks[i + 1]) if i + 1 < len(blocks) else None
        x1, h2 = cur
        x2 = x1 + m[5:6] * _mlp_rows(h2, w1_ref, w2_ref)
        x2_ref[rows, :] = x2
        if next_refs is not None:
            nnext_ref, modn_ref = next_refs
            mn = modn_ref[0]
            hn_ref[rows, :] = _ada_norm(x2, nnext_ref[...], mn[1:2], mn[0:1]).astype(BF16)
        cur = nxt


def _load_mix_weights(wo_hbm, w1_hbm, w2_hbm, layer, wo_ref, w1_ref, w2_ref, stage_w, stage_n, sem):
    _load_bf16(wo_hbm, wo_ref, stage_n, sem)
    _load_bf16(w1_hbm.at[layer], w1_ref, stage_w, sem)
    _load_bf16(w2_hbm.at[layer], w2_ref, stage_n, sem)


def _mla_mlp_kernel(oT_ref, x_ref, mod_ref, nmlp_ref, nnext_ref, modn_ref, wo_hbm, w1_hbm, w2_hbm, x2_ref, hn_ref,
                    wo_ref, w1_ref, w2_ref, stage_w, stage_n, sem, *, layer):
    @pl.when(pl.program_id(0) == 0)
    def _():
        _load_mix_weights(wo_hbm, w1_hbm, w2_hbm, layer, wo_ref, w1_ref, w2_ref, stage_w, stage_n, sem)

    def mixer_out(rows, wo):
        return lax.dot_general(oT_ref[0, 0, :, rows], wo[...], TN_DIMS, preferred_element_type=F32)

    _mix_mlp_body(mixer_out, x_ref, mod_ref, nmlp_ref, (nnext_ref, modn_ref), x2_ref, hn_ref, wo_ref, w1_ref, w2_ref)


def _dil_mlp_kernel(o0_ref, o1_ref, o2_ref, s0_ref, s1_ref, s2_ref, e_ref, x_ref, mod_ref, nmlp_ref,
                    wo_hbm, w1_hbm, w2_hbm, x2_ref, wo_ref, w1_ref, w2_ref, stage_w, stage_n, sem,
                    oscr1, sscr1, oscr2, sscr2, *, layer):
    @pl.when(pl.program_id(0) == 0)
    def _():
        _load_mix_weights(wo_hbm, w1_hbm, w2_hbm, layer, wo_ref, w1_ref, w2_ref, stage_w, stage_n, sem)

    tm = x_ref.shape[0]
    c = wo_ref.shape[0]

    def token_major(o_ref, s_ref, oscr, sscr, dilation):
        ra = tm // dilation
        for r in range(dilation):
            for s in range(c // LANES):
                lo = r * c + s * LANES
                oscr[s, pl.ds(r, ra, stride=dilation), :] = o_ref[0, :, lo:lo + LANES].astype(F32)
            sscr[pl.ds(r, ra, stride=dilation), :] = s_ref[0, :, r * LANES:(r + 1) * LANES]

    token_major(o1_ref, s1_ref, oscr1, sscr1, DIL_GROUPS[1][1])
    token_major(o2_ref, s2_ref, oscr2, sscr2, DIL_GROUPS[2][1])
    e = e_ref[...]

    def expand(w):
        return jnp.dot(w.astype(BF16), e, preferred_element_type=F32)

    def mixer_out(rows, wo):
        gather = lambda oscr: jnp.concatenate([oscr[s, rows, :] for s in range(c // LANES)], axis=1)
        o0, l0 = o0_ref[0, rows, :].astype(F32), s0_ref[0, rows, :]
        o1, l1 = gather(oscr1), sscr1[rows, :]
        o2, l2 = gather(oscr2), sscr2[rows, :]
        mx = jnp.maximum(jnp.maximum(l0, l1), l2)
        e0, e1, e2 = jnp.exp(l0 - mx), jnp.exp(l1 - mx), jnp.exp(l2 - mx)
        rden = 1.0 / (e0 + e1 + e2)
        o = expand(e0 * rden) * o0 + expand(e1 * rden) * o1 + expand(e2 * rden) * o2
        return jnp.dot(o.astype(BF16), wo[...], preferred_element_type=F32)

    _mix_mlp_body(mixer_out, x_ref, mod_ref, nmlp_ref, None, x2_ref, None, wo_ref, w1_ref, w2_ref)


def _mix_mlp_call(kern, name, lead_specs, lead_args, w_o, w1, w2, x2d, mod_l, nmlp, seq, next_args, extra_scratch):
    t, d = x2d.shape
    ff = w1.shape[2]
    c = w_o.shape[0]
    tm = MIX_ROWS
    tpb = seq // tm
    row = pl.BlockSpec((tm, d), lambda i: (i, 0))
    mod_spec = pl.BlockSpec((1, N_MOD, d), lambda i: (i // tpb, 0, 0))
    vec = pl.BlockSpec((1, d), lambda i: (0, 0))
    hbm = pl.BlockSpec(memory_space=pl.ANY)
    in_specs = lead_specs + [row, mod_spec, vec]
    args = lead_args + [x2d, mod_l, nmlp[None, :]]
    out_specs, out_shape = [row], [jax.ShapeDtypeStruct((t, d), F32)]
    if next_args is not None:
        nnext, mod_next = next_args
        in_specs += [vec, mod_spec]
        args += [nnext[None, :], mod_next]
        out_specs.append(row)
        out_shape.append(jax.ShapeDtypeStruct((t, d), BF16))
    in_specs += [hbm, hbm, hbm]
    args += [w_o, w1, w2]
    scratch = [pltpu.VMEM((c, d), BF16), pltpu.VMEM((d, ff), BF16), pltpu.VMEM((ff, d), BF16),
               pltpu.VMEM((LOAD_SLOTS, 64, ff), F32), pltpu.VMEM((LOAD_SLOTS, 256, d), F32),
               pltpu.SemaphoreType.DMA((LOAD_SLOTS,))]
    return pl.pallas_call(
        kern,
        grid=(t // tm,),
        in_specs=in_specs,
        out_specs=out_specs,
        out_shape=out_shape,
        scratch_shapes=scratch + extra_scratch,
        compiler_params=_params(("arbitrary",)),
        name=name,
    )(*args)


def _mla_mlp(o_t, w_o, w1, w2, layer, x2d, mod_l, nmlp, nnext, mod_next, seq):
    tm = MIX_ROWS
    tpb = seq // tm
    c = o_t.shape[2]
    assert o_t.shape[3] == tm
    lead = [pl.BlockSpec((1, 1, c, tm), lambda i: (i // tpb, i % tpb, 0, 0))]
    return _mix_mlp_call(functools.partial(_mla_mlp_kernel, layer=layer), "mla_mlp", lead, [o_t], w_o, w1, w2, x2d,
                         mod_l, nmlp, seq, (nnext, mod_next), [])


def _dil_mlp(outs, stats, w_o, w1, w2, layer, x2d, mod_l, nmlp, seq):
    tm = MIX_ROWS
    tpb = seq // tm
    c = w_o.shape[0]
    expand = jnp.repeat(jnp.eye(DIL_HEADS, dtype=F32), DIL_HEAD_DIM, axis=1)
    expand = jnp.pad(expand, ((0, LANES - DIL_HEADS), (0, 0))).astype(BF16)
    grp = lambda g, w: pl.BlockSpec((1, tm // DIL_GROUPS[g][1], DIL_GROUPS[g][1] * w),
                                    lambda i: (i // tpb, i % tpb, 0))
    lead = [grp(0, c), grp(1, c), grp(2, c), grp(0, LANES), grp(1, LANES), grp(2, LANES),
            pl.BlockSpec((LANES, c), lambda i: (0, 0))]
    extra = []
    for g in (1, 2):
        extra += [pltpu.VMEM((c // LANES, tm, LANES), F32), pltpu.VMEM((tm, LANES), F32)]
    (x2,) = _mix_mlp_call(functools.partial(_dil_mlp_kernel, layer=layer), "dil_mlp", lead, [*outs, *stats, expand],
                          w_o, w1, w2, x2d, mod_l, nmlp, seq, None, extra)
    return x2


DIL_TQ = 128
DIL_HALF = 64
DIL_ATTN_POSITIONS = 2048
assert all(w // (2 * d) == DIL_HALF for w, d in DIL_GROUPS)


def _dil_geometry(seq, dilation):
    length = seq // dilation
    kwin = min(2 * DIL_TQ, length)
    return length, kwin, length // DIL_TQ


def _t5_log_thresholds():
    nb = N_BUCKETS // 2
    max_exact = nb // 2
    steps = nb - max_exact
    thresholds = []
    for k in range(1, steps):
        n = max_exact
        while math.floor(math.log(n / max_exact) / math.log(MAX_DISTANCE / max_exact) * steps) < k:
            n += 1
        thresholds.append(n)
    return thresholds


def _t5_bucket(rel):
    nb = N_BUCKETS // 2
    max_exact = nb // 2
    n = jnp.abs(rel)
    large = max_exact
    for thr in _t5_log_thresholds():
        large = large + jnp.where(n >= thr, 1, 0)
    return jnp.where(rel > 0, nb, 0) + jnp.where(n < max_exact, n, large)


BIAS_ROWS = 16


def _dil_bias_kernel(tab_ref, o_ref, *, group, dilation, kwin, n_var):
    var = pl.program_id(0)

    def rows(step, carry):
        r0 = pl.multiple_of(step * BIAS_ROWS, BIAS_ROWS)
        kk = lax.broadcasted_iota(jnp.int32, (BIAS_ROWS, DIL_TQ), 0) + r0
        qi = lax.broadcasted_iota(jnp.int32, (BIAS_ROWS, DIL_TQ), 1)
        rel_a = kk - qi - (DIL_HALF if n_var > 1 else 0)
        valid = jnp.abs(rel_a) <= DIL_HALF
        if n_var > 1:
            crossing = jnp.where((qi < DIL_HALF) == (kk < DIL_TQ), 0, var)
            valid = jnp.where(valid, 1, 0) - crossing == 1
        bucket = _t5_bucket(rel_a * dilation)
        accs = [jnp.zeros((BIAS_ROWS, DIL_TQ), F32)] * DIL_HEADS
        for b in range(N_BUCKETS):
            hit = bucket == b
            accs = [jnp.where(hit, tab_ref[b, group * DIL_HEADS + h], a) for h, a in enumerate(accs)]
        for h, a in enumerate(accs):
            o_ref[0, h // 2, pl.ds(r0, BIAS_ROWS), (h % 2) * DIL_TQ:(h % 2 + 1) * DIL_TQ] = jnp.where(
                valid, a * LOG2E, MASKED)
        return carry

    lax.fori_loop(0, kwin // BIAS_ROWS, rows, 0)


def _dil_bias(rel_bias, group, seq):
    _, dilation = DIL_GROUPS[group]
    _, kwin, n_tiles = _dil_geometry(seq, dilation)
    n_var = 1 if n_tiles == 1 else 2
    n_pairs = DIL_HEADS // 2
    kern = functools.partial(_dil_bias_kernel, group=group, dilation=dilation, kwin=kwin, n_var=n_var)
    return pl.pallas_call(
        kern,
        grid=(n_var,),
        in_specs=[pl.BlockSpec(memory_space=pltpu.SMEM)],
        out_specs=pl.BlockSpec((1, n_pairs, kwin, 2 * DIL_TQ), lambda v: (v, 0, 0, 0)),
        out_shape=jax.ShapeDtypeStruct((n_var, n_pairs, kwin, 2 * DIL_TQ), F32),
        compiler_params=_params(("arbitrary",)),
        name=f"dil_bias{group}",
    )(rel_bias)


DIL_PROJ_ROWS = 512


def _dil_proj_kernel(h_ref, wqf_ref, wkf_ref, wvf_ref, gq_ref, qT_ref, k_ref, vT_ref, wq_ref, wk_ref, wv_ref,
                     *scratch, dilation, rb, ra):
    hd = DIL_HEAD_DIM
    d_model = h_ref.shape[2]
    n_slabs = d_model // LANES
    mt = rb * ra

    @pl.when((pl.program_id(0) == 0) & (pl.program_id(1) == 0) & (pl.program_id(2) == 0))
    def _():
        wq_ref[...] = wqf_ref[0].T.astype(BF16)
        wk_ref[...] = wkf_ref[0].astype(BF16)
        wv_ref[...] = wvf_ref[0].T.astype(BF16)

    if dilation == 1:
        hcat = h_ref[0]
    else:
        (scr,) = scratch
        sub = pl.program_id(2)

        @pl.when(sub == 0)
        def _():
            for s in range(n_slabs):
                scr[s] = h_ref[0, :, s * LANES:(s + 1) * LANES].astype(F32)

        rows = []
        for j in range(rb):
            res = sub * rb + j
            rows.append(jnp.concatenate(
                [scr[s, pl.ds(res, ra, stride=dilation), :] for s in range(n_slabs)], axis=1))
        hcat = jnp.concatenate(rows, axis=0).astype(BF16)

    q_t = lax.dot_general(wq_ref[...], hcat, NT_DIMS, preferred_element_type=F32)
    gq = jnp.concatenate([gq_ref[...]] * (mt // LANES), axis=1)
    for h in range(DIL_HEADS):
        blk = q_t[h * hd:(h + 1) * hd]
        r = lax.rsqrt(jnp.sum(blk * blk, axis=0, keepdims=True) * (1.0 / hd) + EPS)
        qb = (blk * r * gq).astype(BF16)
        for j in range(rb):
            for s in range(ra // LANES):
                lo = j * ra + s * LANES
                qT_ref[0, j, s, h * hd:(h + 1) * hd, :] = qb[:, lo:lo + LANES]

    kf = jnp.dot(hcat, wk_ref[...], preferred_element_type=F32)
    lane = lax.broadcasted_iota(jnp.int32, (mt, LANES), 1)
    low = lane < hd
    for c in range(DIL_HEADS * hd // LANES):
        y = kf[:, c * LANES:(c + 1) * LANES]
        y2 = y * y
        s_lo = jnp.sum(jnp.where(low, y2, 0.0), axis=1, keepdims=True)
        s_hi = jnp.sum(jnp.where(low, 0.0, y2), axis=1, keepdims=True)
        r = lax.rsqrt(jnp.where(low, s_lo, s_hi) * (1.0 / hd) + EPS)
        kb = (y * r).astype(BF16)
        for j in range(rb):
            k_ref[0, j, :, c * LANES:(c + 1) * LANES] = kb[j * ra:(j + 1) * ra]

    v_t = lax.dot_general(wv_ref[...], hcat, NT_DIMS, preferred_element_type=F32).astype(BF16)
    for j in range(rb):
        for s in range(ra // LANES):
            lo = j * ra + s * LANES
            vT_ref[0, j, s] = v_t[:, lo:lo + LANES]


def _dil_proj(hn3, w_in, g_q, g_k, group):
    batch, seq, d_model = hn3.shape
    _, dilation = DIL_GROUPS[group]
    length = seq // dilation
    c = DIL_HEADS * DIL_HEAD_DIM
    rb = min(dilation, 4)
    rows = 1024 if dilation <= rb else DIL_PROJ_ROWS
    ra = rows // rb
    tok = ra * dilation
    n_sub = dilation // rb
    gq_col = jnp.broadcast_to((g_q * g_k * (DIL_HEAD_DIM ** -0.5 * LOG2E))[:, None], (DIL_HEAD_DIM, LANES))
    kern = functools.partial(_dil_proj_kernel, dilation=dilation, rb=rb, ra=ra)
    const = lambda shape: pl.BlockSpec(shape, lambda b, t, s: (0,) * len(shape))
    w_blk = lambda j: pl.BlockSpec((1, d_model, c), lambda b, t, s: (0, 0, 3 * group + j))
    scratch = [pltpu.VMEM((c, d_model), BF16), pltpu.VMEM((d_model, c), BF16), pltpu.VMEM((c, d_model), BF16)]
    if dilation > 1:
        scratch.append(pltpu.VMEM((d_model // LANES, tok, LANES), F32))
    return pl.pallas_call(
        kern,
        grid=(batch, seq // tok, n_sub),
        in_specs=[
            pl.BlockSpec((1, tok, d_model), lambda b, t, s: (b, t, 0)),
            w_blk(0), w_blk(1), w_blk(2),
            const((DIL_HEAD_DIM, LANES)),
        ],
        out_specs=[
            pl.BlockSpec((1, rb, ra // LANES, c, LANES), lambda b, t, s: (b, s, t, 0, 0)),
            pl.BlockSpec((1, rb, ra, c), lambda b, t, s: (b, s, t, 0)),
            pl.BlockSpec((1, rb, ra // LANES, c, LANES), lambda b, t, s: (b, s, t, 0, 0)),
        ],
        out_shape=[
            jax.ShapeDtypeStruct((batch, dilation, length // LANES, c, LANES), BF16),
            jax.ShapeDtypeStruct((batch, dilation, length, c), BF16),
            jax.ShapeDtypeStruct((batch, dilation, length // LANES, c, LANES), BF16),
        ],
        scratch_shapes=scratch,
        compiler_params=_params(("arbitrary", "arbitrary", "arbitrary")),
        name=f"dil_proj{group}",
    )(hn3, w_in, w_in, w_in, gq_col)


def _dil_attn_kernel(safe_ref, qT_ref, k_ref, vT_ref, bm_ref, o_ref, st_ref, s_ref, *, n_tiles, rb):
    tq = DIL_TQ
    hd = DIL_HEAD_DIM
    n_pairs = DIL_HEADS // 2
    c = DIL_HEADS * hd
    shifted = n_tiles > 1
    low = lax.broadcasted_iota(jnp.int32, (LANES, tq), 0) < hd
    ones = jnp.ones((ONES_ROWS, 2 * tq if shifted else tq), BF16)
    prow = lambda p: slice(p * LANES, (p + 1) * LANES)

    def block_diag(q2):
        zero = jnp.zeros_like(q2)
        return jnp.concatenate([jnp.where(low, q2, zero), jnp.where(low, zero, q2)], axis=1)

    def attend(s_t, v2, use_max):
        if use_max:
            m = jnp.max(s_t, axis=0, keepdims=True)
            pb = jnp.exp2(s_t - m).astype(BF16)
        else:
            pb = jnp.exp2(s_t).astype(BF16)
        res = jnp.dot(jnp.concatenate([v2, ones], axis=0), pb, preferred_element_type=F32)
        den = res[LANES:LANES + 1]
        rinv = 1.0 / den
        o_pair = jnp.concatenate([res[:hd, :tq] * rinv[:, :tq], res[hd:LANES, tq:] * rinv[:, tq:]], axis=0)
        lse = jnp.log2(den) + m if use_max else jnp.log2(den)
        lse = lse * LN2
        return o_pair.T.astype(BF16), [lse[:, :tq], lse[:, tq:]]

    def stats_rows(stats):
        return jnp.concatenate(stats + [jnp.zeros((LANES - DIL_HEADS, tq), F32)], axis=0).T

    tiles = [(j, t) for j in range(rb) for t in range(n_tiles)]

    @pl.when(safe_ref[0] == 1)
    def _():
        def score_tile(i):
            j, t = tiles[i]
            t1 = (t + 1) % n_tiles
            for p in range(n_pairs):
                if shifted:
                    q2 = jnp.concatenate(
                        [qT_ref[0, j, t, prow(p), DIL_HALF:], qT_ref[0, j, t1, prow(p), :DIL_HALF]], axis=1)
                    k2 = jnp.concatenate([k_ref[0, j, t * tq:(t + 1) * tq, prow(p)],
                                          k_ref[0, j, t1 * tq:(t1 + 1) * tq, prow(p)]], axis=0)
                else:
                    q2 = qT_ref[0, j, 0, prow(p), :]
                    k2 = k_ref[0, j, :, prow(p)]
                s_ref[i % 2, p] = jnp.dot(k2, block_diag(q2), preferred_element_type=F32)

        def attend_tile(i):
            j, t = tiles[i]
            t1 = (t + 1) % n_tiles
            var = 1 if shifted and t + 1 == n_tiles else 0

            def store(ref, lanes, val):
                if shifted:
                    ref[0, t * tq + DIL_HALF:(t + 1) * tq, lanes] = val[:DIL_HALF]
                    ref[0, t1 * tq:t1 * tq + DIL_HALF, lanes] = val[DIL_HALF:]
                else:
                    ref[0, :, lanes] = val

            stats = []
            for p in range(n_pairs):
                if shifted:
                    v2 = jnp.concatenate([vT_ref[0, j, t, prow(p), :], vT_ref[0, j, t1, prow(p), :]], axis=1)
                else:
                    v2 = vT_ref[0, j, 0, prow(p), :]
                o_rows, lses = attend(s_ref[i % 2, p] + bm_ref[var, p], v2, use_max=False)
                store(o_ref, slice(j * c + p * LANES, j * c + (p + 1) * LANES), o_rows)
                stats += lses
            store(st_ref, slice(j * LANES, (j + 1) * LANES), stats_rows(stats))

        score_tile(0)
        for i in range(len(tiles)):
            if i + 1 < len(tiles):
                score_tile(i + 1)
            attend_tile(i)

    @pl.when(safe_ref[0] != 1)
    def _():
        def tile(j, t):
            if shifted:
                last = t + 1 == n_tiles
                t1 = jnp.where(last, 0, t + 1)
                var = jnp.where(last, 1, 0)
                r0 = pl.multiple_of(t * tq, tq)
                r1 = pl.multiple_of(t1 * tq, tq)

            def store(ref, lanes, val):
                if shifted:
                    ref[0, pl.ds(r0 + DIL_HALF, DIL_HALF), lanes] = val[:DIL_HALF]
                    ref[0, pl.ds(r1, DIL_HALF), lanes] = val[DIL_HALF:]
                else:
                    ref[0, :, lanes] = val

            stats = []
            for p in range(n_pairs):
                if shifted:
                    q2 = jnp.concatenate(
                        [qT_ref[0, j, t, prow(p), DIL_HALF:], qT_ref[0, j, t1, prow(p), :DIL_HALF]], axis=1)
                    k2 = jnp.concatenate(
                        [k_ref[0, j, pl.ds(r0, tq), prow(p)], k_ref[0, j, pl.ds(r1, tq), prow(p)]], axis=0)
                    v2 = jnp.concatenate([vT_ref[0, j, t, prow(p), :], vT_ref[0, j, t1, prow(p), :]], axis=1)
                    bias = bm_ref[var, p]
                else:
                    q2 = qT_ref[0, j, 0, prow(p), :]
                    k2 = k_ref[0, j, :, prow(p)]
                    v2 = vT_ref[0, j, 0, prow(p), :]
                    bias = bm_ref[0, p]
                s_t = jnp.dot(k2, block_diag(q2), preferred_element_type=F32) + bias
                o_rows, lses = attend(s_t, v2, use_max=True)
                store(o_ref, slice(j * c + p * LANES, j * c + (p + 1) * LANES), o_rows)
                stats += lses
            store(st_ref, slice(j * LANES, (j + 1) * LANES), stats_rows(stats))

        for j in range(rb):
            if shifted:
                lax.fori_loop(0, n_tiles, lambda t, carry, j=j: (tile(j, t), carry)[1], 0)
            else:
                tile(j, 0)


def _dil_attn(q_t, k, v_t, bias, safe, group, seq):
    batch = q_t.shape[0]
    _, dilation = DIL_GROUPS[group]
    length, kwin, n_tiles = _dil_geometry(seq, dilation)
    c = DIL_HEADS * DIL_HEAD_DIM
    n_slabs = length // LANES
    rb = min(dilation, DIL_ATTN_POSITIONS // length, 8)
    kern = functools.partial(_dil_attn_kernel, n_tiles=n_tiles, rb=rb)
    return pl.pallas_call(
        kern,
        grid=(batch, dilation // rb),
        in_specs=[
            pl.BlockSpec(memory_space=pltpu.SMEM),
            pl.BlockSpec((1, rb, n_slabs, c, LANES), lambda b, r: (b, r, 0, 0, 0)),
            pl.BlockSpec((1, rb, length, c), lambda b, r: (b, r, 0, 0)),
            pl.BlockSpec((1, rb, n_slabs, c, LANES), lambda b, r: (b, r, 0, 0, 0)),
            pl.BlockSpec(bias.shape, lambda b, r: (0, 0, 0, 0)),
        ],
        out_specs=[
            pl.BlockSpec((1, length, rb * c), lambda b, r: (b, 0, r)),
            pl.BlockSpec((1, length, rb * LANES), lambda b, r: (b, 0, r)),
        ],
        out_shape=[
            jax.ShapeDtypeStruct((batch, length, dilation * c), BF16),
            jax.ShapeDtypeStruct((batch, length, dilation * LANES), F32),
        ],
        scratch_shapes=[pltpu.VMEM((2, DIL_HEADS // 2, kwin, 2 * DIL_TQ), F32)],
        compiler_params=_params(("arbitrary", "arbitrary")),
        name=f"dil_attn{group}",
    )(safe, q_t, k, v_t, bias)


def kernel(x, c, positions, ada_w, ada_b, norm_mix, norm_mlp, mlp_w1, mlp_w2, mla_w_in, mla_g_qa, mla_w_qb,
           mla_g_kva, mla_w_kvb, mla_g_q, mla_g_k, mla_w_o, dil_w_in, dil_g_q, dil_g_k, dil_w_o, rel_bias):
    batch, seq, d = x.shape
    depth = ada_w.shape[0]
    assert depth == 2 and len(DIL_GROUPS) == 3
    mod = _mod(c, ada_w, ada_b).reshape(depth, batch, N_MOD, d)
    x2d = x.reshape(batch * seq, d)

    n_heads = mla_w_qb.shape[2] // mla_g_q.shape[1]
    q_t, k_pad, v_t = _mla_proj(x2d, mod[0], norm_mix[0], positions, mla_w_in[0], mla_g_qa[0], mla_w_qb[0],
                                mla_g_kva[0], mla_w_kvb[0], mla_g_q[0], mla_g_k[0], batch, seq)
    o_t = _mla_attn(q_t, k_pad, v_t, _score_bound_is_safe(mla_g_q[0] * mla_g_k[0], mla_g_q.shape[1]), n_heads)
    x2, hn = _mla_mlp(o_t, mla_w_o[0], mlp_w1, mlp_w2, 0, x2d, mod[0], norm_mlp[0], norm_mix[1], mod[1], seq)

    outs, stats = [], []
    hn3 = hn.reshape(batch, seq, d)
    for g in range(len(DIL_GROUPS)):
        q_g, k_g, v_g = _dil_proj(hn3, dil_w_in, dil_g_q[0, g], dil_g_k[0, g], g)
        safe = _score_bound_is_safe(dil_g_q[0, g] * dil_g_k[0, g], DIL_HEAD_DIM,
                                    rel_bias[:, g * DIL_HEADS:(g + 1) * DIL_HEADS])
        o_g, st_g = _dil_attn(q_g, k_g, v_g, _dil_bias(rel_bias, g, seq), safe, g, seq)
        outs.append(o_g)
        stats.append(st_g)
    x4 = _dil_mlp(outs, stats, dil_w_o[0], mlp_w1, mlp_w2, 1, x2, mod[1], norm_mlp[1], seq)
    return x4.reshape(batch, seq, d)
```

```python
import functools
import math

import jax
import jax.numpy as jnp
from jax import lax
from jax.experimental import pallas as pl
from jax.experimental.pallas import tpu as pltpu

F32 = jnp.float32
BF16 = jnp.bfloat16

EPS = 1e-6
LOG2E = 1.4426950408889634
LN2 = 0.6931471805599453
MASKED = -1e30

N_MOD = 6
ROPE_THETA = 10000.0
MLA_ROPE = 32
DIL_GROUPS = ((128, 1), (512, 4), (2048, 16))
DIL_HEADS = 16
DIL_HEAD_DIM = 64
N_BUCKETS = 32
MAX_DISTANCE = 1024

LANES = 128
HEAD_PAD = 128

VMEM_LIMIT = 56 * 1024 * 1024

NT_DIMS = (((1,), (1,)), ((), ()))
TN_DIMS = (((0,), (0,)), ((), ()))


def _params(sem):
    return pltpu.CompilerParams(dimension_semantics=sem, vmem_limit_bytes=VMEM_LIMIT)


def _ada_norm(x, g, scale, shift):
    ms = jnp.mean(x * x, axis=-1, keepdims=True)
    return (x * lax.rsqrt(ms + EPS) * g) * (1.0 + scale) + shift


def _rms(x, g):
    ms = jnp.mean(x * x, axis=-1, keepdims=True)
    return x * lax.rsqrt(ms + EPS) * g


def _mod_kernel(c_ref, w_ref, b_ref, o_ref):
    c = c_ref[...]
    cond = c / (1.0 + jnp.exp(-c))
    o_ref[0] = (
        jnp.dot(cond.astype(BF16), w_ref[0].astype(BF16), preferred_element_type=F32) + b_ref[0]
    )


def _mod(c, ada_w, ada_b):
    depth, d, n = ada_w.shape
    b = c.shape[0]
    tn = 1536
    return pl.pallas_call(
        _mod_kernel,
        grid=(depth, n // tn),
        in_specs=[
            pl.BlockSpec((b, d), lambda l, j: (0, 0)),
            pl.BlockSpec((1, d, tn), lambda l, j: (l, 0, j)),
            pl.BlockSpec((1, 1, tn), lambda l, j: (l, 0, j)),
        ],
        out_specs=pl.BlockSpec((1, b, tn), lambda l, j: (l, 0, j)),
        out_shape=jax.ShapeDtypeStruct((depth, b, n), F32),
        compiler_params=_params(("arbitrary", "arbitrary")),
        name="mod",
    )(c, ada_w, ada_b.reshape(depth, 1, n))


def _mla_proj_kernel(x_ref, mod_ref, nmix_ref, pos_ref, inv_ref, wlat_ref, wkr_ref, gqa_ref, gkva_ref,
                     wqb_ref, wkn_ref, wv_ref, gq_ref, qT_ref, k_ref, vT_ref, *, n_heads, q_lora, qk_dim):
    x = x_ref[...]
    tm = x.shape[0]
    m = mod_ref[0]
    hb = _ada_norm(x, nmix_ref[...], m[1:2], m[0:1]).astype(BF16)
    lat = jnp.dot(hb, wlat_ref[...], preferred_element_type=F32)
    qn = _rms(lat[:, :q_lora], gqa_ref[...]).astype(BF16)
    kvn = _rms(lat[:, q_lora:], gkva_ref[...]).astype(BF16)

    half = MLA_ROPE // 2
    nope = qk_dim - MLA_ROPE
    pos = pos_ref[...].astype(F32)
    ang = jnp.concatenate([inv_ref[...]] * (tm // LANES), axis=1) * pos
    cos_t = jnp.cos(ang)
    sin_t = jnp.sin(ang)

    def rope_rows(blk):
        x1 = blk[nope:nope + half]
        x2 = blk[nope + half:qk_dim]
        return jnp.concatenate(
            [blk[:nope], x1 * cos_t - x2 * sin_t, x2 * cos_t + x1 * sin_t, blk[qk_dim:]], axis=0)

    kr_t = lax.dot_general(wkr_ref[...], hb, NT_DIMS, preferred_element_type=F32)
    kr = rope_rows(kr_t).T

    q_t = lax.dot_general(wqb_ref[...], qn, NT_DIMS, preferred_element_type=F32)
    gq = jnp.concatenate([gq_ref[...]] * (tm // LANES), axis=1)
    for h in range(n_heads):
        blk = rope_rows(q_t[h * HEAD_PAD:(h + 1) * HEAD_PAD])
        ssq = jnp.sum(blk * blk, axis=0, keepdims=True)
        r = lax.rsqrt(ssq * (1.0 / qk_dim) + EPS)
        qT_ref[0, 0, h * HEAD_PAD:(h + 1) * HEAD_PAD, :] = (blk * r * gq).astype(BF16)

    kn = jnp.dot(kvn, wkn_ref[...], preferred_element_type=F32)
    for h in range(n_heads):
        kb = kn[:, h * HEAD_PAD:(h + 1) * HEAD_PAD] + kr
        ssq = jnp.sum(kb * kb, axis=1, keepdims=True)
        r = lax.rsqrt(ssq * (1.0 / qk_dim) + EPS)
        k_ref[:, h * HEAD_PAD:(h + 1) * HEAD_PAD] = (kb * r).astype(BF16)

    vT_ref[0] = lax.dot_general(wv_ref[...], kvn, NT_DIMS, preferred_element_type=F32).astype(BF16)


def _mla_proj(x2d, mod_l, nmix, positions, w_in, g_qa, w_qb, g_kva, w_kvb, g_q, g_k, batch, seq):
    t, d = x2d.shape
    q_lora = g_qa.shape[0]
    kv_lora = g_kva.shape[0]
    qk_dim = g_q.shape[0]
    n_heads = w_qb.shape[1] // qk_dim
    nope = qk_dim - MLA_ROPE
    v_dim = w_kvb.shape[1] // n_heads - nope
    assert nope + v_dim == HEAD_PAD and qk_dim <= HEAD_PAD
    half = MLA_ROPE // 2
    tm = 512
    tpb = seq // tm

    w_lat = w_in[:, :q_lora + kv_lora].astype(BF16)
    w_kr = jnp.zeros((HEAD_PAD, d), F32).at[nope:qk_dim].set(w_in[:, q_lora + kv_lora:].T).astype(BF16)
    w_qb_t = jnp.pad(w_qb.T.reshape(n_heads, qk_dim, q_lora), ((0, 0), (0, HEAD_PAD - qk_dim), (0, 0)))
    w_qb_t = w_qb_t.reshape(n_heads * HEAD_PAD, q_lora).astype(BF16)
    w_kvb3 = w_kvb.reshape(kv_lora, n_heads, nope + v_dim)
    w_kn = jnp.where(jnp.arange(nope + v_dim) < nope, w_kvb3, 0.0).reshape(kv_lora, n_heads * HEAD_PAD).astype(BF16)
    w_v_t = w_kvb3[:, :, nope:].reshape(kv_lora, n_heads * v_dim).T.astype(BF16)
    q_gain = jnp.pad(g_q * g_k * (qk_dim ** -0.5 * LOG2E), (0, HEAD_PAD - qk_dim))
    gq_col = jnp.broadcast_to(q_gain[:, None], (HEAD_PAD, LANES))
    inv = 1.0 / (ROPE_THETA ** (jnp.arange(half, dtype=F32) / half))
    inv_tab = jnp.broadcast_to(inv[:, None], (half, LANES))
    pos_row = positions.reshape(1, t)

    full = lambda shape: pl.BlockSpec(shape, lambda i: (0,) * len(shape))
    kern = functools.partial(_mla_proj_kernel, n_heads=n_heads, q_lora=q_lora, qk_dim=qk_dim)
    return pl.pallas_call(
        kern,
        grid=(t // tm,),
        in_specs=[
            pl.BlockSpec((tm, d), lambda i: (i, 0)),
            pl.BlockSpec((1, N_MOD, d), lambda i: (i // tpb, 0, 0)),
            full((1, d)),
            pl.BlockSpec((1, tm), lambda i: (0, i)),
            full((half, LANES)),
            full(w_lat.shape), full(w_kr.shape), full((1, q_lora)), full((1, kv_lora)),
            full(w_qb_t.shape), full(w_kn.shape), full(w_v_t.shape),
            full((HEAD_PAD, LANES)),
        ],
        out_specs=[
            pl.BlockSpec((1, 1, n_heads * HEAD_PAD, tm), lambda i: (i // tpb, i % tpb, 0, 0)),
            pl.BlockSpec((tm, n_heads * HEAD_PAD), lambda i: (i, 0)),
            pl.BlockSpec((1, n_heads * v_dim, tm), lambda i: (i // tpb, 0, i % tpb)),
        ],
        out_shape=[
            jax.ShapeDtypeStruct((batch, tpb, n_heads * HEAD_PAD, tm), BF16),
            jax.ShapeDtypeStruct((t, n_heads * HEAD_PAD), BF16),
            jax.ShapeDtypeStruct((batch, n_heads * v_dim, seq), BF16),
        ],
        compiler_params=_params(("arbitrary",)),
        name="mla_proj",
    )(x2d, mod_l, nmix[None, :], pos_row, inv_tab, w_lat, w_kr, g_qa[None, :], g_kva[None, :],
      w_qb_t, w_kn, w_v_t, gq_col)


ONES_ROWS = 16


SAFE_EXPONENT = 60.0


def _score_bound_is_safe(gain_product, head_dim, bias=None):
    bound = 1.02 * math.sqrt(head_dim) * LOG2E * jnp.max(jnp.abs(gain_product))
    if bias is not None:
        bound = bound + LOG2E * jnp.max(jnp.abs(bias))
    return (bound <= SAFE_EXPONENT).astype(jnp.int32).reshape(1)


def _mla_attn_kernel(safe_ref, qT_ref, k_ref, vT_ref, oT_ref, s_ref, *, hb, tn, kc):
    seq = k_ref.shape[1]
    n_tiles, _, tq = qT_ref.shape[1:]
    v_dim = vT_ref.shape[1] // hb
    ones = jnp.ones((ONES_ROWS, seq), BF16)
    n_chunks = seq // kc
    n_cols = tq // tn
    hrows = lambda h: slice(h * HEAD_PAD, (h + 1) * HEAD_PAD)
    vrows = lambda h: slice(h * v_dim, (h + 1) * v_dim)
    v_ext = lambda h: jnp.concatenate([vT_ref[0, vrows(h), :], ones], axis=0)

    @pl.when(safe_ref[0] == 1)
    def _():
        tiles = [(h, i) for h in range(hb) for i in range(n_tiles)]

        def scores(t, c):
            h, i = tiles[t]
            s_ref[t % 2, c * kc:(c + 1) * kc, :] = jnp.dot(
                k_ref[0, c * kc:(c + 1) * kc, hrows(h)], qT_ref[0, i, hrows(h), :], preferred_element_type=F32)

        for c in range(n_chunks):
            scores(0, c)
        for t, (h, i) in enumerate(tiles):
            ve = v_ext(h)
            for j in range(n_cols):
                acc = jnp.zeros((v_dim + ONES_ROWS, tn), F32)
                for c in range(n_chunks):
                    step = j * n_chunks + c
                    if t + 1 < len(tiles) and step % n_cols == 0:
                        scores(t + 1, step // n_cols)
                    p = jnp.exp2(s_ref[t % 2, c * kc:(c + 1) * kc, j * tn:(j + 1) * tn]).astype(BF16)
                    acc = acc + jnp.dot(ve[:, c * kc:(c + 1) * kc], p, preferred_element_type=F32)
                o = acc[:v_dim] * (1.0 / acc[v_dim:v_dim + 1])
                oT_ref[0, i, vrows(h), j * tn:(j + 1) * tn] = o.astype(BF16)

    @pl.when(safe_ref[0] != 1)
    def _():
        for h in range(hb):
            ve = v_ext(h)
            k_h = k_ref[0, :, hrows(h)]

            def tile(i, carry):
                s_t = jnp.dot(k_h, qT_ref[0, i, hrows(h), :], preferred_element_type=F32)
                p = jnp.exp2(s_t - jnp.max(s_t, axis=0, keepdims=True)).astype(BF16)
                r = jnp.dot(ve, p, preferred_element_type=F32)
                oT_ref[0, i, vrows(h), :] = (r[:v_dim] * (1.0 / r[v_dim:v_dim + 1])).astype(BF16)
                return carry

            lax.fori_loop(0, n_tiles, tile, 0)


def _mla_attn(q_t, k_pad, v_t, safe, n_heads):
    batch, n_tiles, _, tq = q_t.shape
    seq = n_tiles * tq
    v_dim = v_t.shape[1] // n_heads
    k3 = k_pad.reshape(batch, seq, n_heads * HEAD_PAD)
    hb = 2
    return pl.pallas_call(
        functools.partial(_mla_attn_kernel, hb=hb, tn=256, kc=256),
        grid=(batch, n_heads // hb),
        in_specs=[
            pl.BlockSpec(memory_space=pltpu.SMEM),
            pl.BlockSpec((1, n_tiles, hb * HEAD_PAD, tq), lambda b, h: (b, 0, h, 0)),
            pl.BlockSpec((1, seq, hb * HEAD_PAD), lambda b, h: (b, 0, h)),
            pl.BlockSpec((1, hb * v_dim, seq), lambda b, h: (b, h, 0)),
        ],
        out_specs=pl.BlockSpec((1, n_tiles, hb * v_dim, tq), lambda b, h: (b, 0, h, 0)),
        out_shape=jax.ShapeDtypeStruct((batch, n_tiles, n_heads * v_dim, tq), BF16),
        scratch_shapes=[pltpu.VMEM((2, seq, tq), F32)],
        compiler_params=_params(("arbitrary", "arbitrary")),
        name="mla_attn",
    )(safe, q_t, k3, v_t)


MIX_ROWS = 512
MIX_SUB = 256
FF_CHUNK = 2048
LOAD_SLOTS = 4


def _load_bf16(src, dst, stage, sem):
    slots, chunk = stage.shape[:2]
    n = dst.shape[0] // chunk

    def copy(i):
        return pltpu.make_async_copy(src.at[pl.ds(i * chunk, chunk), :], stage.at[i % slots], sem.at[i % slots])

    for i in range(min(slots - 1, n)):
        copy(i).start()
    for i in range(n):
        if i + slots - 1 < n:
            copy(i + slots - 1).start()
        copy(i).wait()
        dst[i * chunk:(i + 1) * chunk, :] = stage[i % slots].astype(BF16)


def _mlp_rows(h2, w1_ref, w2_ref):
    z = None
    for f in range(w1_ref.shape[1] // FF_CHUNK):
        cols = slice(f * FF_CHUNK, (f + 1) * FF_CHUNK)
        u = jnp.dot(h2, w1_ref[:, cols], preferred_element_type=F32)
        u = jnp.square(jnp.maximum(u, 0.0)).astype(BF16)
        part = jnp.dot(u, w2_ref[cols, :], preferred_element_type=F32)
        z = part if z is None else z + part
    return z


def _mix_mlp_body(mixer_out, x_ref, mod_ref, nmlp_ref, next_refs, x2_ref, hn_ref, wo_ref, w1_ref, w2_ref):
    m = mod_ref[0]
    tm = x_ref.shape[0]
    blocks = [slice(i * MIX_SUB, (i + 1) * MIX_SUB) for i in range(tm // MIX_SUB)]

    def mix(rows):
        x1 = x_ref[rows, :] + m[2:3] * mixer_out(rows, wo_ref)
        return x1, _ada_norm(x1, nmlp_ref[...], m[4:5], m[3:4]).astype(BF16)

    cur = mix(blocks[0])
    for i, rows in enumerate(blocks):
        nxt = mix(blocks[i + 1]) if i + 1 < len(blocks) else None
        x1, h2 = cur
        x2 = x1 + m[5:6] * _mlp_rows(h2, w1_ref, w2_ref)
        x2_ref[rows, :] = x2
        if next_refs is not None:
            nnext_ref, modn_ref = next_refs
            mn = modn_ref[0]
            hn_ref[rows, :] = _ada_norm(x2, nnext_ref[...], mn[1:2], mn[0:1]).astype(BF16)
        cur = nxt


def _load_mix_weights(wo_hbm, w1_hbm, w2_hbm, layer, wo_ref, w1_ref, w2_ref, stage_w, stage_n, sem):
    _load_bf16(wo_hbm, wo_ref, stage_n, sem)
    _load_bf16(w1_hbm.at[layer], w1_ref, stage_w, sem)
    _load_bf16(w2_hbm.at[layer], w2_ref, stage_n, sem)


def _mla_mlp_kernel(oT_ref, x_ref, mod_ref, nmlp_ref, nnext_ref, modn_ref, wo_hbm, w1_hbm, w2_hbm, x2_ref, hn_ref,
                    wo_ref, w1_ref, w2_ref, stage_w, stage_n, sem, *, layer):
    @pl.when(pl.program_id(0) == 0)
    def _():
        _load_mix_weights(wo_hbm, w1_hbm, w2_hbm, layer, wo_ref, w1_ref, w2_ref, stage_w, stage_n, sem)

    def mixer_out(rows, wo):
        return lax.dot_general(oT_ref[0, 0, :, rows], wo[...], TN_DIMS, preferred_element_type=F32)

    _mix_mlp_body(mixer_out, x_ref, mod_ref, nmlp_ref, (nnext_ref, modn_ref), x2_ref, hn_ref, wo_ref, w1_ref, w2_ref)


def _dil_mlp_kernel(o0_ref, o1_ref, o2_ref, s0_ref, s1_ref, s2_ref, e_ref, x_ref, mod_ref, nmlp_ref,
                    wo_hbm, w1_hbm, w2_hbm, x2_ref, wo_ref, w1_ref, w2_ref, stage_w, stage_n, sem,
                    oscr1, sscr1, oscr2, sscr2, *, layer):
    @pl.when(pl.program_id(0) == 0)
    def _():
        _load_mix_weights(wo_hbm, w1_hbm, w2_hbm, layer, wo_ref, w1_ref, w2_ref, stage_w, stage_n, sem)

    tm = x_ref.shape[0]
    c = wo_ref.shape[0]

    def token_major(o_ref, s_ref, oscr, sscr, dilation):
        ra = tm // dilation
        for r in range(dilation):
            for s in range(c // LANES):
                lo = r * c + s * LANES
                oscr[s, pl.ds(r, ra, stride=dilation), :] = o_ref[0, :, lo:lo + LANES].astype(F32)
            sscr[pl.ds(r, ra, stride=dilation), :] = s_ref[0, :, r * LANES:(r + 1) * LANES]

    token_major(o1_ref, s1_ref, oscr1, sscr1, DIL_GROUPS[1][1])
    token_major(o2_ref, s2_ref, oscr2, sscr2, DIL_GROUPS[2][1])
    e = e_ref[...]

    def expand(w):
        return jnp.dot(w.astype(BF16), e, preferred_element_type=F32)

    def mixer_out(rows, wo):
        gather = lambda oscr: jnp.concatenate([oscr[s, rows, :] for s in range(c // LANES)], axis=1)
        o0, l0 = o0_ref[0, rows, :].astype(F32), s0_ref[0, rows, :]
        o1, l1 = gather(oscr1), sscr1[rows, :]
        o2, l2 = gather(oscr2), sscr2[rows, :]
        mx = jnp.maximum(jnp.maximum(l0, l1), l2)
        e0, e1, e2 = jnp.exp(l0 - mx), jnp.exp(l1 - mx), jnp.exp(l2 - mx)
        rden = 1.0 / (e0 + e1 + e2)
        o = expand(e0 * rden) * o0 + expand(e1 * rden) * o1 + expand(e2 * rden) * o2
        return jnp.dot(o.astype(BF16), wo[...], preferred_element_type=F32)

    _mix_mlp_body(mixer_out, x_ref, mod_ref, nmlp_ref, None, x2_ref, None, wo_ref, w1_ref, w2_ref)


def _mix_mlp_call(kern, name, lead_specs, lead_args, w_o, w1, w2, x2d, mod_l, nmlp, seq, next_args, extra_scratch):
    t, d = x2d.shape
    ff = w1.shape[2]
    c = w_o.shape[0]
    tm = MIX_ROWS
    tpb = seq // tm
    row = pl.BlockSpec((tm, d), lambda i: (i, 0))
    mod_spec = pl.BlockSpec((1, N_MOD, d), lambda i: (i // tpb, 0, 0))
    vec = pl.BlockSpec((1, d), lambda i: (0, 0))
    hbm = pl.BlockSpec(memory_space=pl.ANY)
    in_specs = lead_specs + [row, mod_spec, vec]
    args = lead_args + [x2d, mod_l, nmlp[None, :]]
    out_specs, out_shape = [row], [jax.ShapeDtypeStruct((t, d), F32)]
    if next_args is not None:
        nnext, mod_next = next_args
        in_specs += [vec, mod_spec]
        args += [nnext[None, :], mod_next]
        out_specs.append(row)
        out_shape.append(jax.ShapeDtypeStruct((t, d), BF16))
    in_specs += [hbm, hbm, hbm]
    args += [w_o, w1, w2]
    scratch = [pltpu.VMEM((c, d), BF16), pltpu.VMEM((d, ff), BF16), pltpu.VMEM((ff, d), BF16),
               pltpu.VMEM((LOAD_SLOTS, 64, ff), F32), pltpu.VMEM((LOAD_SLOTS, 256, d), F32),
               pltpu.SemaphoreType.DMA((LOAD_SLOTS,))]
    return pl.pallas_call(
        kern,
        grid=(t // tm,),
        in_specs=in_specs,
        out_specs=out_specs,
        out_shape=out_shape,
        scratch_shapes=scratch + extra_scratch,
        compiler_params=_params(("arbitrary",)),
        name=name,
    )(*args)


def _mla_mlp(o_t, w_o, w1, w2, layer, x2d, mod_l, nmlp, nnext, mod_next, seq):
    tm = MIX_ROWS
    tpb = seq // tm
    c = o_t.shape[2]
    assert o_t.shape[3] == tm
    lead = [pl.BlockSpec((1, 1, c, tm), lambda i: (i // tpb, i % tpb, 0, 0))]
    return _mix_mlp_call(functools.partial(_mla_mlp_kernel, layer=layer), "mla_mlp", lead, [o_t], w_o, w1, w2, x2d,
                         mod_l, nmlp, seq, (nnext, mod_next), [])


def _dil_mlp(outs, stats, w_o, w1, w2, layer, x2d, mod_l, nmlp, seq):
    tm = MIX_ROWS
    tpb = seq // tm
    c = w_o.shape[0]
    expand = jnp.repeat(jnp.eye(DIL_HEADS, dtype=F32), DIL_HEAD_DIM, axis=1)
    expand = jnp.pad(expand, ((0, LANES - DIL_HEADS), (0, 0))).astype(BF16)
    grp = lambda g, w: pl.BlockSpec((1, tm // DIL_GROUPS[g][1], DIL_GROUPS[g][1] * w),
                                    lambda i: (i // tpb, i % tpb, 0))
    lead = [grp(0, c), grp(1, c), grp(2, c), grp(0, LANES), grp(1, LANES), grp(2, LANES),
            pl.BlockSpec((LANES, c), lambda i: (0, 0))]
    extra = []
    for g in (1, 2):
        extra += [pltpu.VMEM((c // LANES, tm, LANES), F32), pltpu.VMEM((tm, LANES), F32)]
    (x2,) = _mix_mlp_call(functools.partial(_dil_mlp_kernel, layer=layer), "dil_mlp", lead, [*outs, *stats, expand],
                          w_o, w1, w2, x2d, mod_l, nmlp, seq, None, extra)
    return x2


DIL_TQ = 128
DIL_HALF = 64
DIL_ATTN_POSITIONS = 2048
assert all(w // (2 * d) == DIL_HALF for w, d in DIL_GROUPS)


def _dil_geometry(seq, dilation):
    length = seq // dilation
    kwin = min(2 * DIL_TQ, length)
    return length, kwin, length // DIL_TQ


def _t5_log_thresholds():
    nb = N_BUCKETS // 2
    max_exact = nb // 2
    steps = nb - max_exact
    thresholds = []
    for k in range(1, steps):
        n = max_exact
        while math.floor(math.log(n / max_exact) / math.log(MAX_DISTANCE / max_exact) * steps) < k:
            n += 1
        thresholds.append(n)
    return thresholds


def _t5_bucket(rel):
    nb = N_BUCKETS // 2
    max_exact = nb // 2
    n = jnp.abs(rel)
    large = max_exact
    for thr in _t5_log_thresholds():
        large = large + jnp.where(n >= thr, 1, 0)
    return jnp.where(rel > 0, nb, 0) + jnp.where(n < max_exact, n, large)


BIAS_ROWS = 16


def _dil_bias_kernel(tab_ref, o_ref, *, group, dilation, kwin, n_var):
    def rows(step, carry):
        r0 = pl.multiple_of(step * BIAS_ROWS, BIAS_ROWS)
        kk = lax.broadcasted_iota(jnp.int32, (BIAS_ROWS, DIL_TQ), 0) + r0
        qi = lax.broadcasted_iota(jnp.int32, (BIAS_ROWS, DIL_TQ), 1)
        rel_a = kk - qi - (DIL_HALF if n_var > 1 else 0)
        valid = jnp.abs(rel_a) <= DIL_HALF
        same_end = (qi < DIL_HALF) == (kk < DIL_TQ)
        bucket = _t5_bucket(rel_a * dilation)
        accs = [jnp.zeros((BIAS_ROWS, DIL_TQ), F32)] * DIL_HEADS
        for b in range(N_BUCKETS):
            hit = bucket == b
            accs = [jnp.where(hit, tab_ref[b, group * DIL_HEADS + h], a) for h, a in enumerate(accs)]
        for h, a in enumerate(accs):
            where = (h // 2, pl.ds(r0, BIAS_ROWS), slice((h % 2) * DIL_TQ, (h % 2 + 1) * DIL_TQ))
            tile = jnp.where(valid, a * LOG2E, MASKED)
            o_ref[(0, *where)] = tile
            if n_var > 1:
                o_ref[(1, *where)] = jnp.where(same_end, tile, MASKED)
        return carry

    lax.fori_loop(0, kwin // BIAS_ROWS, rows, 0)


def _dil_bias(rel_bias, group, seq):
    _, dilation = DIL_GROUPS[group]
    _, kwin, n_tiles = _dil_geometry(seq, dilation)
    n_var = 1 if n_tiles == 1 else 2
    n_pairs = DIL_HEADS // 2
    kern = functools.partial(_dil_bias_kernel, group=group, dilation=dilation, kwin=kwin, n_var=n_var)
    return pl.pallas_call(
        kern,
        grid=(1,),
        in_specs=[pl.BlockSpec(memory_space=pltpu.SMEM)],
        out_specs=pl.BlockSpec((n_var, n_pairs, kwin, 2 * DIL_TQ), lambda v: (0, 0, 0, 0)),
        out_shape=jax.ShapeDtypeStruct((n_var, n_pairs, kwin, 2 * DIL_TQ), F32),
        compiler_params=_params(("arbitrary",)),
        name=f"dil_bias{group}",
    )(rel_bias)


DIL_PROJ_ROWS = 512


def _dil_proj_kernel(h_ref, wqf_ref, wkf_ref, wvf_ref, gq_ref, qT_ref, k_ref, vT_ref, wq_ref, wk_ref, wv_ref,
                     *scratch, dilation, rb, ra):
    hd = DIL_HEAD_DIM
    d_model = h_ref.shape[2]
    n_slabs = d_model // LANES
    mt = rb * ra

    @pl.when((pl.program_id(0) == 0) & (pl.program_id(1) == 0) & (pl.program_id(2) == 0))
    def _():
        wq_ref[...] = wqf_ref[0].T.astype(BF16)
        wk_ref[...] = wkf_ref[0].astype(BF16)
        wv_ref[...] = wvf_ref[0].T.astype(BF16)

    if dilation == 1:
        hcat = h_ref[0]
    else:
        (scr,) = scratch
        sub = pl.program_id(2)

        @pl.when(sub == 0)
        def _():
            for s in range(n_slabs):
                scr[s] = h_ref[0, :, s * LANES:(s + 1) * LANES].astype(F32)

        rows = []
        for j in range(rb):
            res = sub * rb + j
            rows.append(jnp.concatenate(
                [scr[s, pl.ds(res, ra, stride=dilation), :] for s in range(n_slabs)], axis=1))
        hcat = jnp.concatenate(rows, axis=0).astype(BF16)

    q_t = lax.dot_general(wq_ref[...], hcat, NT_DIMS, preferred_element_type=F32)
    gq = jnp.concatenate([gq_ref[...]] * (mt // LANES), axis=1)
    for h in range(DIL_HEADS):
        blk = q_t[h * hd:(h + 1) * hd]
        r = lax.rsqrt(jnp.sum(blk * blk, axis=0, keepdims=True) * (1.0 / hd) + EPS)
        qb = (blk * r * gq).astype(BF16)
        for j in range(rb):
            for s in range(ra // LANES):
                lo = j * ra + s * LANES
                qT_ref[0, j, s, h * hd:(h + 1) * hd, :] = qb[:, lo:lo + LANES]

    kf = jnp.dot(hcat, wk_ref[...], preferred_element_type=F32)
    lane = lax.broadcasted_iota(jnp.int32, (mt, LANES), 1)
    low = lane < hd
    for c in range(DIL_HEADS * hd // LANES):
        y = kf[:, c * LANES:(c + 1) * LANES]
        y2 = y * y
        s_lo = jnp.sum(jnp.where(low, y2, 0.0), axis=1, keepdims=True)
        s_hi = jnp.sum(jnp.where(low, 0.0, y2), axis=1, keepdims=True)
        r = lax.rsqrt(jnp.where(low, s_lo, s_hi) * (1.0 / hd) + EPS)
        kb = (y * r).astype(BF16)
        for j in range(rb):
            k_ref[0, j, :, c * LANES:(c + 1) * LANES] = kb[j * ra:(j + 1) * ra]

    v_t = lax.dot_general(wv_ref[...], hcat, NT_DIMS, preferred_element_type=F32).astype(BF16)
    for j in range(rb):
        for s in range(ra // LANES):
            lo = j * ra + s * LANES
            vT_ref[0, j, s] = v_t[:, lo:lo + LANES]


def _dil_proj(hn3, w_in, g_q, g_k, group):
    batch, seq, d_model = hn3.shape
    _, dilation = DIL_GROUPS[group]
    length = seq // dilation
    c = DIL_HEADS * DIL_HEAD_DIM
    rb = min(dilation, 4)
    rows = 1024 if dilation <= rb else DIL_PROJ_ROWS
    ra = rows // rb
    tok = ra * dilation
    n_sub = dilation // rb
    gq_col = jnp.broadcast_to((g_q * g_k * (DIL_HEAD_DIM ** -0.5 * LOG2E))[:, None], (DIL_HEAD_DIM, LANES))
    kern = functools.partial(_dil_proj_kernel, dilation=dilation, rb=rb, ra=ra)
    const = lambda shape: pl.BlockSpec(shape, lambda b, t, s: (0,) * len(shape))
    w_blk = lambda j: pl.BlockSpec((1, d_model, c), lambda b, t, s: (0, 0, 3 * group + j))
    scratch = [pltpu.VMEM((c, d_model), BF16), pltpu.VMEM((d_model, c), BF16), pltpu.VMEM((c, d_model), BF16)]
    if dilation > 1:
        scratch.append(pltpu.VMEM((d_model // LANES, tok, LANES), F32))
    return pl.pallas_call(
        kern,
        grid=(batch, seq // tok, n_sub),
        in_specs=[
            pl.BlockSpec((1, tok, d_model), lambda b, t, s: (b, t, 0)),
            w_blk(0), w_blk(1), w_blk(2),
            const((DIL_HEAD_DIM, LANES)),
        ],
        out_specs=[
            pl.BlockSpec((1, rb, ra // LANES, c, LANES), lambda b, t, s: (b, s, t, 0, 0)),
            pl.BlockSpec((1, rb, ra, c), lambda b, t, s: (b, s, t, 0)),
            pl.BlockSpec((1, rb, ra // LANES, c, LANES), lambda b, t, s: (b, s, t, 0, 0)),
        ],
        out_shape=[
            jax.ShapeDtypeStruct((batch, dilation, length // LANES, c, LANES), BF16),
            jax.ShapeDtypeStruct((batch, dilation, length, c), BF16),
            jax.ShapeDtypeStruct((batch, dilation, length // LANES, c, LANES), BF16),
        ],
        scratch_shapes=scratch,
        compiler_params=_params(("arbitrary", "arbitrary", "arbitrary")),
        name=f"dil_proj{group}",
    )(hn3, w_in, w_in, w_in, gq_col)


def _dil_attn_kernel(safe_ref, qT_ref, k_ref, vT_ref, bm_ref, o_ref, st_ref, s_ref, *, n_tiles, rb):
    tq = DIL_TQ
    hd = DIL_HEAD_DIM
    n_pairs = DIL_HEADS // 2
    c = DIL_HEADS * hd
    shifted = n_tiles > 1
    low = lax.broadcasted_iota(jnp.int32, (LANES, tq), 0) < hd
    ones = jnp.ones((ONES_ROWS, 2 * tq if shifted else tq), BF16)
    prow = lambda p: slice(p * LANES, (p + 1) * LANES)

    def block_diag(q2):
        zero = jnp.zeros_like(q2)
        return jnp.concatenate([jnp.where(low, q2, zero), jnp.where(low, zero, q2)], axis=1)

    def attend(s_t, v2, use_max):
        if use_max:
            m = jnp.max(s_t, axis=0, keepdims=True)
            pb = jnp.exp2(s_t - m).astype(BF16)
        else:
            pb = jnp.exp2(s_t).astype(BF16)
        res = jnp.dot(jnp.concatenate([v2, ones], axis=0), pb, preferred_element_type=F32)
        den = res[LANES:LANES + 1]
        rinv = 1.0 / den
        o_pair = jnp.concatenate([res[:hd, :tq] * rinv[:, :tq], res[hd:LANES, tq:] * rinv[:, tq:]], axis=0)
        lse = jnp.log2(den) + m if use_max else jnp.log2(den)
        lse = lse * LN2
        return o_pair.T.astype(BF16), [lse[:, :tq], lse[:, tq:]]

    def stats_rows(stats):
        return jnp.concatenate(stats + [jnp.zeros((LANES - DIL_HEADS, tq), F32)], axis=0).T

    tiles = [(j, t) for j in range(rb) for t in range(n_tiles)]

    @pl.when(safe_ref[0] == 1)
    def _():
        def score_tile(i):
            j, t = tiles[i]
            t1 = (t + 1) % n_tiles
            for p in range(n_pairs):
                if shifted:
                    q2 = jnp.concatenate(
                        [qT_ref[0, j, t, prow(p), DIL_HALF:], qT_ref[0, j, t1, prow(p), :DIL_HALF]], axis=1)
                    k2 = jnp.concatenate([k_ref[0, j, t * tq:(t + 1) * tq, prow(p)],
                                          k_ref[0, j, t1 * tq:(t1 + 1) * tq, prow(p)]], axis=0)
                else:
                    q2 = qT_ref[0, j, 0, prow(p), :]
                    k2 = k_ref[0, j, :, prow(p)]
                s_ref[i % 2, p] = jnp.dot(k2, block_diag(q2), preferred_element_type=F32)

        def attend_tile(i):
            j, t = tiles[i]
            t1 = (t + 1) % n_tiles
            var = 1 if shifted and t + 1 == n_tiles else 0

            def store(ref, lanes, val):
                if shifted:
                    ref[0, t * tq + DIL_HALF:(t + 1) * tq, lanes] = val[:DIL_HALF]
                    ref[0, t1 * tq:t1 * tq + DIL_HALF, lanes] = val[DIL_HALF:]
                else:
                    ref[0, :, lanes] = val

            stats = []
            for p in range(n_pairs):
                if shifted:
                    v2 = jnp.concatenate([vT_ref[0, j, t, prow(p), :], vT_ref[0, j, t1, prow(p), :]], axis=1)
                else:
                    v2 = vT_ref[0, j, 0, prow(p), :]
                o_rows, lses = attend(s_ref[i % 2, p] + bm_ref[var, p], v2, use_max=False)
                store(o_ref, slice(j * c + p * LANES, j * c + (p + 1) * LANES), o_rows)
                stats += lses
            store(st_ref, slice(j * LANES, (j + 1) * LANES), stats_rows(stats))

        score_tile(0)
        for i in range(len(tiles)):
            if i + 1 < len(tiles):
                score_tile(i + 1)
            attend_tile(i)

    @pl.when(safe_ref[0] != 1)
    def _():
        def tile(j, t):
            if shifted:
                last = t + 1 == n_tiles
                t1 = jnp.where(last, 0, t + 1)
                var = jnp.where(last, 1, 0)
                r0 = pl.multiple_of(t * tq, tq)
                r1 = pl.multiple_of(t1 * tq, tq)

            def store(ref, lanes, val):
                if shifted:
                    ref[0, pl.ds(r0 + DIL_HALF, DIL_HALF), lanes] = val[:DIL_HALF]
                    ref[0, pl.ds(r1, DIL_HALF), lanes] = val[DIL_HALF:]
                else:
                    ref[0, :, lanes] = val

            stats = []
            for p in range(n_pairs):
                if shifted:
                    q2 = jnp.concatenate(
                        [qT_ref[0, j, t, prow(p), DIL_HALF:], qT_ref[0, j, t1, prow(p), :DIL_HALF]], axis=1)
                    k2 = jnp.concatenate(
                        [k_ref[0, j, pl.ds(r0, tq), prow(p)], k_ref[0, j, pl.ds(r1, tq), prow(p)]], axis=0)
                    v2 = jnp.concatenate([vT_ref[0, j, t, prow(p), :], vT_ref[0, j, t1, prow(p), :]], axis=1)
                    bias = bm_ref[var, p]
                else:
                    q2 = qT_ref[0, j, 0, prow(p), :]
                    k2 = k_ref[0, j, :, prow(p)]
                    v2 = vT_ref[0, j, 0, prow(p), :]
                    bias = bm_ref[0, p]
                s_t = jnp.dot(k2, block_diag(q2), preferred_element_type=F32) + bias
                o_rows, lses = attend(s_t, v2, use_max=True)
                store(o_ref, slice(j * c + p * LANES, j * c + (p + 1) * LANES), o_rows)
                stats += lses
            store(st_ref, slice(j * LANES, (j + 1) * LANES), stats_rows(stats))

        for j in range(rb):
            if shifted:
                lax.fori_loop(0, n_tiles, lambda t, carry, j=j: (tile(j, t), carry)[1], 0)
            else:
                tile(j, 0)


def _dil_attn(q_t, k, v_t, bias, safe, group, seq):
    batch = q_t.shape[0]
    _, dilation = DIL_GROUPS[group]
    length, kwin, n_tiles = _dil_geometry(seq, dilation)
    c = DIL_HEADS * DIL_HEAD_DIM
    n_slabs = length // LANES
    rb = min(dilation, DIL_ATTN_POSITIONS // length, 8)
    kern = functools.partial(_dil_attn_kernel, n_tiles=n_tiles, rb=rb)
    return pl.pallas_call(
        kern,
        grid=(batch, dilation // rb),
        in_specs=[
            pl.BlockSpec(memory_space=pltpu.SMEM),
            pl.BlockSpec((1, rb, n_slabs, c, LANES), lambda b, r: (b, r, 0, 0, 0)),
            pl.BlockSpec((1, rb, length, c), lambda b, r: (b, r, 0, 0)),
            pl.BlockSpec((1, rb, n_slabs, c, LANES), lambda b, r: (b, r, 0, 0, 0)),
            pl.BlockSpec(bias.shape, lambda b, r: (0, 0, 0, 0)),
        ],
        out_specs=[
            pl.BlockSpec((1, length, rb * c), lambda b, r: (b, 0, r)),
            pl.BlockSpec((1, length, rb * LANES), lambda b, r: (b, 0, r)),
        ],
        out_shape=[
            jax.ShapeDtypeStruct((batch, length, dilation * c), BF16),
            jax.ShapeDtypeStruct((batch, length, dilation * LANES), F32),
        ],
        scratch_shapes=[pltpu.VMEM((2, DIL_HEADS // 2, kwin, 2 * DIL_TQ), F32)],
        compiler_params=_params(("arbitrary", "arbitrary")),
        name=f"dil_attn{group}",
    )(safe, q_t, k, v_t, bias)


def kernel(x, c, positions, ada_w, ada_b, norm_mix, norm_mlp, mlp_w1, mlp_w2, mla_w_in, mla_g_qa, mla_w_qb,
           mla_g_kva, mla_w_kvb, mla_g_q, mla_g_k, mla_w_o, dil_w_in, dil_g_q, dil_g_k, dil_w_o, rel_bias):
    batch, seq, d = x.shape
    depth = ada_w.shape[0]
    assert depth == 2 and len(DIL_GROUPS) == 3
    mod = _mod(c, ada_w, ada_b).reshape(depth, batch, N_MOD, d)
    x2d = x.reshape(batch * seq, d)

    n_heads = mla_w_qb.shape[2] // mla_g_q.shape[1]
    q_t, k_pad, v_t = _mla_proj(x2d, mod[0], norm_mix[0], positions, mla_w_in[0], mla_g_qa[0], mla_w_qb[0],
                                mla_g_kva[0], mla_w_kvb[0], mla_g_q[0], mla_g_k[0], batch, seq)
    o_t = _mla_attn(q_t, k_pad, v_t, _score_bound_is_safe(mla_g_q[0] * mla_g_k[0], mla_g_q.shape[1]), n_heads)
    x2, hn = _mla_mlp(o_t, mla_w_o[0], mlp_w1, mlp_w2, 0, x2d, mod[0], norm_mlp[0], norm_mix[1], mod[1], seq)

    outs, stats = [], []
    hn3 = hn.reshape(batch, seq, d)
    for g in range(len(DIL_GROUPS)):
        q_g, k_g, v_g = _dil_proj(hn3, dil_w_in, dil_g_q[0, g], dil_g_k[0, g], g)
        safe = _score_bound_is_safe(dil_g_q[0, g] * dil_g_k[0, g], DIL_HEAD_DIM,
                                    rel_bias[:, g * DIL_HEADS:(g + 1) * DIL_HEADS])
        o_g, st_g = _dil_attn(q_g, k_g, v_g, _dil_bias(rel_bias, g, seq), safe, g, seq)
        outs.append(o_g)
        stats.append(st_g)
    x4 = _dil_mlp(outs, stats, dil_w_o[0], mlp_w1, mlp_w2, 1, x2, mod[1], norm_mlp[1], seq)
    return x4.reshape(batch, seq, d)
```

```python
import functools
import math

import jax
import jax.numpy as jnp
from jax import lax
from jax.experimental import pallas as pl
from jax.experimental.pallas import tpu as pltpu

F32 = jnp.float32
BF16 = jnp.bfloat16

EPS = 1e-6
LOG2E = 1.4426950408889634
LN2 = 0.6931471805599453
MASKED = -1e30

N_MOD = 6
ROPE_THETA = 10000.0
MLA_ROPE = 32
DIL_GROUPS = ((128, 1), (512, 4), (2048, 16))
DIL_HEADS = 16
DIL_HEAD_DIM = 64
N_BUCKETS = 32
MAX_DISTANCE = 1024

LANES = 128
HEAD_PAD = 128

VMEM_LIMIT = 56 * 1024 * 1024

NT_DIMS = (((1,), (1,)), ((), ()))
TN_DIMS = (((0,), (0,)), ((), ()))


def _params(sem):
    return pltpu.CompilerParams(dimension_semantics=sem, vmem_limit_bytes=VMEM_LIMIT)


def _ada_norm(x, g, scale, shift):
    ms = jnp.mean(x * x, axis=-1, keepdims=True)
    return (x * lax.rsqrt(ms + EPS) * g) * (1.0 + scale) + shift


def _rms(x, g):
    ms = jnp.mean(x * x, axis=-1, keepdims=True)
    return x * lax.rsqrt(ms + EPS) * g


def _mod_kernel(c_ref, w_ref, b_ref, o_ref):
    c = c_ref[...]
    cond = c / (1.0 + jnp.exp(-c))
    o_ref[0] = (
        jnp.dot(cond.astype(BF16), w_ref[0].astype(BF16), preferred_element_type=F32) + b_ref[0]
    )


def _mod(c, ada_w, ada_b):
    depth, d, n = ada_w.shape
    b = c.shape[0]
    tn = 1536
    return pl.pallas_call(
        _mod_kernel,
        grid=(depth, n // tn),
        in_specs=[
            pl.BlockSpec((b, d), lambda l, j: (0, 0)),
            pl.BlockSpec((1, d, tn), lambda l, j: (l, 0, j)),
            pl.BlockSpec((1, 1, tn), lambda l, j: (l, 0, j)),
        ],
        out_specs=pl.BlockSpec((1, b, tn), lambda l, j: (l, 0, j)),
        out_shape=jax.ShapeDtypeStruct((depth, b, n), F32),
        compiler_params=_params(("arbitrary", "arbitrary")),
        name="mod",
    )(c, ada_w, ada_b.reshape(depth, 1, n))


def _mla_proj_kernel(x_ref, mod_ref, nmix_ref, pos_ref, inv_ref, wlat_ref, wkr_ref, gqa_ref, gkva_ref,
                     wqb_ref, wkn_ref, wv_ref, gq_ref, qT_ref, k_ref, vT_ref, *, n_heads, q_lora, qk_dim):
    x = x_ref[...]
    tm = x.shape[0]
    m = mod_ref[0]
    hb = _ada_norm(x, nmix_ref[...], m[1:2], m[0:1]).astype(BF16)
    lat = jnp.dot(hb, wlat_ref[...], preferred_element_type=F32)
    qn = _rms(lat[:, :q_lora], gqa_ref[...]).astype(BF16)
    kvn = _rms(lat[:, q_lora:], gkva_ref[...]).astype(BF16)

    half = MLA_ROPE // 2
    nope = qk_dim - MLA_ROPE
    pos = pos_ref[...].astype(F32)
    ang = jnp.concatenate([inv_ref[...]] * (tm // LANES), axis=1) * pos
    cos_t = jnp.cos(ang)
    sin_t = jnp.sin(ang)

    def rope_rows(blk):
        x1 = blk[nope:nope + half]
        x2 = blk[nope + half:qk_dim]
        return jnp.concatenate(
            [blk[:nope], x1 * cos_t - x2 * sin_t, x2 * cos_t + x1 * sin_t, blk[qk_dim:]], axis=0)

    kr_t = lax.dot_general(wkr_ref[...], hb, NT_DIMS, preferred_element_type=F32)
    kr = rope_rows(kr_t).T

    q_t = lax.dot_general(wqb_ref[...], qn, NT_DIMS, preferred_element_type=F32)
    gq = jnp.concatenate([gq_ref[...]] * (tm // LANES), axis=1)
    for h in range(n_heads):
        blk = rope_rows(q_t[h * HEAD_PAD:(h + 1) * HEAD_PAD])
        ssq = jnp.sum(blk * blk, axis=0, keepdims=True)
        r = lax.rsqrt(ssq * (1.0 / qk_dim) + EPS)
        qT_ref[0, 0, h * HEAD_PAD:(h + 1) * HEAD_PAD, :] = (blk * r * gq).astype(BF16)

    kn = jnp.dot(kvn, wkn_ref[...], preferred_element_type=F32)
    for h in range(n_heads):
        kb = kn[:, h * HEAD_PAD:(h + 1) * HEAD_PAD] + kr
        ssq = jnp.sum(kb * kb, axis=1, keepdims=True)
        r = lax.rsqrt(ssq * (1.0 / qk_dim) + EPS)
        k_ref[:, h * HEAD_PAD:(h + 1) * HEAD_PAD] = (kb * r).astype(BF16)

    vT_ref[0] = lax.dot_general(wv_ref[...], kvn, NT_DIMS, preferred_element_type=F32).astype(BF16)


def _mla_proj(x2d, mod_l, nmix, positions, w_in, g_qa, w_qb, g_kva, w_kvb, g_q, g_k, batch, seq):
    t, d = x2d.shape
    q_lora = g_qa.shape[0]
    kv_lora = g_kva.shape[0]
    qk_dim = g_q.shape[0]
    n_heads = w_qb.shape[1] // qk_dim
    nope = qk_dim - MLA_ROPE
    v_dim = w_kvb.shape[1] // n_heads - nope
    assert nope + v_dim == HEAD_PAD and qk_dim <= HEAD_PAD
    half = MLA_ROPE // 2
    tm = 512
    tpb = seq // tm

    w_lat = w_in[:, :q_lora + kv_lora].astype(BF16)
    w_kr = jnp.zeros((HEAD_PAD, d), F32).at[nope:qk_dim].set(w_in[:, q_lora + kv_lora:].T).astype(BF16)
    w_qb_t = jnp.pad(w_qb.T.reshape(n_heads, qk_dim, q_lora), ((0, 0), (0, HEAD_PAD - qk_dim), (0, 0)))
    w_qb_t = w_qb_t.reshape(n_heads * HEAD_PAD, q_lora).astype(BF16)
    w_kvb3 = w_kvb.reshape(kv_lora, n_heads, nope + v_dim)
    w_kn = jnp.where(jnp.arange(nope + v_dim) < nope, w_kvb3, 0.0).reshape(kv_lora, n_heads * HEAD_PAD).astype(BF16)
    w_v_t = w_kvb3[:, :, nope:].reshape(kv_lora, n_heads * v_dim).T.astype(BF16)
    q_gain = jnp.pad(g_q * g_k * (qk_dim ** -0.5 * LOG2E), (0, HEAD_PAD - qk_dim))
    gq_col = jnp.broadcast_to(q_gain[:, None], (HEAD_PAD, LANES))
    inv = 1.0 / (ROPE_THETA ** (jnp.arange(half, dtype=F32) / half))
    inv_tab = jnp.broadcast_to(inv[:, None], (half, LANES))
    pos_row = positions.reshape(1, t)

    full = lambda shape: pl.BlockSpec(shape, lambda i: (0,) * len(shape))
    kern = functools.partial(_mla_proj_kernel, n_heads=n_heads, q_lora=q_lora, qk_dim=qk_dim)
    return pl.pallas_call(
        kern,
        grid=(t // tm,),
        in_specs=[
            pl.BlockSpec((tm, d), lambda i: (i, 0)),
            pl.BlockSpec((1, N_MOD, d), lambda i: (i // tpb, 0, 0)),
            full((1, d)),
            pl.BlockSpec((1, tm), lambda i: (0, i)),
            full((half, LANES)),
            full(w_lat.shape), full(w_kr.shape), full((1, q_lora)), full((1, kv_lora)),
            full(w_qb_t.shape), full(w_kn.shape), full(w_v_t.shape),
            full((HEAD_PAD, LANES)),
        ],
        out_specs=[
            pl.BlockSpec((1, 1, n_heads * HEAD_PAD, tm), lambda i: (i // tpb, i % tpb, 0, 0)),
            pl.BlockSpec((tm, n_heads * HEAD_PAD), lambda i: (i, 0)),
            pl.BlockSpec((1, n_heads * v_dim, tm), lambda i: (i // tpb, 0, i % tpb)),
        ],
        out_shape=[
            jax.ShapeDtypeStruct((batch, tpb, n_heads * HEAD_PAD, tm), BF16),
            jax.ShapeDtypeStruct((t, n_heads * HEAD_PAD), BF16),
            jax.ShapeDtypeStruct((batch, n_heads * v_dim, seq), BF16),
        ],
        compiler_params=_params(("arbitrary",)),
        name="mla_proj",
    )(x2d, mod_l, nmix[None, :], pos_row, inv_tab, w_lat, w_kr, g_qa[None, :], g_kva[None, :],
      w_qb_t, w_kn, w_v_t, gq_col)


ONES_ROWS = 16


SAFE_EXPONENT = 60.0


def _score_bound_is_safe(gain_product, head_dim, bias=None):
    bound = 1.02 * math.sqrt(head_dim) * LOG2E * jnp.max(jnp.abs(gain_product))
    if bias is not None:
        bound = bound + LOG2E * jnp.max(jnp.abs(bias))
    return (bound <= SAFE_EXPONENT).astype(jnp.int32).reshape(1)


def _mla_attn_kernel(safe_ref, qT_ref, k_ref, vT_ref, oT_ref, s_ref, *, hb, tn, kc):
    seq = k_ref.shape[1]
    n_tiles, _, tq = qT_ref.shape[1:]
    v_dim = vT_ref.shape[1] // hb
    ones = jnp.ones((ONES_ROWS, seq), BF16)
    n_chunks = seq // kc
    n_cols = tq // tn
    hrows = lambda h: slice(h * HEAD_PAD, (h + 1) * HEAD_PAD)
    vrows = lambda h: slice(h * v_dim, (h + 1) * v_dim)
    v_ext = lambda h: jnp.concatenate([vT_ref[0, vrows(h), :], ones], axis=0)

    @pl.when(safe_ref[0] == 1)
    def _():
        tiles = [(h, i) for h in range(hb) for i in range(n_tiles)]

        def scores(t, c):
            h, i = tiles[t]
            s_ref[t % 2, c * kc:(c + 1) * kc, :] = jnp.dot(
                k_ref[0, c * kc:(c + 1) * kc, hrows(h)], qT_ref[0, i, hrows(h), :], preferred_element_type=F32)

        for c in range(n_chunks):
            scores(0, c)
        for t, (h, i) in enumerate(tiles):
            ve = v_ext(h)
            for j in range(n_cols):
                acc = jnp.zeros((v_dim + ONES_ROWS, tn), F32)
                for c in range(n_chunks):
                    step = j * n_chunks + c
                    if t + 1 < len(tiles) and step % n_cols == 0:
                        scores(t + 1, step // n_cols)
                    p = jnp.exp2(s_ref[t % 2, c * kc:(c + 1) * kc, j * tn:(j + 1) * tn]).astype(BF16)
                    acc = acc + jnp.dot(ve[:, c * kc:(c + 1) * kc], p, preferred_element_type=F32)
                o = acc[:v_dim] * (1.0 / acc[v_dim:v_dim + 1])
                oT_ref[0, i, vrows(h), j * tn:(j + 1) * tn] = o.astype(BF16)

    @pl.when(safe_ref[0] != 1)
    def _():
        for h in range(hb):
            ve = v_ext(h)
            k_h = k_ref[0, :, hrows(h)]

            def tile(i, carry):
                s_t = jnp.dot(k_h, qT_ref[0, i, hrows(h), :], preferred_element_type=F32)
                p = jnp.exp2(s_t - jnp.max(s_t, axis=0, keepdims=True)).astype(BF16)
                r = jnp.dot(ve, p, preferred_element_type=F32)
                oT_ref[0, i, vrows(h), :] = (r[:v_dim] * (1.0 / r[v_dim:v_dim + 1])).astype(BF16)
                return carry

            lax.fori_loop(0, n_tiles, tile, 0)


def _mla_attn(q_t, k_pad, v_t, safe, n_heads):
    batch, n_tiles, _, tq = q_t.shape
    seq = n_tiles * tq
    v_dim = v_t.shape[1] // n_heads
    k3 = k_pad.reshape(batch, seq, n_heads * HEAD_PAD)
    hb = 4
    return pl.pallas_call(
        functools.partial(_mla_attn_kernel, hb=hb, tn=256, kc=256),
        grid=(batch, n_heads // hb),
        in_specs=[
            pl.BlockSpec(memory_space=pltpu.SMEM),
            pl.BlockSpec((1, n_tiles, hb * HEAD_PAD, tq), lambda b, h: (b, 0, h, 0)),
            pl.BlockSpec((1, seq, hb * HEAD_PAD), lambda b, h: (b, 0, h)),
            pl.BlockSpec((1, hb * v_dim, seq), lambda b, h: (b, h, 0)),
        ],
        out_specs=pl.BlockSpec((1, n_tiles, hb * v_dim, tq), lambda b, h: (b, 0, h, 0)),
        out_shape=jax.ShapeDtypeStruct((batch, n_tiles, n_heads * v_dim, tq), BF16),
        scratch_shapes=[pltpu.VMEM((2, seq, tq), F32)],
        compiler_params=_params(("arbitrary", "arbitrary")),
        name="mla_attn",
    )(safe, q_t, k3, v_t)


MIX_ROWS = 512
MIX_SUB = 256
FF_CHUNK = 2048
LOAD_SLOTS = 4


def _load_bf16(src, dst, stage, sem):
    slots, chunk = stage.shape[:2]
    n = dst.shape[0] // chunk

    def copy(i):
        return pltpu.make_async_copy(src.at[pl.ds(i * chunk, chunk), :], stage.at[i % slots], sem.at[i % slots])

    for i in range(min(slots - 1, n)):
        copy(i).start()
    for i in range(n):
        if i + slots - 1 < n:
            copy(i + slots - 1).start()
        copy(i).wait()
        dst[i * chunk:(i + 1) * chunk, :] = stage[i % slots].astype(BF16)


def _mlp_rows(h2, w1_ref, w2_ref):
    z = None
    for f in range(w1_ref.shape[1] // FF_CHUNK):
        cols = slice(f * FF_CHUNK, (f + 1) * FF_CHUNK)
        u = jnp.dot(h2, w1_ref[:, cols], preferred_element_type=F32)
        u = jnp.square(jnp.maximum(u, 0.0)).astype(BF16)
        part = jnp.dot(u, w2_ref[cols, :], preferred_element_type=F32)
        z = part if z is None else z + part
    return z


def _mix_mlp_body(mixer_out, x_ref, mod_ref, nmlp_ref, next_refs, x2_ref, hn_ref, wo_ref, w1_ref, w2_ref):
    m = mod_ref[0]
    tm = x_ref.shape[0]
    blocks = [slice(i * MIX_SUB, (i + 1) * MIX_SUB) for i in range(tm // MIX_SUB)]

    def mix(rows):
        x1 = x_ref[rows, :] + m[2:3] * mixer_out(rows, wo_ref)
        return x1, _ada_norm(x1, nmlp_ref[...], m[4:5], m[3:4]).astype(BF16)

    cur = mix(blocks[0])
    for i, rows in enumerate(blocks):
        nxt = mix(blocks[i + 1]) if i + 1 < len(blocks) else None
        x1, h2 = cur
        x2 = x1 + m[5:6] * _mlp_rows(h2, w1_ref, w2_ref)
        x2_ref[rows, :] = x2
        if next_refs is not None:
            nnext_ref, modn_ref = next_refs
            mn = modn_ref[0]
            hn_ref[rows, :] = _ada_norm(x2, nnext_ref[...], mn[1:2], mn[0:1]).astype(BF16)
        cur = nxt


def _load_mix_weights(wo_hbm, w1_hbm, w2_hbm, layer, wo_ref, w1_ref, w2_ref, stage_w, stage_n, sem):
    _load_bf16(wo_hbm, wo_ref, stage_n, sem)
    _load_bf16(w1_hbm.at[layer], w1_ref, stage_w, sem)
    _load_bf16(w2_hbm.at[layer], w2_ref, stage_n, sem)


def _mla_mlp_kernel(oT_ref, x_ref, mod_ref, nmlp_ref, nnext_ref, modn_ref, wo_hbm, w1_hbm, w2_hbm, x2_ref, hn_ref,
                    wo_ref, w1_ref, w2_ref, stage_w, stage_n, sem, *, layer):
    @pl.when(pl.program_id(0) == 0)
    def _():
        _load_mix_weights(wo_hbm, w1_hbm, w2_hbm, layer, wo_ref, w1_ref, w2_ref, stage_w, stage_n, sem)

    def mixer_out(rows, wo):
        return lax.dot_general(oT_ref[0, 0, :, rows], wo[...], TN_DIMS, preferred_element_type=F32)

    _mix_mlp_body(mixer_out, x_ref, mod_ref, nmlp_ref, (nnext_ref, modn_ref), x2_ref, hn_ref, wo_ref, w1_ref, w2_ref)


def _dil_mlp_kernel(o0_ref, o1_ref, o2_ref, s0_ref, s1_ref, s2_ref, e_ref, x_ref, mod_ref, nmlp_ref,
                    wo_hbm, w1_hbm, w2_hbm, x2_ref, wo_ref, w1_ref, w2_ref, stage_w, stage_n, sem,
                    oscr1, sscr1, oscr2, sscr2, *, layer):
    @pl.when(pl.program_id(0) == 0)
    def _():
        _load_mix_weights(wo_hbm, w1_hbm, w2_hbm, layer, wo_ref, w1_ref, w2_ref, stage_w, stage_n, sem)

    tm = x_ref.shape[0]
    c = wo_ref.shape[0]

    def token_major(o_ref, s_ref, oscr, sscr, dilation):
        ra = tm // dilation
        for r in range(dilation):
            for s in range(c // LANES):
                lo = r * c + s * LANES
                oscr[s, pl.ds(r, ra, stride=dilation), :] = o_ref[0, :, lo:lo + LANES].astype(F32)
            sscr[pl.ds(r, ra, stride=dilation), :] = s_ref[0, :, r * LANES:(r + 1) * LANES]

    token_major(o1_ref, s1_ref, oscr1, sscr1, DIL_GROUPS[1][1])
    token_major(o2_ref, s2_ref, oscr2, sscr2, DIL_GROUPS[2][1])
    e = e_ref[...]

    def expand(w):
        return jnp.dot(w.astype(BF16), e, preferred_element_type=F32)

    def mixer_out(rows, wo):
        gather = lambda oscr: jnp.concatenate([oscr[s, rows, :] for s in range(c // LANES)], axis=1)
        o0, l0 = o0_ref[0, rows, :].astype(F32), s0_ref[0, rows, :]
        o1, l1 = gather(oscr1), sscr1[rows, :]
        o2, l2 = gather(oscr2), sscr2[rows, :]
        mx = jnp.maximum(jnp.maximum(l0, l1), l2)
        e0, e1, e2 = jnp.exp(l0 - mx), jnp.exp(l1 - mx), jnp.exp(l2 - mx)
        rden = 1.0 / (e0 + e1 + e2)
        o = expand(e0 * rden) * o0 + expand(e1 * rden) * o1 + expand(e2 * rden) * o2
        return jnp.dot(o.astype(BF16), wo[...], preferred_element_type=F32)

    _mix_mlp_body(mixer_out, x_ref, mod_ref, nmlp_ref, None, x2_ref, None, wo_ref, w1_ref, w2_ref)


def _mix_mlp_call(kern, name, lead_specs, lead_args, w_o, w1, w2, x2d, mod_l, nmlp, seq, next_args, extra_scratch):
    t, d = x2d.shape
    ff = w1.shape[2]
    c = w_o.shape[0]
    tm = MIX_ROWS
    tpb = seq // tm
    row = pl.BlockSpec((tm, d), lambda i: (i, 0))
    mod_spec = pl.BlockSpec((1, N_MOD, d), lambda i: (i // tpb, 0, 0))
    vec = pl.BlockSpec((1, d), lambda i: (0, 0))
    hbm = pl.BlockSpec(memory_space=pl.ANY)
    in_specs = lead_specs + [row, mod_spec, vec]
    args = lead_args + [x2d, mod_l, nmlp[None, :]]
    out_specs, out_shape = [row], [jax.ShapeDtypeStruct((t, d), F32)]
    if next_args is not None:
        nnext, mod_next = next_args
        in_specs += [vec, mod_spec]
        args += [nnext[None, :], mod_next]
        out_specs.append(row)
        out_shape.append(jax.ShapeDtypeStruct((t, d), BF16))
    in_specs += [hbm, hbm, hbm]
    args += [w_o, w1, w2]
    scratch = [pltpu.VMEM((c, d), BF16), pltpu.VMEM((d, ff), BF16), pltpu.VMEM((ff, d), BF16),
               pltpu.VMEM((LOAD_SLOTS, 64, ff), F32), pltpu.VMEM((LOAD_SLOTS, 256, d), F32),
               pltpu.SemaphoreType.DMA((LOAD_SLOTS,))]
    return pl.pallas_call(
        kern,
        grid=(t // tm,),
        in_specs=in_specs,
        out_specs=out_specs,
        out_shape=out_shape,
        scratch_shapes=scratch + extra_scratch,
        compiler_params=_params(("arbitrary",)),
        name=name,
    )(*args)


def _mla_mlp(o_t, w_o, w1, w2, layer, x2d, mod_l, nmlp, nnext, mod_next, seq):
    tm = MIX_ROWS
    tpb = seq // tm
    c = o_t.shape[2]
    assert o_t.shape[3] == tm
    lead = [pl.BlockSpec((1, 1, c, tm), lambda i: (i // tpb, i % tpb, 0, 0))]
    return _mix_mlp_call(functools.partial(_mla_mlp_kernel, layer=layer), "mla_mlp", lead, [o_t], w_o, w1, w2, x2d,
                         mod_l, nmlp, seq, (nnext, mod_next), [])


def _dil_mlp(outs, stats, w_o, w1, w2, layer, x2d, mod_l, nmlp, seq):
    tm = MIX_ROWS
    tpb = seq // tm
    c = w_o.shape[0]
    expand = jnp.repeat(jnp.eye(DIL_HEADS, dtype=F32), DIL_HEAD_DIM, axis=1)
    expand = jnp.pad(expand, ((0, LANES - DIL_HEADS), (0, 0))).astype(BF16)
    grp = lambda g, w: pl.BlockSpec((1, tm // DIL_GROUPS[g][1], DIL_GROUPS[g][1] * w),
                                    lambda i: (i // tpb, i % tpb, 0))
    lead = [grp(0, c), grp(1, c), grp(2, c), grp(0, LANES), grp(1, LANES), grp(2, LANES),
            pl.BlockSpec((LANES, c), lambda i: (0, 0))]
    extra = []
    for g in (1, 2):
        extra += [pltpu.VMEM((c // LANES, tm, LANES), F32), pltpu.VMEM((tm, LANES), F32)]
    (x2,) = _mix_mlp_call(functools.partial(_dil_mlp_kernel, layer=layer), "dil_mlp", lead, [*outs, *stats, expand],
                          w_o, w1, w2, x2d, mod_l, nmlp, seq, None, extra)
    return x2


DIL_TQ = 128
DIL_HALF = 64
DIL_ATTN_POSITIONS = 2048
assert all(w // (2 * d) == DIL_HALF for w, d in DIL_GROUPS)


def _dil_geometry(seq, dilation):
    length = seq // dilation
    kwin = min(2 * DIL_TQ, length)
    return length, kwin, length // DIL_TQ


def _t5_log_thresholds():
    nb = N_BUCKETS // 2
    max_exact = nb // 2
    steps = nb - max_exact
    thresholds = []
    for k in range(1, steps):
        n = max_exact
        while math.floor(math.log(n / max_exact) / math.log(MAX_DISTANCE / max_exact) * steps) < k:
            n += 1
        thresholds.append(n)
    return thresholds


def _t5_bucket(rel):
    nb = N_BUCKETS // 2
    max_exact = nb // 2
    n = jnp.abs(rel)
    large = max_exact
    for thr in _t5_log_thresholds():
        large = large + jnp.where(n >= thr, 1, 0)
    return jnp.where(rel > 0, nb, 0) + jnp.where(n < max_exact, n, large)


BIAS_ROWS = 16


def _dil_bias_kernel(tab_ref, o_ref, *, group, dilation, kwin, n_var):
    def rows(step, carry):
        r0 = pl.multiple_of(step * BIAS_ROWS, BIAS_ROWS)
        kk = lax.broadcasted_iota(jnp.int32, (BIAS_ROWS, DIL_TQ), 0) + r0
        qi = lax.broadcasted_iota(jnp.int32, (BIAS_ROWS, DIL_TQ), 1)
        rel_a = kk - qi - (DIL_HALF if n_var > 1 else 0)
        valid = jnp.abs(rel_a) <= DIL_HALF
        same_end = (qi < DIL_HALF) == (kk < DIL_TQ)
        bucket = _t5_bucket(rel_a * dilation)
        accs = [jnp.zeros((BIAS_ROWS, DIL_TQ), F32)] * DIL_HEADS
        for b in range(N_BUCKETS):
            hit = bucket == b
            accs = [jnp.where(hit, tab_ref[b, group * DIL_HEADS + h], a) for h, a in enumerate(accs)]
        for h, a in enumerate(accs):
            where = (h // 2, pl.ds(r0, BIAS_ROWS), slice((h % 2) * DIL_TQ, (h % 2 + 1) * DIL_TQ))
            tile = jnp.where(valid, a * LOG2E, MASKED)
            o_ref[(0, *where)] = tile
            if n_var > 1:
                o_ref[(1, *where)] = jnp.where(same_end, tile, MASKED)
        return carry

    lax.fori_loop(0, kwin // BIAS_ROWS, rows, 0)


def _dil_bias(rel_bias, group, seq):
    _, dilation = DIL_GROUPS[group]
    _, kwin, n_tiles = _dil_geometry(seq, dilation)
    n_var = 1 if n_tiles == 1 else 2
    n_pairs = DIL_HEADS // 2
    kern = functools.partial(_dil_bias_kernel, group=group, dilation=dilation, kwin=kwin, n_var=n_var)
    return pl.pallas_call(
        kern,
        grid=(1,),
        in_specs=[pl.BlockSpec(memory_space=pltpu.SMEM)],
        out_specs=pl.BlockSpec((n_var, n_pairs, kwin, 2 * DIL_TQ), lambda v: (0, 0, 0, 0)),
        out_shape=jax.ShapeDtypeStruct((n_var, n_pairs, kwin, 2 * DIL_TQ), F32),
        compiler_params=_params(("arbitrary",)),
        name=f"dil_bias{group}",
    )(rel_bias)


DIL_PROJ_ROWS = 512


def _dil_proj_kernel(h_ref, wqf_ref, wkf_ref, wvf_ref, gq_ref, qT_ref, k_ref, vT_ref, wq_ref, wk_ref, wv_ref,
                     *scratch, dilation, rb, ra):
    hd = DIL_HEAD_DIM
    d_model = h_ref.shape[2]
    n_slabs = d_model // LANES
    mt = rb * ra

    @pl.when((pl.program_id(0) == 0) & (pl.program_id(1) == 0) & (pl.program_id(2) == 0))
    def _():
        wq_ref[...] = wqf_ref[0].T.astype(BF16)
        wk_ref[...] = wkf_ref[0].astype(BF16)
        wv_ref[...] = wvf_ref[0].T.astype(BF16)

    if dilation == 1:
        hcat = h_ref[0]
    else:
        (scr,) = scratch
        sub = pl.program_id(2)

        @pl.when(sub == 0)
        def _():
            for s in range(n_slabs):
                scr[s] = h_ref[0, :, s * LANES:(s + 1) * LANES].astype(F32)

        rows = []
        for j in range(rb):
            res = sub * rb + j
            rows.append(jnp.concatenate(
                [scr[s, pl.ds(res, ra, stride=dilation), :] for s in range(n_slabs)], axis=1))
        hcat = jnp.concatenate(rows, axis=0).astype(BF16)

    q_t = lax.dot_general(wq_ref[...], hcat, NT_DIMS, preferred_element_type=F32)
    gq = jnp.concatenate([gq_ref[...]] * (mt // LANES), axis=1)
    for h in range(DIL_HEADS):
        blk = q_t[h * hd:(h + 1) * hd]
        r = lax.rsqrt(jnp.sum(blk * blk, axis=0, keepdims=True) * (1.0 / hd) + EPS)
        qb = (blk * r * gq).astype(BF16)
        for j in range(rb):
            for s in range(ra // LANES):
                lo = j * ra + s * LANES
                qT_ref[0, j, s, h * hd:(h + 1) * hd, :] = qb[:, lo:lo + LANES]

    kf = jnp.dot(hcat, wk_ref[...], preferred_element_type=F32)
    lane = lax.broadcasted_iota(jnp.int32, (mt, LANES), 1)
    low = lane < hd
    for c in range(DIL_HEADS * hd // LANES):
        y = kf[:, c * LANES:(c + 1) * LANES]
        y2 = y * y
        s_lo = jnp.sum(jnp.where(low, y2, 0.0), axis=1, keepdims=True)
        s_hi = jnp.sum(jnp.where(low, 0.0, y2), axis=1, keepdims=True)
        r = lax.rsqrt(jnp.where(low, s_lo, s_hi) * (1.0 / hd) + EPS)
        kb = (y * r).astype(BF16)
        for j in range(rb):
            k_ref[0, j, :, c * LANES:(c + 1) * LANES] = kb[j * ra:(j + 1) * ra]

    v_t = lax.dot_general(wv_ref[...], hcat, NT_DIMS, preferred_element_type=F32).astype(BF16)
    for j in range(rb):
        for s in range(ra // LANES):
            lo = j * ra + s * LANES
            vT_ref[0, j, s] = v_t[:, lo:lo + LANES]


def _dil_proj(hn3, w_in, g_q, g_k, group):
    batch, seq, d_model = hn3.shape
    _, dilation = DIL_GROUPS[group]
    length = seq // dilation
    c = DIL_HEADS * DIL_HEAD_DIM
    rb = min(dilation, 4)
    rows = 1024 if dilation <= rb else DIL_PROJ_ROWS
    ra = rows // rb
    tok = ra * dilation
    n_sub = dilation // rb
    gq_col = jnp.broadcast_to((g_q * g_k * (DIL_HEAD_DIM ** -0.5 * LOG2E))[:, None], (DIL_HEAD_DIM, LANES))
    kern = functools.partial(_dil_proj_kernel, dilation=dilation, rb=rb, ra=ra)
    const = lambda shape: pl.BlockSpec(shape, lambda b, t, s: (0,) * len(shape))
    w_blk = lambda j: pl.BlockSpec((1, d_model, c), lambda b, t, s: (0, 0, 3 * group + j))
    scratch = [pltpu.VMEM((c, d_model), BF16), pltpu.VMEM((d_model, c), BF16), pltpu.VMEM((c, d_model), BF16)]
    if dilation > 1:
        scratch.append(pltpu.VMEM((d_model // LANES, tok, LANES), F32))
    return pl.pallas_call(
        kern,
        grid=(batch, seq // tok, n_sub),
        in_specs=[
            pl.BlockSpec((1, tok, d_model), lambda b, t, s: (b, t, 0)),
            w_blk(0), w_blk(1), w_blk(2),
            const((DIL_HEAD_DIM, LANES)),
        ],
        out_specs=[
            pl.BlockSpec((1, rb, ra // LANES, c, LANES), lambda b, t, s: (b, s, t, 0, 0)),
            pl.BlockSpec((1, rb, ra, c), lambda b, t, s: (b, s, t, 0)),
            pl.BlockSpec((1, rb, ra // LANES, c, LANES), lambda b, t, s: (b, s, t, 0, 0)),
        ],
        out_shape=[
            jax.ShapeDtypeStruct((batch, dilation, length // LANES, c, LANES), BF16),
            jax.ShapeDtypeStruct((batch, dilation, length, c), BF16),
            jax.ShapeDtypeStruct((batch, dilation, length // LANES, c, LANES), BF16),
        ],
        scratch_shapes=scratch,
        compiler_params=_params(("arbitrary", "arbitrary", "arbitrary")),
        name=f"dil_proj{group}",
    )(hn3, w_in, w_in, w_in, gq_col)


def _dil_attn_kernel(safe_ref, qT_ref, k_ref, vT_ref, bm_ref, o_ref, st_ref, s_ref, *, n_tiles, rb):
    tq = DIL_TQ
    hd = DIL_HEAD_DIM
    n_pairs = DIL_HEADS // 2
    c = DIL_HEADS * hd
    shifted = n_tiles > 1
    low = lax.broadcasted_iota(jnp.int32, (LANES, tq), 0) < hd
    ones = jnp.ones((ONES_ROWS, 2 * tq if shifted else tq), BF16)
    prow = lambda p: slice(p * LANES, (p + 1) * LANES)

    def block_diag(q2):
        zero = jnp.zeros_like(q2)
        return jnp.concatenate([jnp.where(low, q2, zero), jnp.where(low, zero, q2)], axis=1)

    def attend(s_t, v2, use_max):
        if use_max:
            m = jnp.max(s_t, axis=0, keepdims=True)
            pb = jnp.exp2(s_t - m).astype(BF16)
        else:
            pb = jnp.exp2(s_t).astype(BF16)
        res = jnp.dot(jnp.concatenate([v2, ones], axis=0), pb, preferred_element_type=F32)
        den = res[LANES:LANES + 1]
        rinv = 1.0 / den
        o_pair = jnp.concatenate([res[:hd, :tq] * rinv[:, :tq], res[hd:LANES, tq:] * rinv[:, tq:]], axis=0)
        lse = jnp.log2(den) + m if use_max else jnp.log2(den)
        lse = lse * LN2
        return o_pair.T.astype(BF16), [lse[:, :tq], lse[:, tq:]]

    def stats_rows(stats):
        return jnp.concatenate(stats + [jnp.zeros((LANES - DIL_HEADS, tq), F32)], axis=0).T

    tiles = [(j, t) for j in range(rb) for t in range(n_tiles)]

    @pl.when(safe_ref[0] == 1)
    def _():
        def score_tile(i):
            j, t = tiles[i]
            t1 = (t + 1) % n_tiles
            for p in range(n_pairs):
                if shifted:
                    q2 = jnp.concatenate(
                        [qT_ref[0, j, t, prow(p), DIL_HALF:], qT_ref[0, j, t1, prow(p), :DIL_HALF]], axis=1)
                    k2 = jnp.concatenate([k_ref[0, j, t * tq:(t + 1) * tq, prow(p)],
                                          k_ref[0, j, t1 * tq:(t1 + 1) * tq, prow(p)]], axis=0)
                else:
                    q2 = qT_ref[0, j, 0, prow(p), :]
                    k2 = k_ref[0, j, :, prow(p)]
                s_ref[i % 2, p] = jnp.dot(k2, block_diag(q2), preferred_element_type=F32)

        def attend_tile(i):
            j, t = tiles[i]
            t1 = (t + 1) % n_tiles
            var = 1 if shifted and t + 1 == n_tiles else 0

            def store(ref, lanes, val):
                if shifted:
                    ref[0, t * tq + DIL_HALF:(t + 1) * tq, lanes] = val[:DIL_HALF]
                    ref[0, t1 * tq:t1 * tq + DIL_HALF, lanes] = val[DIL_HALF:]
                else:
                    ref[0, :, lanes] = val

            stats = []
            for p in range(n_pairs):
                if shifted:
                    v2 = jnp.concatenate([vT_ref[0, j, t, prow(p), :], vT_ref[0, j, t1, prow(p), :]], axis=1)
                else:
                    v2 = vT_ref[0, j, 0, prow(p), :]
                o_rows, lses = attend(s_ref[i % 2, p] + bm_ref[var, p], v2, use_max=False)
                store(o_ref, slice(j * c + p * LANES, j * c + (p + 1) * LANES), o_rows)
                stats += lses
            store(st_ref, slice(j * LANES, (j + 1) * LANES), stats_rows(stats))

        score_tile(0)
        for i in range(len(tiles)):
            if i + 1 < len(tiles):
                score_tile(i + 1)
            attend_tile(i)

    @pl.when(safe_ref[0] != 1)
    def _():
        def tile(j, t):
            if shifted:
                last = t + 1 == n_tiles
                t1 = jnp.where(last, 0, t + 1)
                var = jnp.where(last, 1, 0)
                r0 = pl.multiple_of(t * tq, tq)
                r1 = pl.multiple_of(t1 * tq, tq)

            def store(ref, lanes, val):
                if shifted:
                    ref[0, pl.ds(r0 + DIL_HALF, DIL_HALF), lanes] = val[:DIL_HALF]
                    ref[0, pl.ds(r1, DIL_HALF), lanes] = val[DIL_HALF:]
                else:
                    ref[0, :, lanes] = val

            stats = []
            for p in range(n_pairs):
                if shifted:
                    q2 = jnp.concatenate(
                        [qT_ref[0, j, t, prow(p), DIL_HALF:], qT_ref[0, j, t1, prow(p), :DIL_HALF]], axis=1)
                    k2 = jnp.concatenate(
                        [k_ref[0, j, pl.ds(r0, tq), prow(p)], k_ref[0, j, pl.ds(r1, tq), prow(p)]], axis=0)
                    v2 = jnp.concatenate([vT_ref[0, j, t, prow(p), :], vT_ref[0, j, t1, prow(p), :]], axis=1)
                    bias = bm_ref[var, p]
                else:
                    q2 = qT_ref[0, j, 0, prow(p), :]
                    k2 = k_ref[0, j, :, prow(p)]
                    v2 = vT_ref[0, j, 0, prow(p), :]
                    bias = bm_ref[0, p]
                s_t = jnp.dot(k2, block_diag(q2), preferred_element_type=F32) + bias
                o_rows, lses = attend(s_t, v2, use_max=True)
                store(o_ref, slice(j * c + p * LANES, j * c + (p + 1) * LANES), o_rows)
                stats += lses
            store(st_ref, slice(j * LANES, (j + 1) * LANES), stats_rows(stats))

        for j in range(rb):
            if shifted:
                lax.fori_loop(0, n_tiles, lambda t, carry, j=j: (tile(j, t), carry)[1], 0)
            else:
                tile(j, 0)


def _dil_attn(q_t, k, v_t, bias, safe, group, seq):
    batch = q_t.shape[0]
    _, dilation = DIL_GROUPS[group]
    length, kwin, n_tiles = _dil_geometry(seq, dilation)
    c = DIL_HEADS * DIL_HEAD_DIM
    n_slabs = length // LANES
    rb = min(dilation, DIL_ATTN_POSITIONS // length, 8)
    kern = functools.partial(_dil_attn_kernel, n_tiles=n_tiles, rb=rb)
    return pl.pallas_call(
        kern,
        grid=(batch, dilation // rb),
        in_specs=[
            pl.BlockSpec(memory_space=pltpu.SMEM),
            pl.BlockSpec((1, rb, n_slabs, c, LANES), lambda b, r: (b, r, 0, 0, 0)),
            pl.BlockSpec((1, rb, length, c), lambda b, r: (b, r, 0, 0)),
            pl.BlockSpec((1, rb, n_slabs, c, LANES), lambda b, r: (b, r, 0, 0, 0)),
            pl.BlockSpec(bias.shape, lambda b, r: (0, 0, 0, 0)),
        ],
        out_specs=[
            pl.BlockSpec((1, length, rb * c), lambda b, r: (b, 0, r)),
            pl.BlockSpec((1, length, rb * LANES), lambda b, r: (b, 0, r)),
        ],
        out_shape=[
            jax.ShapeDtypeStruct((batch, length, dilation * c), BF16),
            jax.ShapeDtypeStruct((batch, length, dilation * LANES), F32),
        ],
        scratch_shapes=[pltpu.VMEM((2, DIL_HEADS // 2, kwin, 2 * DIL_TQ), F32)],
        compiler_params=_params(("arbitrary", "arbitrary")),
        name=f"dil_attn{group}",
    )(safe, q_t, k, v_t, bias)


def kernel(x, c, positions, ada_w, ada_b, norm_mix, norm_mlp, mlp_w1, mlp_w2, mla_w_in, mla_g_qa, mla_w_qb,
           mla_g_kva, mla_w_kvb, mla_g_q, mla_g_k, mla_w_o, dil_w_in, dil_g_q, dil_g_k, dil_w_o, rel_bias):
    batch, seq, d = x.shape
    depth = ada_w.shape[0]
    assert depth == 2 and len(DIL_GROUPS) == 3
    mod = _mod(c, ada_w, ada_b).reshape(depth, batch, N_MOD, d)
    x2d = x.reshape(batch * seq, d)

    n_heads = mla_w_qb.shape[2] // mla_g_q.shape[1]
    q_t, k_pad, v_t = _mla_proj(x2d, mod[0], norm_mix[0], positions, mla_w_in[0], mla_g_qa[0], mla_w_qb[0],
                                mla_g_kva[0], mla_w_kvb[0], mla_g_q[0], mla_g_k[0], batch, seq)
    o_t = _mla_attn(q_t, k_pad, v_t, _score_bound_is_safe(mla_g_q[0] * mla_g_k[0], mla_g_q.shape[1]), n_heads)
    x2, hn = _mla_mlp(o_t, mla_w_o[0], mlp_w1, mlp_w2, 0, x2d, mod[0], norm_mlp[0], norm_mix[1], mod[1], seq)

    outs, stats = [], []
    hn3 = hn.reshape(batch, seq, d)
    for g in range(len(DIL_GROUPS)):
        q_g, k_g, v_g = _dil_proj(hn3, dil_w_in, dil_g_q[0, g], dil_g_k[0, g], g)
        safe = _score_bound_is_safe(dil_g_q[0, g] * dil_g_k[0, g], DIL_HEAD_DIM,
                                    rel_bias[:, g * DIL_HEADS:(g + 1) * DIL_HEADS])
        o_g, st_g = _dil_attn(q_g, k_g, v_g, _dil_bias(rel_bias, g, seq), safe, g, seq)
        outs.append(o_g)
        stats.append(st_g)
    x4 = _dil_mlp(outs, stats, dil_w_o[0], mlp_w1, mlp_w2, 1, x2, mod[1], norm_mlp[1], seq)
    return x4.reshape(batch, seq, d)
```

```python
import functools
import math

import jax
import jax.numpy as jnp
from jax import lax
from jax.experimental import pallas as pl
from jax.experimental.pallas import tpu as pltpu

F32 = jnp.float32
BF16 = jnp.bfloat16

EPS = 1e-6
LOG2E = 1.4426950408889634
LN2 = 0.6931471805599453
MASKED = -1e30

N_MOD = 6
ROPE_THETA = 10000.0
MLA_ROPE = 32
DIL_GROUPS = ((128, 1), (512, 4), (2048, 16))
DIL_HEADS = 16
DIL_HEAD_DIM = 64
N_BUCKETS = 32
MAX_DISTANCE = 1024

LANES = 128
HEAD_PAD = 128

VMEM_LIMIT = 56 * 1024 * 1024

NT_DIMS = (((1,), (1,)), ((), ()))
TN_DIMS = (((0,), (0,)), ((), ()))


def _params(sem):
    return pltpu.CompilerParams(dimension_semantics=sem, vmem_limit_bytes=VMEM_LIMIT)


def _ada_norm(x, g, scale, shift):
    ms = jnp.mean(x * x, axis=-1, keepdims=True)
    return (x * lax.rsqrt(ms + EPS) * g) * (1.0 + scale) + shift


def _rms(x, g):
    ms = jnp.mean(x * x, axis=-1, keepdims=True)
    return x * lax.rsqrt(ms + EPS) * g


def _mod_kernel(c_ref, w_ref, b_ref, o_ref):
    c = c_ref[...]
    cond = c / (1.0 + jnp.exp(-c))
    o_ref[0] = (
        jnp.dot(cond.astype(BF16), w_ref[0].astype(BF16), preferred_element_type=F32) + b_ref[0]
    )


MOD_COLS = 1536


def _mod(c, ada_w, ada_b):
    depth, d, n = ada_w.shape
    b = c.shape[0]
    tn = MOD_COLS
    return pl.pallas_call(
        _mod_kernel,
        grid=(depth, n // tn),
        in_specs=[
            pl.BlockSpec((b, d), lambda l, j: (0, 0)),
            pl.BlockSpec((1, d, tn), lambda l, j: (l, 0, j)),
            pl.BlockSpec((1, 1, tn), lambda l, j: (l, 0, j)),
        ],
        out_specs=pl.BlockSpec((1, b, tn), lambda l, j: (l, 0, j)),
        out_shape=jax.ShapeDtypeStruct((depth, b, n), F32),
        compiler_params=_params(("arbitrary", "arbitrary")),
        name="mod",
    )(c, ada_w, ada_b.reshape(depth, 1, n))


def _mla_proj_kernel(x_ref, mod_ref, nmix_ref, pos_ref, inv_ref, wlat_ref, wkr_ref, gqa_ref, gkva_ref,
                     wqb_ref, wkn_ref, wv_ref, gq_ref, qT_ref, k_ref, vT_ref, *, n_heads, q_lora, qk_dim):
    x = x_ref[...]
    tm = x.shape[0]
    m = mod_ref[0]
    hb = _ada_norm(x, nmix_ref[...], m[1:2], m[0:1]).astype(BF16)
    lat = jnp.dot(hb, wlat_ref[...], preferred_element_type=F32)
    qn = _rms(lat[:, :q_lora], gqa_ref[...]).astype(BF16)
    kvn = _rms(lat[:, q_lora:], gkva_ref[...]).astype(BF16)

    half = MLA_ROPE // 2
    nope = qk_dim - MLA_ROPE
    pos = pos_ref[...].astype(F32)
    ang = jnp.concatenate([inv_ref[...]] * (tm // LANES), axis=1) * pos
    cos_t = jnp.cos(ang)
    sin_t = jnp.sin(ang)

    def rope_rows(blk):
        x1 = blk[nope:nope + half]
        x2 = blk[nope + half:qk_dim]
        return jnp.concatenate(
            [blk[:nope], x1 * cos_t - x2 * sin_t, x2 * cos_t + x1 * sin_t, blk[qk_dim:]], axis=0)

    kr_t = lax.dot_general(wkr_ref[...], hb, NT_DIMS, preferred_element_type=F32)
    kr = rope_rows(kr_t).T

    q_t = lax.dot_general(wqb_ref[...], qn, NT_DIMS, preferred_element_type=F32)
    gq = jnp.concatenate([gq_ref[...]] * (tm // LANES), axis=1)
    for h in range(n_heads):
        blk = rope_rows(q_t[h * HEAD_PAD:(h + 1) * HEAD_PAD])
        ssq = jnp.sum(blk * blk, axis=0, keepdims=True)
        r = lax.rsqrt(ssq * (1.0 / qk_dim) + EPS)
        qT_ref[0, 0, h * HEAD_PAD:(h + 1) * HEAD_PAD, :] = (blk * r * gq).astype(BF16)

    kn = jnp.dot(kvn, wkn_ref[...], preferred_element_type=F32)
    for h in range(n_heads):
        kb = kn[:, h * HEAD_PAD:(h + 1) * HEAD_PAD] + kr
        ssq = jnp.sum(kb * kb, axis=1, keepdims=True)
        r = lax.rsqrt(ssq * (1.0 / qk_dim) + EPS)
        k_ref[:, h * HEAD_PAD:(h + 1) * HEAD_PAD] = (kb * r).astype(BF16)

    vT_ref[0] = lax.dot_general(wv_ref[...], kvn, NT_DIMS, preferred_element_type=F32).astype(BF16)


def _mla_proj(x2d, mod_l, nmix, positions, w_in, g_qa, w_qb, g_kva, w_kvb, g_q, g_k, batch, seq):
    t, d = x2d.shape
    q_lora = g_qa.shape[0]
    kv_lora = g_kva.shape[0]
    qk_dim = g_q.shape[0]
    n_heads = w_qb.shape[1] // qk_dim
    nope = qk_dim - MLA_ROPE
    v_dim = w_kvb.shape[1] // n_heads - nope
    assert nope + v_dim == HEAD_PAD and qk_dim <= HEAD_PAD
    half = MLA_ROPE // 2
    tm = MIX_ROWS
    tpb = seq // tm

    w_lat = w_in[:, :q_lora + kv_lora].astype(BF16)
    w_kr = jnp.zeros((HEAD_PAD, d), F32).at[nope:qk_dim].set(w_in[:, q_lora + kv_lora:].T).astype(BF16)
    w_qb_t = jnp.pad(w_qb.T.reshape(n_heads, qk_dim, q_lora), ((0, 0), (0, HEAD_PAD - qk_dim), (0, 0)))
    w_qb_t = w_qb_t.reshape(n_heads * HEAD_PAD, q_lora).astype(BF16)
    w_kvb3 = w_kvb.reshape(kv_lora, n_heads, nope + v_dim)
    w_kn = jnp.where(jnp.arange(nope + v_dim) < nope, w_kvb3, 0.0).reshape(kv_lora, n_heads * HEAD_PAD).astype(BF16)
    w_v_t = w_kvb3[:, :, nope:].reshape(kv_lora, n_heads * v_dim).T.astype(BF16)
    q_gain = jnp.pad(g_q * g_k * (qk_dim ** -0.5 * LOG2E), (0, HEAD_PAD - qk_dim))
    gq_col = jnp.broadcast_to(q_gain[:, None], (HEAD_PAD, LANES))
    inv = 1.0 / (ROPE_THETA ** (jnp.arange(half, dtype=F32) / half))
    inv_tab = jnp.broadcast_to(inv[:, None], (half, LANES))
    pos_row = positions.reshape(1, t)

    full = lambda shape: pl.BlockSpec(shape, lambda i: (0,) * len(shape))
    kern = functools.partial(_mla_proj_kernel, n_heads=n_heads, q_lora=q_lora, qk_dim=qk_dim)
    return pl.pallas_call(
        kern,
        grid=(t // tm,),
        in_specs=[
            pl.BlockSpec((tm, d), lambda i: (i, 0)),
            pl.BlockSpec((1, N_MOD, d), lambda i: (i // tpb, 0, 0)),
            full((1, d)),
            pl.BlockSpec((1, tm), lambda i: (0, i)),
            full((half, LANES)),
            full(w_lat.shape), full(w_kr.shape), full((1, q_lora)), full((1, kv_lora)),
            full(w_qb_t.shape), full(w_kn.shape), full(w_v_t.shape),
            full((HEAD_PAD, LANES)),
        ],
        out_specs=[
            pl.BlockSpec((1, 1, n_heads * HEAD_PAD, tm), lambda i: (i // tpb, i % tpb, 0, 0)),
            pl.BlockSpec((tm, n_heads * HEAD_PAD), lambda i: (i, 0)),
            pl.BlockSpec((1, n_heads * v_dim, tm), lambda i: (i // tpb, 0, i % tpb)),
        ],
        out_shape=[
            jax.ShapeDtypeStruct((batch, tpb, n_heads * HEAD_PAD, tm), BF16),
            jax.ShapeDtypeStruct((t, n_heads * HEAD_PAD), BF16),
            jax.ShapeDtypeStruct((batch, n_heads * v_dim, seq), BF16),
        ],
        compiler_params=_params(("arbitrary",)),
        name="mla_proj",
    )(x2d, mod_l, nmix[None, :], pos_row, inv_tab, w_lat, w_kr, g_qa[None, :], g_kva[None, :],
      w_qb_t, w_kn, w_v_t, gq_col)


ONES_ROWS = 16


MLA_ATTN_COLS = 256
MLA_ATTN_KEYS = 256
SAFE_EXPONENT = 60.0


def _score_bound_is_safe(gain_product, head_dim, bias=None):
    bound = 1.02 * math.sqrt(head_dim) * LOG2E * jnp.max(jnp.abs(gain_product))
    if bias is not None:
        bound = bound + LOG2E * jnp.max(jnp.abs(bias))
    return (bound <= SAFE_EXPONENT).astype(jnp.int32).reshape(1)


def _mla_attn_kernel(safe_ref, qT_ref, k_ref, vT_ref, oT_ref, s_ref, *, hb, tn, kc):
    seq = k_ref.shape[1]
    n_tiles, _, tq = qT_ref.shape[1:]
    v_dim = vT_ref.shape[1] // hb
    ones = jnp.ones((ONES_ROWS, seq), BF16)
    n_chunks = seq // kc
    n_cols = tq // tn
    hrows = lambda h: slice(h * HEAD_PAD, (h + 1) * HEAD_PAD)
    vrows = lambda h: slice(h * v_dim, (h + 1) * v_dim)
    v_ext = lambda h: jnp.concatenate([vT_ref[0, vrows(h), :], ones], axis=0)

    @pl.when(safe_ref[0] == 1)
    def _():
        tiles = [(h, i) for h in range(hb) for i in range(n_tiles)]

        def scores(t, c):
            h, i = tiles[t]
            s_ref[t % 2, c * kc:(c + 1) * kc, :] = jnp.dot(
                k_ref[0, c * kc:(c + 1) * kc, hrows(h)], qT_ref[0, i, hrows(h), :], preferred_element_type=F32)

        for c in range(n_chunks):
            scores(0, c)
        for t, (h, i) in enumerate(tiles):
            ve = v_ext(h)
            for j in range(n_cols):
                acc = jnp.zeros((v_dim + ONES_ROWS, tn), F32)
                for c in range(n_chunks):
                    step = j * n_chunks + c
                    if t + 1 < len(tiles) and step % n_cols == 0:
                        scores(t + 1, step // n_cols)
                    p = jnp.exp2(s_ref[t % 2, c * kc:(c + 1) * kc, j * tn:(j + 1) * tn]).astype(BF16)
                    acc = acc + jnp.dot(ve[:, c * kc:(c + 1) * kc], p, preferred_element_type=F32)
                o = acc[:v_dim] * (1.0 / acc[v_dim:v_dim + 1])
                oT_ref[0, i, vrows(h), j * tn:(j + 1) * tn] = o.astype(BF16)

    @pl.when(safe_ref[0] != 1)
    def _():
        for h in range(hb):
            ve = v_ext(h)
            k_h = k_ref[0, :, hrows(h)]

            def tile(i, carry):
                s_t = jnp.dot(k_h, qT_ref[0, i, hrows(h), :], preferred_element_type=F32)
                p = jnp.exp2(s_t - jnp.max(s_t, axis=0, keepdims=True)).astype(BF16)
                r = jnp.dot(ve, p, preferred_element_type=F32)
                oT_ref[0, i, vrows(h), :] = (r[:v_dim] * (1.0 / r[v_dim:v_dim + 1])).astype(BF16)
                return carry

            lax.fori_loop(0, n_tiles, tile, 0)


def _mla_attn(q_t, k_pad, v_t, safe, n_heads):
    batch, n_tiles, _, tq = q_t.shape
    seq = n_tiles * tq
    v_dim = v_t.shape[1] // n_heads
    k3 = k_pad.reshape(batch, seq, n_heads * HEAD_PAD)
    hb = 2
    return pl.pallas_call(
        functools.partial(_mla_attn_kernel, hb=hb, tn=MLA_ATTN_COLS, kc=MLA_ATTN_KEYS),
        grid=(batch, n_heads // hb),
        in_specs=[
            pl.BlockSpec(memory_space=pltpu.SMEM),
            pl.BlockSpec((1, n_tiles, hb * HEAD_PAD, tq), lambda b, h: (b, 0, h, 0)),
            pl.BlockSpec((1, seq, hb * HEAD_PAD), lambda b, h: (b, 0, h)),
            pl.BlockSpec((1, hb * v_dim, seq), lambda b, h: (b, h, 0)),
        ],
        out_specs=pl.BlockSpec((1, n_tiles, hb * v_dim, tq), lambda b, h: (b, 0, h, 0)),
        out_shape=jax.ShapeDtypeStruct((batch, n_tiles, n_heads * v_dim, tq), BF16),
        scratch_shapes=[pltpu.VMEM((2, seq, tq), F32)],
        compiler_params=_params(("arbitrary", "arbitrary")),
        name="mla_attn",
    )(safe, q_t, k3, v_t)


MIX_ROWS = 512
MIX_SUB = 256
FF_CHUNK = 2048
LOAD_SLOTS = 4
LOAD_ROWS_FF = 64
LOAD_ROWS_D = 256


def _load_bf16(src, dst, stage, sem):
    slots, chunk = stage.shape[:2]
    n = dst.shape[0] // chunk

    def copy(i):
        return pltpu.make_async_copy(src.at[pl.ds(i * chunk, chunk), :], stage.at[i % slots], sem.at[i % slots])

    for i in range(min(slots - 1, n)):
        copy(i).start()
    for i in range(n):
        if i + slots - 1 < n:
            copy(i + slots - 1).start()
        copy(i).wait()
        dst[i * chunk:(i + 1) * chunk, :] = stage[i % slots].astype(BF16)


def _mlp_rows(h2, w1_ref, w2_ref):
    z = None
    for f in range(w1_ref.shape[1] // FF_CHUNK):
        cols = slice(f * FF_CHUNK, (f + 1) * FF_CHUNK)
        u = jnp.dot(h2, w1_ref[:, cols], preferred_element_type=F32)
        u = jnp.square(jnp.maximum(u, 0.0)).astype(BF16)
        part = jnp.dot(u, w2_ref[cols, :], preferred_element_type=F32)
        z = part if z is None else z + part
    return z


def _mix_mlp_body(mixer_out, x_ref, mod_ref, nmlp_ref, next_refs, x2_ref, hn_ref, wo_ref, w1_ref, w2_ref):
    m = mod_ref[0]
    tm = x_ref.shape[0]
    blocks = [slice(i * MIX_SUB, (i + 1) * MIX_SUB) for i in range(tm // MIX_SUB)]

    def mix(rows):
        x1 = x_ref[rows, :] + m[2:3] * mixer_out(rows, wo_ref)
        return x1, _ada_norm(x1, nmlp_ref[...], m[4:5], m[3:4]).astype(BF16)

    cur = mix(blocks[0])
    for i, rows in enumerate(blocks):
        nxt = mix(blocks[i + 1]) if i + 1 < len(blocks) else None
        x1, h2 = cur
        x2 = x1 + m[5:6] * _mlp_rows(h2, w1_ref, w2_ref)
        x2_ref[rows, :] = x2
        if next_refs is not None:
            nnext_ref, modn_ref = next_refs
            mn = modn_ref[0]
            hn_ref[rows, :] = _ada_norm(x2, nnext_ref[...], mn[1:2], mn[0:1]).astype(BF16)
        cur = nxt


def _load_mix_weights(wo_hbm, w1_hbm, w2_hbm, layer, wo_ref, w1_ref, w2_ref, stage_w, stage_n, sem):
    _load_bf16(wo_hbm, wo_ref, stage_n, sem)
    _load_bf16(w1_hbm.at[layer], w1_ref, stage_w, sem)
    _load_bf16(w2_hbm.at[layer], w2_ref, stage_n, sem)


def _mla_mlp_kernel(oT_ref, x_ref, mod_ref, nmlp_ref, nnext_ref, modn_ref, wo_hbm, w1_hbm, w2_hbm, x2_ref, hn_ref,
                    wo_ref, w1_ref, w2_ref, stage_w, stage_n, sem, *, layer):
    @pl.when(pl.program_id(0) == 0)
    def _():
        _load_mix_weights(wo_hbm, w1_hbm, w2_hbm, layer, wo_ref, w1_ref, w2_ref, stage_w, stage_n, sem)

    def mixer_out(rows, wo):
        return lax.dot_general(oT_ref[0, 0, :, rows], wo[...], TN_DIMS, preferred_element_type=F32)

    _mix_mlp_body(mixer_out, x_ref, mod_ref, nmlp_ref, (nnext_ref, modn_ref), x2_ref, hn_ref, wo_ref, w1_ref, w2_ref)


def _dil_mlp_kernel(o0_ref, o1_ref, o2_ref, s0_ref, s1_ref, s2_ref, e_ref, x_ref, mod_ref, nmlp_ref,
                    wo_hbm, w1_hbm, w2_hbm, x2_ref, wo_ref, w1_ref, w2_ref, stage_w, stage_n, sem,
                    oscr1, sscr1, oscr2, sscr2, *, layer):
    @pl.when(pl.program_id(0) == 0)
    def _():
        _load_mix_weights(wo_hbm, w1_hbm, w2_hbm, layer, wo_ref, w1_ref, w2_ref, stage_w, stage_n, sem)

    tm = x_ref.shape[0]
    c = wo_ref.shape[0]

    def token_major(o_ref, s_ref, oscr, sscr, dilation):
        ra = tm // dilation
        for r in range(dilation):
            for s in range(c // LANES):
                lo = r * c + s * LANES
                oscr[s, pl.ds(r, ra, stride=dilation), :] = o_ref[0, :, lo:lo + LANES].astype(F32)
            sscr[pl.ds(r, ra, stride=dilation), :] = s_ref[0, :, r * LANES:(r + 1) * LANES]

    token_major(o1_ref, s1_ref, oscr1, sscr1, DIL_GROUPS[1][1])
    token_major(o2_ref, s2_ref, oscr2, sscr2, DIL_GROUPS[2][1])
    e = e_ref[...]

    def expand(w):
        return jnp.dot(w.astype(BF16), e, preferred_element_type=F32)

    def mixer_out(rows, wo):
        gather = lambda oscr: jnp.concatenate([oscr[s, rows, :] for s in range(c // LANES)], axis=1)
        o0, l0 = o0_ref[0, rows, :].astype(F32), s0_ref[0, rows, :]
        o1, l1 = gather(oscr1), sscr1[rows, :]
        o2, l2 = gather(oscr2), sscr2[rows, :]
        mx = jnp.maximum(jnp.maximum(l0, l1), l2)
        e0, e1, e2 = jnp.exp(l0 - mx), jnp.exp(l1 - mx), jnp.exp(l2 - mx)
        rden = 1.0 / (e0 + e1 + e2)
        o = expand(e0 * rden) * o0 + expand(e1 * rden) * o1 + expand(e2 * rden) * o2
        return jnp.dot(o.astype(BF16), wo[...], preferred_element_type=F32)

    _mix_mlp_body(mixer_out, x_ref, mod_ref, nmlp_ref, None, x2_ref, None, wo_ref, w1_ref, w2_ref)


def _mix_mlp_call(kern, name, lead_specs, lead_args, w_o, w1, w2, x2d, mod_l, nmlp, seq, next_args, extra_scratch):
    t, d = x2d.shape
    ff = w1.shape[2]
    c = w_o.shape[0]
    tm = MIX_ROWS
    tpb = seq // tm
    row = pl.BlockSpec((tm, d), lambda i: (i, 0))
    mod_spec = pl.BlockSpec((1, N_MOD, d), lambda i: (i // tpb, 0, 0))
    vec = pl.BlockSpec((1, d), lambda i: (0, 0))
    hbm = pl.BlockSpec(memory_space=pl.ANY)
    in_specs = lead_specs + [row, mod_spec, vec]
    args = lead_args + [x2d, mod_l, nmlp[None, :]]
    out_specs, out_shape = [row], [jax.ShapeDtypeStruct((t, d), F32)]
    if next_args is not None:
        nnext, mod_next = next_args
        in_specs += [vec, mod_spec]
        args += [nnext[None, :], mod_next]
        out_specs.append(row)
        out_shape.append(jax.ShapeDtypeStruct((t, d), BF16))
    in_specs += [hbm, hbm, hbm]
    args += [w_o, w1, w2]
    scratch = [pltpu.VMEM((c, d), BF16), pltpu.VMEM((d, ff), BF16), pltpu.VMEM((ff, d), BF16),
               pltpu.VMEM((LOAD_SLOTS, LOAD_ROWS_FF, ff), F32), pltpu.VMEM((LOAD_SLOTS, LOAD_ROWS_D, d), F32),
               pltpu.SemaphoreType.DMA((LOAD_SLOTS,))]
    return pl.pallas_call(
        kern,
        grid=(t // tm,),
        in_specs=in_specs,
        out_specs=out_specs,
        out_shape=out_shape,
        scratch_shapes=scratch + extra_scratch,
        compiler_params=_params(("arbitrary",)),
        name=name,
    )(*args)


def _mla_mlp(o_t, w_o, w1, w2, layer, x2d, mod_l, nmlp, nnext, mod_next, seq):
    tm = MIX_ROWS
    tpb = seq // tm
    c = o_t.shape[2]
    assert o_t.shape[3] == tm
    lead = [pl.BlockSpec((1, 1, c, tm), lambda i: (i // tpb, i % tpb, 0, 0))]
    return _mix_mlp_call(functools.partial(_mla_mlp_kernel, layer=layer), "mla_mlp", lead, [o_t], w_o, w1, w2, x2d,
                         mod_l, nmlp, seq, (nnext, mod_next), [])


def _dil_mlp(outs, stats, w_o, w1, w2, layer, x2d, mod_l, nmlp, seq):
    tm = MIX_ROWS
    tpb = seq // tm
    c = w_o.shape[0]
    expand = jnp.repeat(jnp.eye(DIL_HEADS, dtype=F32), DIL_HEAD_DIM, axis=1)
    expand = jnp.pad(expand, ((0, LANES - DIL_HEADS), (0, 0))).astype(BF16)
    grp = lambda g, w: pl.BlockSpec((1, tm // DIL_GROUPS[g][1], DIL_GROUPS[g][1] * w),
                                    lambda i: (i // tpb, i % tpb, 0))
    lead = [grp(0, c), grp(1, c), grp(2, c), grp(0, LANES), grp(1, LANES), grp(2, LANES),
            pl.BlockSpec((LANES, c), lambda i: (0, 0))]
    extra = []
    for g in (1, 2):
        extra += [pltpu.VMEM((c // LANES, tm, LANES), F32), pltpu.VMEM((tm, LANES), F32)]
    (x2,) = _mix_mlp_call(functools.partial(_dil_mlp_kernel, layer=layer), "dil_mlp", lead, [*outs, *stats, expand],
                          w_o, w1, w2, x2d, mod_l, nmlp, seq, None, extra)
    return x2


DIL_TQ = 128
DIL_HALF = 64
DIL_ATTN_POSITIONS = 2048
DIL_ATTN_CLASSES = 8
assert all(w // (2 * d) == DIL_HALF for w, d in DIL_GROUPS)


def _dil_geometry(seq, dilation):
    length = seq // dilation
    kwin = min(2 * DIL_TQ, length)
    return length, kwin, length // DIL_TQ


def _t5_log_thresholds():
    nb = N_BUCKETS // 2
    max_exact = nb // 2
    steps = nb - max_exact
    thresholds = []
    for k in range(1, steps):
        n = max_exact
        while math.floor(math.log(n / max_exact) / math.log(MAX_DISTANCE / max_exact) * steps) < k:
            n += 1
        thresholds.append(n)
    return thresholds


def _t5_bucket(rel):
    nb = N_BUCKETS // 2
    max_exact = nb // 2
    n = jnp.abs(rel)
    large = max_exact
    for thr in _t5_log_thresholds():
        large = large + jnp.where(n >= thr, 1, 0)
    return jnp.where(rel > 0, nb, 0) + jnp.where(n < max_exact, n, large)


BIAS_ROWS = 16


def _dil_bias_kernel(tab_ref, o_ref, *, group, dilation, kwin, n_var):
    def rows(step, carry):
        r0 = pl.multiple_of(step * BIAS_ROWS, BIAS_ROWS)
        kk = lax.broadcasted_iota(jnp.int32, (BIAS_ROWS, DIL_TQ), 0) + r0
        qi = lax.broadcasted_iota(jnp.int32, (BIAS_ROWS, DIL_TQ), 1)
        rel_a = kk - qi - (DIL_HALF if n_var > 1 else 0)
        valid = jnp.abs(rel_a) <= DIL_HALF
        same_end = (qi < DIL_HALF) == (kk < DIL_TQ)
        bucket = _t5_bucket(rel_a * dilation)
        accs = [jnp.zeros((BIAS_ROWS, DIL_TQ), F32)] * DIL_HEADS
        for b in range(N_BUCKETS):
            hit = bucket == b
            accs = [jnp.where(hit, tab_ref[b, group * DIL_HEADS + h], a) for h, a in enumerate(accs)]
        for h, a in enumerate(accs):
            where = (h // 2, pl.ds(r0, BIAS_ROWS), slice((h % 2) * DIL_TQ, (h % 2 + 1) * DIL_TQ))
            tile = jnp.where(valid, a * LOG2E, MASKED)
            o_ref[(0, *where)] = tile
            if n_var > 1:
                o_ref[(1, *where)] = jnp.where(same_end, tile, MASKED)
        return carry

    lax.fori_loop(0, kwin // BIAS_ROWS, rows, 0)


def _dil_bias(rel_bias, group, seq):
    _, dilation = DIL_GROUPS[group]
    _, kwin, n_tiles = _dil_geometry(seq, dilation)
    n_var = 1 if n_tiles == 1 else 2
    n_pairs = DIL_HEADS // 2
    kern = functools.partial(_dil_bias_kernel, group=group, dilation=dilation, kwin=kwin, n_var=n_var)
    return pl.pallas_call(
        kern,
        grid=(1,),
        in_specs=[pl.BlockSpec(memory_space=pltpu.SMEM)],
        out_specs=pl.BlockSpec((n_var, n_pairs, kwin, 2 * DIL_TQ), lambda v: (0, 0, 0, 0)),
        out_shape=jax.ShapeDtypeStruct((n_var, n_pairs, kwin, 2 * DIL_TQ), F32),
        compiler_params=_params(("arbitrary",)),
        name=f"dil_bias{group}",
    )(rel_bias)


DIL_PROJ_ROWS = 512
DIL_PROJ_ROWS_WIDE = 1024
DIL_PROJ_CLASSES = 4


def _dil_proj_kernel(h_ref, wqf_ref, wkf_ref, wvf_ref, gq_ref, qT_ref, k_ref, vT_ref, wq_ref, wk_ref, wv_ref,
                     *scratch, dilation, rb, ra):
    hd = DIL_HEAD_DIM
    d_model = h_ref.shape[2]
    n_slabs = d_model // LANES
    mt = rb * ra

    @pl.when((pl.program_id(0) == 0) & (pl.program_id(1) == 0) & (pl.program_id(2) == 0))
    def _():
        wq_ref[...] = wqf_ref[0].T.astype(BF16)
        wk_ref[...] = wkf_ref[0].astype(BF16)
        wv_ref[...] = wvf_ref[0].T.astype(BF16)

    if dilation == 1:
        hcat = h_ref[0]
    else:
        (scr,) = scratch
        sub = pl.program_id(2)

        @pl.when(sub == 0)
        def _():
            for s in range(n_slabs):
                scr[s] = h_ref[0, :, s * LANES:(s + 1) * LANES].astype(F32)

        rows = []
        for j in range(rb):
            res = sub * rb + j
            rows.append(jnp.concatenate(
                [scr[s, pl.ds(res, ra, stride=dilation), :] for s in range(n_slabs)], axis=1))
        hcat = jnp.concatenate(rows, axis=0).astype(BF16)

    q_t = lax.dot_general(wq_ref[...], hcat, NT_DIMS, preferred_element_type=F32)
    gq = jnp.concatenate([gq_ref[...]] * (mt // LANES), axis=1)
    for h in range(DIL_HEADS):
        blk = q_t[h * hd:(h + 1) * hd]
        r = lax.rsqrt(jnp.sum(blk * blk, axis=0, keepdims=True) * (1.0 / hd) + EPS)
        qb = (blk * r * gq).astype(BF16)
        for j in range(rb):
            for s in range(ra // LANES):
                lo = j * ra + s * LANES
                qT_ref[0, j, s, h * hd:(h + 1) * hd, :] = qb[:, lo:lo + LANES]

    kf = jnp.dot(hcat, wk_ref[...], preferred_element_type=F32)
    lane = lax.broadcasted_iota(jnp.int32, (mt, LANES), 1)
    low = lane < hd
    for c in range(DIL_HEADS * hd // LANES):
        y = kf[:, c * LANES:(c + 1) * LANES]
        y2 = y * y
        s_lo = jnp.sum(jnp.where(low, y2, 0.0), axis=1, keepdims=True)
        s_hi = jnp.sum(jnp.where(low, 0.0, y2), axis=1, keepdims=True)
        r = lax.rsqrt(jnp.where(low, s_lo, s_hi) * (1.0 / hd) + EPS)
        kb = (y * r).astype(BF16)
        for j in range(rb):
            k_ref[0, j, :, c * LANES:(c + 1) * LANES] = kb[j * ra:(j + 1) * ra]

    v_t = lax.dot_general(wv_ref[...], hcat, NT_DIMS, preferred_element_type=F32).astype(BF16)
    for j in range(rb):
        for s in range(ra // LANES):
            lo = j * ra + s * LANES
            vT_ref[0, j, s] = v_t[:, lo:lo + LANES]


def _dil_proj(hn3, w_in, g_q, g_k, group):
    batch, seq, d_model = hn3.shape
    _, dilation = DIL_GROUPS[group]
    length = seq // dilation
    c = DIL_HEADS * DIL_HEAD_DIM
    rb = min(dilation, DIL_PROJ_CLASSES)
    rows = DIL_PROJ_ROWS_WIDE if dilation <= rb else DIL_PROJ_ROWS
    ra = rows // rb
    tok = ra * dilation
    n_sub = dilation // rb
    gq_col = jnp.broadcast_to((g_q * g_k * (DIL_HEAD_DIM ** -0.5 * LOG2E))[:, None], (DIL_HEAD_DIM, LANES))
    kern = functools.partial(_dil_proj_kernel, dilation=dilation, rb=rb, ra=ra)
    const = lambda shape: pl.BlockSpec(shape, lambda b, t, s: (0,) * len(shape))
    w_blk = lambda j: pl.BlockSpec((1, d_model, c), lambda b, t, s: (0, 0, 3 * group + j))
    scratch = [pltpu.VMEM((c, d_model), BF16), pltpu.VMEM((d_model, c), BF16), pltpu.VMEM((c, d_model), BF16)]
    if dilation > 1:
        scratch.append(pltpu.VMEM((d_model // LANES, tok, LANES), F32))
    return pl.pallas_call(
        kern,
        grid=(batch, seq // tok, n_sub),
        in_specs=[
            pl.BlockSpec((1, tok, d_model), lambda b, t, s: (b, t, 0)),
            w_blk(0), w_blk(1), w_blk(2),
            const((DIL_HEAD_DIM, LANES)),
        ],
        out_specs=[
            pl.BlockSpec((1, rb, ra // LANES, c, LANES), lambda b, t, s: (b, s, t, 0, 0)),
            pl.BlockSpec((1, rb, ra, c), lambda b, t, s: (b, s, t, 0)),
            pl.BlockSpec((1, rb, ra // LANES, c, LANES), lambda b, t, s: (b, s, t, 0, 0)),
        ],
        out_shape=[
            jax.ShapeDtypeStruct((batch, dilation, length // LANES, c, LANES), BF16),
            jax.ShapeDtypeStruct((batch, dilation, length, c), BF16),
            jax.ShapeDtypeStruct((batch, dilation, length // LANES, c, LANES), BF16),
        ],
        scratch_shapes=scratch,
        compiler_params=_params(("arbitrary", "arbitrary", "arbitrary")),
        name=f"dil_proj{group}",
    )(hn3, w_in, w_in, w_in, gq_col)


def _dil_attn_kernel(safe_ref, qT_ref, k_ref, vT_ref, bm_ref, o_ref, st_ref, s_ref, *, n_tiles, rb):
    tq = DIL_TQ
    hd = DIL_HEAD_DIM
    n_pairs = DIL_HEADS // 2
    c = DIL_HEADS * hd
    shifted = n_tiles > 1
    low = lax.broadcasted_iota(jnp.int32, (LANES, tq), 0) < hd
    ones = jnp.ones((ONES_ROWS, 2 * tq if shifted else tq), BF16)
    prow = lambda p: slice(p * LANES, (p + 1) * LANES)

    def block_diag(q2):
        zero = jnp.zeros_like(q2)
        return jnp.concatenate([jnp.where(low, q2, zero), jnp.where(low, zero, q2)], axis=1)

    def attend(s_t, v2, use_max):
        if use_max:
            m = jnp.max(s_t, axis=0, keepdims=True)
            pb = jnp.exp2(s_t - m).astype(BF16)
        else:
            pb = jnp.exp2(s_t).astype(BF16)
        res = jnp.dot(jnp.concatenate([v2, ones], axis=0), pb, preferred_element_type=F32)
        den = res[LANES:LANES + 1]
        rinv = 1.0 / den
        o_pair = jnp.concatenate([res[:hd, :tq] * rinv[:, :tq], res[hd:LANES, tq:] * rinv[:, tq:]], axis=0)
        lse = jnp.log2(den) + m if use_max else jnp.log2(den)
        lse = lse * LN2
        return o_pair.T.astype(BF16), [lse[:, :tq], lse[:, tq:]]

    def stats_rows(stats):
        return jnp.concatenate(stats + [jnp.zeros((LANES - DIL_HEADS, tq), F32)], axis=0).T

    tiles = [(j, t) for j in range(rb) for t in range(n_tiles)]

    @pl.when(safe_ref[0] == 1)
    def _():
        def score_tile(i):
            j, t = tiles[i]
            t1 = (t + 1) % n_tiles
            for p in range(n_pairs):
                if shifted:
                    q2 = jnp.concatenate(
                        [qT_ref[0, j, t, prow(p), DIL_HALF:], qT_ref[0, j, t1, prow(p), :DIL_HALF]], axis=1)
                    k2 = jnp.concatenate([k_ref[0, j, t * tq:(t + 1) * tq, prow(p)],
                                          k_ref[0, j, t1 * tq:(t1 + 1) * tq, prow(p)]], axis=0)
                else:
                    q2 = qT_ref[0, j, 0, prow(p), :]
                    k2 = k_ref[0, j, :, prow(p)]
                s_ref[i % 2, p] = jnp.dot(k2, block_diag(q2), preferred_element_type=F32)

        def attend_tile(i):
            j, t = tiles[i]
            t1 = (t + 1) % n_tiles
            var = 1 if shifted and t + 1 == n_tiles else 0

            def store(ref, lanes, val):
                if shifted:
                    ref[0, t * tq + DIL_HALF:(t + 1) * tq, lanes] = val[:DIL_HALF]
                    ref[0, t1 * tq:t1 * tq + DIL_HALF, lanes] = val[DIL_HALF:]
                else:
                    ref[0, :, lanes] = val

            stats = []
            for p in range(n_pairs):
                if shifted:
                    v2 = jnp.concatenate([vT_ref[0, j, t, prow(p), :], vT_ref[0, j, t1, prow(p), :]], axis=1)
                else:
                    v2 = vT_ref[0, j, 0, prow(p), :]
                o_rows, lses = attend(s_ref[i % 2, p] + bm_ref[var, p], v2, use_max=False)
                store(o_ref, slice(j * c + p * LANES, j * c + (p + 1) * LANES), o_rows)
                stats += lses
            store(st_ref, slice(j * LANES, (j + 1) * LANES), stats_rows(stats))

        score_tile(0)
        for i in range(len(tiles)):
            if i + 1 < len(tiles):
                score_tile(i + 1)
            attend_tile(i)

    @pl.when(safe_ref[0] != 1)
    def _():
        def tile(j, t):
            if shifted:
                last = t + 1 == n_tiles
                t1 = jnp.where(last, 0, t + 1)
                var = jnp.where(last, 1, 0)
                r0 = pl.multiple_of(t * tq, tq)
                r1 = pl.multiple_of(t1 * tq, tq)

            def store(ref, lanes, val):
                if shifted:
                    ref[0, pl.ds(r0 + DIL_HALF, DIL_HALF), lanes] = val[:DIL_HALF]
                    ref[0, pl.ds(r1, DIL_HALF), lanes] = val[DIL_HALF:]
                else:
                    ref[0, :, lanes] = val

            stats = []
            for p in range(n_pairs):
                if shifted:
                    q2 = jnp.concatenate(
                        [qT_ref[0, j, t, prow(p), DIL_HALF:], qT_ref[0, j, t1, prow(p), :DIL_HALF]], axis=1)
                    k2 = jnp.concatenate(
                        [k_ref[0, j, pl.ds(r0, tq), prow(p)], k_ref[0, j, pl.ds(r1, tq), prow(p)]], axis=0)
                    v2 = jnp.concatenate([vT_ref[0, j, t, prow(p), :], vT_ref[0, j, t1, prow(p), :]], axis=1)
                    bias = bm_ref[var, p]
                else:
                    q2 = qT_ref[0, j, 0, prow(p), :]
                    k2 = k_ref[0, j, :, prow(p)]
                    v2 = vT_ref[0, j, 0, prow(p), :]
                    bias = bm_ref[0, p]
                s_t = jnp.dot(k2, block_diag(q2), preferred_element_type=F32) + bias
                o_rows, lses = attend(s_t, v2, use_max=True)
                store(o_ref, slice(j * c + p * LANES, j * c + (p + 1) * LANES), o_rows)
                stats += lses
            store(st_ref, slice(j * LANES, (j + 1) * LANES), stats_rows(stats))

        for j in range(rb):
            if shifted:
                lax.fori_loop(0, n_tiles, lambda t, carry, j=j: (tile(j, t), carry)[1], 0)
            else:
                tile(j, 0)


def _dil_attn(q_t, k, v_t, bias, safe, group, seq):
    batch = q_t.shape[0]
    _, dilation = DIL_GROUPS[group]
    length, kwin, n_tiles = _dil_geometry(seq, dilation)
    c = DIL_HEADS * DIL_HEAD_DIM
    n_slabs = length // LANES
    rb = min(dilation, DIL_ATTN_POSITIONS // length, DIL_ATTN_CLASSES)
    kern = functools.partial(_dil_attn_kernel, n_tiles=n_tiles, rb=rb)
    return pl.pallas_call(
        kern,
        grid=(batch, dilation // rb),
        in_specs=[
            pl.BlockSpec(memory_space=pltpu.SMEM),
            pl.BlockSpec((1, rb, n_slabs, c, LANES), lambda b, r: (b, r, 0, 0, 0)),
            pl.BlockSpec((1, rb, length, c), lambda b, r: (b, r, 0, 0)),
            pl.BlockSpec((1, rb, n_slabs, c, LANES), lambda b, r: (b, r, 0, 0, 0)),
            pl.BlockSpec(bias.shape, lambda b, r: (0, 0, 0, 0)),
        ],
        out_specs=[
            pl.BlockSpec((1, length, rb * c), lambda b, r: (b, 0, r)),
            pl.BlockSpec((1, length, rb * LANES), lambda b, r: (b, 0, r)),
        ],
        out_shape=[
            jax.ShapeDtypeStruct((batch, length, dilation * c), BF16),
            jax.ShapeDtypeStruct((batch, length, dilation * LANES), F32),
        ],
        scratch_shapes=[pltpu.VMEM((2, DIL_HEADS // 2, kwin, 2 * DIL_TQ), F32)],
        compiler_params=_params(("arbitrary", "arbitrary")),
        name=f"dil_attn{group}",
    )(safe, q_t, k, v_t, bias)


def kernel(x, c, positions, ada_w, ada_b, norm_mix, norm_mlp, mlp_w1, mlp_w2, mla_w_in, mla_g_qa, mla_w_qb,
           mla_g_kva, mla_w_kvb, mla_g_q, mla_g_k, mla_w_o, dil_w_in, dil_g_q, dil_g_k, dil_w_o, rel_bias):
    batch, seq, d = x.shape
    depth = ada_w.shape[0]
    assert depth == 2 and len(DIL_GROUPS) == 3
    mod = _mod(c, ada_w, ada_b).reshape(depth, batch, N_MOD, d)
    x2d = x.reshape(batch * seq, d)

    n_heads = mla_w_qb.shape[2] // mla_g_q.shape[1]
    q_t, k_pad, v_t = _mla_proj(x2d, mod[0], norm_mix[0], positions, mla_w_in[0], mla_g_qa[0], mla_w_qb[0],
                                mla_g_kva[0], mla_w_kvb[0], mla_g_q[0], mla_g_k[0], batch, seq)
    o_t = _mla_attn(q_t, k_pad, v_t, _score_bound_is_safe(mla_g_q[0] * mla_g_k[0], mla_g_q.shape[1]), n_heads)
    x2, hn = _mla_mlp(o_t, mla_w_o[0], mlp_w1, mlp_w2, 0, x2d, mod[0], norm_mlp[0], norm_mix[1], mod[1], seq)

    outs, stats = [], []
    hn3 = hn.reshape(batch, seq, d)
    for g in range(len(DIL_GROUPS)):
        q_g, k_g, v_g = _dil_proj(hn3, dil_w_in, dil_g_q[0, g], dil_g_k[0, g], g)
        safe = _score_bound_is_safe(dil_g_q[0, g] * dil_g_k[0, g], DIL_HEAD_DIM,
                                    rel_bias[:, g * DIL_HEADS:(g + 1) * DIL_HEADS])
        o_g, st_g = _dil_attn(q_g, k_g, v_g, _dil_bias(rel_bias, g, seq), safe, g, seq)
        outs.append(o_g)
        stats.append(st_g)
    x4 = _dil_mlp(outs, stats, dil_w_o[0], mlp_w1, mlp_w2, 1, x2, mod[1], norm_mlp[1], seq)
    return x4.reshape(batch, seq, d)
```

```python
import functools
import math

import jax
import jax.numpy as jnp
from jax import lax
from jax.experimental import pallas as pl
from jax.experimental.pallas import tpu as pltpu

F32 = jnp.float32
BF16 = jnp.bfloat16

EPS = 1e-6
LOG2E = 1.4426950408889634
LN2 = 0.6931471805599453
MASKED = -1e30

N_MOD = 6
ROPE_THETA = 10000.0
MLA_ROPE = 32
DIL_GROUPS = ((128, 1), (512, 4), (2048, 16))
DIL_HEADS = 16
DIL_HEAD_DIM = 64
N_BUCKETS = 32
MAX_DISTANCE = 1024

LANES = 128
HEAD_PAD = 128

VMEM_LIMIT = 56 * 1024 * 1024

NT_DIMS = (((1,), (1,)), ((), ()))
TN_DIMS = (((0,), (0,)), ((), ()))


def _params(sem):
    return pltpu.CompilerParams(dimension_semantics=sem, vmem_limit_bytes=VMEM_LIMIT)


def _ada_norm(x, g, scale, shift):
    ms = jnp.mean(x * x, axis=-1, keepdims=True)
    return (x * lax.rsqrt(ms + EPS) * g) * (1.0 + scale) + shift


def _rms(x, g):
    ms = jnp.mean(x * x, axis=-1, keepdims=True)
    return x * lax.rsqrt(ms + EPS) * g


def _mod_kernel(c_ref, w_ref, b_ref, o_ref):
    c = c_ref[...]
    cond = c / (1.0 + jnp.exp(-c))
    o_ref[0] = (
        jnp.dot(cond.astype(BF16), w_ref[0].astype(BF16), preferred_element_type=F32) + b_ref[0]
    )


MOD_COLS = 1536


def _mod(c, ada_w, ada_b):
    depth, d, n = ada_w.shape
    b = c.shape[0]
    tn = MOD_COLS
    return pl.pallas_call(
        _mod_kernel,
        grid=(depth, n // tn),
        in_specs=[
            pl.BlockSpec((b, d), lambda l, j: (0, 0)),
            pl.BlockSpec((1, d, tn), lambda l, j: (l, 0, j)),
            pl.BlockSpec((1, 1, tn), lambda l, j: (l, 0, j)),
        ],
        out_specs=pl.BlockSpec((1, b, tn), lambda l, j: (l, 0, j)),
        out_shape=jax.ShapeDtypeStruct((depth, b, n), F32),
        compiler_params=_params(("arbitrary", "arbitrary")),
        name="mod",
    )(c, ada_w, ada_b.reshape(depth, 1, n))


def _mla_proj_kernel(x_ref, mod_ref, nmix_ref, pos_ref, inv_ref, wlat_ref, wkr_ref, gqa_ref, gkva_ref,
                     wqb_ref, wkn_ref, wv_ref, gq_ref, qT_ref, k_ref, vT_ref, *, n_heads, q_lora, qk_dim):
    x = x_ref[...]
    tm = x.shape[0]
    m = mod_ref[0]
    hb = _ada_norm(x, nmix_ref[...], m[1:2], m[0:1]).astype(BF16)
    lat = jnp.dot(hb, wlat_ref[...], preferred_element_type=F32)
    qn = _rms(lat[:, :q_lora], gqa_ref[...]).astype(BF16)
    kvn = _rms(lat[:, q_lora:], gkva_ref[...]).astype(BF16)

    half = MLA_ROPE // 2
    nope = qk_dim - MLA_ROPE
    pos = pos_ref[...].astype(F32)
    ang = jnp.concatenate([inv_ref[...]] * (tm // LANES), axis=1) * pos
    cos_t = jnp.cos(ang)
    sin_t = jnp.sin(ang)

    def rope_rows(blk):
        x1 = blk[nope:nope + half]
        x2 = blk[nope + half:qk_dim]
        return jnp.concatenate(
            [blk[:nope], x1 * cos_t - x2 * sin_t, x2 * cos_t + x1 * sin_t, blk[qk_dim:]], axis=0)

    kr_t = lax.dot_general(wkr_ref[...], hb, NT_DIMS, preferred_element_type=F32)
    kr = rope_rows(kr_t).T

    q_t = lax.dot_general(wqb_ref[...], qn, NT_DIMS, preferred_element_type=F32)
    gq = jnp.concatenate([gq_ref[...]] * (tm // LANES), axis=1)
    for h in range(n_heads):
        blk = rope_rows(q_t[h * HEAD_PAD:(h + 1) * HEAD_PAD])
        ssq = jnp.sum(blk * blk, axis=0, keepdims=True)
        r = lax.rsqrt(ssq * (1.0 / qk_dim) + EPS)
        qT_ref[0, 0, h * HEAD_PAD:(h + 1) * HEAD_PAD, :] = (blk * r * gq).astype(BF16)

    kn = jnp.dot(kvn, wkn_ref[...], preferred_element_type=F32)
    for h in range(n_heads):
        kb = kn[:, h * HEAD_PAD:(h + 1) * HEAD_PAD] + kr
        ssq = jnp.sum(kb * kb, axis=1, keepdims=True)
        r = lax.rsqrt(ssq * (1.0 / qk_dim) + EPS)
        k_ref[:, h * HEAD_PAD:(h + 1) * HEAD_PAD] = (kb * r).astype(BF16)

    vT_ref[0] = lax.dot_general(wv_ref[...], kvn, NT_DIMS, preferred_element_type=F32).astype(BF16)


def _mla_proj(x2d, mod_l, nmix, positions, w_in, g_qa, w_qb, g_kva, w_kvb, g_q, g_k, batch, seq):
    t, d = x2d.shape
    q_lora = g_qa.shape[0]
    kv_lora = g_kva.shape[0]
    qk_dim = g_q.shape[0]
    n_heads = w_qb.shape[1] // qk_dim
    nope = qk_dim - MLA_ROPE
    v_dim = w_kvb.shape[1] // n_heads - nope
    assert nope + v_dim == HEAD_PAD and qk_dim <= HEAD_PAD
    half = MLA_ROPE // 2
    tm = MIX_ROWS
    tpb = seq // tm

    w_lat = w_in[:, :q_lora + kv_lora].astype(BF16)
    w_kr = jnp.zeros((HEAD_PAD, d), F32).at[nope:qk_dim].set(w_in[:, q_lora + kv_lora:].T).astype(BF16)
    w_qb_t = jnp.pad(w_qb.T.reshape(n_heads, qk_dim, q_lora), ((0, 0), (0, HEAD_PAD - qk_dim), (0, 0)))
    w_qb_t = w_qb_t.reshape(n_heads * HEAD_PAD, q_lora).astype(BF16)
    w_kvb3 = w_kvb.reshape(kv_lora, n_heads, nope + v_dim)
    w_kn = jnp.where(jnp.arange(nope + v_dim) < nope, w_kvb3, 0.0).reshape(kv_lora, n_heads * HEAD_PAD).astype(BF16)
    w_v_t = w_kvb3[:, :, nope:].reshape(kv_lora, n_heads * v_dim).T.astype(BF16)
    q_gain = jnp.pad(g_q * g_k * (qk_dim ** -0.5 * LOG2E), (0, HEAD_PAD - qk_dim))
    gq_col = jnp.broadcast_to(q_gain[:, None], (HEAD_PAD, LANES))
    inv = 1.0 / (ROPE_THETA ** (jnp.arange(half, dtype=F32) / half))
    inv_tab = jnp.broadcast_to(inv[:, None], (half, LANES))
    pos_row = positions.reshape(1, t)

    full = lambda shape: pl.BlockSpec(shape, lambda i: (0,) * len(shape))
    kern = functools.partial(_mla_proj_kernel, n_heads=n_heads, q_lora=q_lora, qk_dim=qk_dim)
    return pl.pallas_call(
        kern,
        grid=(t // tm,),
        in_specs=[
            pl.BlockSpec((tm, d), lambda i: (i, 0)),
            pl.BlockSpec((1, N_MOD, d), lambda i: (i // tpb, 0, 0)),
            full((1, d)),
            pl.BlockSpec((1, tm), lambda i: (0, i)),
            full((half, LANES)),
            full(w_lat.shape), full(w_kr.shape), full((1, q_lora)), full((1, kv_lora)),
            full(w_qb_t.shape), full(w_kn.shape), full(w_v_t.shape),
            full((HEAD_PAD, LANES)),
        ],
        out_specs=[
            pl.BlockSpec((1, 1, n_heads * HEAD_PAD, tm), lambda i: (i // tpb, i % tpb, 0, 0)),
            pl.BlockSpec((tm, n_heads * HEAD_PAD), lambda i: (i, 0)),
            pl.BlockSpec((1, n_heads * v_dim, tm), lambda i: (i // tpb, 0, i % tpb)),
        ],
        out_shape=[
            jax.ShapeDtypeStruct((batch, tpb, n_heads * HEAD_PAD, tm), BF16),
            jax.ShapeDtypeStruct((t, n_heads * HEAD_PAD), BF16),
            jax.ShapeDtypeStruct((batch, n_heads * v_dim, seq), BF16),
        ],
        compiler_params=_params(("arbitrary",)),
        name="mla_proj",
    )(x2d, mod_l, nmix[None, :], pos_row, inv_tab, w_lat, w_kr, g_qa[None, :], g_kva[None, :],
      w_qb_t, w_kn, w_v_t, gq_col)


ONES_ROWS = 16


MLA_ATTN_COLS = 256
MLA_ATTN_KEYS = 256
SAFE_EXPONENT = 60.0


def _score_bound_is_safe(gain_product, head_dim, bias=None):
    bound = 1.02 * math.sqrt(head_dim) * LOG2E * jnp.max(jnp.abs(gain_product))
    if bias is not None:
        bound = bound + LOG2E * jnp.max(jnp.abs(bias))
    return (bound <= SAFE_EXPONENT).astype(jnp.int32).reshape(1)


def _mla_attn_kernel(safe_ref, qT_ref, k_ref, vT_ref, oT_ref, s_ref, *, hb, tn, kc):
    seq = k_ref.shape[1]
    n_tiles, _, tq = qT_ref.shape[1:]
    v_dim = vT_ref.shape[1] // hb
    ones = jnp.ones((ONES_ROWS, seq), BF16)
    n_chunks = seq // kc
    n_cols = tq // tn
    hrows = lambda h: slice(h * HEAD_PAD, (h + 1) * HEAD_PAD)
    vrows = lambda h: slice(h * v_dim, (h + 1) * v_dim)
    v_ext = lambda h: jnp.concatenate([vT_ref[0, vrows(h), :], ones], axis=0)

    @pl.when(safe_ref[0] == 1)
    def _():
        tiles = [(h, i) for h in range(hb) for i in range(n_tiles)]

        def scores(t, c):
            h, i = tiles[t]
            s_ref[t % 2, c * kc:(c + 1) * kc, :] = jnp.dot(
                k_ref[0, c * kc:(c + 1) * kc, hrows(h)], qT_ref[0, i, hrows(h), :], preferred_element_type=F32)

        for c in range(n_chunks):
            scores(0, c)
        for t, (h, i) in enumerate(tiles):
            ve = v_ext(h)
            for j in range(n_cols):
                acc = jnp.zeros((v_dim + ONES_ROWS, tn), F32)
                for c in range(n_chunks):
                    step = j * n_chunks + c
                    if t + 1 < len(tiles) and step % n_cols == 0:
                        scores(t + 1, step // n_cols)
                    p = jnp.exp2(s_ref[t % 2, c * kc:(c + 1) * kc, j * tn:(j + 1) * tn]).astype(BF16)
                    acc = acc + jnp.dot(ve[:, c * kc:(c + 1) * kc], p, preferred_element_type=F32)
                o = acc[:v_dim] * (1.0 / acc[v_dim:v_dim + 1])
                oT_ref[0, i, vrows(h), j * tn:(j + 1) * tn] = o.astype(BF16)

    @pl.when(safe_ref[0] != 1)
    def _():
        for h in range(hb):
            ve = v_ext(h)
            k_h = k_ref[0, :, hrows(h)]

            def tile(i, carry):
                s_t = jnp.dot(k_h, qT_ref[0, i, hrows(h), :], preferred_element_type=F32)
                p = jnp.exp2(s_t - jnp.max(s_t, axis=0, keepdims=True)).astype(BF16)
                r = jnp.dot(ve, p, preferred_element_type=F32)
                oT_ref[0, i, vrows(h), :] = (r[:v_dim] * (1.0 / r[v_dim:v_dim + 1])).astype(BF16)
                return carry

            lax.fori_loop(0, n_tiles, tile, 0)


def _mla_attn(q_t, k_pad, v_t, safe, n_heads):
    batch, n_tiles, _, tq = q_t.shape
    seq = n_tiles * tq
    v_dim = v_t.shape[1] // n_heads
    k3 = k_pad.reshape(batch, seq, n_heads * HEAD_PAD)
    hb = 2
    return pl.pallas_call(
        functools.partial(_mla_attn_kernel, hb=hb, tn=MLA_ATTN_COLS, kc=MLA_ATTN_KEYS),
        grid=(batch, n_heads // hb),
        in_specs=[
            pl.BlockSpec(memory_space=pltpu.SMEM),
            pl.BlockSpec((1, n_tiles, hb * HEAD_PAD, tq), lambda b, h: (b, 0, h, 0)),
            pl.BlockSpec((1, seq, hb * HEAD_PAD), lambda b, h: (b, 0, h)),
            pl.BlockSpec((1, hb * v_dim, seq), lambda b, h: (b, h, 0)),
        ],
        out_specs=pl.BlockSpec((1, n_tiles, hb * v_dim, tq), lambda b, h: (b, 0, h, 0)),
        out_shape=jax.ShapeDtypeStruct((batch, n_tiles, n_heads * v_dim, tq), BF16),
        scratch_shapes=[pltpu.VMEM((2, seq, tq), F32)],
        compiler_params=_params(("arbitrary", "arbitrary")),
        name="mla_attn",
    )(safe, q_t, k3, v_t)


MIX_ROWS = 512
MIX_SUB = 256
FF_CHUNK = 2048
LOAD_SLOTS = 4
LOAD_ROWS_FF = 64
LOAD_ROWS_D = 256


def _load_bf16(src, dst, stage, sem):
    slots, chunk = stage.shape[:2]
    n = dst.shape[0] // chunk

    def copy(i):
        return pltpu.make_async_copy(src.at[pl.ds(i * chunk, chunk), :], stage.at[i % slots], sem.at[i % slots])

    for i in range(min(slots - 1, n)):
        copy(i).start()
    for i in range(n):
        if i + slots - 1 < n:
            copy(i + slots - 1).start()
        copy(i).wait()
        dst[i * chunk:(i + 1) * chunk, :] = stage[i % slots].astype(BF16)


def _mlp_rows(h2, w1_ref, w2_ref):
    z = None
    for f in range(w1_ref.shape[1] // FF_CHUNK):
        cols = slice(f * FF_CHUNK, (f + 1) * FF_CHUNK)
        u = jnp.dot(h2, w1_ref[:, cols], preferred_element_type=F32)
        u = jnp.square(jnp.maximum(u, 0.0)).astype(BF16)
        part = jnp.dot(u, w2_ref[cols, :], preferred_element_type=F32)
        z = part if z is None else z + part
    return z


def _mix_mlp_body(mixer_out, x_ref, mod_ref, nmlp_ref, next_refs, x2_ref, hn_ref, wo_ref, w1_ref, w2_ref):
    m = mod_ref[0]
    tm = x_ref.shape[0]
    blocks = [slice(i * MIX_SUB, (i + 1) * MIX_SUB) for i in range(tm // MIX_SUB)]

    def mix(rows):
        x1 = x_ref[rows, :] + m[2:3] * mixer_out(rows, wo_ref)
        return x1, _ada_norm(x1, nmlp_ref[...], m[4:5], m[3:4]).astype(BF16)

    cur = mix(blocks[0])
    for i, rows in enumerate(blocks):
        nxt = mix(blocks[i + 1]) if i + 1 < len(blocks) else None
        x1, h2 = cur
        x2 = x1 + m[5:6] * _mlp_rows(h2, w1_ref, w2_ref)
        x2_ref[rows, :] = x2
        if next_refs is not None:
            nnext_ref, modn_ref = next_refs
            mn = modn_ref[0]
            hn_ref[rows, :] = _ada_norm(x2, nnext_ref[...], mn[1:2], mn[0:1]).astype(BF16)
        cur = nxt


def _load_mix_weights(wo_hbm, w1_hbm, w2_hbm, layer, wo_ref, w1_ref, w2_ref, stage_w, stage_n, sem):
    _load_bf16(wo_hbm, wo_ref, stage_n, sem)
    _load_bf16(w1_hbm.at[layer], w1_ref, stage_w, sem)
    _load_bf16(w2_hbm.at[layer], w2_ref, stage_n, sem)


def _mla_mlp_kernel(oT_ref, x_ref, mod_ref, nmlp_ref, nnext_ref, modn_ref, wo_hbm, w1_hbm, w2_hbm, x2_ref, hn_ref,
                    wo_ref, w1_ref, w2_ref, stage_w, stage_n, sem, *, layer):
    @pl.when(pl.program_id(0) == 0)
    def _():
        _load_mix_weights(wo_hbm, w1_hbm, w2_hbm, layer, wo_ref, w1_ref, w2_ref, stage_w, stage_n, sem)

    def mixer_out(rows, wo):
        return lax.dot_general(oT_ref[0, 0, :, rows], wo[...], TN_DIMS, preferred_element_type=F32)

    _mix_mlp_body(mixer_out, x_ref, mod_ref, nmlp_ref, (nnext_ref, modn_ref), x2_ref, hn_ref, wo_ref, w1_ref, w2_ref)


def _dil_mlp_kernel(o0_ref, o1_ref, o2_ref, s0_ref, s1_ref, s2_ref, e_ref, x_ref, mod_ref, nmlp_ref,
                    wo_hbm, w1_hbm, w2_hbm, x2_ref, wo_ref, w1_ref, w2_ref, stage_w, stage_n, sem,
                    oscr1, sscr1, oscr2, sscr2, *, layer):
    @pl.when(pl.program_id(0) == 0)
    def _():
        _load_mix_weights(wo_hbm, w1_hbm, w2_hbm, layer, wo_ref, w1_ref, w2_ref, stage_w, stage_n, sem)

    tm = x_ref.shape[0]
    c = wo_ref.shape[0]

    def token_major(o_ref, s_ref, oscr, sscr, dilation):
        ra = tm // dilation
        for r in range(dilation):
            for s in range(c // LANES):
                lo = r * c + s * LANES
                oscr[s, pl.ds(r, ra, stride=dilation), :] = o_ref[0, :, lo:lo + LANES].astype(F32)
            sscr[pl.ds(r, ra, stride=dilation), :] = s_ref[0, :, r * LANES:(r + 1) * LANES]

    token_major(o1_ref, s1_ref, oscr1, sscr1, DIL_GROUPS[1][1])
    token_major(o2_ref, s2_ref, oscr2, sscr2, DIL_GROUPS[2][1])
    e = e_ref[...]

    def expand(w):
        return jnp.dot(w.astype(BF16), e, preferred_element_type=F32)

    def mixer_out(rows, wo):
        gather = lambda oscr: jnp.concatenate([oscr[s, rows, :] for s in range(c // LANES)], axis=1)
        o0, l0 = o0_ref[0, rows, :].astype(F32), s0_ref[0, rows, :]
        o1, l1 = gather(oscr1), sscr1[rows, :]
        o2, l2 = gather(oscr2), sscr2[rows, :]
        mx = jnp.maximum(jnp.maximum(l0, l1), l2)
        e0, e1, e2 = jnp.exp(l0 - mx), jnp.exp(l1 - mx), jnp.exp(l2 - mx)
        rden = 1.0 / (e0 + e1 + e2)
        o = expand(e0 * rden) * o0 + expand(e1 * rden) * o1 + expand(e2 * rden) * o2
        return jnp.dot(o.astype(BF16), wo[...], preferred_element_type=F32)

    _mix_mlp_body(mixer_out, x_ref, mod_ref, nmlp_ref, None, x2_ref, None, wo_ref, w1_ref, w2_ref)


def _mix_mlp_call(kern, name, lead_specs, lead_args, w_o, w1, w2, x2d, mod_l, nmlp, seq, next_args, extra_scratch):
    t, d = x2d.shape
    ff = w1.shape[2]
    c = w_o.shape[0]
    tm = MIX_ROWS
    tpb = seq // tm
    row = pl.BlockSpec((tm, d), lambda i: (i, 0))
    mod_spec = pl.BlockSpec((1, N_MOD, d), lambda i: (i // tpb, 0, 0))
    vec = pl.BlockSpec((1, d), lambda i: (0, 0))
    hbm = pl.BlockSpec(memory_space=pl.ANY)
    in_specs = lead_specs + [row, mod_spec, vec]
    args = lead_args + [x2d, mod_l, nmlp[None, :]]
    out_specs, out_shape = [row], [jax.ShapeDtypeStruct((t, d), F32)]
    if next_args is not None:
        nnext, mod_next = next_args
        in_specs += [vec, mod_spec]
        args += [nnext[None, :], mod_next]
        out_specs.append(row)
        out_shape.append(jax.ShapeDtypeStruct((t, d), BF16))
    in_specs += [hbm, hbm, hbm]
    args += [w_o, w1, w2]
    scratch = [pltpu.VMEM((c, d), BF16), pltpu.VMEM((d, ff), BF16), pltpu.VMEM((ff, d), BF16),
               pltpu.VMEM((LOAD_SLOTS, LOAD_ROWS_FF, ff), F32), pltpu.VMEM((LOAD_SLOTS, LOAD_ROWS_D, d), F32),
               pltpu.SemaphoreType.DMA((LOAD_SLOTS,))]
    return pl.pallas_call(
        kern,
        grid=(t // tm,),
        in_specs=in_specs,
        out_specs=out_specs,
        out_shape=out_shape,
        scratch_shapes=scratch + extra_scratch,
        compiler_params=_params(("arbitrary",)),
        name=name,
    )(*args)


def _mla_mlp(o_t, w_o, w1, w2, layer, x2d, mod_l, nmlp, nnext, mod_next, seq):
    tm = MIX_ROWS
    tpb = seq // tm
    c = o_t.shape[2]
    assert o_t.shape[3] == tm
    lead = [pl.BlockSpec((1, 1, c, tm), lambda i: (i // tpb, i % tpb, 0, 0))]
    return _mix_mlp_call(functools.partial(_mla_mlp_kernel, layer=layer), "mla_mlp", lead, [o_t], w_o, w1, w2, x2d,
                         mod_l, nmlp, seq, (nnext, mod_next), [])


def _dil_mlp(outs, stats, w_o, w1, w2, layer, x2d, mod_l, nmlp, seq):
    tm = MIX_ROWS
    tpb = seq // tm
    c = w_o.shape[0]
    expand = jnp.repeat(jnp.eye(DIL_HEADS, dtype=F32), DIL_HEAD_DIM, axis=1)
    expand = jnp.pad(expand, ((0, LANES - DIL_HEADS), (0, 0))).astype(BF16)
    grp = lambda g, w: pl.BlockSpec((1, tm // DIL_GROUPS[g][1], DIL_GROUPS[g][1] * w),
                                    lambda i: (i // tpb, i % tpb, 0))
    lead = [grp(0, c), grp(1, c), grp(2, c), grp(0, LANES), grp(1, LANES), grp(2, LANES),
            pl.BlockSpec((LANES, c), lambda i: (0, 0))]
    extra = []
    for g in (1, 2):
        extra += [pltpu.VMEM((c // LANES, tm, LANES), F32), pltpu.VMEM((tm, LANES), F32)]
    (x2,) = _mix_mlp_call(functools.partial(_dil_mlp_kernel, layer=layer), "dil_mlp", lead, [*outs, *stats, expand],
                          w_o, w1, w2, x2d, mod_l, nmlp, seq, None, extra)
    return x2


DIL_TQ = 128
DIL_HALF = 64
DIL_ATTN_POSITIONS = 2048
DIL_ATTN_CLASSES = 8
assert all(w // (2 * d) == DIL_HALF for w, d in DIL_GROUPS)


def _dil_geometry(seq, dilation):
    length = seq // dilation
    kwin = min(2 * DIL_TQ, length)
    return length, kwin, length // DIL_TQ


def _t5_log_thresholds():
    nb = N_BUCKETS // 2
    max_exact = nb // 2
    steps = nb - max_exact
    thresholds = []
    for k in range(1, steps):
        n = max_exact
        while math.floor(math.log(n / max_exact) / math.log(MAX_DISTANCE / max_exact) * steps) < k:
            n += 1
        thresholds.append(n)
    return thresholds


def _t5_bucket(rel):
    nb = N_BUCKETS // 2
    max_exact = nb // 2
    n = jnp.abs(rel)
    large = max_exact
    for thr in _t5_log_thresholds():
        large = large + jnp.where(n >= thr, 1, 0)
    return jnp.where(rel > 0, nb, 0) + jnp.where(n < max_exact, n, large)


BIAS_ROWS = 16


def _dil_bias_kernel(tab_ref, o_ref, *, group, dilation, kwin, n_var):
    def rows(step, carry):
        r0 = pl.multiple_of(step * BIAS_ROWS, BIAS_ROWS)
        kk = lax.broadcasted_iota(jnp.int32, (BIAS_ROWS, DIL_TQ), 0) + r0
        qi = lax.broadcasted_iota(jnp.int32, (BIAS_ROWS, DIL_TQ), 1)
        rel_a = kk - qi - (DIL_HALF if n_var > 1 else 0)
        valid = jnp.abs(rel_a) <= DIL_HALF
        same_end = (qi < DIL_HALF) == (kk < DIL_TQ)
        bucket = _t5_bucket(rel_a * dilation)
        accs = [jnp.zeros((BIAS_ROWS, DIL_TQ), F32)] * DIL_HEADS
        for b in range(N_BUCKETS):
            hit = bucket == b
            accs = [jnp.where(hit, tab_ref[b, group * DIL_HEADS + h], a) for h, a in enumerate(accs)]
        for h, a in enumerate(accs):
            where = (h // 2, pl.ds(r0, BIAS_ROWS), slice((h % 2) * DIL_TQ, (h % 2 + 1) * DIL_TQ))
            tile = jnp.where(valid, a * LOG2E, MASKED)
            o_ref[(0, *where)] = tile
            if n_var > 1:
                o_ref[(1, *where)] = jnp.where(same_end, tile, MASKED)
        return carry

    lax.fori_loop(0, kwin // BIAS_ROWS, rows, 0)


def _dil_bias(rel_bias, group, seq):
    _, dilation = DIL_GROUPS[group]
    _, kwin, n_tiles = _dil_geometry(seq, dilation)
    n_var = 1 if n_tiles == 1 else 2
    n_pairs = DIL_HEADS // 2
    kern = functools.partial(_dil_bias_kernel, group=group, dilation=dilation, kwin=kwin, n_var=n_var)
    return pl.pallas_call(
        kern,
        grid=(1,),
        in_specs=[pl.BlockSpec(memory_space=pltpu.SMEM)],
        out_specs=pl.BlockSpec((n_var, n_pairs, kwin, 2 * DIL_TQ), lambda v: (0, 0, 0, 0)),
        out_shape=jax.ShapeDtypeStruct((n_var, n_pairs, kwin, 2 * DIL_TQ), F32),
        compiler_params=_params(("arbitrary",)),
        name=f"dil_bias{group}",
    )(rel_bias)


DIL_PROJ_ROWS = 512
DIL_PROJ_ROWS_WIDE = 1024
DIL_PROJ_CLASSES = 4


def _dil_proj_kernel(h_ref, wqf_ref, wkf_ref, wvf_ref, gq_ref, qT_ref, k_ref, vT_ref, wq_ref, wk_ref, wv_ref,
                     *scratch, dilation, rb, ra):
    hd = DIL_HEAD_DIM
    d_model = h_ref.shape[2]
    n_slabs = d_model // LANES
    mt = rb * ra

    @pl.when((pl.program_id(0) == 0) & (pl.program_id(1) == 0) & (pl.program_id(2) == 0))
    def _():
        wq_ref[...] = wqf_ref[0].T.astype(BF16)
        wk_ref[...] = wkf_ref[0].astype(BF16)
        wv_ref[...] = wvf_ref[0].T.astype(BF16)

    if dilation == 1:
        hcat = h_ref[0]
    else:
        (scr,) = scratch
        sub = pl.program_id(2)

        @pl.when(sub == 0)
        def _():
            for s in range(n_slabs):
                scr[s] = h_ref[0, :, s * LANES:(s + 1) * LANES].astype(F32)

        rows = []
        for j in range(rb):
            res = sub * rb + j
            rows.append(jnp.concatenate(
                [scr[s, pl.ds(res, ra, stride=dilation), :] for s in range(n_slabs)], axis=1))
        hcat = jnp.concatenate(rows, axis=0).astype(BF16)

    q_t = lax.dot_general(wq_ref[...], hcat, NT_DIMS, preferred_element_type=F32)
    gq = jnp.concatenate([gq_ref[...]] * (mt // LANES), axis=1)
    for h in range(DIL_HEADS):
        blk = q_t[h * hd:(h + 1) * hd]
        r = lax.rsqrt(jnp.sum(blk * blk, axis=0, keepdims=True) * (1.0 / hd) + EPS)
        qb = (blk * r * gq).astype(BF16)
        for j in range(rb):
            for s in range(ra // LANES):
                lo = j * ra + s * LANES
                qT_ref[0, j, s, h * hd:(h + 1) * hd, :] = qb[:, lo:lo + LANES]

    kf = jnp.dot(hcat, wk_ref[...], preferred_element_type=F32)
    lane = lax.broadcasted_iota(jnp.int32, (mt, LANES), 1)
    low = lane < hd
    for c in range(DIL_HEADS * hd // LANES):
        y = kf[:, c * LANES:(c + 1) * LANES]
        y2 = y * y
        s_lo = jnp.sum(jnp.where(low, y2, 0.0), axis=1, keepdims=True)
        s_hi = jnp.sum(jnp.where(low, 0.0, y2), axis=1, keepdims=True)
        r = lax.rsqrt(jnp.where(low, s_lo, s_hi) * (1.0 / hd) + EPS)
        kb = (y * r).astype(BF16)
        for j in range(rb):
            k_ref[0, j, :, c * LANES:(c + 1) * LANES] = kb[j * ra:(j + 1) * ra]

    v_t = lax.dot_general(wv_ref[...], hcat, NT_DIMS, preferred_element_type=F32).astype(BF16)
    for j in range(rb):
        for s in range(ra // LANES):
            lo = j * ra + s * LANES
            vT_ref[0, j, s] = v_t[:, lo:lo + LANES]


def _dil_proj(hn3, w_in, g_q, g_k, group):
    batch, seq, d_model = hn3.shape
    _, dilation = DIL_GROUPS[group]
    length = seq // dilation
    c = DIL_HEADS * DIL_HEAD_DIM
    rb = min(dilation, DIL_PROJ_CLASSES if dilation <= DIL_PROJ_CLASSES else 2 * DIL_PROJ_CLASSES)
    rows = DIL_PROJ_ROWS_WIDE
    ra = rows // rb
    tok = ra * dilation
    n_sub = dilation // rb
    gq_col = jnp.broadcast_to((g_q * g_k * (DIL_HEAD_DIM ** -0.5 * LOG2E))[:, None], (DIL_HEAD_DIM, LANES))
    kern = functools.partial(_dil_proj_kernel, dilation=dilation, rb=rb, ra=ra)
    const = lambda shape: pl.BlockSpec(shape, lambda b, t, s: (0,) * len(shape))
    w_blk = lambda j: pl.BlockSpec((1, d_model, c), lambda b, t, s: (0, 0, 3 * group + j))
    scratch = [pltpu.VMEM((c, d_model), BF16), pltpu.VMEM((d_model, c), BF16), pltpu.VMEM((c, d_model), BF16)]
    if dilation > 1:
        scratch.append(pltpu.VMEM((d_model // LANES, tok, LANES), F32))
    return pl.pallas_call(
        kern,
        grid=(batch, seq // tok, n_sub),
        in_specs=[
            pl.BlockSpec((1, tok, d_model), lambda b, t, s: (b, t, 0)),
            w_blk(0), w_blk(1), w_blk(2),
            const((DIL_HEAD_DIM, LANES)),
        ],
        out_specs=[
            pl.BlockSpec((1, rb, ra // LANES, c, LANES), lambda b, t, s: (b, s, t, 0, 0)),
            pl.BlockSpec((1, rb, ra, c), lambda b, t, s: (b, s, t, 0)),
            pl.BlockSpec((1, rb, ra // LANES, c, LANES), lambda b, t, s: (b, s, t, 0, 0)),
        ],
        out_shape=[
            jax.ShapeDtypeStruct((batch, dilation, length // LANES, c, LANES), BF16),
            jax.ShapeDtypeStruct((batch, dilation, length, c), BF16),
            jax.ShapeDtypeStruct((batch, dilation, length // LANES, c, LANES), BF16),
        ],
        scratch_shapes=scratch,
        compiler_params=_params(("arbitrary", "arbitrary", "arbitrary")),
        name=f"dil_proj{group}",
    )(hn3, w_in, w_in, w_in, gq_col)


def _dil_attn_kernel(safe_ref, qT_ref, k_ref, vT_ref, bm_ref, o_ref, st_ref, s_ref, *, n_tiles, rb):
    tq = DIL_TQ
    hd = DIL_HEAD_DIM
    n_pairs = DIL_HEADS // 2
    c = DIL_HEADS * hd
    shifted = n_tiles > 1
    low = lax.broadcasted_iota(jnp.int32, (LANES, tq), 0) < hd
    ones = jnp.ones((ONES_ROWS, 2 * tq if shifted else tq), BF16)
    prow = lambda p: slice(p * LANES, (p + 1) * LANES)

    def block_diag(q2):
        zero = jnp.zeros_like(q2)
        return jnp.concatenate([jnp.where(low, q2, zero), jnp.where(low, zero, q2)], axis=1)

    def attend(s_t, v2, use_max):
        if use_max:
            m = jnp.max(s_t, axis=0, keepdims=True)
            pb = jnp.exp2(s_t - m).astype(BF16)
        else:
            pb = jnp.exp2(s_t).astype(BF16)
        res = jnp.dot(jnp.concatenate([v2, ones], axis=0), pb, preferred_element_type=F32)
        den = res[LANES:LANES + 1]
        rinv = 1.0 / den
        o_pair = jnp.concatenate([res[:hd, :tq] * rinv[:, :tq], res[hd:LANES, tq:] * rinv[:, tq:]], axis=0)
        lse = jnp.log2(den) + m if use_max else jnp.log2(den)
        lse = lse * LN2
        return o_pair.T.astype(BF16), [lse[:, :tq], lse[:, tq:]]

    def stats_rows(stats):
        return jnp.concatenate(stats + [jnp.zeros((LANES - DIL_HEADS, tq), F32)], axis=0).T

    tiles = [(j, t) for j in range(rb) for t in range(n_tiles)]

    @pl.when(safe_ref[0] == 1)
    def _():
        def score_tile(i):
            j, t = tiles[i]
            t1 = (t + 1) % n_tiles
            for p in range(n_pairs):
                if shifted:
                    q2 = jnp.concatenate(
                        [qT_ref[0, j, t, prow(p), DIL_HALF:], qT_ref[0, j, t1, prow(p), :DIL_HALF]], axis=1)
                    k2 = jnp.concatenate([k_ref[0, j, t * tq:(t + 1) * tq, prow(p)],
                                          k_ref[0, j, t1 * tq:(t1 + 1) * tq, prow(p)]], axis=0)
                else:
                    q2 = qT_ref[0, j, 0, prow(p), :]
                    k2 = k_ref[0, j, :, prow(p)]
                s_ref[i % 2, p] = jnp.dot(k2, block_diag(q2), preferred_element_type=F32)

        def attend_tile(i):
            j, t = tiles[i]
            t1 = (t + 1) % n_tiles
            var = 1 if shifted and t + 1 == n_tiles else 0

            def store(ref, lanes, val):
                if shifted:
                    ref[0, t * tq + DIL_HALF:(t + 1) * tq, lanes] = val[:DIL_HALF]
                    ref[0, t1 * tq:t1 * tq + DIL_HALF, lanes] = val[DIL_HALF:]
                else:
                    ref[0, :, lanes] = val

            stats = []
            for p in range(n_pairs):
                if shifted:
                    v2 = jnp.concatenate([vT_ref[0, j, t, prow(p), :], vT_ref[0, j, t1, prow(p), :]], axis=1)
                else:
                    v2 = vT_ref[0, j, 0, prow(p), :]
                o_rows, lses = attend(s_ref[i % 2, p] + bm_ref[var, p], v2, use_max=False)
                store(o_ref, slice(j * c + p * LANES, j * c + (p + 1) * LANES), o_rows)
                stats += lses
            store(st_ref, slice(j * LANES, (j + 1) * LANES), stats_rows(stats))

        score_tile(0)
        for i in range(len(tiles)):
            if i + 1 < len(tiles):
                score_tile(i + 1)
            attend_tile(i)

    @pl.when(safe_ref[0] != 1)
    def _():
        def tile(j, t):
            if shifted:
                last = t + 1 == n_tiles
                t1 = jnp.where(last, 0, t + 1)
                var = jnp.where(last, 1, 0)
                r0 = pl.multiple_of(t * tq, tq)
                r1 = pl.multiple_of(t1 * tq, tq)

            def store(ref, lanes, val):
                if shifted:
                    ref[0, pl.ds(r0 + DIL_HALF, DIL_HALF), lanes] = val[:DIL_HALF]
                    ref[0, pl.ds(r1, DIL_HALF), lanes] = val[DIL_HALF:]
                else:
                    ref[0, :, lanes] = val

            stats = []
            for p in range(n_pairs):
                if shifted:
                    q2 = jnp.concatenate(
                        [qT_ref[0, j, t, prow(p), DIL_HALF:], qT_ref[0, j, t1, prow(p), :DIL_HALF]], axis=1)
                    k2 = jnp.concatenate(
                        [k_ref[0, j, pl.ds(r0, tq), prow(p)], k_ref[0, j, pl.ds(r1, tq), prow(p)]], axis=0)
                    v2 = jnp.concatenate([vT_ref[0, j, t, prow(p), :], vT_ref[0, j, t1, prow(p), :]], axis=1)
                    bias = bm_ref[var, p]
                else:
                    q2 = qT_ref[0, j, 0, prow(p), :]
                    k2 = k_ref[0, j, :, prow(p)]
                    v2 = vT_ref[0, j, 0, prow(p), :]
                    bias = bm_ref[0, p]
                s_t = jnp.dot(k2, block_diag(q2), preferred_element_type=F32) + bias
                o_rows, lses = attend(s_t, v2, use_max=True)
                store(o_ref, slice(j * c + p * LANES, j * c + (p + 1) * LANES), o_rows)
                stats += lses
            store(st_ref, slice(j * LANES, (j + 1) * LANES), stats_rows(stats))

        for j in range(rb):
            if shifted:
                lax.fori_loop(0, n_tiles, lambda t, carry, j=j: (tile(j, t), carry)[1], 0)
            else:
                tile(j, 0)


def _dil_attn(q_t, k, v_t, bias, safe, group, seq):
    batch = q_t.shape[0]
    _, dilation = DIL_GROUPS[group]
    length, kwin, n_tiles = _dil_geometry(seq, dilation)
    c = DIL_HEADS * DIL_HEAD_DIM
    n_slabs = length // LANES
    rb = min(dilation, DIL_ATTN_POSITIONS // length, DIL_ATTN_CLASSES)
    kern = functools.partial(_dil_attn_kernel, n_tiles=n_tiles, rb=rb)
    return pl.pallas_call(
        kern,
        grid=(batch, dilation // rb),
        in_specs=[
            pl.BlockSpec(memory_space=pltpu.SMEM),
            pl.BlockSpec((1, rb, n_slabs, c, LANES), lambda b, r: (b, r, 0, 0, 0)),
            pl.BlockSpec((1, rb, length, c), lambda b, r: (b, r, 0, 0)),
            pl.BlockSpec((1, rb, n_slabs, c, LANES), lambda b, r: (b, r, 0, 0, 0)),
            pl.BlockSpec(bias.shape, lambda b, r: (0, 0, 0, 0)),
        ],
        out_specs=[
            pl.BlockSpec((1, length, rb * c), lambda b, r: (b, 0, r)),
            pl.BlockSpec((1, length, rb * LANES), lambda b, r: (b, 0, r)),
        ],
        out_shape=[
            jax.ShapeDtypeStruct((batch, length, dilation * c), BF16),
            jax.ShapeDtypeStruct((batch, length, dilation * LANES), F32),
        ],
        scratch_shapes=[pltpu.VMEM((2, DIL_HEADS // 2, kwin, 2 * DIL_TQ), F32)],
        compiler_params=_params(("arbitrary", "arbitrary")),
        name=f"dil_attn{group}",
    )(safe, q_t, k, v_t, bias)


def kernel(x, c, positions, ada_w, ada_b, norm_mix, norm_mlp, mlp_w1, mlp_w2, mla_w_in, mla_g_qa, mla_w_qb,
           mla_g_kva, mla_w_kvb, mla_g_q, mla_g_k, mla_w_o, dil_w_in, dil_g_q, dil_g_k, dil_w_o, rel_bias):
    batch, seq, d = x.shape
    depth = ada_w.shape[0]
    assert depth == 2 and len(DIL_GROUPS) == 3
    mod = _mod(c, ada_w, ada_b).reshape(depth, batch, N_MOD, d)
    x2d = x.reshape(batch * seq, d)

    n_heads = mla_w_qb.shape[2] // mla_g_q.shape[1]
    q_t, k_pad, v_t = _mla_proj(x2d, mod[0], norm_mix[0], positions, mla_w_in[0], mla_g_qa[0], mla_w_qb[0],
                                mla_g_kva[0], mla_w_kvb[0], mla_g_q[0], mla_g_k[0], batch, seq)
    o_t = _mla_attn(q_t, k_pad, v_t, _score_bound_is_safe(mla_g_q[0] * mla_g_k[0], mla_g_q.shape[1]), n_heads)
    x2, hn = _mla_mlp(o_t, mla_w_o[0], mlp_w1, mlp_w2, 0, x2d, mod[0], norm_mlp[0], norm_mix[1], mod[1], seq)

    outs, stats = [], []
    hn3 = hn.reshape(batch, seq, d)
    for g in range(len(DIL_GROUPS)):
        q_g, k_g, v_g = _dil_proj(hn3, dil_w_in, dil_g_q[0, g], dil_g_k[0, g], g)
        safe = _score_bound_is_safe(dil_g_q[0, g] * dil_g_k[0, g], DIL_HEAD_DIM,
                                    rel_bias[:, g * DIL_HEADS:(g + 1) * DIL_HEADS])
        o_g, st_g = _dil_attn(q_g, k_g, v_g, _dil_bias(rel_bias, g, seq), safe, g, seq)
        outs.append(o_g)
        stats.append(st_g)
    x4 = _dil_mlp(outs, stats, dil_w_o[0], mlp_w1, mlp_w2, 1, x2, mod[1], norm_mlp[1], seq)
    return x4.reshape(batch, seq, d)
```

```python
import functools
import math

import jax
import jax.numpy as jnp
from jax import lax
from jax.experimental import pallas as pl
from jax.experimental.pallas import tpu as pltpu

F32 = jnp.float32
BF16 = jnp.bfloat16

EPS = 1e-6
LOG2E = 1.4426950408889634
LN2 = 0.6931471805599453
MASKED = -1e30

N_MOD = 6
ROPE_THETA = 10000.0
MLA_ROPE = 32
DIL_GROUPS = ((128, 1), (512, 4), (2048, 16))
DIL_HEADS = 16
DIL_HEAD_DIM = 64
N_BUCKETS = 32
MAX_DISTANCE = 1024

LANES = 128
HEAD_PAD = 128

VMEM_LIMIT = 56 * 1024 * 1024

NT_DIMS = (((1,), (1,)), ((), ()))
TN_DIMS = (((0,), (0,)), ((), ()))


def _params(sem):
    return pltpu.CompilerParams(dimension_semantics=sem, vmem_limit_bytes=VMEM_LIMIT)


def _ada_norm(x, g, scale, shift):
    ms = jnp.mean(x * x, axis=-1, keepdims=True)
    return (x * lax.rsqrt(ms + EPS) * g) * (1.0 + scale) + shift


def _rms(x, g):
    ms = jnp.mean(x * x, axis=-1, keepdims=True)
    return x * lax.rsqrt(ms + EPS) * g


def _mod_kernel(c_ref, w_ref, b_ref, o_ref):
    c = c_ref[...]
    cond = c / (1.0 + jnp.exp(-c))
    o_ref[0] = (
        jnp.dot(cond.astype(BF16), w_ref[0].astype(BF16), preferred_element_type=F32) + b_ref[0]
    )


MOD_COLS = 1536


def _mod(c, ada_w, ada_b):
    depth, d, n = ada_w.shape
    b = c.shape[0]
    tn = MOD_COLS
    return pl.pallas_call(
        _mod_kernel,
        grid=(depth, n // tn),
        in_specs=[
            pl.BlockSpec((b, d), lambda l, j: (0, 0)),
            pl.BlockSpec((1, d, tn), lambda l, j: (l, 0, j)),
            pl.BlockSpec((1, 1, tn), lambda l, j: (l, 0, j)),
        ],
        out_specs=pl.BlockSpec((1, b, tn), lambda l, j: (l, 0, j)),
        out_shape=jax.ShapeDtypeStruct((depth, b, n), F32),
        compiler_params=_params(("arbitrary", "arbitrary")),
        name="mod",
    )(c, ada_w, ada_b.reshape(depth, 1, n))


def _mla_proj_kernel(x_ref, mod_ref, nmix_ref, pos_ref, inv_ref, wlat_ref, wkr_ref, gqa_ref, gkva_ref,
                     wqb_ref, wkn_ref, wv_ref, gq_ref, qT_ref, k_ref, vT_ref, *, n_heads, q_lora, qk_dim):
    x = x_ref[...]
    tm = x.shape[0]
    m = mod_ref[0]
    hb = _ada_norm(x, nmix_ref[...], m[1:2], m[0:1]).astype(BF16)
    lat = jnp.dot(hb, wlat_ref[...], preferred_element_type=F32)
    qn = _rms(lat[:, :q_lora], gqa_ref[...]).astype(BF16)
    kvn = _rms(lat[:, q_lora:], gkva_ref[...]).astype(BF16)

    half = MLA_ROPE // 2
    nope = qk_dim - MLA_ROPE
    pos = pos_ref[...].astype(F32)
    ang = jnp.concatenate([inv_ref[...]] * (tm // LANES), axis=1) * pos
    cos_t = jnp.cos(ang)
    sin_t = jnp.sin(ang)

    def rope_rows(blk):
        x1 = blk[nope:nope + half]
        x2 = blk[nope + half:qk_dim]
        return jnp.concatenate(
            [blk[:nope], x1 * cos_t - x2 * sin_t, x2 * cos_t + x1 * sin_t, blk[qk_dim:]], axis=0)

    kr_t = lax.dot_general(wkr_ref[...], hb, NT_DIMS, preferred_element_type=F32)
    kr = rope_rows(kr_t).T

    q_t = lax.dot_general(wqb_ref[...], qn, NT_DIMS, preferred_element_type=F32)
    gq = jnp.concatenate([gq_ref[...]] * (tm // LANES), axis=1)
    for h in range(n_heads):
        blk = rope_rows(q_t[h * HEAD_PAD:(h + 1) * HEAD_PAD])
        ssq = jnp.sum(blk * blk, axis=0, keepdims=True)
        r = lax.rsqrt(ssq * (1.0 / qk_dim) + EPS)
        qT_ref[0, 0, h * HEAD_PAD:(h + 1) * HEAD_PAD, :] = (blk * r * gq).astype(BF16)

    kn = jnp.dot(kvn, wkn_ref[...], preferred_element_type=F32)
    for h in range(n_heads):
        kb = kn[:, h * HEAD_PAD:(h + 1) * HEAD_PAD] + kr
        ssq = jnp.sum(kb * kb, axis=1, keepdims=True)
        r = lax.rsqrt(ssq * (1.0 / qk_dim) + EPS)
        k_ref[:, h * HEAD_PAD:(h + 1) * HEAD_PAD] = (kb * r).astype(BF16)

    vT_ref[0] = lax.dot_general(wv_ref[...], kvn, NT_DIMS, preferred_element_type=F32).astype(BF16)


def _mla_proj(x2d, mod_l, nmix, positions, w_in, g_qa, w_qb, g_kva, w_kvb, g_q, g_k, batch, seq):
    t, d = x2d.shape
    q_lora = g_qa.shape[0]
    kv_lora = g_kva.shape[0]
    qk_dim = g_q.shape[0]
    n_heads = w_qb.shape[1] // qk_dim
    nope = qk_dim - MLA_ROPE
    v_dim = w_kvb.shape[1] // n_heads - nope
    assert nope + v_dim == HEAD_PAD and qk_dim <= HEAD_PAD
    half = MLA_ROPE // 2
    tm = MIX_ROWS
    tpb = seq // tm

    w_lat = w_in[:, :q_lora + kv_lora].astype(BF16)
    w_kr = jnp.zeros((HEAD_PAD, d), F32).at[nope:qk_dim].set(w_in[:, q_lora + kv_lora:].T).astype(BF16)
    w_qb_t = jnp.pad(w_qb.T.reshape(n_heads, qk_dim, q_lora), ((0, 0), (0, HEAD_PAD - qk_dim), (0, 0)))
    w_qb_t = w_qb_t.reshape(n_heads * HEAD_PAD, q_lora).astype(BF16)
    w_kvb3 = w_kvb.reshape(kv_lora, n_heads, nope + v_dim)
    w_kn = jnp.where(jnp.arange(nope + v_dim) < nope, w_kvb3, 0.0).reshape(kv_lora, n_heads * HEAD_PAD).astype(BF16)
    w_v_t = w_kvb3[:, :, nope:].reshape(kv_lora, n_heads * v_dim).T.astype(BF16)
    q_gain = jnp.pad(g_q * g_k * (qk_dim ** -0.5 * LOG2E), (0, HEAD_PAD - qk_dim))
    gq_col = jnp.broadcast_to(q_gain[:, None], (HEAD_PAD, LANES))
    inv = 1.0 / (ROPE_THETA ** (jnp.arange(half, dtype=F32) / half))
    inv_tab = jnp.broadcast_to(inv[:, None], (half, LANES))
    pos_row = positions.reshape(1, t)

    full = lambda shape: pl.BlockSpec(shape, lambda i: (0,) * len(shape))
    kern = functools.partial(_mla_proj_kernel, n_heads=n_heads, q_lora=q_lora, qk_dim=qk_dim)
    return pl.pallas_call(
        kern,
        grid=(t // tm,),
        in_specs=[
            pl.BlockSpec((tm, d), lambda i: (i, 0)),
            pl.BlockSpec((1, N_MOD, d), lambda i: (i // tpb, 0, 0)),
            full((1, d)),
            pl.BlockSpec((1, tm), lambda i: (0, i)),
            full((half, LANES)),
            full(w_lat.shape), full(w_kr.shape), full((1, q_lora)), full((1, kv_lora)),
            full(w_qb_t.shape), full(w_kn.shape), full(w_v_t.shape),
            full((HEAD_PAD, LANES)),
        ],
        out_specs=[
            pl.BlockSpec((1, 1, n_heads * HEAD_PAD, tm), lambda i: (i // tpb, i % tpb, 0, 0)),
            pl.BlockSpec((tm, n_heads * HEAD_PAD), lambda i: (i, 0)),
            pl.BlockSpec((1, n_heads * v_dim, tm), lambda i: (i // tpb, 0, i % tpb)),
        ],
        out_shape=[
            jax.ShapeDtypeStruct((batch, tpb, n_heads * HEAD_PAD, tm), BF16),
            jax.ShapeDtypeStruct((t, n_heads * HEAD_PAD), BF16),
            jax.ShapeDtypeStruct((batch, n_heads * v_dim, seq), BF16),
        ],
        compiler_params=_params(("arbitrary",)),
        name="mla_proj",
    )(x2d, mod_l, nmix[None, :], pos_row, inv_tab, w_lat, w_kr, g_qa[None, :], g_kva[None, :],
      w_qb_t, w_kn, w_v_t, gq_col)


ONES_ROWS = 16


MLA_ATTN_COLS = 256
MLA_ATTN_KEYS = 256
SAFE_EXPONENT = 60.0


def _score_bound_is_safe(gain_product, head_dim, bias=None):
    bound = 1.02 * math.sqrt(head_dim) * LOG2E * jnp.max(jnp.abs(gain_product))
    if bias is not None:
        bound = bound + LOG2E * jnp.max(jnp.abs(bias))
    return (bound <= SAFE_EXPONENT).astype(jnp.int32).reshape(1)


def _mla_attn_kernel(safe_ref, qT_ref, k_ref, vT_ref, oT_ref, s_ref, *, hb, tn, kc):
    seq = k_ref.shape[1]
    n_tiles, _, tq = qT_ref.shape[1:]
    v_dim = vT_ref.shape[1] // hb
    ones = jnp.ones((ONES_ROWS, seq), BF16)
    n_chunks = seq // kc
    n_cols = tq // tn
    hrows = lambda h: slice(h * HEAD_PAD, (h + 1) * HEAD_PAD)
    vrows = lambda h: slice(h * v_dim, (h + 1) * v_dim)
    v_ext = lambda h: jnp.concatenate([vT_ref[0, vrows(h), :], ones], axis=0)

    @pl.when(safe_ref[0] == 1)
    def _():
        tiles = [(h, i) for h in range(hb) for i in range(n_tiles)]

        def scores(t, c):
            h, i = tiles[t]
            s_ref[t % 2, c * kc:(c + 1) * kc, :] = jnp.dot(
                k_ref[0, c * kc:(c + 1) * kc, hrows(h)], qT_ref[0, i, hrows(h), :], preferred_element_type=F32)

        for c in range(n_chunks):
            scores(0, c)
        for t, (h, i) in enumerate(tiles):
            ve = v_ext(h)
            for j in range(n_cols):
                acc = jnp.zeros((v_dim + ONES_ROWS, tn), F32)
                for c in range(n_chunks):
                    step = j * n_chunks + c
                    if t + 1 < len(tiles) and step % n_cols == 0:
                        scores(t + 1, step // n_cols)
                    p = jnp.exp2(s_ref[t % 2, c * kc:(c + 1) * kc, j * tn:(j + 1) * tn]).astype(BF16)
                    acc = acc + jnp.dot(ve[:, c * kc:(c + 1) * kc], p, preferred_element_type=F32)
                o = acc[:v_dim] * (1.0 / acc[v_dim:v_dim + 1])
                oT_ref[0, i, vrows(h), j * tn:(j + 1) * tn] = o.astype(BF16)

    @pl.when(safe_ref[0] != 1)
    def _():
        for h in range(hb):
            ve = v_ext(h)
            k_h = k_ref[0, :, hrows(h)]

            def tile(i, carry):
                s_t = jnp.dot(k_h, qT_ref[0, i, hrows(h), :], preferred_element_type=F32)
                p = jnp.exp2(s_t - jnp.max(s_t, axis=0, keepdims=True)).astype(BF16)
                r = jnp.dot(ve, p, preferred_element_type=F32)
                oT_ref[0, i, vrows(h), :] = (r[:v_dim] * (1.0 / r[v_dim:v_dim + 1])).astype(BF16)
                return carry

            lax.fori_loop(0, n_tiles, tile, 0)


def _mla_attn(q_t, k_pad, v_t, safe, n_heads):
    batch, n_tiles, _, tq = q_t.shape
    seq = n_tiles * tq
    v_dim = v_t.shape[1] // n_heads
    k3 = k_pad.reshape(batch, seq, n_heads * HEAD_PAD)
    hb = 2
    return pl.pallas_call(
        functools.partial(_mla_attn_kernel, hb=hb, tn=MLA_ATTN_COLS, kc=MLA_ATTN_KEYS),
        grid=(batch, n_heads // hb),
        in_specs=[
            pl.BlockSpec(memory_space=pltpu.SMEM),
            pl.BlockSpec((1, n_tiles, hb * HEAD_PAD, tq), lambda b, h: (b, 0, h, 0)),
            pl.BlockSpec((1, seq, hb * HEAD_PAD), lambda b, h: (b, 0, h)),
            pl.BlockSpec((1, hb * v_dim, seq), lambda b, h: (b, h, 0)),
        ],
        out_specs=pl.BlockSpec((1, n_tiles, hb * v_dim, tq), lambda b, h: (b, 0, h, 0)),
        out_shape=jax.ShapeDtypeStruct((batch, n_tiles, n_heads * v_dim, tq), BF16),
        scratch_shapes=[pltpu.VMEM((2, seq, tq), F32)],
        compiler_params=_params(("arbitrary", "arbitrary")),
        name="mla_attn",
    )(safe, q_t, k3, v_t)


MIX_ROWS = 512
MIX_SUB = 256
FF_CHUNK = 2048
LOAD_SLOTS = 4
LOAD_ROWS_FF = 64
LOAD_ROWS_D = 256


def _load_bf16(src, dst, stage, sem):
    slots, chunk = stage.shape[:2]
    n = dst.shape[0] // chunk

    def copy(i):
        return pltpu.make_async_copy(src.at[pl.ds(i * chunk, chunk), :], stage.at[i % slots], sem.at[i % slots])

    for i in range(min(slots - 1, n)):
        copy(i).start()
    for i in range(n):
        if i + slots - 1 < n:
            copy(i + slots - 1).start()
        copy(i).wait()
        dst[i * chunk:(i + 1) * chunk, :] = stage[i % slots].astype(BF16)


def _mlp_rows(h2, w1_ref, w2_ref):
    z = None
    for f in range(w1_ref.shape[1] // FF_CHUNK):
        cols = slice(f * FF_CHUNK, (f + 1) * FF_CHUNK)
        u = jnp.dot(h2, w1_ref[:, cols], preferred_element_type=F32)
        u = jnp.square(jnp.maximum(u, 0.0)).astype(BF16)
        part = jnp.dot(u, w2_ref[cols, :], preferred_element_type=F32)
        z = part if z is None else z + part
    return z


def _mix_mlp_body(mixer_out, x_ref, mod_ref, nmlp_ref, next_refs, x2_ref, hn_ref, wo_ref, w1_ref, w2_ref):
    m = mod_ref[0]
    tm = x_ref.shape[0]
    blocks = [slice(i * MIX_SUB, (i + 1) * MIX_SUB) for i in range(tm // MIX_SUB)]

    def mix(rows):
        x1 = x_ref[rows, :] + m[2:3] * mixer_out(rows, wo_ref)
        return x1, _ada_norm(x1, nmlp_ref[...], m[4:5], m[3:4]).astype(BF16)

    cur = mix(blocks[0])
    for i, rows in enumerate(blocks):
        nxt = mix(blocks[i + 1]) if i + 1 < len(blocks) else None
        x1, h2 = cur
        x2 = x1 + m[5:6] * _mlp_rows(h2, w1_ref, w2_ref)
        x2_ref[rows, :] = x2
        if next_refs is not None:
            nnext_ref, modn_ref = next_refs
            mn = modn_ref[0]
            hn_ref[rows, :] = _ada_norm(x2, nnext_ref[...], mn[1:2], mn[0:1]).astype(BF16)
        cur = nxt


def _load_mix_weights(wo_hbm, w1_hbm, w2_hbm, layer, wo_ref, w1_ref, w2_ref, stage_w, stage_n, sem):
    _load_bf16(wo_hbm, wo_ref, stage_n, sem)
    _load_bf16(w1_hbm.at[layer], w1_ref, stage_w, sem)
    _load_bf16(w2_hbm.at[layer], w2_ref, stage_n, sem)


def _mla_mlp_kernel(oT_ref, x_ref, mod_ref, nmlp_ref, nnext_ref, modn_ref, wo_hbm, w1_hbm, w2_hbm, x2_ref, hn_ref,
                    wo_ref, w1_ref, w2_ref, stage_w, stage_n, sem, *, layer):
    @pl.when(pl.program_id(0) == 0)
    def _():
        _load_mix_weights(wo_hbm, w1_hbm, w2_hbm, layer, wo_ref, w1_ref, w2_ref, stage_w, stage_n, sem)

    def mixer_out(rows, wo):
        return lax.dot_general(oT_ref[0, 0, :, rows], wo[...], TN_DIMS, preferred_element_type=F32)

    _mix_mlp_body(mixer_out, x_ref, mod_ref, nmlp_ref, (nnext_ref, modn_ref), x2_ref, hn_ref, wo_ref, w1_ref, w2_ref)


def _dil_mlp_kernel(o0_ref, o1_ref, o2_ref, s0_ref, s1_ref, s2_ref, e_ref, x_ref, mod_ref, nmlp_ref,
                    wo_hbm, w1_hbm, w2_hbm, x2_ref, wo_ref, w1_ref, w2_ref, stage_w, stage_n, sem,
                    oscr1, sscr1, oscr2, sscr2, *, layer):
    @pl.when(pl.program_id(0) == 0)
    def _():
        _load_mix_weights(wo_hbm, w1_hbm, w2_hbm, layer, wo_ref, w1_ref, w2_ref, stage_w, stage_n, sem)

    tm = x_ref.shape[0]
    c = wo_ref.shape[0]

    def token_major(o_ref, s_ref, oscr, sscr, dilation):
        ra = tm // dilation
        for r in range(dilation):
            for s in range(c // LANES):
                lo = r * c + s * LANES
                oscr[s, pl.ds(r, ra, stride=dilation), :] = o_ref[0, :, lo:lo + LANES].astype(F32)
            sscr[pl.ds(r, ra, stride=dilation), :] = s_ref[0, :, r * LANES:(r + 1) * LANES]

    token_major(o1_ref, s1_ref, oscr1, sscr1, DIL_GROUPS[1][1])
    token_major(o2_ref, s2_ref, oscr2, sscr2, DIL_GROUPS[2][1])
    e = e_ref[...]

    def expand(w):
        return jnp.dot(w.astype(BF16), e, preferred_element_type=F32)

    def mixer_out(rows, wo):
        gather = lambda oscr: jnp.concatenate([oscr[s, rows, :] for s in range(c // LANES)], axis=1)
        o0, l0 = o0_ref[0, rows, :].astype(F32), s0_ref[0, rows, :]
        o1, l1 = gather(oscr1), sscr1[rows, :]
        o2, l2 = gather(oscr2), sscr2[rows, :]
        mx = jnp.maximum(jnp.maximum(l0, l1), l2)
        e0, e1, e2 = jnp.exp(l0 - mx), jnp.exp(l1 - mx), jnp.exp(l2 - mx)
        rden = 1.0 / (e0 + e1 + e2)
        o = expand(e0 * rden) * o0 + expand(e1 * rden) * o1 + expand(e2 * rden) * o2
        return jnp.dot(o.astype(BF16), wo[...], preferred_element_type=F32)

    _mix_mlp_body(mixer_out, x_ref, mod_ref, nmlp_ref, None, x2_ref, None, wo_ref, w1_ref, w2_ref)


def _mix_mlp_call(kern, name, lead_specs, lead_args, w_o, w1, w2, x2d, mod_l, nmlp, seq, next_args, extra_scratch):
    t, d = x2d.shape
    ff = w1.shape[2]
    c = w_o.shape[0]
    tm = MIX_ROWS
    tpb = seq // tm
    row = pl.BlockSpec((tm, d), lambda i: (i, 0))
    mod_spec = pl.BlockSpec((1, N_MOD, d), lambda i: (i // tpb, 0, 0))
    vec = pl.BlockSpec((1, d), lambda i: (0, 0))
    hbm = pl.BlockSpec(memory_space=pl.ANY)
    in_specs = lead_specs + [row, mod_spec, vec]
    args = lead_args + [x2d, mod_l, nmlp[None, :]]
    out_specs, out_shape = [row], [jax.ShapeDtypeStruct((t, d), F32)]
    if next_args is not None:
        nnext, mod_next = next_args
        in_specs += [vec, mod_spec]
        args += [nnext[None, :], mod_next]
        out_specs.append(row)
        out_shape.append(jax.ShapeDtypeStruct((t, d), BF16))
    in_specs += [hbm, hbm, hbm]
    args += [w_o, w1, w2]
    scratch = [pltpu.VMEM((c, d), BF16), pltpu.VMEM((d, ff), BF16), pltpu.VMEM((ff, d), BF16),
               pltpu.VMEM((LOAD_SLOTS, LOAD_ROWS_FF, ff), F32), pltpu.VMEM((LOAD_SLOTS, LOAD_ROWS_D, d), F32),
               pltpu.SemaphoreType.DMA((LOAD_SLOTS,))]
    return pl.pallas_call(
        kern,
        grid=(t // tm,),
        in_specs=in_specs,
        out_specs=out_specs,
        out_shape=out_shape,
        scratch_shapes=scratch + extra_scratch,
        compiler_params=_params(("arbitrary",)),
        name=name,
    )(*args)


def _mla_mlp(o_t, w_o, w1, w2, layer, x2d, mod_l, nmlp, nnext, mod_next, seq):
    tm = MIX_ROWS
    tpb = seq // tm
    c = o_t.shape[2]
    assert o_t.shape[3] == tm
    lead = [pl.BlockSpec((1, 1, c, tm), lambda i: (i // tpb, i % tpb, 0, 0))]
    return _mix_mlp_call(functools.partial(_mla_mlp_kernel, layer=layer), "mla_mlp", lead, [o_t], w_o, w1, w2, x2d,
                         mod_l, nmlp, seq, (nnext, mod_next), [])


def _dil_mlp(outs, stats, w_o, w1, w2, layer, x2d, mod_l, nmlp, seq):
    tm = MIX_ROWS
    tpb = seq // tm
    c = w_o.shape[0]
    expand = jnp.repeat(jnp.eye(DIL_HEADS, dtype=F32), DIL_HEAD_DIM, axis=1)
    expand = jnp.pad(expand, ((0, LANES - DIL_HEADS), (0, 0))).astype(BF16)
    grp = lambda g, w: pl.BlockSpec((1, tm // DIL_GROUPS[g][1], DIL_GROUPS[g][1] * w),
                                    lambda i: (i // tpb, i % tpb, 0))
    lead = [grp(0, c), grp(1, c), grp(2, c), grp(0, LANES), grp(1, LANES), grp(2, LANES),
            pl.BlockSpec((LANES, c), lambda i: (0, 0))]
    extra = []
    for g in (1, 2):
        extra += [pltpu.VMEM((c // LANES, tm, LANES), F32), pltpu.VMEM((tm, LANES), F32)]
    (x2,) = _mix_mlp_call(functools.partial(_dil_mlp_kernel, layer=layer), "dil_mlp", lead, [*outs, *stats, expand],
                          w_o, w1, w2, x2d, mod_l, nmlp, seq, None, extra)
    return x2


DIL_TQ = 128
DIL_HALF = 64
DIL_ATTN_POSITIONS = 2048
DIL_ATTN_CLASSES = 8
assert all(w // (2 * d) == DIL_HALF for w, d in DIL_GROUPS)


def _dil_geometry(seq, dilation):
    length = seq // dilation
    kwin = min(2 * DIL_TQ, length)
    return length, kwin, length // DIL_TQ


def _t5_log_thresholds():
    nb = N_BUCKETS // 2
    max_exact = nb // 2
    steps = nb - max_exact
    thresholds = []
    for k in range(1, steps):
        n = max_exact
        while math.floor(math.log(n / max_exact) / math.log(MAX_DISTANCE / max_exact) * steps) < k:
            n += 1
        thresholds.append(n)
    return thresholds


def _t5_bucket(rel):
    nb = N_BUCKETS // 2
    max_exact = nb // 2
    n = jnp.abs(rel)
    large = max_exact
    for thr in _t5_log_thresholds():
        large = large + jnp.where(n >= thr, 1, 0)
    return jnp.where(rel > 0, nb, 0) + jnp.where(n < max_exact, n, large)


BIAS_ROWS = 16


def _dil_bias_kernel(tab_ref, o_ref, *, group, dilation, kwin, n_var):
    def rows(step, carry):
        r0 = pl.multiple_of(step * BIAS_ROWS, BIAS_ROWS)
        kk = lax.broadcasted_iota(jnp.int32, (BIAS_ROWS, DIL_TQ), 0) + r0
        qi = lax.broadcasted_iota(jnp.int32, (BIAS_ROWS, DIL_TQ), 1)
        rel_a = kk - qi - (DIL_HALF if n_var > 1 else 0)
        valid = jnp.abs(rel_a) <= DIL_HALF
        same_end = (qi < DIL_HALF) == (kk < DIL_TQ)
        bucket = _t5_bucket(rel_a * dilation)
        accs = [jnp.zeros((BIAS_ROWS, DIL_TQ), F32)] * DIL_HEADS
        for b in range(N_BUCKETS):
            hit = bucket == b
            accs = [jnp.where(hit, tab_ref[b, group * DIL_HEADS + h], a) for h, a in enumerate(accs)]
        for h, a in enumerate(accs):
            where = (h // 2, pl.ds(r0, BIAS_ROWS), slice((h % 2) * DIL_TQ, (h % 2 + 1) * DIL_TQ))
            tile = jnp.where(valid, a * LOG2E, MASKED)
            o_ref[(0, *where)] = tile
            if n_var > 1:
                o_ref[(1, *where)] = jnp.where(same_end, tile, MASKED)
        return carry

    lax.fori_loop(0, kwin // BIAS_ROWS, rows, 0)


def _dil_bias(rel_bias, group, seq):
    _, dilation = DIL_GROUPS[group]
    _, kwin, n_tiles = _dil_geometry(seq, dilation)
    n_var = 1 if n_tiles == 1 else 2
    n_pairs = DIL_HEADS // 2
    kern = functools.partial(_dil_bias_kernel, group=group, dilation=dilation, kwin=kwin, n_var=n_var)
    return pl.pallas_call(
        kern,
        grid=(1,),
        in_specs=[pl.BlockSpec(memory_space=pltpu.SMEM)],
        out_specs=pl.BlockSpec((n_var, n_pairs, kwin, 2 * DIL_TQ), lambda v: (0, 0, 0, 0)),
        out_shape=jax.ShapeDtypeStruct((n_var, n_pairs, kwin, 2 * DIL_TQ), F32),
        compiler_params=_params(("arbitrary",)),
        name=f"dil_bias{group}",
    )(rel_bias)


DIL_PROJ_ROWS = 1024
DIL_PROJ_CLASSES = 4


def _dil_proj_kernel(h_ref, wqf_ref, wkf_ref, wvf_ref, gq_ref, qT_ref, k_ref, vT_ref, wq_ref, wk_ref, wv_ref,
                     *scratch, dilation, rb, ra):
    hd = DIL_HEAD_DIM
    d_model = h_ref.shape[2]
    n_slabs = d_model // LANES
    mt = rb * ra

    @pl.when((pl.program_id(0) == 0) & (pl.program_id(1) == 0) & (pl.program_id(2) == 0))
    def _():
        wq_ref[...] = wqf_ref[0].T.astype(BF16)
        wk_ref[...] = wkf_ref[0].astype(BF16)
        wv_ref[...] = wvf_ref[0].T.astype(BF16)

    if dilation == 1:
        hcat = h_ref[0]
    else:
        (scr,) = scratch
        sub = pl.program_id(2)

        @pl.when(sub == 0)
        def _():
            for s in range(n_slabs):
                scr[s] = h_ref[0, :, s * LANES:(s + 1) * LANES].astype(F32)

        rows = []
        for j in range(rb):
            res = sub * rb + j
            rows.append(jnp.concatenate(
                [scr[s, pl.ds(res, ra, stride=dilation), :] for s in range(n_slabs)], axis=1))
        hcat = jnp.concatenate(rows, axis=0).astype(BF16)

    q_t = lax.dot_general(wq_ref[...], hcat, NT_DIMS, preferred_element_type=F32)
    gq = jnp.concatenate([gq_ref[...]] * (mt // LANES), axis=1)
    for h in range(DIL_HEADS):
        blk = q_t[h * hd:(h + 1) * hd]
        r = lax.rsqrt(jnp.sum(blk * blk, axis=0, keepdims=True) * (1.0 / hd) + EPS)
        qb = (blk * r * gq).astype(BF16)
        for j in range(rb):
            for s in range(ra // LANES):
                lo = j * ra + s * LANES
                qT_ref[0, j, s, h * hd:(h + 1) * hd, :] = qb[:, lo:lo + LANES]

    kf = jnp.dot(hcat, wk_ref[...], preferred_element_type=F32)
    lane = lax.broadcasted_iota(jnp.int32, (mt, LANES), 1)
    low = lane < hd
    for c in range(DIL_HEADS * hd // LANES):
        y = kf[:, c * LANES:(c + 1) * LANES]
        y2 = y * y
        s_lo = jnp.sum(jnp.where(low, y2, 0.0), axis=1, keepdims=True)
        s_hi = jnp.sum(jnp.where(low, 0.0, y2), axis=1, keepdims=True)
        r = lax.rsqrt(jnp.where(low, s_lo, s_hi) * (1.0 / hd) + EPS)
        kb = (y * r).astype(BF16)
        for j in range(rb):
            k_ref[0, j, :, c * LANES:(c + 1) * LANES] = kb[j * ra:(j + 1) * ra]

    v_t = lax.dot_general(wv_ref[...], hcat, NT_DIMS, preferred_element_type=F32).astype(BF16)
    for j in range(rb):
        for s in range(ra // LANES):
            lo = j * ra + s * LANES
            vT_ref[0, j, s] = v_t[:, lo:lo + LANES]


def _dil_proj(hn3, w_in, g_q, g_k, group):
    batch, seq, d_model = hn3.shape
    _, dilation = DIL_GROUPS[group]
    length = seq // dilation
    c = DIL_HEADS * DIL_HEAD_DIM
    rb = min(dilation, DIL_PROJ_CLASSES if dilation <= DIL_PROJ_CLASSES else 2 * DIL_PROJ_CLASSES)
    rows = DIL_PROJ_ROWS
    ra = rows // rb
    tok = ra * dilation
    n_sub = dilation // rb
    gq_col = jnp.broadcast_to((g_q * g_k * (DIL_HEAD_DIM ** -0.5 * LOG2E))[:, None], (DIL_HEAD_DIM, LANES))
    kern = functools.partial(_dil_proj_kernel, dilation=dilation, rb=rb, ra=ra)
    const = lambda shape: pl.BlockSpec(shape, lambda b, t, s: (0,) * len(shape))
    w_blk = lambda j: pl.BlockSpec((1, d_model, c), lambda b, t, s: (0, 0, 3 * group + j))
    scratch = [pltpu.VMEM((c, d_model), BF16), pltpu.VMEM((d_model, c), BF16), pltpu.VMEM((c, d_model), BF16)]
    if dilation > 1:
        scratch.append(pltpu.VMEM((d_model // LANES, tok, LANES), F32))
    return pl.pallas_call(
        kern,
        grid=(batch, seq // tok, n_sub),
        in_specs=[
            pl.BlockSpec((1, tok, d_model), lambda b, t, s: (b, t, 0)),
            w_blk(0), w_blk(1), w_blk(2),
            const((DIL_HEAD_DIM, LANES)),
        ],
        out_specs=[
            pl.BlockSpec((1, rb, ra // LANES, c, LANES), lambda b, t, s: (b, s, t, 0, 0)),
            pl.BlockSpec((1, rb, ra, c), lambda b, t, s: (b, s, t, 0)),
            pl.BlockSpec((1, rb, ra // LANES, c, LANES), lambda b, t, s: (b, s, t, 0, 0)),
        ],
        out_shape=[
            jax.ShapeDtypeStruct((batch, dilation, length // LANES, c, LANES), BF16),
            jax.ShapeDtypeStruct((batch, dilation, length, c), BF16),
            jax.ShapeDtypeStruct((batch, dilation, length // LANES, c, LANES), BF16),
        ],
        scratch_shapes=scratch,
        compiler_params=_params(("arbitrary", "arbitrary", "arbitrary")),
        name=f"dil_proj{group}",
    )(hn3, w_in, w_in, w_in, gq_col)


def _dil_attn_kernel(safe_ref, qT_ref, k_ref, vT_ref, bm_ref, o_ref, st_ref, s_ref, *, n_tiles, rb):
    tq = DIL_TQ
    hd = DIL_HEAD_DIM
    n_pairs = DIL_HEADS // 2
    c = DIL_HEADS * hd
    shifted = n_tiles > 1
    low = lax.broadcasted_iota(jnp.int32, (LANES, tq), 0) < hd
    ones = jnp.ones((ONES_ROWS, 2 * tq if shifted else tq), BF16)
    prow = lambda p: slice(p * LANES, (p + 1) * LANES)

    def block_diag(q2):
        zero = jnp.zeros_like(q2)
        return jnp.concatenate([jnp.where(low, q2, zero), jnp.where(low, zero, q2)], axis=1)

    def attend(s_t, v2, use_max):
        if use_max:
            m = jnp.max(s_t, axis=0, keepdims=True)
            pb = jnp.exp2(s_t - m).astype(BF16)
        else:
            pb = jnp.exp2(s_t).astype(BF16)
        res = jnp.dot(jnp.concatenate([v2, ones], axis=0), pb, preferred_element_type=F32)
        den = res[LANES:LANES + 1]
        rinv = 1.0 / den
        o_pair = jnp.concatenate([res[:hd, :tq] * rinv[:, :tq], res[hd:LANES, tq:] * rinv[:, tq:]], axis=0)
        lse = jnp.log2(den) + m if use_max else jnp.log2(den)
        lse = lse * LN2
        return o_pair.T.astype(BF16), [lse[:, :tq], lse[:, tq:]]

    def stats_rows(stats):
        return jnp.concatenate(stats + [jnp.zeros((LANES - DIL_HEADS, tq), F32)], axis=0).T

    tiles = [(j, t) for j in range(rb) for t in range(n_tiles)]

    @pl.when(safe_ref[0] == 1)
    def _():
        def score_tile(i):
            j, t = tiles[i]
            t1 = (t + 1) % n_tiles
            for p in range(n_pairs):
                if shifted:
                    q2 = jnp.concatenate(
                        [qT_ref[0, j, t, prow(p), DIL_HALF:], qT_ref[0, j, t1, prow(p), :DIL_HALF]], axis=1)
                    k2 = jnp.concatenate([k_ref[0, j, t * tq:(t + 1) * tq, prow(p)],
                                          k_ref[0, j, t1 * tq:(t1 + 1) * tq, prow(p)]], axis=0)
                else:
                    q2 = qT_ref[0, j, 0, prow(p), :]
                    k2 = k_ref[0, j, :, prow(p)]
                s_ref[i % 2, p] = jnp.dot(k2, block_diag(q2), preferred_element_type=F32)

        def attend_tile(i):
            j, t = tiles[i]
            t1 = (t + 1) % n_tiles
            var = 1 if shifted and t + 1 == n_tiles else 0

            def store(ref, lanes, val):
                if shifted:
                    ref[0, t * tq + DIL_HALF:(t + 1) * tq, lanes] = val[:DIL_HALF]
                    ref[0, t1 * tq:t1 * tq + DIL_HALF, lanes] = val[DIL_HALF:]
                else:
                    ref[0, :, lanes] = val

            stats = []
            for p in range(n_pairs):
                if shifted:
                    v2 = jnp.concatenate([vT_ref[0, j, t, prow(p), :], vT_ref[0, j, t1, prow(p), :]], axis=1)
                else:
                    v2 = vT_ref[0, j, 0, prow(p), :]
                o_rows, lses = attend(s_ref[i % 2, p] + bm_ref[var, p], v2, use_max=False)
                store(o_ref, slice(j * c + p * LANES, j * c + (p + 1) * LANES), o_rows)
                stats += lses
            store(st_ref, slice(j * LANES, (j + 1) * LANES), stats_rows(stats))

        score_tile(0)
        for i in range(len(tiles)):
            if i + 1 < len(tiles):
                score_tile(i + 1)
            attend_tile(i)

    @pl.when(safe_ref[0] != 1)
    def _():
        def tile(j, t):
            if shifted:
                last = t + 1 == n_tiles
                t1 = jnp.where(last, 0, t + 1)
                var = jnp.where(last, 1, 0)
                r0 = pl.multiple_of(t * tq, tq)
                r1 = pl.multiple_of(t1 * tq, tq)

            def store(ref, lanes, val):
                if shifted:
                    ref[0, pl.ds(r0 + DIL_HALF, DIL_HALF), lanes] = val[:DIL_HALF]
                    ref[0, pl.ds(r1, DIL_HALF), lanes] = val[DIL_HALF:]
                else:
                    ref[0, :, lanes] = val

            stats = []
            for p in range(n_pairs):
                if shifted:
                    q2 = jnp.concatenate(
                        [qT_ref[0, j, t, prow(p), DIL_HALF:], qT_ref[0, j, t1, prow(p), :DIL_HALF]], axis=1)
                    k2 = jnp.concatenate(
                        [k_ref[0, j, pl.ds(r0, tq), prow(p)], k_ref[0, j, pl.ds(r1, tq), prow(p)]], axis=0)
                    v2 = jnp.concatenate([vT_ref[0, j, t, prow(p), :], vT_ref[0, j, t1, prow(p), :]], axis=1)
                    bias = bm_ref[var, p]
                else:
                    q2 = qT_ref[0, j, 0, prow(p), :]
                    k2 = k_ref[0, j, :, prow(p)]
                    v2 = vT_ref[0, j, 0, prow(p), :]
                    bias = bm_ref[0, p]
                s_t = jnp.dot(k2, block_diag(q2), preferred_element_type=F32) + bias
                o_rows, lses = attend(s_t, v2, use_max=True)
                store(o_ref, slice(j * c + p * LANES, j * c + (p + 1) * LANES), o_rows)
                stats += lses
            store(st_ref, slice(j * LANES, (j + 1) * LANES), stats_rows(stats))

        for j in range(rb):
            if shifted:
                lax.fori_loop(0, n_tiles, lambda t, carry, j=j: (tile(j, t), carry)[1], 0)
            else:
                tile(j, 0)


def _dil_attn(q_t, k, v_t, bias, safe, group, seq):
    batch = q_t.shape[0]
    _, dilation = DIL_GROUPS[group]
    length, kwin, n_tiles = _dil_geometry(seq, dilation)
    c = DIL_HEADS * DIL_HEAD_DIM
    n_slabs = length // LANES
    rb = min(dilation, DIL_ATTN_POSITIONS // length, DIL_ATTN_CLASSES)
    kern = functools.partial(_dil_attn_kernel, n_tiles=n_tiles, rb=rb)
    return pl.pallas_call(
        kern,
        grid=(batch, dilation // rb),
        in_specs=[
            pl.BlockSpec(memory_space=pltpu.SMEM),
            pl.BlockSpec((1, rb, n_slabs, c, LANES), lambda b, r: (b, r, 0, 0, 0)),
            pl.BlockSpec((1, rb, length, c), lambda b, r: (b, r, 0, 0)),
            pl.BlockSpec((1, rb, n_slabs, c, LANES), lambda b, r: (b, r, 0, 0, 0)),
            pl.BlockSpec(bias.shape, lambda b, r: (0, 0, 0, 0)),
        ],
        out_specs=[
            pl.BlockSpec((1, length, rb * c), lambda b, r: (b, 0, r)),
            pl.BlockSpec((1, length, rb * LANES), lambda b, r: (b, 0, r)),
        ],
        out_shape=[
            jax.ShapeDtypeStruct((batch, length, dilation * c), BF16),
            jax.ShapeDtypeStruct((batch, length, dilation * LANES), F32),
        ],
        scratch_shapes=[pltpu.VMEM((2, DIL_HEADS // 2, kwin, 2 * DIL_TQ), F32)],
        compiler_params=_params(("arbitrary", "arbitrary")),
        name=f"dil_attn{group}",
    )(safe, q_t, k, v_t, bias)


def kernel(x, c, positions, ada_w, ada_b, norm_mix, norm_mlp, mlp_w1, mlp_w2, mla_w_in, mla_g_qa, mla_w_qb,
           mla_g_kva, mla_w_kvb, mla_g_q, mla_g_k, mla_w_o, dil_w_in, dil_g_q, dil_g_k, dil_w_o, rel_bias):
    batch, seq, d = x.shape
    depth = ada_w.shape[0]
    assert depth == 2 and len(DIL_GROUPS) == 3
    mod = _mod(c, ada_w, ada_b).reshape(depth, batch, N_MOD, d)
    x2d = x.reshape(batch * seq, d)

    n_heads = mla_w_qb.shape[2] // mla_g_q.shape[1]
    q_t, k_pad, v_t = _mla_proj(x2d, mod[0], norm_mix[0], positions, mla_w_in[0], mla_g_qa[0], mla_w_qb[0],
                                mla_g_kva[0], mla_w_kvb[0], mla_g_q[0], mla_g_k[0], batch, seq)
    o_t = _mla_attn(q_t, k_pad, v_t, _score_bound_is_safe(mla_g_q[0] * mla_g_k[0], mla_g_q.shape[1]), n_heads)
    x2, hn = _mla_mlp(o_t, mla_w_o[0], mlp_w1, mlp_w2, 0, x2d, mod[0], norm_mlp[0], norm_mix[1], mod[1], seq)

    outs, stats = [], []
    hn3 = hn.reshape(batch, seq, d)
    for g in range(len(DIL_GROUPS)):
        q_g, k_g, v_g = _dil_proj(hn3, dil_w_in, dil_g_q[0, g], dil_g_k[0, g], g)
        safe = _score_bound_is_safe(dil_g_q[0, g] * dil_g_k[0, g], DIL_HEAD_DIM,
                                    rel_bias[:, g * DIL_HEADS:(g + 1) * DIL_HEADS])
        o_g, st_g = _dil_attn(q_g, k_g, v_g, _dil_bias(rel_bias, g, seq), safe, g, seq)
        outs.append(o_g)
        stats.append(st_g)
    x4 = _dil_mlp(outs, stats, dil_w_o[0], mlp_w1, mlp_w2, 1, x2, mod[1], norm_mlp[1], seq)
    return x4.reshape(batch, seq, d)
```
